```python
import math
import jax, jax.numpy as jnp
from jax import lax
import numpy as np

D_MODEL = 1024
BATCH = 16
SEQ = 256
DEPTH = 2
DEC_BATCH = 4
DEC_SEQ = 2048
PAST_LEN = 256

GRID_W = 64
HEAD_DIM = 64
N_HEADS_A = 8
N_KV_A = 2
REP_A = N_HEADS_A // N_KV_A
N_HEADS_C = 4
SSM_WIDTH = 256
SSM_GROUP = 16
SSM_GROUPS = SSM_WIDTH // SSM_GROUP
SSM_STATE = 64
NA_WIN_R = 8
NA_WIN_C = 16
Q_BLOCK = 128
D_FF = -(-8 * D_MODEL // (3 * 256)) * 256
ROPE_THETA = 10000.0
ROT_HALF = HEAD_DIM // 2
ROT_FREQS = ROT_HALF // 2
WIDTH_A = N_HEADS_A * HEAD_DIM
KV_WIDTH_A = N_KV_A * HEAD_DIM
WIDTH_C = N_HEADS_C * HEAD_DIM
N_BRANCH = 3
IN_WIDTH = WIDTH_A + 2 * KV_WIDTH_A + SSM_WIDTH + 3 * WIDTH_C + N_BRANCH * D_MODEL
EPS = 1e-6

kernel_name = "hybrid_flow_backbone_ctx_prefix_step"

F32 = jnp.float32


def rms_norm(x, g):
    x32 = x.astype(F32)
    y = x32 * lax.rsqrt(jnp.mean(x32 * x32, axis=-1, keepdims=True) + EPS) * g.astype(F32)
    return y.astype(x.dtype)


def adaln(cvec, lp):
    m = jax.nn.silu(cvec) @ lp["w_mod"] + lp["b_mod"]
    return jnp.split(m, 6, axis=-1)


def modulate(h, shift, scale):
    return h * (1 + scale[:, None, :]) + shift[:, None, :]


def axial_rope_tables(L):
    t = jnp.arange(L)
    row = (t // GRID_W).astype(F32)
    col = (t % GRID_W).astype(F32)
    inv = 1.0 / (ROPE_THETA ** (jnp.arange(ROT_FREQS, dtype=F32) / ROT_FREQS))
    ar = row[:, None] * inv[None]
    ac = col[:, None] * inv[None]
    return jnp.cos(ar), jnp.sin(ar), jnp.cos(ac), jnp.sin(ac)


def _rot(xs, cos, sin):
    x1, x2 = xs[..., :ROT_FREQS], xs[..., ROT_FREQS:]
    return jnp.concatenate([x1 * cos - x2 * sin, x2 * cos + x1 * sin], axis=-1)


def apply_axial_rope(x, tables):
    cr, sr, cc, sc = tables
    shp = (x.shape[1],) + (1,) * (x.ndim - 3) + (ROT_FREQS,)
    cr, sr, cc, sc = (a.reshape(shp) for a in (cr, sr, cc, sc))
    x32 = x.astype(F32)
    out = jnp.concatenate([_rot(x32[..., :ROT_HALF], cr, sr), _rot(x32[..., ROT_HALF:], cc, sc)], axis=-1)
    return out.astype(x.dtype)


def blocked_attention(q, k, v):
    bsz, lq, g, r, hd = q.shape
    nb = lq // Q_BLOCK
    qb = jnp.moveaxis(q.reshape(bsz, nb, Q_BLOCK, g, r, hd), 1, 0)
    scale = HEAD_DIM ** -0.5

    def block(qi):
        s = jnp.einsum('bqgrd,bkgd->bgrqk', qi, k).astype(F32) * scale
        p = jax.nn.softmax(s, axis=-1).astype(v.dtype)
        return jnp.einsum('bgrqk,bkgd->bqgrd', p, v)

    o = lax.map(block, qb)
    return jnp.moveaxis(o, 0, 1).reshape(bsz, lq, g, r, hd)


def neighborhood_attention(q, k, v, ck, cv, bias_table):
    bsz, L, H, hd = q.shape
    rows = L // GRID_W
    wr = min(NA_WIN_R, rows)
    wc = NA_WIN_C
    nw = wr * wc
    t = jnp.arange(L)
    r = t // GRID_W
    col = t % GRID_W
    rs = jnp.clip(r - wr // 2, 0, rows - wr)
    cs = jnp.clip(col - wc // 2, 0, GRID_W - wc)
    kr = rs[:, None] + jnp.arange(wr)[None]
    kc = cs[:, None] + jnp.arange(wc)[None]
    idx = (kr[:, :, None] * GRID_W + kc[:, None, :]).reshape(L, nw)
    dr = kr - r[:, None] + (NA_WIN_R - 1)
    dc = kc - col[:, None] + (NA_WIN_C - 1)
    bias = bias_table[:, dr[:, :, None], dc[:, None, :]].reshape(H, L, nw)
    nb = L // Q_BLOCK
    qb = jnp.moveaxis(q.reshape(bsz, nb, Q_BLOCK, H, hd), 1, 0)
    ib = idx.reshape(nb, Q_BLOCK, nw)
    bb = jnp.moveaxis(bias.reshape(H, nb, Q_BLOCK, nw), 1, 0)
    scale = HEAD_DIM ** -0.5

    def block(args):
        qi, ii, bi = args
        kg = jnp.take(k, ii, axis=1)
        vg = jnp.take(v, ii, axis=1)
        s_loc = jnp.einsum('bqhd,bqwhd->bhqw', qi, kg).astype(F32) * scale + bi[None].astype(F32)
        s_ctx = jnp.einsum('bqhd,bkhd->bhqk', qi, ck).astype(F32) * scale
        p = jax.nn.softmax(jnp.concatenate([s_loc, s_ctx], axis=-1), axis=-1).astype(v.dtype)
        return (jnp.einsum('bhqw,bqwhd->bqhd', p[..., :nw], vg)
                + jnp.einsum('bhqk,bkhd->bqhd', p[..., nw:], cv))

    o = lax.map(block, (qb, ib, bb))
    return jnp.moveaxis(o, 0, 1).reshape(bsz, L, H, hd)


def _scan_combine(e1, e2):
    a1, b1 = e1
    a2, b2 = e2
    return a1 * a2, a2 * b1 + b2


def ssm_scan(u32, lam_re, lam_im, log_step, b_re, b_im, c_re, c_im, h0, reverse):
    step = jnp.exp(log_step.astype(F32))
    lam = lax.complex(lam_re.astype(F32), lam_im.astype(F32))
    lam_bar = jnp.exp(lam * step[:, None])
    b = lax.complex(b_re.astype(F32), b_im.astype(F32))
    b_bar = ((lam_bar - 1) / lam)[..., None] * b
    cmat = lax.complex(c_re.astype(F32), c_im.astype(F32))
    bu = jnp.einsum('blgc,gpc->blgp', u32.astype(jnp.complex64), b_bar)
    if reverse:
        bu = jnp.flip(bu, axis=1)
    bu = bu.at[:, 0].add(lam_bar[None] * h0)
    a = jnp.broadcast_to(lam_bar, bu.shape)
    _, h = lax.associative_scan(_scan_combine, (a, bu), axis=1)
    h_final = h[:, -1]
    if reverse:
        h = jnp.flip(h, axis=1)
    y = jnp.real(jnp.einsum('blgp,gcp->blgc', h, cmat))
    return y, h_final


def s5_mixer(u, lp, state):
    bsz, L, _ = u.shape
    u32 = u.astype(F32).reshape(bsz, L, SSM_GROUPS, SSM_GROUP)
    st = state.astype(F32)
    ys, finals = [], []
    for d in range(2):
        h0 = lax.complex(st[:, d, 0], st[:, d, 1])
        y, hf = ssm_scan(u32, lp["ssm_lam_re"][d], lp["ssm_lam_im"][d], lp["ssm_log_step"][d],
                         lp["ssm_b_re"][d], lp["ssm_b_im"][d], lp["ssm_c_re"][d], lp["ssm_c_im"][d],
                         h0, reverse=(d == 1))
        ys.append(y)
        finals.append(jnp.stack([jnp.real(hf), jnp.imag(hf)], axis=1))
    y = (ys[0] + ys[1]).reshape(bsz, L, SSM_WIDTH) + lp["ssm_d"].astype(F32) * u32.reshape(bsz, L, SSM_WIDTH)
    y = jax.nn.gelu(y)
    y = y * jax.nn.sigmoid(y @ lp["ssm_w_glu"].astype(F32))
    return y.astype(u.dtype), jnp.stack(finals, axis=1)


def project_inputs(h, lp):
    bsz, L, _ = h.shape
    p = h @ lp["w_in"]
    sizes = [WIDTH_A, KV_WIDTH_A, KV_WIDTH_A, SSM_WIDTH, WIDTH_C, WIDTH_C, WIDTH_C]
    offs = [int(o) for o in np.cumsum(sizes)]
    qa, ka, va, u, qc, kc, vc, g = jnp.split(p, offs, axis=-1)
    qa = rms_norm(qa.reshape(bsz, L, N_KV_A, REP_A, HEAD_DIM), lp["qn_g"])
    ka = rms_norm(ka.reshape(bsz, L, N_KV_A, HEAD_DIM), lp["kn_g"])
    va = va.reshape(bsz, L, N_KV_A, HEAD_DIM)
    qc = qc.reshape(bsz, L, N_HEADS_C, HEAD_DIM)
    kc = kc.reshape(bsz, L, N_HEADS_C, HEAD_DIM)
    vc = vc.reshape(bsz, L, N_HEADS_C, HEAD_DIM)
    g = g.reshape(bsz, L, N_BRANCH, D_MODEL)
    return qa, ka, va, u, qc, kc, vc, g


def merge_branches(oa, ob, oc, g, lp):
    bsz, L = oa.shape[:2]
    gs = jax.nn.sigmoid(g)
    merged = (gs[:, :, 0] * (oa.reshape(bsz, L, WIDTH_A) @ lp["w_br_a"])
              + gs[:, :, 1] * (ob @ lp["w_br_b"])
              + gs[:, :, 2] * (oc.reshape(bsz, L, WIDTH_C) @ lp["w_br_c"]))
    return merged @ lp["w_out"]


def swiglu(h, lp):
    gu = h @ lp["w_ffn_gu"]
    gt, up = jnp.split(gu, 2, axis=-1)
    return (jax.nn.silu(gt) * up) @ lp["w_ffn_d"]


def trunk_layer(x, cvec, lp, mix_fn):
    sh1, sc1, gt1, sh2, sc2, gt2 = adaln(cvec, lp)
    h = modulate(rms_norm(x, lp["norm1_g"]), sh1, sc1)
    qa, ka, va, u, qc, kc, vc, g = project_inputs(h, lp)
    oa, ob, oc, aux = mix_fn(qa, ka, va, u, qc, kc, vc)
    x = x + gt1[:, None, :] * merge_branches(oa, ob, oc, g, lp)
    h2 = modulate(rms_norm(x, lp["norm2_g"]), sh2, sc2)
    x = x + gt2[:, None, :] * swiglu(h2, lp)
    return x, aux


def setup_inputs(seed: int = 0) -> dict:
    key = jax.random.key(seed)
    ks = iter(jax.random.split(key, 48))
    nrm = lambda shape, s=1.0: jax.random.normal(next(ks), shape, F32) * s
    D = D_MODEL
    n_idx = jnp.arange(SSM_STATE, dtype=F32)
    inp = {}
    inp["x_prompt"] = nrm((BATCH, SEQ, D))
    inp["x_sample"] = nrm((DEC_BATCH, DEC_SEQ, D))
    inp["c"] = nrm((DEC_BATCH, D))
    inp["cache_ga_k"] = nrm((DEC_BATCH, DEPTH, PAST_LEN, N_KV_A, HEAD_DIM))
    inp["cache_ga_v"] = nrm((DEC_BATCH, DEPTH, PAST_LEN, N_KV_A, HEAD_DIM))
    inp["cache_na_k"] = nrm((DEC_BATCH, DEPTH, PAST_LEN, N_HEADS_C, HEAD_DIM))
    inp["cache_na_v"] = nrm((DEC_BATCH, DEPTH, PAST_LEN, N_HEADS_C, HEAD_DIM))
    inp["state_ssm"] = nrm((DEC_BATCH, DEPTH, 2, 2, SSM_GROUPS, SSM_STATE), 0.1)
    inp["c_ctx"] = nrm((D,))
    inp["w_mod"] = nrm((DEPTH, D, 6 * D), 0.5 * D ** -0.5)
    inp["b_mod"] = nrm((DEPTH, 6 * D), 0.02)
    inp["norm1_g"] = 1.0 + nrm((DEPTH, D), 0.01)
    inp["w_in"] = nrm((DEPTH, D, IN_WIDTH), D ** -0.5)
    inp["qn_g"] = 1.0 + nrm((DEPTH, HEAD_DIM), 0.01)
    inp["kn_g"] = 1.0 + nrm((DEPTH, HEAD_DIM), 0.01)
    inp["ssm_lam_re"] = -0.5 + nrm((DEPTH, 2, SSM_GROUPS, SSM_STATE), 0.01)
    inp["ssm_lam_im"] = math.pi * n_idx + nrm((DEPTH, 2, SSM_GROUPS, SSM_STATE), 0.01)
    inp["ssm_log_step"] = jax.random.uniform(next(ks), (DEPTH, 2, SSM_GROUPS), F32,
                                             minval=math.log(1e-3), maxval=math.log(1e-1))
    inp["ssm_b_re"] = nrm((DEPTH, 2, SSM_GROUPS, SSM_STATE, SSM_GROUP), (2 * SSM_GROUP) ** -0.5)
    inp["ssm_b_im"] = nrm((DEPTH, 2, SSM_GROUPS, SSM_STATE, SSM_GROUP), (2 * SSM_GROUP) ** -0.5)
    inp["ssm_c_re"] = nrm((DEPTH, 2, SSM_GROUPS, SSM_GROUP, SSM_STATE), SSM_STATE ** -0.5)
    inp["ssm_c_im"] = nrm((DEPTH, 2, SSM_GROUPS, SSM_GROUP, SSM_STATE), SSM_STATE ** -0.5)
    inp["ssm_d"] = nrm((DEPTH, SSM_WIDTH))
    inp["ssm_w_glu"] = nrm((DEPTH, SSM_WIDTH, SSM_WIDTH), SSM_WIDTH ** -0.5)
    inp["na_bias"] = nrm((DEPTH, N_HEADS_C, 2 * NA_WIN_R - 1, 2 * NA_WIN_C - 1), 0.02)
    inp["w_br_a"] = nrm((DEPTH, WIDTH_A, D), WIDTH_A ** -0.5)
    inp["w_br_b"] = nrm((DEPTH, SSM_WIDTH, D), SSM_WIDTH ** -0.5)
    inp["w_br_c"] = nrm((DEPTH, WIDTH_C, D), WIDTH_C ** -0.5)
    inp["w_out"] = nrm((DEPTH, D, D), D ** -0.5)
    inp["norm2_g"] = 1.0 + nrm((DEPTH, D), 0.01)
    inp["w_ffn_gu"] = nrm((DEPTH, D, 2 * D_FF), D ** -0.5)
    inp["w_ffn_d"] = nrm((DEPTH, D_FF, D), D_FF ** -0.5)
    inp["final_g"] = 1.0 + nrm((D,), 0.01)
    return inp


def reference(x_prompt, x_sample, c, cache_ga_k, cache_ga_v, cache_na_k, cache_na_v, state_ssm, c_ctx,
              w_mod, b_mod, norm1_g, w_in, qn_g, kn_g, ssm_lam_re, ssm_lam_im, ssm_log_step,
              ssm_b_re, ssm_b_im, ssm_c_re, ssm_c_im, ssm_d, ssm_w_glu, na_bias,
              w_br_a, w_br_b, w_br_c, w_out, norm2_g, w_ffn_gu, w_ffn_d, final_g):
    def layer_params(i):
        return dict(w_mod=w_mod[i], b_mod=b_mod[i], norm1_g=norm1_g[i], w_in=w_in[i], qn_g=qn_g[i],
                    kn_g=kn_g[i], ssm_lam_re=ssm_lam_re[i], ssm_lam_im=ssm_lam_im[i],
                    ssm_log_step=ssm_log_step[i], ssm_b_re=ssm_b_re[i], ssm_b_im=ssm_b_im[i],
                    ssm_c_re=ssm_c_re[i], ssm_c_im=ssm_c_im[i], ssm_d=ssm_d[i], ssm_w_glu=ssm_w_glu[i],
                    na_bias=na_bias[i], w_br_a=w_br_a[i], w_br_b=w_br_b[i], w_br_c=w_br_c[i],
                    w_out=w_out[i], norm2_g=norm2_g[i], w_ffn_gu=w_ffn_gu[i], w_ffn_d=w_ffn_d[i])

    xp = x_prompt
    bp = xp.shape[0]
    cvec_ctx = c_ctx[None, :]
    zero_state = jnp.zeros((bp, 2, 2, SSM_GROUPS, SSM_STATE), F32)
    ga_k, ga_v, na_k, na_v, ssm_st = [], [], [], [], []
    for i in range(DEPTH):
        lp = layer_params(i)

        def ctx_mix(qa, ka, va, u, qc, kc, vc, lp=lp):
            oa = blocked_attention(qa, ka, va)
            ob, st = s5_mixer(u, lp, zero_state)
            oc = blocked_attention(qc[:, :, :, None, :], kc, vc)[:, :, :, 0, :]
            return oa, ob, oc, (ka, va, kc, vc, st)

        xp, (ka_i, va_i, kc_i, vc_i, st_i) = trunk_layer(xp, cvec_ctx, lp, ctx_mix)
        ga_k.append(ka_i)
        ga_v.append(va_i)
        na_k.append(kc_i)
        na_v.append(vc_i)
        ssm_st.append(st_i)
    y_prompt = rms_norm(xp, final_g)
    new_ga_k = jnp.stack(ga_k, axis=1)
    new_ga_v = jnp.stack(ga_v, axis=1)
    new_na_k = jnp.stack(na_k, axis=1)
    new_na_v = jnp.stack(na_v, axis=1)
    new_ssm = jnp.stack(ssm_st, axis=1)

    xs = x_sample
    rope = axial_rope_tables(xs.shape[1])
    for i in range(DEPTH):
        lp = layer_params(i)
        cka, cva = cache_ga_k[:, i], cache_ga_v[:, i]
        ckc, cvc = cache_na_k[:, i], cache_na_v[:, i]
        st0 = state_ssm[:, i]

        def lat_mix(qa, ka, va, u, qc, kc, vc, lp=lp, cka=cka, cva=cva, ckc=ckc, cvc=cvc, st0=st0):
            qr = apply_axial_rope(qa, rope)
            kr = apply_axial_rope(ka, rope)
            oa = blocked_attention(qr, jnp.concatenate([cka.astype(kr.dtype), kr], axis=1),
                                   jnp.concatenate([cva.astype(va.dtype), va], axis=1))
            ob, _ = s5_mixer(u, lp, st0)
            oc = neighborhood_attention(qc, kc, vc, ckc.astype(kc.dtype), cvc.astype(vc.dtype), lp["na_bias"])
            return oa, ob, oc, None

        xs, _ = trunk_layer(xs, c, lp, lat_mix)
    y_sample = rms_norm(xs, final_g)

    return (y_prompt, y_sample, new_ga_k, new_ga_v, new_na_k, new_na_v, new_ssm)
```

```python
import functools
import math

import numpy as np
import jax
import jax.numpy as jnp
from jax import lax
from jax.experimental import pallas as pl
from jax.experimental.pallas import tpu as pltpu

D_MODEL = 1024
BATCH = 16
SEQ = 256
DEPTH = 2
DEC_BATCH = 4
DEC_SEQ = 2048
PAST_LEN = 256
GRID_W = 64
GRID_ROWS = DEC_SEQ // GRID_W
HEAD_DIM = 64
N_HEADS_A = 8
N_KV_A = 2
REP_A = N_HEADS_A // N_KV_A
N_HEADS_C = 4
SSM_WIDTH = 256
SSM_GROUP = 16
SSM_GROUPS = SSM_WIDTH // SSM_GROUP
SSM_STATE = 64
SSM_LANES = SSM_GROUPS * SSM_STATE
NA_WIN_R = 8
NA_WIN_C = 16
D_FF = -(-8 * D_MODEL // (3 * 256)) * 256
ROPE_THETA = 10000.0
ROT_HALF = HEAD_DIM // 2
ROT_FREQS = ROT_HALF // 2
WIDTH_A = N_HEADS_A * HEAD_DIM
KV_WIDTH_A = N_KV_A * HEAD_DIM
WIDTH_C = N_HEADS_C * HEAD_DIM
N_BRANCH = 3
IN_WIDTH = WIDTH_A + 2 * KV_WIDTH_A + SSM_WIDTH + 3 * WIDTH_C + N_BRANCH * D_MODEL
EPS = 1e-6

OFF_QA = 0
OFF_KA = OFF_QA + WIDTH_A
OFF_VA = OFF_KA + KV_WIDTH_A
OFF_U = OFF_VA + KV_WIDTH_A
OFF_QC = OFF_U + SSM_WIDTH
OFF_KC = OFF_QC + WIDTH_C
OFF_VC = OFF_KC + WIDTH_C
OFF_G = OFF_VC + WIDTH_C

N_MOD_ROWS = 8
ROW_TILE = 256
Q_TILE_A = 256
NA_Q_ROWS = 2
NA_K_ROWS = 10
NA_R0_MAX = GRID_ROWS - NA_K_ROWS
SCAN_ROWS = 1024
NEG = -1e30
VMEM_LIMIT_V7X = 56 * 1024 * 1024

F32 = jnp.float32
BF16 = jnp.bfloat16


def _dot(a, b):
    return jnp.dot(a, b, preferred_element_type=F32)


def _dot_t(a, b):
    return lax.dot_general(a, b, (((1,), (1,)), ((), ())), preferred_element_type=F32)


def _params(n_axes):
    return pltpu.CompilerParams(dimension_semantics=("arbitrary",) * n_axes,
                                vmem_limit_bytes=VMEM_LIMIT_V7X)


def _const_spec(shape):
    nd = len(shape)
    return pl.BlockSpec(shape, lambda *_: (0,) * nd, pipeline_mode=pl.Buffered(1))


def _rms(x, g):
    return x * lax.rsqrt(jnp.mean(x * x, axis=-1, keepdims=True) + EPS) * g


def _silu(x):
    return x * jax.nn.sigmoid(x)


def _gelu_tanh(x):
    c = math.sqrt(2.0 / math.pi)
    return x * (0.5 * (1.0 + jnp.tanh(c * (x + 0.044715 * (x * x * x)))))


def _seg_rms(x, seg, g):
    x2 = x * x
    hi = x2.astype(BF16)
    lo = (x2 - hi.astype(F32)).astype(BF16)
    ms = _dot(hi, seg) + _dot(lo, seg)
    return x * lax.rsqrt(ms + EPS) * g


def _rope(x, cos, sin_signed):
    w = x.shape[-1]
    lane = lax.broadcasted_iota(jnp.int32, x.shape, 1)
    first = (lane & ROT_FREQS) == 0
    partner = jnp.where(first, pltpu.roll(x, w - ROT_FREQS, 1), pltpu.roll(x, ROT_FREQS, 1))
    return x * cos + partner * sin_signed


def _rep_heads(kv):
    lane = lax.broadcasted_iota(jnp.int32, kv.shape, 1)
    swapped = pltpu.roll(kv, HEAD_DIM, 1)
    lo = lane < HEAD_DIM
    h0 = jnp.where(lo, kv, swapped)
    h1 = jnp.where(lo, swapped, kv)
    return jnp.concatenate([h0, h0, h1, h1], axis=1)


def _adaln_kernel(c_ref, w_ref, b_ref, o_ref):
    s = _silu(c_ref[...])
    o_ref[...] = _dot(s.astype(BF16), w_ref[...].astype(BF16)) + b_ref[...]


def _adaln(cvec, w_mod, b_mod):
    n_col = 6 * D_MODEL
    tn = n_col // 4
    return pl.pallas_call(
        _adaln_kernel,
        grid=(DEPTH, n_col // tn),
        in_specs=[pl.BlockSpec((N_MOD_ROWS, D_MODEL), lambda l, n: (0, 0)),
                  pl.BlockSpec((None, D_MODEL, tn), lambda l, n: (l, 0, n)),
                  pl.BlockSpec((None, 1, tn), lambda l, n: (l, 0, n))],
        out_specs=pl.BlockSpec((None, N_MOD_ROWS, tn), lambda l, n: (l, 0, n)),
        out_shape=jax.ShapeDtypeStruct((DEPTH, N_MOD_ROWS, n_col), F32),
        compiler_params=_params(2),
        name="adaln",
    )(cvec, w_mod, b_mod.reshape(DEPTH, 1, n_col))


def _inproj_kernel(*refs, latent):
    if latent:
        (x_ref, mod_ref, g1_ref, w_ref, seg_ref, qg_ref, kg_ref, cos_ref, sin_ref,
         qa_ref, krep_ref, vrep_ref, u_ref, qc_ref, kc_ref, vc_ref, g_ref) = refs
    else:
        (x_ref, mod_ref, g1_ref, w_ref, seg_ref, qg_ref, kg_ref,
         qa_ref, krep_ref, vrep_ref, ka_ref, va_ref, u_ref, qc_ref, kc_ref, vc_ref, g_ref) = refs
    mod = mod_ref[...]
    shift = mod[:, 0:D_MODEL]
    scale = mod[:, D_MODEL:2 * D_MODEL]
    h = _rms(x_ref[...], g1_ref[...]) * (1 + scale) + shift
    hb = h.astype(BF16)
    seg = seg_ref[...]
    q_scale = HEAD_DIM ** -0.5

    qa = _seg_rms(_dot(hb, w_ref[:, OFF_QA:OFF_KA]), seg, qg_ref[...])
    ka = _seg_rms(_dot(hb, w_ref[:, OFF_KA:OFF_VA]), seg[0:KV_WIDTH_A, 0:KV_WIDTH_A], kg_ref[...])
    va = _dot(hb, w_ref[:, OFF_VA:OFF_U])
    if latent:
        cos = cos_ref[...]
        sin = sin_ref[...]
        qa = _rope(qa, jnp.concatenate([cos] * REP_A, axis=1), jnp.concatenate([sin] * REP_A, axis=1))
        ka = _rope(ka, cos, sin)
    else:
        ka_ref[...] = ka
        va_ref[...] = va
    qa_ref[...] = (qa * q_scale).astype(BF16)
    krep_ref[...] = _rep_heads(ka).astype(BF16)
    vrep_ref[...] = _rep_heads(va).astype(BF16)

    u_ref[...] = _dot(hb, w_ref[:, OFF_U:OFF_QC])
    qc_ref[...] = (_dot(hb, w_ref[:, OFF_QC:OFF_KC]) * q_scale).astype(BF16)
    kc_ref[...] = _dot(hb, w_ref[:, OFF_KC:OFF_VC]).astype(kc_ref.dtype)
    vc_ref[...] = _dot(hb, w_ref[:, OFF_VC:OFF_G]).astype(vc_ref.dtype)
    for cidx in range(N_BRANCH):
        lo = OFF_G + cidx * D_MODEL
        g_ref[:, cidx * D_MODEL:(cidx + 1) * D_MODEL] = _dot(hb, w_ref[:, lo:lo + D_MODEL])


def _inproj(x, mod_l, g1, w_in, seg, qg, kg, rope_tabs, *, latent):
    n_tok = x.shape[0]
    n_steps = n_tok // ROW_TILE
    if latent:
        seq, nb = DEC_SEQ, DEC_BATCH
    else:
        seq, nb = SEQ, BATCH
    tps = seq // ROW_TILE
    tok = lambda i: (i, 0)
    tmaj = lambda i: (i % tps, i // tps)
    mod_idx = (lambda i: (1 + i // tps, 0, 0)) if latent else (lambda i: (0, 0, 0))
    in_specs = [pl.BlockSpec((ROW_TILE, D_MODEL), tok),
                pl.BlockSpec((None, 1, 6 * D_MODEL), mod_idx),
                _const_spec((1, D_MODEL)),
                _const_spec((D_MODEL, IN_WIDTH)),
                _const_spec((WIDTH_A, WIDTH_A)),
                _const_spec((1, WIDTH_A)),
                _const_spec((1, KV_WIDTH_A))]
    args = [x, mod_l, g1, w_in, seg, qg, kg]
    kv_dtype = BF16 if latent else F32
    out_shape = [jax.ShapeDtypeStruct((n_tok, WIDTH_A), BF16),
                 jax.ShapeDtypeStruct((n_tok, REP_A * KV_WIDTH_A), BF16),
                 jax.ShapeDtypeStruct((n_tok, REP_A * KV_WIDTH_A), BF16)]
    out_specs = [pl.BlockSpec((ROW_TILE, WIDTH_A), tok),
                 pl.BlockSpec((ROW_TILE, REP_A * KV_WIDTH_A), tok),
                 pl.BlockSpec((ROW_TILE, REP_A * KV_WIDTH_A), tok)]
    if latent:
        in_specs += [pl.BlockSpec((ROW_TILE, 2 * HEAD_DIM), lambda i: (i % tps, 0))] * 2
        args += list(rope_tabs)
    else:
        out_shape += [jax.ShapeDtypeStruct((n_tok, KV_WIDTH_A), F32)] * 2
        out_specs += [pl.BlockSpec((ROW_TILE, KV_WIDTH_A), tok)] * 2
    out_shape += [jax.ShapeDtypeStruct((seq, nb * SSM_WIDTH), F32),
                  jax.ShapeDtypeStruct((n_tok, WIDTH_C), BF16),
                  jax.ShapeDtypeStruct((n_tok, WIDTH_C), kv_dtype),
                  jax.ShapeDtypeStruct((n_tok, WIDTH_C), kv_dtype),
                  jax.ShapeDtypeStruct((n_tok, N_BRANCH * D_MODEL), F32)]
    out_specs += [pl.BlockSpec((ROW_TILE, SSM_WIDTH), tmaj),
                  pl.BlockSpec((ROW_TILE, WIDTH_C), tok),
                  pl.BlockSpec((ROW_TILE, WIDTH_C), tok),
                  pl.BlockSpec((ROW_TILE, WIDTH_C), tok),
                  pl.BlockSpec((ROW_TILE, N_BRANCH * D_MODEL), tok)]
    return pl.pallas_call(
        functools.partial(_inproj_kernel, latent=latent),
        grid=(n_steps,),
        in_specs=in_specs,
        out_specs=out_specs,
        out_shape=out_shape,
        compiler_params=_params(1),
        name="inproj_lat" if latent else "inproj_ctx",
    )(*args)


def _head_mask(shape, head):
    lane = lax.broadcasted_iota(jnp.int32, shape, 1)
    return (lane // HEAD_DIM) == head


def _softmax_pv(parts):
    m = None
    for s, _ in parts:
        pm = jnp.max(s, axis=-1, keepdims=True)
        m = pm if m is None else jnp.maximum(m, pm)
    l = None
    o = None
    for s, v in parts:
        p = jnp.exp(s - m)
        pl_ = jnp.sum(p, axis=-1, keepdims=True)
        po = _dot(p.astype(BF16), v)
        l = pl_ if l is None else l + pl_
        o = po if o is None else o + po
    return o / l


def _attn_a_kernel(*refs, has_cache):
    if has_cache:
        q_ref, kn_ref, vn_ref, ck_ref, cv_ref, o_ref, ckrep_ref, cvrep_ref = refs

        @pl.when(pl.program_id(1) == 0)
        def _():
            ckrep_ref[...] = _rep_heads(ck_ref[...]).astype(BF16)
            cvrep_ref[...] = _rep_heads(cv_ref[...]).astype(BF16)
    else:
        q_ref, kn_ref, vn_ref, o_ref = refs
    gw = REP_A * HEAD_DIM
    for g in range(N_KV_A):
        sl = slice(g * gw, (g + 1) * gw)
        qg = q_ref[:, sl]
        kg = kn_ref[:, sl]
        vg = vn_ref[:, sl]
        acc = jnp.zeros(qg.shape, F32)
        for j in range(REP_A):
            hm = _head_mask(qg.shape, j)
            qm = jnp.where(hm, qg, jnp.zeros_like(qg))
            parts = []
            if has_cache:
                parts.append((_dot_t(qm, ckrep_ref[:, sl]), cvrep_ref[:, sl]))
            parts.append((_dot_t(qm, kg), vg))
            acc = jnp.where(hm, _softmax_pv(parts), acc)
        o_ref[:, sl] = acc.astype(BF16)


def _attn_a_ctx(qa, krep, vrep):
    blk = lambda w: pl.BlockSpec((SEQ, w), lambda b: (b, 0))
    return pl.pallas_call(
        functools.partial(_attn_a_kernel, has_cache=False),
        grid=(BATCH,),
        in_specs=[blk(WIDTH_A), blk(WIDTH_A), blk(WIDTH_A)],
        out_specs=blk(WIDTH_A),
        out_shape=jax.ShapeDtypeStruct((BATCH * SEQ, WIDTH_A), BF16),
        compiler_params=_params(1),
        name="attn_a_ctx",
    )(qa, krep, vrep)


def _attn_a_lat(qa, krep, vrep, cache_k, cache_v):
    nq = DEC_SEQ // Q_TILE_A
    seq_blk = pl.BlockSpec((DEC_SEQ, WIDTH_A), lambda b, t: (b, 0))
    cache_blk = pl.BlockSpec((None, PAST_LEN, KV_WIDTH_A), lambda b, t: (b, 0, 0))
    q_blk = pl.BlockSpec((Q_TILE_A, WIDTH_A), lambda b, t: (b * nq + t, 0))
    return pl.pallas_call(
        functools.partial(_attn_a_kernel, has_cache=True),
        grid=(DEC_BATCH, nq),
        in_specs=[q_blk, seq_blk, seq_blk, cache_blk, cache_blk],
        out_specs=q_blk,
        out_shape=jax.ShapeDtypeStruct((DEC_BATCH * DEC_SEQ, WIDTH_A), BF16),
        scratch_shapes=[pltpu.VMEM((PAST_LEN, WIDTH_A), BF16), pltpu.VMEM((PAST_LEN, WIDTH_A), BF16)],
        compiler_params=_params(2),
        name="attn_a_lat",
    )(qa, krep, vrep, cache_k, cache_v)


def _attn_c_kernel(q_ref, k_ref, v_ref, o_ref):
    q = q_ref[...]
    k = k_ref[...].astype(BF16)
    v = v_ref[...].astype(BF16)
    acc = jnp.zeros(q.shape, F32)
    for h in range(N_HEADS_C):
        hm = _head_mask(q.shape, h)
        qm = jnp.where(hm, q, jnp.zeros_like(q))
        acc = jnp.where(hm, _softmax_pv([(_dot_t(qm, k), v)]), acc)
    o_ref[...] = acc.astype(BF16)


def _attn_c_ctx(qc, kc, vc):
    blk = pl.BlockSpec((SEQ, WIDTH_C), lambda b: (b, 0))
    return pl.pallas_call(
        _attn_c_kernel,
        grid=(BATCH,),
        in_specs=[blk, blk, blk],
        out_specs=blk,
        out_shape=jax.ShapeDtypeStruct((BATCH * SEQ, WIDTH_C), BF16),
        compiler_params=_params(1),
        name="attn_c_ctx",
    )(qc, kc, vc)


def _na_kernel(q_ref, k_ref, v_ref, ck_ref, cv_ref, b2_ref, o_ref):
    nq = NA_Q_ROWS * GRID_W
    nk = NA_K_ROWS * GRID_W
    qrow0 = NA_Q_ROWS * pl.program_id(1)
    row0 = jnp.clip(qrow0 - NA_WIN_R // 2, 0, NA_R0_MAX)
    start = pl.multiple_of(row0 * GRID_W, GRID_W)
    kwin = k_ref[pl.ds(start, nk), :]
    vwin = v_ref[pl.ds(start, nk), :]
    ck = ck_ref[...].astype(BF16)
    cv = cv_ref[...].astype(BF16)

    qrow = qrow0 + lax.broadcasted_iota(jnp.int32, (nq, nk), 0) // GRID_W
    krow = row0 + lax.broadcasted_iota(jnp.int32, (nq, nk), 1) // GRID_W
    win0 = jnp.clip(qrow - NA_WIN_R // 2, 0, GRID_ROWS - NA_WIN_R)
    row_bias = jnp.where(krow < win0, NEG, jnp.where(krow >= win0 + NA_WIN_R, NEG, 0.0))

    q = q_ref[...]
    acc = jnp.zeros(q.shape, F32)
    for h in range(N_HEADS_C):
        hm = _head_mask(q.shape, h)
        qm = jnp.where(hm, q, jnp.zeros_like(q))
        rows = []
        for i in range(NA_Q_ROWS):
            tiles = []
            for m in range(NA_K_ROWS // 2):
                e = jnp.clip(row0 + 2 * m - (qrow0 + i) + NA_WIN_R, 0, 2 * NA_WIN_R - 1)
                tiles.append(b2_ref[h, e])
            rows.append(jnp.concatenate(tiles, axis=1))
        bias = jnp.concatenate(rows, axis=0) + row_bias
        parts = [(_dot_t(qm, kwin) + bias, vwin), (_dot_t(qm, ck), cv)]
        acc = jnp.where(hm, _softmax_pv(parts), acc)
    o_ref[...] = acc.astype(BF16)


def _na_bias_tiles(tbl):
    c = np.arange(GRID_W)
    cs = np.clip(c - NA_WIN_C // 2, 0, GRID_W - NA_WIN_C)
    kc = np.arange(GRID_W)
    valid = (kc[None, :] >= cs[:, None]) & (kc[None, :] < cs[:, None] + NA_WIN_C)
    dc = np.clip(kc[None, :] - c[:, None] + NA_WIN_C - 1, 0, 2 * NA_WIN_C - 2)
    tiles = jnp.where(valid[None, None], tbl[:, :, dc], NEG)
    zero = jnp.zeros((tbl.shape[0], 1, GRID_W, GRID_W), F32)
    t = jnp.concatenate([zero, tiles, zero], axis=1)
    return jnp.concatenate([t[:, :-1], t[:, 1:]], axis=-1)


def _na_lat(qc, kc, vc, cache_k, cache_v, b2):
    nq = NA_Q_ROWS * GRID_W
    nblk = DEC_SEQ // nq
    seq_blk = pl.BlockSpec((DEC_SEQ, WIDTH_C), lambda b, j: (b, 0))
    cache_blk = pl.BlockSpec((None, PAST_LEN, WIDTH_C), lambda b, j: (b, 0, 0))
    q_blk = pl.BlockSpec((nq, WIDTH_C), lambda b, j: (b * nblk + j, 0))
    return pl.pallas_call(
        _na_kernel,
        grid=(DEC_BATCH, nblk),
        in_specs=[q_blk, seq_blk, seq_blk, cache_blk, cache_blk, _const_spec(b2.shape)],
        out_specs=q_blk,
        out_shape=jax.ShapeDtypeStruct((DEC_BATCH * DEC_SEQ, WIDTH_C), BF16),
        compiler_params=_params(2),
        name="na_lat",
    )(qc, kc, vc, cache_k, cache_v, b2)


def _scan_kernel(uf_ref, ub_ref, bmat_ref, lam_ref, cmat_ref, h0_ref, yf_ref, yb_ref, hfin_ref,
                 hf_ref, hb_ref, st_ref, *, nb, lane_w):
    steps = SCAN_ROWS // nb

    @pl.when(pl.program_id(0) == 0)
    def _():
        st_ref[...] = h0_ref[...]

    hf_ref[...] = _dot(uf_ref[...].astype(BF16), bmat_ref[0])
    hb_ref[...] = _dot(ub_ref[...].astype(BF16), bmat_ref[1])

    for lb in range(SSM_LANES // lane_w):
        re = slice(lb * lane_w, (lb + 1) * lane_w)
        im = slice(SSM_LANES + lb * lane_w, SSM_LANES + (lb + 1) * lane_w)
        lam = [[jnp.broadcast_to(lam_ref[d, c, :, re], (nb, lane_w)) for c in range(2)] for d in range(2)]
        init = tuple(st_ref[d, c, :, re] for d in range(2) for c in range(2))

        def step(k, carry, re=re, im=im, lam=lam):
            fr, fi, br, bi = carry
            rf = pl.multiple_of(k * nb, nb)
            rb = pl.multiple_of((steps - 1 - k) * nb, nb)
            nfr = lam[0][0] * fr - lam[0][1] * fi + hf_ref[pl.ds(rf, nb), re]
            nfi = lam[0][0] * fi + lam[0][1] * fr + hf_ref[pl.ds(rf, nb), im]
            nbr = lam[1][0] * br - lam[1][1] * bi + hb_ref[pl.ds(rb, nb), re]
            nbi = lam[1][0] * bi + lam[1][1] * br + hb_ref[pl.ds(rb, nb), im]
            hf_ref[pl.ds(rf, nb), re] = nfr
            hf_ref[pl.ds(rf, nb), im] = nfi
            hb_ref[pl.ds(rb, nb), re] = nbr
            hb_ref[pl.ds(rb, nb), im] = nbi
            return nfr, nfi, nbr, nbi

        fin = lax.fori_loop(0, steps, step, init)
        for d in range(2):
            for c in range(2):
                st_ref[d, c, :, re] = fin[2 * d + c]

    yf_ref[...] = _dot(hf_ref[...].astype(BF16), cmat_ref[0])
    yb_ref[...] = _dot(hb_ref[...].astype(BF16), cmat_ref[1])
    hfin_ref[...] = st_ref[...]


def _scan_pair_kernel(uf_ref, ub_ref, bmat_ref, lam_ref, cmat_ref, h0_ref, yf_ref, yb_ref,
                      hf_ref, hb_ref, st_ref, *, lane_w):
    half = DEC_BATCH
    tiles = SCAN_ROWS // (2 * half)

    @pl.when(pl.program_id(0) == 0)
    def _():
        st_ref[...] = h0_ref[...]

    hf_ref[...] = _dot(uf_ref[...].astype(BF16), bmat_ref[0])
    hb_ref[...] = _dot(ub_ref[...].astype(BF16), bmat_ref[1])

    top = lax.broadcasted_iota(jnp.int32, (2 * half, lane_w), 0) < half
    swap = lambda a: pltpu.roll(a, half, 0)
    for lb in range(SSM_LANES // lane_w):
        re = slice(lb * lane_w, (lb + 1) * lane_w)
        im = slice(SSM_LANES + lb * lane_w, SSM_LANES + (lb + 1) * lane_w)
        la_r = lam_ref[0, :, re]
        la_i = lam_ref[1, :, re]
        lb_r = swap(la_r)
        lb_i = swap(la_i)

        def body(m, carry, re=re, im=im, la_r=la_r, la_i=la_i, lb_r=lb_r, lb_i=lb_i):
            sr, si = carry
            rf = pl.multiple_of(m * 2 * half, 2 * half)
            rb = pl.multiple_of((tiles - 1 - m) * 2 * half, 2 * half)
            fr = hf_ref[pl.ds(rf, 2 * half), re]
            fi = hf_ref[pl.ds(rf, 2 * half), im]
            br = hb_ref[pl.ds(rb, 2 * half), re]
            bi = hb_ref[pl.ds(rb, 2 * half), im]
            vr = la_r * sr - la_i * si + jnp.where(top, fr, br)
            vi = la_r * si + la_i * sr + jnp.where(top, fi, bi)
            tr = swap(vr)
            ti = swap(vi)
            wr = lb_r * tr - lb_i * ti + jnp.where(top, br, fr)
            wi = lb_r * ti + lb_i * tr + jnp.where(top, bi, fi)
            hf_ref[pl.ds(rf, 2 * half), re] = jnp.where(top, vr, wr)
            hf_ref[pl.ds(rf, 2 * half), im] = jnp.where(top, vi, wi)
            hb_ref[pl.ds(rb, 2 * half), re] = jnp.where(top, wr, vr)
            hb_ref[pl.ds(rb, 2 * half), im] = jnp.where(top, wi, vi)
            return swap(wr), swap(wi)

        fin = lax.fori_loop(0, tiles, body, (st_ref[0, :, re], st_ref[1, :, re]))
        st_ref[0, :, re] = fin[0]
        st_ref[1, :, re] = fin[1]

    yf_ref[...] = _dot(hf_ref[...].astype(BF16), cmat_ref[0])
    yb_ref[...] = _dot(hb_ref[...].astype(BF16), cmat_ref[1])


def _scan_specs(n_rows):
    n = n_rows // SCAN_ROWS
    fwd = pl.BlockSpec((SCAN_ROWS, SSM_WIDTH), lambda j: (j, 0))
    bwd = pl.BlockSpec((SCAN_ROWS, SSM_WIDTH), lambda j: (n - 1 - j, 0))
    y_shape = jax.ShapeDtypeStruct((n_rows, SSM_WIDTH), F32)
    buf = pltpu.VMEM((SCAN_ROWS, 2 * SSM_LANES), F32)
    return n, fwd, bwd, y_shape, buf


def _scan_ctx(u_rows, bmat, lam, cmat, h0):
    n, fwd, bwd, y_shape, buf = _scan_specs(u_rows.shape[0])
    st_shape = (2, 2, BATCH, SSM_LANES)
    return pl.pallas_call(
        functools.partial(_scan_kernel, nb=BATCH, lane_w=256),
        grid=(n,),
        in_specs=[fwd, bwd, _const_spec(bmat.shape), _const_spec(lam.shape), _const_spec(cmat.shape),
                  _const_spec(st_shape)],
        out_specs=[fwd, bwd, pl.BlockSpec(st_shape, lambda j: (0, 0, 0, 0))],
        out_shape=[y_shape, y_shape, jax.ShapeDtypeStruct(st_shape, F32)],
        scratch_shapes=[buf, buf, pltpu.VMEM(st_shape, F32)],
        compiler_params=_params(1),
        name="scan_ctx",
    )(u_rows, u_rows, bmat, lam, cmat, h0)


def _scan_lat(u_rows, bmat, lam_pair, cmat, h0_pair):
    n, fwd, bwd, y_shape, buf = _scan_specs(u_rows.shape[0])
    st_shape = (2, 2 * DEC_BATCH, SSM_LANES)
    return pl.pallas_call(
        functools.partial(_scan_pair_kernel, lane_w=512),
        grid=(n,),
        in_specs=[fwd, bwd, _const_spec(bmat.shape), _const_spec(st_shape), _const_spec(cmat.shape),
                  _const_spec(st_shape)],
        out_specs=[fwd, bwd],
        out_shape=[y_shape, y_shape],
        scratch_shapes=[buf, buf, pltpu.VMEM(st_shape, F32)],
        compiler_params=_params(1),
        name="scan_lat",
    )(u_rows, u_rows, bmat, lam_pair, cmat, h0_pair)


def _ssm_discretise(lam_re, lam_im, log_step, b_re, b_im, c_re, c_im):
    step = jnp.exp(log_step.astype(F32))
    lam = lax.complex(lam_re.astype(F32), lam_im.astype(F32))
    lam_bar = jnp.exp(lam * step[..., None])
    b = lax.complex(b_re.astype(F32), b_im.astype(F32))
    b_bar = ((lam_bar - 1) / lam)[..., None] * b
    eye = jnp.eye(SSM_GROUPS, dtype=F32)
    blk_b = lambda a: jnp.einsum('dgpc,gh->dgchp', a, eye).reshape(2, SSM_WIDTH, SSM_LANES)
    bmat = jnp.concatenate([blk_b(jnp.real(b_bar)), blk_b(jnp.imag(b_bar))], axis=-1)
    blk_c = lambda a: jnp.einsum('dgcp,gh->dgphc', a, eye).reshape(2, SSM_LANES, SSM_WIDTH)
    cmat = jnp.concatenate([blk_c(c_re.astype(F32)), -blk_c(c_im.astype(F32))], axis=1)
    lam_flat = jnp.stack([jnp.real(lam_bar), jnp.imag(lam_bar)], axis=1).reshape(2, 2, 1, SSM_LANES)
    return bmat.astype(BF16), lam_flat, cmat.astype(BF16)


def _merge_kernel(x_ref, oa_ref, yf_ref, yb_ref, u_ref, oc_ref, g_ref, mod_ref, d_ref, wglu_ref,
                  wa_ref, wb_ref, wc_ref, wo_ref, g2_ref, wgu_ref, wd_ref, fg_ref, o_ref, *, final):
    y = _gelu_tanh(yf_ref[...] + yb_ref[...] + d_ref[...] * u_ref[...])
    ob = y * jax.nn.sigmoid(_dot(y.astype(BF16), wglu_ref[...]))
    gate = lambda i: jax.nn.sigmoid(g_ref[:, i * D_MODEL:(i + 1) * D_MODEL])
    merged = (gate(0) * _dot(oa_ref[...], wa_ref[...])
              + gate(1) * _dot(ob.astype(BF16), wb_ref[...])
              + gate(2) * _dot(oc_ref[...], wc_ref[...]))
    mod = mod_ref[...]
    gate1 = mod[:, 2 * D_MODEL:3 * D_MODEL]
    shift2 = mod[:, 3 * D_MODEL:4 * D_MODEL]
    scale2 = mod[:, 4 * D_MODEL:5 * D_MODEL]
    gate2 = mod[:, 5 * D_MODEL:6 * D_MODEL]
    x1 = x_ref[...] + gate1 * _dot(merged.astype(BF16), wo_ref[...])
    h2 = _rms(x1, g2_ref[...]) * (1 + scale2) + shift2
    gu = _dot(h2.astype(BF16), wgu_ref[...])
    act = _silu(gu[:, :D_FF]) * gu[:, D_FF:]
    x2 = x1 + gate2 * _dot(act.astype(BF16), wd_ref[...])
    if final:
        x2 = _rms(x2, fg_ref[...])
    o_ref[...] = x2


def _merge(x, oa, yf, yb, u_tm, oc, g, mod_l, lw, final_g, *, latent, final):
    n_tok = x.shape[0]
    seq = DEC_SEQ if latent else SEQ
    tps = seq // ROW_TILE
    tok = lambda w: pl.BlockSpec((ROW_TILE, w), lambda i: (i, 0))
    tmaj = pl.BlockSpec((ROW_TILE, SSM_WIDTH), lambda i: (i % tps, i // tps))
    mod_idx = (lambda i: (1 + i // tps, 0, 0)) if latent else (lambda i: (0, 0, 0))
    weights = [lw["ssm_d"], lw["w_glu"], lw["w_br_a"], lw["w_br_b"], lw["w_br_c"], lw["w_out"],
               lw["norm2_g"], lw["w_ffn_gu"], lw["w_ffn_d"], final_g]
    return pl.pallas_call(
        functools.partial(_merge_kernel, final=final),
        grid=(n_tok // ROW_TILE,),
        in_specs=[tok(D_MODEL), tok(WIDTH_A), tmaj, tmaj, tmaj, tok(WIDTH_C), tok(N_BRANCH * D_MODEL),
                  pl.BlockSpec((None, 1, 6 * D_MODEL), mod_idx)] + [_const_spec(w.shape) for w in weights],
        out_specs=tok(D_MODEL),
        out_shape=jax.ShapeDtypeStruct((n_tok, D_MODEL), F32),
        compiler_params=_params(1),
        name=("merge_lat" if latent else "merge_ctx") + ("_final" if final else ""),
    )(x, oa, yf, yb, u_tm, oc, g, mod_l, *weights)


def _rope_tables():
    t = jnp.arange(DEC_SEQ)
    row = (t // GRID_W).astype(F32)
    col = (t % GRID_W).astype(F32)
    inv = 1.0 / (ROPE_THETA ** (jnp.arange(ROT_FREQS, dtype=F32) / ROT_FREQS))
    ar = row[:, None] * inv[None]
    ac = col[:, None] * inv[None]
    cos = jnp.concatenate([jnp.cos(ar), jnp.cos(ar), jnp.cos(ac), jnp.cos(ac)], axis=-1)
    sin = jnp.concatenate([-jnp.sin(ar), jnp.sin(ar), -jnp.sin(ac), jnp.sin(ac)], axis=-1)
    return jnp.tile(cos, (1, 2)), jnp.tile(sin, (1, 2))


def kernel(x_prompt, x_sample, c, cache_ga_k, cache_ga_v, cache_na_k, cache_na_v, state_ssm, c_ctx, w_mod, b_mod, norm1_g, w_in, qn_g, kn_g, ssm_lam_re, ssm_lam_im, ssm_log_step, ssm_b_re, ssm_b_im, ssm_c_re, ssm_c_im, ssm_d, ssm_w_glu, na_bias, w_br_a, w_br_b, w_br_c, w_out, norm2_g, w_ffn_gu, w_ffn_d, final_g):
    cvec = jnp.concatenate([c_ctx[None, :], c, jnp.zeros((N_MOD_ROWS - 1 - DEC_BATCH, D_MODEL), F32)], axis=0)
    mod = _adaln(cvec, w_mod, b_mod).reshape(DEPTH, N_MOD_ROWS, 1, 6 * D_MODEL)

    seg = jnp.kron(jnp.eye(N_HEADS_A, dtype=F32), jnp.full((HEAD_DIM, HEAD_DIM), 1.0 / HEAD_DIM, F32)).astype(BF16)
    rope_tabs = _rope_tables()
    fg = final_g.reshape(1, D_MODEL)

    xp = x_prompt.reshape(BATCH * SEQ, D_MODEL)
    xs = x_sample.reshape(DEC_BATCH * DEC_SEQ, D_MODEL)
    ga_k, ga_v, na_k, na_v, ssm_st = [], [], [], [], []
    for l in range(DEPTH):
        lw = dict(ssm_d=ssm_d[l].reshape(1, SSM_WIDTH), w_glu=ssm_w_glu[l].astype(BF16),
                  w_br_a=w_br_a[l].astype(BF16), w_br_b=w_br_b[l].astype(BF16), w_br_c=w_br_c[l].astype(BF16),
                  w_out=w_out[l].astype(BF16), norm2_g=norm2_g[l].reshape(1, D_MODEL),
                  w_ffn_gu=w_ffn_gu[l].astype(BF16), w_ffn_d=w_ffn_d[l].astype(BF16))
        w_in_l = w_in[l].astype(BF16)
        g1 = norm1_g[l].reshape(1, D_MODEL)
        qg = jnp.tile(qn_g[l], N_HEADS_A).reshape(1, WIDTH_A)
        kg = jnp.tile(kn_g[l], N_KV_A).reshape(1, KV_WIDTH_A)
        bmat, lam, cmat = _ssm_discretise(ssm_lam_re[l], ssm_lam_im[l], ssm_log_step[l], ssm_b_re[l], ssm_b_im[l],
                                          ssm_c_re[l], ssm_c_im[l])
        final = l == DEPTH - 1

        qa, krep, vrep, ka, va, u_tm, qc, kc, vc, g = _inproj(xp, mod[l], g1, w_in_l, seg, qg, kg, None, latent=False)
        oa = _attn_a_ctx(qa, krep, vrep)
        oc = _attn_c_ctx(qc, kc, vc)
        zero_state = jnp.zeros((2, 2, BATCH, SSM_LANES), F32)
        yf, yb, hfin = _scan_ctx(u_tm.reshape(SEQ * BATCH, SSM_WIDTH), bmat, lam, cmat, zero_state)
        xp = _merge(xp, oa, yf.reshape(SEQ, BATCH * SSM_WIDTH), yb.reshape(SEQ, BATCH * SSM_WIDTH), u_tm, oc, g,
                    mod[l], lw, fg, latent=False, final=final)
        ga_k.append(ka.reshape(BATCH, SEQ, N_KV_A, HEAD_DIM))
        ga_v.append(va.reshape(BATCH, SEQ, N_KV_A, HEAD_DIM))
        na_k.append(kc.reshape(BATCH, SEQ, N_HEADS_C, HEAD_DIM))
        na_v.append(vc.reshape(BATCH, SEQ, N_HEADS_C, HEAD_DIM))
        ssm_st.append(jnp.transpose(hfin, (2, 0, 1, 3)).reshape(BATCH, 2, 2, SSM_GROUPS, SSM_STATE))

        qa, krep, vrep, u_tm, qc, kc, vc, g = _inproj(xs, mod[l], g1, w_in_l, seg, qg, kg, rope_tabs, latent=True)
        oa = _attn_a_lat(qa, krep, vrep,
                         cache_ga_k[:, l].reshape(DEC_BATCH, PAST_LEN, KV_WIDTH_A),
                         cache_ga_v[:, l].reshape(DEC_BATCH, PAST_LEN, KV_WIDTH_A))
        oc = _na_lat(qc, kc, vc,
                     cache_na_k[:, l].reshape(DEC_BATCH, PAST_LEN, WIDTH_C),
                     cache_na_v[:, l].reshape(DEC_BATCH, PAST_LEN, WIDTH_C),
                     _na_bias_tiles(na_bias[l]))
        h0 = jnp.transpose(state_ssm[:, l].reshape(DEC_BATCH, 2, 2, SSM_LANES), (2, 1, 0, 3))
        h0 = h0.reshape(2, 2 * DEC_BATCH, SSM_LANES)
        lam_pair = jnp.broadcast_to(jnp.transpose(lam, (1, 0, 2, 3)), (2, 2, DEC_BATCH, SSM_LANES))
        lam_pair = lam_pair.reshape(2, 2 * DEC_BATCH, SSM_LANES)
        yf, yb = _scan_lat(u_tm.reshape(DEC_SEQ * DEC_BATCH, SSM_WIDTH), bmat, lam_pair, cmat, h0)
        xs = _merge(xs, oa, yf.reshape(DEC_SEQ, DEC_BATCH * SSM_WIDTH), yb.reshape(DEC_SEQ, DEC_BATCH * SSM_WIDTH),
                    u_tm, oc, g, mod[l], lw, fg, latent=True, final=final)

    y_prompt = xp.reshape(BATCH, SEQ, D_MODEL)
    y_sample = xs.reshape(DEC_BATCH, DEC_SEQ, D_MODEL)
    return (y_prompt, y_sample, jnp.stack(ga_k, axis=1), jnp.stack(ga_v, axis=1),
            jnp.stack(na_k, axis=1), jnp.stack(na_v, axis=1), jnp.stack(ssm_st, axis=1))
```

```python
import functools
import math

import numpy as np
import jax
import jax.numpy as jnp
from jax import lax
from jax.experimental import pallas as pl
from jax.experimental.pallas import tpu as pltpu

D_MODEL = 1024
BATCH = 16
SEQ = 256
DEPTH = 2
DEC_BATCH = 4
DEC_SEQ = 2048
PAST_LEN = 256
GRID_W = 64
GRID_ROWS = DEC_SEQ // GRID_W
HEAD_DIM = 64
N_HEADS_A = 8
N_KV_A = 2
REP_A = N_HEADS_A // N_KV_A
N_HEADS_C = 4
SSM_WIDTH = 256
SSM_GROUP = 16
SSM_GROUPS = SSM_WIDTH // SSM_GROUP
SSM_STATE = 64
SSM_LANES = SSM_GROUPS * SSM_STATE
NA_WIN_R = 8
NA_WIN_C = 16
D_FF = -(-8 * D_MODEL // (3 * 256)) * 256
ROPE_THETA = 10000.0
ROT_HALF = HEAD_DIM // 2
ROT_FREQS = ROT_HALF // 2
WIDTH_A = N_HEADS_A * HEAD_DIM
KV_WIDTH_A = N_KV_A * HEAD_DIM
WIDTH_C = N_HEADS_C * HEAD_DIM
N_BRANCH = 3
IN_WIDTH = WIDTH_A + 2 * KV_WIDTH_A + SSM_WIDTH + 3 * WIDTH_C + N_BRANCH * D_MODEL
EPS = 1e-6

OFF_QA = 0
OFF_KA = OFF_QA + WIDTH_A
OFF_VA = OFF_KA + KV_WIDTH_A
OFF_U = OFF_VA + KV_WIDTH_A
OFF_QC = OFF_U + SSM_WIDTH
OFF_KC = OFF_QC + WIDTH_C
OFF_VC = OFF_KC + WIDTH_C
OFF_G = OFF_VC + WIDTH_C

N_MOD_ROWS = 8
ROW_TILE = 256
Q_TILE_A = 256
NA_Q_ROWS = 2
NA_K_ROWS = 10
NA_R0_MAX = GRID_ROWS - NA_K_ROWS
SCAN_ROWS = 1024
NEG = -1e30
VMEM_LIMIT_V7X = 56 * 1024 * 1024

F32 = jnp.float32
BF16 = jnp.bfloat16


def _dot(a, b):
    return jnp.dot(a, b, preferred_element_type=F32)


def _dot_t(a, b):
    return lax.dot_general(a, b, (((1,), (1,)), ((), ())), preferred_element_type=F32)


def _params(n_axes):
    return pltpu.CompilerParams(dimension_semantics=("arbitrary",) * n_axes,
                                vmem_limit_bytes=VMEM_LIMIT_V7X)


def _const_spec(shape):
    nd = len(shape)
    return pl.BlockSpec(shape, lambda *_: (0,) * nd, pipeline_mode=pl.Buffered(1))


def _rms(x, g):
    return x * lax.rsqrt(jnp.mean(x * x, axis=-1, keepdims=True) + EPS) * g


def _silu(x):
    return x * jax.nn.sigmoid(x)


def _gelu_tanh(x):
    c = math.sqrt(2.0 / math.pi)
    return x * (0.5 * (1.0 + jnp.tanh(c * (x + 0.044715 * (x * x * x)))))


def _seg_rms(x, seg, g):
    x2 = x * x
    hi = x2.astype(BF16)
    lo = (x2 - hi.astype(F32)).astype(BF16)
    ms = _dot(hi, seg) + _dot(lo, seg)
    return x * lax.rsqrt(ms + EPS) * g


def _rope(x, cos, sin_signed):
    w = x.shape[-1]
    lane = lax.broadcasted_iota(jnp.int32, x.shape, 1)
    first = (lane & ROT_FREQS) == 0
    partner = jnp.where(first, pltpu.roll(x, w - ROT_FREQS, 1), pltpu.roll(x, ROT_FREQS, 1))
    return x * cos + partner * sin_signed


def _rep_heads(kv):
    lane = lax.broadcasted_iota(jnp.int32, kv.shape, 1)
    swapped = pltpu.roll(kv, HEAD_DIM, 1)
    lo = lane < HEAD_DIM
    h0 = jnp.where(lo, kv, swapped)
    h1 = jnp.where(lo, swapped, kv)
    return jnp.concatenate([h0, h0, h1, h1], axis=1)


def _adaln_kernel(c_ref, w_ref, b_ref, o_ref):
    s = _silu(c_ref[...])
    o_ref[...] = _dot(s.astype(BF16), w_ref[...].astype(BF16)) + b_ref[...]


def _adaln(cvec, w_mod, b_mod):
    n_col = 6 * D_MODEL
    tn = n_col // 4
    return pl.pallas_call(
        _adaln_kernel,
        grid=(DEPTH, n_col // tn),
        in_specs=[pl.BlockSpec((N_MOD_ROWS, D_MODEL), lambda l, n: (0, 0)),
                  pl.BlockSpec((None, D_MODEL, tn), lambda l, n: (l, 0, n)),
                  pl.BlockSpec((None, 1, tn), lambda l, n: (l, 0, n))],
        out_specs=pl.BlockSpec((None, N_MOD_ROWS, tn), lambda l, n: (l, 0, n)),
        out_shape=jax.ShapeDtypeStruct((DEPTH, N_MOD_ROWS, n_col), F32),
        compiler_params=_params(2),
        name="adaln",
    )(cvec, w_mod, b_mod.reshape(DEPTH, 1, n_col))


def _inproj_kernel(*refs, latent):
    if latent:
        (x_ref, mod_ref, g1_ref, w_ref, seg_ref, qg_ref, kg_ref, cos_ref, sin_ref,
         qa_ref, krep_ref, vrep_ref, u_ref, qc_ref, kc_ref, vc_ref, g_ref) = refs
    else:
        (x_ref, mod_ref, g1_ref, w_ref, seg_ref, qg_ref, kg_ref,
         qa_ref, krep_ref, vrep_ref, ka_ref, va_ref, u_ref, qc_ref, kc_ref, vc_ref, g_ref) = refs
    mod = mod_ref[...]
    shift = mod[:, 0:D_MODEL]
    scale = mod[:, D_MODEL:2 * D_MODEL]
    h = _rms(x_ref[...], g1_ref[...]) * (1 + scale) + shift
    hb = h.astype(BF16)
    seg = seg_ref[...]
    q_scale = HEAD_DIM ** -0.5

    qa = _seg_rms(_dot(hb, w_ref[:, OFF_QA:OFF_KA]), seg, qg_ref[...])
    ka = _seg_rms(_dot(hb, w_ref[:, OFF_KA:OFF_VA]), seg[0:KV_WIDTH_A, 0:KV_WIDTH_A], kg_ref[...])
    va = _dot(hb, w_ref[:, OFF_VA:OFF_U])
    if latent:
        cos = cos_ref[...]
        sin = sin_ref[...]
        qa = _rope(qa, jnp.concatenate([cos] * REP_A, axis=1), jnp.concatenate([sin] * REP_A, axis=1))
        ka = _rope(ka, cos, sin)
    else:
        ka_ref[...] = ka
        va_ref[...] = va
    qa_ref[...] = (qa * q_scale).astype(BF16)
    krep_ref[...] = _rep_heads(ka).astype(BF16)
    vrep_ref[...] = _rep_heads(va).astype(BF16)

    u_ref[...] = _dot(hb, w_ref[:, OFF_U:OFF_QC])
    qc_ref[...] = (_dot(hb, w_ref[:, OFF_QC:OFF_KC]) * q_scale).astype(BF16)
    kc_ref[...] = _dot(hb, w_ref[:, OFF_KC:OFF_VC]).astype(kc_ref.dtype)
    vc_ref[...] = _dot(hb, w_ref[:, OFF_VC:OFF_G]).astype(vc_ref.dtype)
    for cidx in range(N_BRANCH):
        lo = OFF_G + cidx * D_MODEL
        g_ref[:, cidx * D_MODEL:(cidx + 1) * D_MODEL] = _dot(hb, w_ref[:, lo:lo + D_MODEL])


def _inproj(x, mod_l, g1, w_in, seg, qg, kg, rope_tabs, *, latent):
    n_tok = x.shape[0]
    n_steps = n_tok // ROW_TILE
    if latent:
        seq, nb = DEC_SEQ, DEC_BATCH
    else:
        seq, nb = SEQ, BATCH
    tps = seq // ROW_TILE
    tok = lambda i: (i, 0)
    tmaj = lambda i: (i % tps, i // tps)
    mod_idx = (lambda i: (1 + i // tps, 0, 0)) if latent else (lambda i: (0, 0, 0))
    in_specs = [pl.BlockSpec((ROW_TILE, D_MODEL), tok),
                pl.BlockSpec((None, 1, 6 * D_MODEL), mod_idx),
                _const_spec((1, D_MODEL)),
                _const_spec((D_MODEL, IN_WIDTH)),
                _const_spec((WIDTH_A, WIDTH_A)),
                _const_spec((1, WIDTH_A)),
                _const_spec((1, KV_WIDTH_A))]
    args = [x, mod_l, g1, w_in, seg, qg, kg]
    kv_dtype = BF16 if latent else F32
    out_shape = [jax.ShapeDtypeStruct((n_tok, WIDTH_A), BF16),
                 jax.ShapeDtypeStruct((n_tok, REP_A * KV_WIDTH_A), BF16),
                 jax.ShapeDtypeStruct((n_tok, REP_A * KV_WIDTH_A), BF16)]
    out_specs = [pl.BlockSpec((ROW_TILE, WIDTH_A), tok),
                 pl.BlockSpec((ROW_TILE, REP_A * KV_WIDTH_A), tok),
                 pl.BlockSpec((ROW_TILE, REP_A * KV_WIDTH_A), tok)]
    if latent:
        in_specs += [pl.BlockSpec((ROW_TILE, 2 * HEAD_DIM), lambda i: (i % tps, 0))] * 2
        args += list(rope_tabs)
    else:
        out_shape += [jax.ShapeDtypeStruct((n_tok, KV_WIDTH_A), F32)] * 2
        out_specs += [pl.BlockSpec((ROW_TILE, KV_WIDTH_A), tok)] * 2
    out_shape += [jax.ShapeDtypeStruct((seq, nb * SSM_WIDTH), F32),
                  jax.ShapeDtypeStruct((n_tok, WIDTH_C), BF16),
                  jax.ShapeDtypeStruct((n_tok, WIDTH_C), kv_dtype),
                  jax.ShapeDtypeStruct((n_tok, WIDTH_C), kv_dtype),
                  jax.ShapeDtypeStruct((n_tok, N_BRANCH * D_MODEL), F32)]
    out_specs += [pl.BlockSpec((ROW_TILE, SSM_WIDTH), tmaj),
                  pl.BlockSpec((ROW_TILE, WIDTH_C), tok),
                  pl.BlockSpec((ROW_TILE, WIDTH_C), tok),
                  pl.BlockSpec((ROW_TILE, WIDTH_C), tok),
                  pl.BlockSpec((ROW_TILE, N_BRANCH * D_MODEL), tok)]
    return pl.pallas_call(
        functools.partial(_inproj_kernel, latent=latent),
        grid=(n_steps,),
        in_specs=in_specs,
        out_specs=out_specs,
        out_shape=out_shape,
        compiler_params=_params(1),
        name="inproj_lat" if latent else "inproj_ctx",
    )(*args)


def _head_mask(shape, head):
    lane = lax.broadcasted_iota(jnp.int32, shape, 1)
    return (lane // HEAD_DIM) == head


def _softmax_pv(parts):
    m = None
    for s, _ in parts:
        pm = jnp.max(s, axis=-1, keepdims=True)
        m = pm if m is None else jnp.maximum(m, pm)
    l = None
    o = None
    for s, v in parts:
        p = jnp.exp(s - m)
        pl_ = jnp.sum(p, axis=-1, keepdims=True)
        po = _dot(p.astype(BF16), v)
        l = pl_ if l is None else l + pl_
        o = po if o is None else o + po
    return o / l


def _attn_a_kernel(*refs, has_cache):
    if has_cache:
        q_ref, kn_ref, vn_ref, ck_ref, cv_ref, o_ref, ckrep_ref, cvrep_ref = refs

        @pl.when(pl.program_id(1) == 0)
        def _():
            ckrep_ref[...] = _rep_heads(ck_ref[...]).astype(BF16)
            cvrep_ref[...] = _rep_heads(cv_ref[...]).astype(BF16)
    else:
        q_ref, kn_ref, vn_ref, o_ref = refs
    gw = REP_A * HEAD_DIM
    for g in range(N_KV_A):
        sl = slice(g * gw, (g + 1) * gw)
        qg = q_ref[:, sl]
        kg = kn_ref[:, sl]
        vg = vn_ref[:, sl]
        acc = jnp.zeros(qg.shape, F32)
        for j in range(REP_A):
            hm = _head_mask(qg.shape, j)
            qm = jnp.where(hm, qg, jnp.zeros_like(qg))
            parts = []
            if has_cache:
                parts.append((_dot_t(qm, ckrep_ref[:, sl]), cvrep_ref[:, sl]))
            parts.append((_dot_t(qm, kg), vg))
            acc = jnp.where(hm, _softmax_pv(parts), acc)
        o_ref[:, sl] = acc.astype(BF16)


def _attn_a_ctx(qa, krep, vrep):
    blk = lambda w: pl.BlockSpec((SEQ, w), lambda b: (b, 0))
    return pl.pallas_call(
        functools.partial(_attn_a_kernel, has_cache=False),
        grid=(BATCH,),
        in_specs=[blk(WIDTH_A), blk(WIDTH_A), blk(WIDTH_A)],
        out_specs=blk(WIDTH_A),
        out_shape=jax.ShapeDtypeStruct((BATCH * SEQ, WIDTH_A), BF16),
        compiler_params=_params(1),
        name="attn_a_ctx",
    )(qa, krep, vrep)


def _attn_a_lat(qa, krep, vrep, cache_k, cache_v):
    nq = DEC_SEQ // Q_TILE_A
    seq_blk = pl.BlockSpec((DEC_SEQ, WIDTH_A), lambda b, t: (b, 0))
    cache_blk = pl.BlockSpec((None, PAST_LEN, KV_WIDTH_A), lambda b, t: (b, 0, 0))
    q_blk = pl.BlockSpec((Q_TILE_A, WIDTH_A), lambda b, t: (b * nq + t, 0))
    return pl.pallas_call(
        functools.partial(_attn_a_kernel, has_cache=True),
        grid=(DEC_BATCH, nq),
        in_specs=[q_blk, seq_blk, seq_blk, cache_blk, cache_blk],
        out_specs=q_blk,
        out_shape=jax.ShapeDtypeStruct((DEC_BATCH * DEC_SEQ, WIDTH_A), BF16),
        scratch_shapes=[pltpu.VMEM((PAST_LEN, WIDTH_A), BF16), pltpu.VMEM((PAST_LEN, WIDTH_A), BF16)],
        compiler_params=_params(2),
        name="attn_a_lat",
    )(qa, krep, vrep, cache_k, cache_v)


def _attn_c_kernel(q_ref, k_ref, v_ref, o_ref):
    q = q_ref[...]
    k = k_ref[...].astype(BF16)
    v = v_ref[...].astype(BF16)
    acc = jnp.zeros(q.shape, F32)
    for h in range(N_HEADS_C):
        hm = _head_mask(q.shape, h)
        qm = jnp.where(hm, q, jnp.zeros_like(q))
        acc = jnp.where(hm, _softmax_pv([(_dot_t(qm, k), v)]), acc)
    o_ref[...] = acc.astype(BF16)


def _attn_c_ctx(qc, kc, vc):
    blk = pl.BlockSpec((SEQ, WIDTH_C), lambda b: (b, 0))
    return pl.pallas_call(
        _attn_c_kernel,
        grid=(BATCH,),
        in_specs=[blk, blk, blk],
        out_specs=blk,
        out_shape=jax.ShapeDtypeStruct((BATCH * SEQ, WIDTH_C), BF16),
        compiler_params=_params(1),
        name="attn_c_ctx",
    )(qc, kc, vc)


def _na_kernel(q_ref, k_ref, v_ref, ck_ref, cv_ref, brow_ref, o_ref, b2_ref):
    nq = NA_Q_ROWS * GRID_W
    nk = NA_K_ROWS * GRID_W
    n_pair = 2 * NA_WIN_R

    @pl.when((pl.program_id(0) == 0) & (pl.program_id(1) == 0))
    def _():
        shp = (GRID_W, 2 * GRID_W)
        c = lax.broadcasted_iota(jnp.int32, shp, 0)
        kc = lax.broadcasted_iota(jnp.int32, shp, 1) & (GRID_W - 1)
        c0 = jnp.clip(c - NA_WIN_C // 2, 0, GRID_W - NA_WIN_C)
        col_bias = jnp.where(kc < c0, NEG, jnp.where(kc >= c0 + NA_WIN_C, NEG, 0.0))
        for h in range(N_HEADS_C):
            for e in range(n_pair):
                row = jnp.broadcast_to(brow_ref[h, e:e + 1, :], shp)
                toep = pltpu.roll(row, 2 * GRID_W - (NA_WIN_C - 1), 1, stride=1, stride_axis=0)
                b2_ref[h, e] = jnp.where(col_bias < 0.0, NEG, toep)

    qrow0 = NA_Q_ROWS * pl.program_id(1)
    row0 = jnp.clip(qrow0 - NA_WIN_R // 2, 0, NA_R0_MAX)
    start = pl.multiple_of(row0 * GRID_W, GRID_W)
    kwin = k_ref[pl.ds(start, nk), :]
    vwin = v_ref[pl.ds(start, nk), :]
    ck = ck_ref[...].astype(BF16)
    cv = cv_ref[...].astype(BF16)

    qrow = qrow0 + lax.broadcasted_iota(jnp.int32, (nq, nk), 0) // GRID_W
    krow = row0 + lax.broadcasted_iota(jnp.int32, (nq, nk), 1) // GRID_W
    win0 = jnp.clip(qrow - NA_WIN_R // 2, 0, GRID_ROWS - NA_WIN_R)
    row_bias = jnp.where(krow < win0, NEG, jnp.where(krow >= win0 + NA_WIN_R, NEG, 0.0))

    q = q_ref[...]
    masks = [_head_mask(q.shape, h) for h in range(N_HEADS_C)]
    qs = jnp.concatenate([jnp.where(hm, q, jnp.zeros_like(q)) for hm in masks], axis=0)
    blocks = []
    for h in range(N_HEADS_C):
        rows = []
        for i in range(NA_Q_ROWS):
            tiles = []
            for m in range(NA_K_ROWS // 2):
                e = jnp.clip(row0 + 2 * m - (qrow0 + i) + NA_WIN_R, 0, n_pair - 1)
                tiles.append(b2_ref[h, e])
            rows.append(jnp.concatenate(tiles, axis=1))
        blocks.append(jnp.concatenate(rows, axis=0) + row_bias)
    bias = jnp.concatenate(blocks, axis=0)
    o_all = _softmax_pv([(_dot_t(qs, kwin) + bias, vwin), (_dot_t(qs, ck), cv)])
    acc = jnp.zeros(q.shape, F32)
    for h, hm in enumerate(masks):
        acc = jnp.where(hm, o_all[h * nq:(h + 1) * nq], acc)
    o_ref[...] = acc.astype(BF16)


def _na_bias_rows(tbl):
    pad = jnp.pad(tbl, ((0, 0), (1, 1), (0, GRID_W - tbl.shape[-1])))
    return jnp.concatenate([pad[:, :-1], pad[:, 1:]], axis=-1)


def _na_lat(qc, kc, vc, cache_k, cache_v, brow):
    nq = NA_Q_ROWS * GRID_W
    nblk = DEC_SEQ // nq
    seq_blk = pl.BlockSpec((DEC_SEQ, WIDTH_C), lambda b, j: (b, 0))
    cache_blk = pl.BlockSpec((None, PAST_LEN, WIDTH_C), lambda b, j: (b, 0, 0))
    q_blk = pl.BlockSpec((nq, WIDTH_C), lambda b, j: (b * nblk + j, 0))
    return pl.pallas_call(
        _na_kernel,
        grid=(DEC_BATCH, nblk),
        in_specs=[q_blk, seq_blk, seq_blk, cache_blk, cache_blk, _const_spec(brow.shape)],
        out_specs=q_blk,
        out_shape=jax.ShapeDtypeStruct((DEC_BATCH * DEC_SEQ, WIDTH_C), BF16),
        scratch_shapes=[pltpu.VMEM((N_HEADS_C, 2 * NA_WIN_R, GRID_W, 2 * GRID_W), F32)],
        compiler_params=_params(2),
        name="na_lat",
    )(qc, kc, vc, cache_k, cache_v, brow)


def _scan_kernel(uf_ref, ub_ref, bmat_ref, lam_ref, cmat_ref, h0_ref, yf_ref, yb_ref, hfin_ref,
                 hf_ref, hb_ref, st_ref, *, nb, lane_w):
    steps = SCAN_ROWS // nb

    @pl.when(pl.program_id(0) == 0)
    def _():
        st_ref[...] = h0_ref[...]

    hf_ref[...] = _dot(uf_ref[...].astype(BF16), bmat_ref[0])
    hb_ref[...] = _dot(ub_ref[...].astype(BF16), bmat_ref[1])

    for lb in range(SSM_LANES // lane_w):
        re = slice(lb * lane_w, (lb + 1) * lane_w)
        im = slice(SSM_LANES + lb * lane_w, SSM_LANES + (lb + 1) * lane_w)
        lam = [[jnp.broadcast_to(lam_ref[d, c, :, re], (nb, lane_w)) for c in range(2)] for d in range(2)]
        init = tuple(st_ref[d, c, :, re] for d in range(2) for c in range(2))

        def step(k, carry, re=re, im=im, lam=lam):
            fr, fi, br, bi = carry
            rf = pl.multiple_of(k * nb, nb)
            rb = pl.multiple_of((steps - 1 - k) * nb, nb)
            nfr = lam[0][0] * fr - lam[0][1] * fi + hf_ref[pl.ds(rf, nb), re]
            nfi = lam[0][0] * fi + lam[0][1] * fr + hf_ref[pl.ds(rf, nb), im]
            nbr = lam[1][0] * br - lam[1][1] * bi + hb_ref[pl.ds(rb, nb), re]
            nbi = lam[1][0] * bi + lam[1][1] * br + hb_ref[pl.ds(rb, nb), im]
            hf_ref[pl.ds(rf, nb), re] = nfr
            hf_ref[pl.ds(rf, nb), im] = nfi
            hb_ref[pl.ds(rb, nb), re] = nbr
            hb_ref[pl.ds(rb, nb), im] = nbi
            return nfr, nfi, nbr, nbi

        fin = lax.fori_loop(0, steps, step, init)
        for d in range(2):
            for c in range(2):
                st_ref[d, c, :, re] = fin[2 * d + c]

    yf_ref[...] = _dot(hf_ref[...].astype(BF16), cmat_ref[0])
    yb_ref[...] = _dot(hb_ref[...].astype(BF16), cmat_ref[1])
    hfin_ref[...] = st_ref[...]


def _scan_pair_kernel(uf_ref, ub_ref, bmat_ref, lam_ref, cmat_ref, h0_ref, yf_ref, yb_ref,
                      hf_ref, hb_ref, st_ref, *, lane_w):
    half = DEC_BATCH
    tiles = SCAN_ROWS // (2 * half)

    @pl.when(pl.program_id(0) == 0)
    def _():
        st_ref[...] = h0_ref[...]

    hf_ref[...] = _dot(uf_ref[...].astype(BF16), bmat_ref[0])
    hb_ref[...] = _dot(ub_ref[...].astype(BF16), bmat_ref[1])

    top = lax.broadcasted_iota(jnp.int32, (2 * half, lane_w), 0) < half
    swap = lambda a: pltpu.roll(a, half, 0)
    for lb in range(SSM_LANES // lane_w):
        re = slice(lb * lane_w, (lb + 1) * lane_w)
        im = slice(SSM_LANES + lb * lane_w, SSM_LANES + (lb + 1) * lane_w)
        la_r = lam_ref[0, :, re]
        la_i = lam_ref[1, :, re]
        lb_r = swap(la_r)
        lb_i = swap(la_i)

        def body(m, carry, re=re, im=im, la_r=la_r, la_i=la_i, lb_r=lb_r, lb_i=lb_i):
            sr, si = carry
            rf = pl.multiple_of(m * 2 * half, 2 * half)
            rb = pl.multiple_of((tiles - 1 - m) * 2 * half, 2 * half)
            fr = hf_ref[pl.ds(rf, 2 * half), re]
            fi = hf_ref[pl.ds(rf, 2 * half), im]
            br = hb_ref[pl.ds(rb, 2 * half), re]
            bi = hb_ref[pl.ds(rb, 2 * half), im]
            vr = la_r * sr - la_i * si + jnp.where(top, fr, br)
            vi = la_r * si + la_i * sr + jnp.where(top, fi, bi)
            tr = swap(vr)
            ti = swap(vi)
            wr = lb_r * tr - lb_i * ti + jnp.where(top, br, fr)
            wi = lb_r * ti + lb_i * tr + jnp.where(top, bi, fi)
            hf_ref[pl.ds(rf, 2 * half), re] = jnp.where(top, vr, wr)
            hf_ref[pl.ds(rf, 2 * half), im] = jnp.where(top, vi, wi)
            hb_ref[pl.ds(rb, 2 * half), re] = jnp.where(top, wr, vr)
            hb_ref[pl.ds(rb, 2 * half), im] = jnp.where(top, wi, vi)
            return swap(wr), swap(wi)

        fin = lax.fori_loop(0, tiles, body, (st_ref[0, :, re], st_ref[1, :, re]))
        st_ref[0, :, re] = fin[0]
        st_ref[1, :, re] = fin[1]

    yf_ref[...] = _dot(hf_ref[...].astype(BF16), cmat_ref[0])
    yb_ref[...] = _dot(hb_ref[...].astype(BF16), cmat_ref[1])


def _scan_specs(n_rows):
    n = n_rows // SCAN_ROWS
    fwd = pl.BlockSpec((SCAN_ROWS, SSM_WIDTH), lambda j: (j, 0))
    bwd = pl.BlockSpec((SCAN_ROWS, SSM_WIDTH), lambda j: (n - 1 - j, 0))
    y_shape = jax.ShapeDtypeStruct((n_rows, SSM_WIDTH), F32)
    buf = pltpu.VMEM((SCAN_ROWS, 2 * SSM_LANES), F32)
    return n, fwd, bwd, y_shape, buf


def _scan_ctx(u_rows, bmat, lam, cmat, h0):
    n, fwd, bwd, y_shape, buf = _scan_specs(u_rows.shape[0])
    st_shape = (2, 2, BATCH, SSM_LANES)
    return pl.pallas_call(
        functools.partial(_scan_kernel, nb=BATCH, lane_w=256),
        grid=(n,),
        in_specs=[fwd, bwd, _const_spec(bmat.shape), _const_spec(lam.shape), _const_spec(cmat.shape),
                  _const_spec(st_shape)],
        out_specs=[fwd, bwd, pl.BlockSpec(st_shape, lambda j: (0, 0, 0, 0))],
        out_shape=[y_shape, y_shape, jax.ShapeDtypeStruct(st_shape, F32)],
        scratch_shapes=[buf, buf, pltpu.VMEM(st_shape, F32)],
        compiler_params=_params(1),
        name="scan_ctx",
    )(u_rows, u_rows, bmat, lam, cmat, h0)


def _scan_lat(u_rows, bmat, lam_pair, cmat, h0_pair):
    n, fwd, bwd, y_shape, buf = _scan_specs(u_rows.shape[0])
    st_shape = (2, 2 * DEC_BATCH, SSM_LANES)
    return pl.pallas_call(
        functools.partial(_scan_pair_kernel, lane_w=512),
        grid=(n,),
        in_specs=[fwd, bwd, _const_spec(bmat.shape), _const_spec(st_shape), _const_spec(cmat.shape),
                  _const_spec(st_shape)],
        out_specs=[fwd, bwd],
        out_shape=[y_shape, y_shape],
        scratch_shapes=[buf, buf, pltpu.VMEM(st_shape, F32)],
        compiler_params=_params(1),
        name="scan_lat",
    )(u_rows, u_rows, bmat, lam_pair, cmat, h0_pair)


def _ssm_discretise(lam_re, lam_im, log_step, b_re, b_im, c_re, c_im):
    step = jnp.exp(log_step.astype(F32))[..., None]
    lr, li = lam_re.astype(F32), lam_im.astype(F32)
    mag = jnp.exp(lr * step)
    bar_r = mag * jnp.cos(li * step)
    bar_i = mag * jnp.sin(li * step)
    den = lr * lr + li * li
    coef_r = (((bar_r - 1) * lr + bar_i * li) / den)[..., None]
    coef_i = ((bar_i * lr - (bar_r - 1) * li) / den)[..., None]
    br, bi = b_re.astype(F32), b_im.astype(F32)
    bbar_r = coef_r * br - coef_i * bi
    bbar_i = coef_r * bi + coef_i * br
    eye = jnp.eye(SSM_GROUPS, dtype=F32)
    blk_b = lambda a: jnp.einsum('dgpc,gh->dgchp', a, eye).reshape(2, SSM_WIDTH, SSM_LANES)
    bmat = jnp.concatenate([blk_b(bbar_r), blk_b(bbar_i)], axis=-1)
    blk_c = lambda a: jnp.einsum('dgcp,gh->dgphc', a, eye).reshape(2, SSM_LANES, SSM_WIDTH)
    cmat = jnp.concatenate([blk_c(c_re.astype(F32)), -blk_c(c_im.astype(F32))], axis=1)
    lam_flat = jnp.stack([bar_r, bar_i], axis=1).reshape(2, 2, 1, SSM_LANES)
    return bmat.astype(BF16), lam_flat, cmat.astype(BF16)


def _merge_kernel(x_ref, oa_ref, yf_ref, yb_ref, u_ref, oc_ref, g_ref, mod_ref, d_ref, wglu_ref,
                  wa_ref, wb_ref, wc_ref, wo_ref, g2_ref, wgu_ref, wd_ref, fg_ref, o_ref, *, final):
    y = _gelu_tanh(yf_ref[...] + yb_ref[...] + d_ref[...] * u_ref[...])
    ob = y * jax.nn.sigmoid(_dot(y.astype(BF16), wglu_ref[...]))
    gate = lambda i: jax.nn.sigmoid(g_ref[:, i * D_MODEL:(i + 1) * D_MODEL])
    merged = (gate(0) * _dot(oa_ref[...], wa_ref[...])
              + gate(1) * _dot(ob.astype(BF16), wb_ref[...])
              + gate(2) * _dot(oc_ref[...], wc_ref[...]))
    mod = mod_ref[...]
    gate1 = mod[:, 2 * D_MODEL:3 * D_MODEL]
    shift2 = mod[:, 3 * D_MODEL:4 * D_MODEL]
    scale2 = mod[:, 4 * D_MODEL:5 * D_MODEL]
    gate2 = mod[:, 5 * D_MODEL:6 * D_MODEL]
    x1 = x_ref[...] + gate1 * _dot(merged.astype(BF16), wo_ref[...])
    h2 = _rms(x1, g2_ref[...]) * (1 + scale2) + shift2
    gu = _dot(h2.astype(BF16), wgu_ref[...])
    act = _silu(gu[:, :D_FF]) * gu[:, D_FF:]
    x2 = x1 + gate2 * _dot(act.astype(BF16), wd_ref[...])
    if final:
        x2 = _rms(x2, fg_ref[...])
    o_ref[...] = x2


def _merge(x, oa, yf, yb, u_tm, oc, g, mod_l, lw, final_g, *, latent, final):
    n_tok = x.shape[0]
    seq = DEC_SEQ if latent else SEQ
    tps = seq // ROW_TILE
    tok = lambda w: pl.BlockSpec((ROW_TILE, w), lambda i: (i, 0))
    tmaj = pl.BlockSpec((ROW_TILE, SSM_WIDTH), lambda i: (i % tps, i // tps))
    mod_idx = (lambda i: (1 + i // tps, 0, 0)) if latent else (lambda i: (0, 0, 0))
    weights = [lw["ssm_d"], lw["w_glu"], lw["w_br_a"], lw["w_br_b"], lw["w_br_c"], lw["w_out"],
               lw["norm2_g"], lw["w_ffn_gu"], lw["w_ffn_d"], final_g]
    return pl.pallas_call(
        functools.partial(_merge_kernel, final=final),
        grid=(n_tok // ROW_TILE,),
        in_specs=[tok(D_MODEL), tok(WIDTH_A), tmaj, tmaj, tmaj, tok(WIDTH_C), tok(N_BRANCH * D_MODEL),
                  pl.BlockSpec((None, 1, 6 * D_MODEL), mod_idx)] + [_const_spec(w.shape) for w in weights],
        out_specs=tok(D_MODEL),
        out_shape=jax.ShapeDtypeStruct((n_tok, D_MODEL), F32),
        compiler_params=_params(1),
        name=("merge_lat" if latent else "merge_ctx") + ("_final" if final else ""),
    )(x, oa, yf, yb, u_tm, oc, g, mod_l, *weights)


def _rope_tables():
    t = jnp.arange(DEC_SEQ)
    row = (t // GRID_W).astype(F32)
    col = (t % GRID_W).astype(F32)
    inv = 1.0 / (ROPE_THETA ** (jnp.arange(ROT_FREQS, dtype=F32) / ROT_FREQS))
    ar = row[:, None] * inv[None]
    ac = col[:, None] * inv[None]
    cos = jnp.concatenate([jnp.cos(ar), jnp.cos(ar), jnp.cos(ac), jnp.cos(ac)], axis=-1)
    sin = jnp.concatenate([-jnp.sin(ar), jnp.sin(ar), -jnp.sin(ac), jnp.sin(ac)], axis=-1)
    return jnp.tile(cos, (1, 2)), jnp.tile(sin, (1, 2))


def kernel(x_prompt, x_sample, c, cache_ga_k, cache_ga_v, cache_na_k, cache_na_v, state_ssm, c_ctx, w_mod, b_mod, norm1_g, w_in, qn_g, kn_g, ssm_lam_re, ssm_lam_im, ssm_log_step, ssm_b_re, ssm_b_im, ssm_c_re, ssm_c_im, ssm_d, ssm_w_glu, na_bias, w_br_a, w_br_b, w_br_c, w_out, norm2_g, w_ffn_gu, w_ffn_d, final_g):
    cvec = jnp.concatenate([c_ctx[None, :], c, jnp.zeros((N_MOD_ROWS - 1 - DEC_BATCH, D_MODEL), F32)], axis=0)
    mod = _adaln(cvec, w_mod, b_mod).reshape(DEPTH, N_MOD_ROWS, 1, 6 * D_MODEL)

    seg = jnp.kron(jnp.eye(N_HEADS_A, dtype=F32), jnp.full((HEAD_DIM, HEAD_DIM), 1.0 / HEAD_DIM, F32)).astype(BF16)
    rope_tabs = _rope_tables()
    fg = final_g.reshape(1, D_MODEL)

    xp = x_prompt.reshape(BATCH * SEQ, D_MODEL)
    xs = x_sample.reshape(DEC_BATCH * DEC_SEQ, D_MODEL)
    ga_k, ga_v, na_k, na_v, ssm_st = [], [], [], [], []
    for l in range(DEPTH):
        lw = dict(ssm_d=ssm_d[l].reshape(1, SSM_WIDTH), w_glu=ssm_w_glu[l].astype(BF16),
                  w_br_a=w_br_a[l].astype(BF16), w_br_b=w_br_b[l].astype(BF16), w_br_c=w_br_c[l].astype(BF16),
                  w_out=w_out[l].astype(BF16), norm2_g=norm2_g[l].reshape(1, D_MODEL),
                  w_ffn_gu=w_ffn_gu[l].astype(BF16), w_ffn_d=w_ffn_d[l].astype(BF16))
        w_in_l = w_in[l].astype(BF16)
        g1 = norm1_g[l].reshape(1, D_MODEL)
        qg = jnp.tile(qn_g[l], N_HEADS_A).reshape(1, WIDTH_A)
        kg = jnp.tile(kn_g[l], N_KV_A).reshape(1, KV_WIDTH_A)
        bmat, lam, cmat = _ssm_discretise(ssm_lam_re[l], ssm_lam_im[l], ssm_log_step[l], ssm_b_re[l], ssm_b_im[l],
                                          ssm_c_re[l], ssm_c_im[l])
        final = l == DEPTH - 1

        qa, krep, vrep, ka, va, u_tm, qc, kc, vc, g = _inproj(xp, mod[l], g1, w_in_l, seg, qg, kg, None, latent=False)
        oa = _attn_a_ctx(qa, krep, vrep)
        oc = _attn_c_ctx(qc, kc, vc)
        zero_state = jnp.zeros((2, 2, BATCH, SSM_LANES), F32)
        yf, yb, hfin = _scan_ctx(u_tm.reshape(SEQ * BATCH, SSM_WIDTH), bmat, lam, cmat, zero_state)
        xp = _merge(xp, oa, yf.reshape(SEQ, BATCH * SSM_WIDTH), yb.reshape(SEQ, BATCH * SSM_WIDTH), u_tm, oc, g,
                    mod[l], lw, fg, latent=False, final=final)
        ga_k.append(ka.reshape(BATCH, SEQ, N_KV_A, HEAD_DIM))
        ga_v.append(va.reshape(BATCH, SEQ, N_KV_A, HEAD_DIM))
        na_k.append(kc.reshape(BATCH, SEQ, N_HEADS_C, HEAD_DIM))
        na_v.append(vc.reshape(BATCH, SEQ, N_HEADS_C, HEAD_DIM))
        ssm_st.append(jnp.transpose(hfin, (2, 0, 1, 3)).reshape(BATCH, 2, 2, SSM_GROUPS, SSM_STATE))

        qa, krep, vrep, u_tm, qc, kc, vc, g = _inproj(xs, mod[l], g1, w_in_l, seg, qg, kg, rope_tabs, latent=True)
        oa = _attn_a_lat(qa, krep, vrep,
                         cache_ga_k[:, l].reshape(DEC_BATCH, PAST_LEN, KV_WIDTH_A),
                         cache_ga_v[:, l].reshape(DEC_BATCH, PAST_LEN, KV_WIDTH_A))
        oc = _na_lat(qc, kc, vc,
                     cache_na_k[:, l].reshape(DEC_BATCH, PAST_LEN, WIDTH_C),
                     cache_na_v[:, l].reshape(DEC_BATCH, PAST_LEN, WIDTH_C),
                     _na_bias_rows(na_bias[l]))
        h0 = jnp.transpose(state_ssm[:, l].reshape(DEC_BATCH, 2, 2, SSM_LANES), (2, 1, 0, 3))
        h0 = h0.reshape(2, 2 * DEC_BATCH, SSM_LANES)
        lam_pair = jnp.broadcast_to(jnp.transpose(lam, (1, 0, 2, 3)), (2, 2, DEC_BATCH, SSM_LANES))
        lam_pair = lam_pair.reshape(2, 2 * DEC_BATCH, SSM_LANES)
        yf, yb = _scan_lat(u_tm.reshape(DEC_SEQ * DEC_BATCH, SSM_WIDTH), bmat, lam_pair, cmat, h0)
        xs = _merge(xs, oa, yf.reshape(DEC_SEQ, DEC_BATCH * SSM_WIDTH), yb.reshape(DEC_SEQ, DEC_BATCH * SSM_WIDTH),
                    u_tm, oc, g, mod[l], lw, fg, latent=True, final=final)

    y_prompt = xp.reshape(BATCH, SEQ, D_MODEL)
    y_sample = xs.reshape(DEC_BATCH, DEC_SEQ, D_MODEL)
    return (y_prompt, y_sample, jnp.stack(ga_k, axis=1), jnp.stack(ga_v, axis=1),
            jnp.stack(na_k, axis=1), jnp.stack(na_v, axis=1), jnp.stack(ssm_st, axis=1))
```

```python
import functools
import math

import jax
import jax.numpy as jnp
from jax import lax
from jax.experimental import pallas as pl
from jax.experimental.pallas import tpu as pltpu

D_MODEL = 1024
BATCH = 16
SEQ = 256
DEPTH = 2
DEC_BATCH = 4
DEC_SEQ = 2048
PAST_LEN = 256
GRID_W = 64
GRID_ROWS = DEC_SEQ // GRID_W
HEAD_DIM = 64
N_HEADS_A = 8
N_KV_A = 2
REP_A = N_HEADS_A // N_KV_A
N_HEADS_C = 4
SSM_WIDTH = 256
SSM_GROUP = 16
SSM_GROUPS = SSM_WIDTH // SSM_GROUP
SSM_STATE = 64
SSM_LANES = SSM_GROUPS * SSM_STATE
NA_WIN_R = 8
NA_WIN_C = 16
D_FF = -(-8 * D_MODEL // (3 * 256)) * 256
ROPE_THETA = 10000.0
ROT_HALF = HEAD_DIM // 2
ROT_FREQS = ROT_HALF // 2
WIDTH_A = N_HEADS_A * HEAD_DIM
KV_WIDTH_A = N_KV_A * HEAD_DIM
WIDTH_C = N_HEADS_C * HEAD_DIM
N_BRANCH = 3
IN_WIDTH = WIDTH_A + 2 * KV_WIDTH_A + SSM_WIDTH + 3 * WIDTH_C + N_BRANCH * D_MODEL
EPS = 1e-6

OFF_QA = 0
OFF_KA = OFF_QA + WIDTH_A
OFF_VA = OFF_KA + KV_WIDTH_A
OFF_U = OFF_VA + KV_WIDTH_A
OFF_QC = OFF_U + SSM_WIDTH
OFF_KC = OFF_QC + WIDTH_C
OFF_VC = OFF_KC + WIDTH_C
OFF_G = OFF_VC + WIDTH_C

N_MOD_ROWS = 8
ROW_TILE = 256
Q_TILE_A = 256
NA_Q_ROWS = 2
NA_K_ROWS = 10
NA_R0_MAX = GRID_ROWS - NA_K_ROWS
SCAN_ROWS = 1024
LANES = 128
NEG = -1e30
VMEM_LIMIT_V7X = 56 * 1024 * 1024

F32 = jnp.float32
BF16 = jnp.bfloat16


def _dot(a, b):
    return jnp.dot(a, b, preferred_element_type=F32)


def _dot_t(a, b):
    return lax.dot_general(a, b, (((1,), (1,)), ((), ())), preferred_element_type=F32)


def _params(n_axes):
    return pltpu.CompilerParams(dimension_semantics=("arbitrary",) * n_axes,
                                vmem_limit_bytes=VMEM_LIMIT_V7X)


def _const_spec(shape):
    nd = len(shape)
    return pl.BlockSpec(shape, lambda *_: (0,) * nd, pipeline_mode=pl.Buffered(1))


def _rms(x, g):
    return x * lax.rsqrt(jnp.mean(x * x, axis=-1, keepdims=True) + EPS) * g


def _silu(x):
    return x * jax.nn.sigmoid(x)


def _gelu_tanh(x):
    c = math.sqrt(2.0 / math.pi)
    return x * (0.5 * (1.0 + jnp.tanh(c * (x + 0.044715 * (x * x * x)))))


def _seg_rms(x, seg, g):
    x2 = x * x
    hi = x2.astype(BF16)
    lo = (x2 - hi.astype(F32)).astype(BF16)
    ms = _dot(hi, seg) + _dot(lo, seg)
    return x * lax.rsqrt(ms + EPS) * g


def _rope(x, cos, sin_signed):
    w = x.shape[-1]
    lane = lax.broadcasted_iota(jnp.int32, x.shape, 1)
    first = (lane & ROT_FREQS) == 0
    partner = jnp.where(first, pltpu.roll(x, w - ROT_FREQS, 1), pltpu.roll(x, ROT_FREQS, 1))
    return x * cos + partner * sin_signed


def _rep_heads(kv):
    lane = lax.broadcasted_iota(jnp.int32, kv.shape, 1)
    swapped = pltpu.roll(kv, HEAD_DIM, 1)
    lo = lane < HEAD_DIM
    h0 = jnp.where(lo, kv, swapped)
    h1 = jnp.where(lo, swapped, kv)
    return jnp.concatenate([h0, h0, h1, h1], axis=1)


def _to_time_major(val, slabs, nb):
    tt = val.shape[0] // nb
    for s, slab in enumerate(slabs):
        for b in range(nb):
            slab[pl.ds(b, tt, stride=nb), :] = val[b * tt:(b + 1) * tt, s * LANES:(s + 1) * LANES]
    return jnp.concatenate([slab[...] for slab in slabs], axis=1)


def _to_batch_major(val, slabs, nb):
    tt = val.shape[0] // nb
    for s, slab in enumerate(slabs):
        slab[...] = val[:, s * LANES:(s + 1) * LANES]
    return jnp.concatenate(
        [jnp.concatenate([slab[pl.ds(b, tt, stride=nb), :] for slab in slabs], axis=1) for b in range(nb)], axis=0)


def _adaln_kernel(c_ref, w_ref, b_ref, o_ref):
    s = _silu(c_ref[...])
    o_ref[...] = _dot(s.astype(BF16), w_ref[...].astype(BF16)) + b_ref[...]


def _adaln(cvec, w_mod, b_mod):
    n_col = 6 * D_MODEL
    tn = n_col // 4
    return pl.pallas_call(
        _adaln_kernel,
        grid=(DEPTH, n_col // tn),
        in_specs=[pl.BlockSpec((N_MOD_ROWS, D_MODEL), lambda l, n: (0, 0)),
                  pl.BlockSpec((None, D_MODEL, tn), lambda l, n: (l, 0, n)),
                  pl.BlockSpec((None, 1, tn), lambda l, n: (l, 0, n))],
        out_specs=pl.BlockSpec((None, N_MOD_ROWS, tn), lambda l, n: (l, 0, n)),
        out_shape=jax.ShapeDtypeStruct((DEPTH, N_MOD_ROWS, n_col), F32),
        compiler_params=_params(2),
        name="adaln",
    )(cvec, w_mod, b_mod.reshape(DEPTH, 1, n_col))


def _inproj_kernel(*refs, latent):
    if latent:
        (x_ref, mod_ref, g1_ref, w_ref, seg_ref, qg_ref, kg_ref, cos_ref, sin_ref,
         qa_ref, krep_ref, vrep_ref, u_ref, qc_ref, kc_ref, vc_ref, g_ref, slab0, slab1) = refs
    else:
        (x_ref, mod_ref, g1_ref, w_ref, seg_ref, qg_ref, kg_ref,
         qa_ref, krep_ref, vrep_ref, ka_ref, va_ref, u_ref, qc_ref, kc_ref, vc_ref, g_ref, slab0, slab1) = refs
    nb, tt, _ = x_ref.shape
    rows = nb * tt
    mod = mod_ref[...]
    shift = mod[:, :, 0:D_MODEL]
    scale = mod[:, :, D_MODEL:2 * D_MODEL]
    h = _rms(x_ref[...], g1_ref[...]) * (1 + scale) + shift
    hb = h.reshape(rows, D_MODEL).astype(BF16)
    seg = seg_ref[...]
    q_scale = HEAD_DIM ** -0.5

    def put(ref, val):
        ref[...] = val.reshape(ref.shape).astype(ref.dtype)

    qa = _seg_rms(_dot(hb, w_ref[:, OFF_QA:OFF_KA]), seg, qg_ref[...])
    ka = _seg_rms(_dot(hb, w_ref[:, OFF_KA:OFF_VA]), seg[0:KV_WIDTH_A, 0:KV_WIDTH_A], kg_ref[...])
    va = _dot(hb, w_ref[:, OFF_VA:OFF_U])
    if latent:
        cos = jnp.concatenate([cos_ref[...]] * nb, axis=0)
        sin = jnp.concatenate([sin_ref[...]] * nb, axis=0)
        qa = _rope(qa, jnp.concatenate([cos] * REP_A, axis=1), jnp.concatenate([sin] * REP_A, axis=1))
        ka = _rope(ka, cos, sin)
    else:
        put(ka_ref, ka)
        put(va_ref, va)
    put(qa_ref, qa * q_scale)
    put(krep_ref, _rep_heads(ka))
    put(vrep_ref, _rep_heads(va))

    u_ref[...] = _to_time_major(_dot(hb, w_ref[:, OFF_U:OFF_QC]), (slab0, slab1), nb)
    put(qc_ref, _dot(hb, w_ref[:, OFF_QC:OFF_KC]) * q_scale)
    put(kc_ref, _dot(hb, w_ref[:, OFF_KC:OFF_VC]))
    put(vc_ref, _dot(hb, w_ref[:, OFF_VC:OFF_G]))
    for cidx in range(N_BRANCH):
        lo = OFF_G + cidx * D_MODEL
        g_ref[:, :, cidx * D_MODEL:(cidx + 1) * D_MODEL] = _dot(hb, w_ref[:, lo:lo + D_MODEL]).reshape(
            nb, tt, D_MODEL)


def _inproj(x, mod_rows, g1, w_in, seg, qg, kg, rope_tabs, *, latent):
    nb, seq, _ = x.shape
    tt = ROW_TILE // nb
    slab = lambda w: pl.BlockSpec((nb, tt, w), lambda j: (0, j, 0))
    act = lambda w, dt: jax.ShapeDtypeStruct((nb, seq, w), dt)
    in_specs = [slab(D_MODEL), _const_spec(mod_rows.shape), _const_spec((1, D_MODEL)),
                _const_spec((D_MODEL, IN_WIDTH)), _const_spec((WIDTH_A, WIDTH_A)),
                _const_spec((1, WIDTH_A)), _const_spec((1, KV_WIDTH_A))]
    args = [x, mod_rows, g1, w_in, seg, qg, kg]
    kv_dtype = BF16 if latent else F32
    out_shape = [act(WIDTH_A, BF16), act(REP_A * KV_WIDTH_A, BF16), act(REP_A * KV_WIDTH_A, BF16)]
    out_specs = [slab(WIDTH_A), slab(REP_A * KV_WIDTH_A), slab(REP_A * KV_WIDTH_A)]
    if latent:
        in_specs += [pl.BlockSpec((tt, 2 * HEAD_DIM), lambda j: (j, 0))] * 2
        args += list(rope_tabs)
    else:
        out_shape += [act(KV_WIDTH_A, F32)] * 2
        out_specs += [slab(KV_WIDTH_A)] * 2
    out_shape += [jax.ShapeDtypeStruct((seq * nb, SSM_WIDTH), F32),
                  act(WIDTH_C, BF16), act(WIDTH_C, kv_dtype), act(WIDTH_C, kv_dtype),
                  act(N_BRANCH * D_MODEL, F32)]
    out_specs += [pl.BlockSpec((ROW_TILE, SSM_WIDTH), lambda j: (j, 0)),
                  slab(WIDTH_C), slab(WIDTH_C), slab(WIDTH_C), slab(N_BRANCH * D_MODEL)]
    return pl.pallas_call(
        functools.partial(_inproj_kernel, latent=latent),
        grid=(seq // tt,),
        in_specs=in_specs,
        out_specs=out_specs,
        out_shape=out_shape,
        scratch_shapes=[pltpu.VMEM((ROW_TILE, LANES), F32)] * 2,
        compiler_params=_params(1),
        name="inproj_lat" if latent else "inproj_ctx",
    )(*args)


def _head_mask(shape, head):
    lane = lax.broadcasted_iota(jnp.int32, shape, 1)
    return (lane // HEAD_DIM) == head


def _softmax_pv(parts):
    m = None
    for s, _ in parts:
        pm = jnp.max(s, axis=-1, keepdims=True)
        m = pm if m is None else jnp.maximum(m, pm)
    l = None
    o = None
    for s, v in parts:
        p = jnp.exp(s - m)
        pl_ = jnp.sum(p, axis=-1, keepdims=True)
        po = _dot(p.astype(BF16), v)
        l = pl_ if l is None else l + pl_
        o = po if o is None else o + po
    return o / l


def _stacked_heads(q, n_heads, attend):
    m_rows = q.shape[0]
    masks = [_head_mask(q.shape, h) for h in range(n_heads)]
    qs = jnp.concatenate([jnp.where(hm, q, jnp.zeros_like(q)) for hm in masks], axis=0)
    o_all = attend(qs)
    acc = jnp.zeros(q.shape, F32)
    for h, hm in enumerate(masks):
        acc = jnp.where(hm, o_all[h * m_rows:(h + 1) * m_rows], acc)
    return acc


def _attn_a_kernel(*refs, has_cache):
    if has_cache:
        q_ref, kn_ref, vn_ref, ck_ref, cv_ref, o_ref, ckrep_ref, cvrep_ref = refs

        @pl.when(pl.program_id(1) == 0)
        def _():
            ckrep_ref[...] = _rep_heads(ck_ref[...]).astype(BF16)
            cvrep_ref[...] = _rep_heads(cv_ref[...]).astype(BF16)
    else:
        q_ref, kn_ref, vn_ref, o_ref = refs
    gw = REP_A * HEAD_DIM
    for g in range(N_KV_A):
        sl = slice(g * gw, (g + 1) * gw)

        def attend(qs, sl=sl):
            parts = [(_dot_t(qs, kn_ref[:, sl]), vn_ref[:, sl])]
            if has_cache:
                parts.append((_dot_t(qs, ckrep_ref[:, sl]), cvrep_ref[:, sl]))
            return _softmax_pv(parts)

        o_ref[:, sl] = _stacked_heads(q_ref[:, sl], REP_A, attend).astype(BF16)


def _attn_a_ctx(qa, krep, vrep):
    blk = pl.BlockSpec((None, SEQ, WIDTH_A), lambda b: (b, 0, 0))
    return pl.pallas_call(
        functools.partial(_attn_a_kernel, has_cache=False),
        grid=(BATCH,),
        in_specs=[blk, blk, blk],
        out_specs=blk,
        out_shape=jax.ShapeDtypeStruct((BATCH, SEQ, WIDTH_A), BF16),
        compiler_params=_params(1),
        name="attn_a_ctx",
    )(qa, krep, vrep)


def _attn_a_lat(qa, krep, vrep, cache_k, cache_v, layer):
    seq_blk = pl.BlockSpec((None, DEC_SEQ, WIDTH_A), lambda b, t: (b, 0, 0))
    cache_blk = pl.BlockSpec((None, None, PAST_LEN, KV_WIDTH_A), lambda b, t: (b, layer, 0, 0))
    q_blk = pl.BlockSpec((None, Q_TILE_A, WIDTH_A), lambda b, t: (b, t, 0))
    return pl.pallas_call(
        functools.partial(_attn_a_kernel, has_cache=True),
        grid=(DEC_BATCH, DEC_SEQ // Q_TILE_A),
        in_specs=[q_blk, seq_blk, seq_blk, cache_blk, cache_blk],
        out_specs=q_blk,
        out_shape=jax.ShapeDtypeStruct((DEC_BATCH, DEC_SEQ, WIDTH_A), BF16),
        scratch_shapes=[pltpu.VMEM((PAST_LEN, WIDTH_A), BF16), pltpu.VMEM((PAST_LEN, WIDTH_A), BF16)],
        compiler_params=_params(2),
        name="attn_a_lat",
    )(qa, krep, vrep, cache_k, cache_v)


def _attn_c_kernel(q_ref, k_ref, v_ref, o_ref):
    k = k_ref[...].astype(BF16)
    v = v_ref[...].astype(BF16)
    attend = lambda qs: _softmax_pv([(_dot_t(qs, k), v)])
    o_ref[...] = _stacked_heads(q_ref[...], N_HEADS_C, attend).astype(BF16)


def _attn_c_ctx(qc, kc, vc):
    blk = pl.BlockSpec((None, SEQ, WIDTH_C), lambda b: (b, 0, 0))
    return pl.pallas_call(
        _attn_c_kernel,
        grid=(BATCH,),
        in_specs=[blk, blk, blk],
        out_specs=blk,
        out_shape=jax.ShapeDtypeStruct((BATCH, SEQ, WIDTH_C), BF16),
        compiler_params=_params(1),
        name="attn_c_ctx",
    )(qc, kc, vc)


def _na_kernel(q_ref, k_ref, v_ref, ck_ref, cv_ref, brow_ref, o_ref, b2_ref):
    nq = NA_Q_ROWS * GRID_W
    nk = NA_K_ROWS * GRID_W
    n_pair = 2 * NA_WIN_R

    @pl.when((pl.program_id(0) == 0) & (pl.program_id(1) == 0))
    def _():
        shp = (GRID_W, 2 * GRID_W)
        c = lax.broadcasted_iota(jnp.int32, shp, 0)
        kc = lax.broadcasted_iota(jnp.int32, shp, 1) & (GRID_W - 1)
        c0 = jnp.clip(c - NA_WIN_C // 2, 0, GRID_W - NA_WIN_C)
        col_bias = jnp.where(kc < c0, NEG, jnp.where(kc >= c0 + NA_WIN_C, NEG, 0.0))
        for h in range(N_HEADS_C):
            for e in range(n_pair):
                row = jnp.broadcast_to(brow_ref[h, e:e + 1, :], shp)
                toep = pltpu.roll(row, 2 * GRID_W - (NA_WIN_C - 1), 1, stride=1, stride_axis=0)
                b2_ref[h, e] = jnp.where(col_bias < 0.0, NEG, toep)

    qrow0 = NA_Q_ROWS * pl.program_id(1)
    row0 = jnp.clip(qrow0 - NA_WIN_R // 2, 0, NA_R0_MAX)
    start = pl.multiple_of(row0 * GRID_W, GRID_W)
    kwin = k_ref[pl.ds(start, nk), :]
    vwin = v_ref[pl.ds(start, nk), :]
    ck = ck_ref[...].astype(BF16)
    cv = cv_ref[...].astype(BF16)

    qrow = qrow0 + lax.broadcasted_iota(jnp.int32, (nq, nk), 0) // GRID_W
    krow = row0 + lax.broadcasted_iota(jnp.int32, (nq, nk), 1) // GRID_W
    win0 = jnp.clip(qrow - NA_WIN_R // 2, 0, GRID_ROWS - NA_WIN_R)
    row_bias = jnp.where(krow < win0, NEG, jnp.where(krow >= win0 + NA_WIN_R, NEG, 0.0))

    blocks = []
    for h in range(N_HEADS_C):
        rows = []
        for i in range(NA_Q_ROWS):
            tiles = []
            for m in range(NA_K_ROWS // 2):
                e = jnp.clip(row0 + 2 * m - (qrow0 + i) + NA_WIN_R, 0, n_pair - 1)
                tiles.append(b2_ref[h, e])
            rows.append(jnp.concatenate(tiles, axis=1))
        blocks.append(jnp.concatenate(rows, axis=0) + row_bias)
    bias = jnp.concatenate(blocks, axis=0)
    attend = lambda qs: _softmax_pv([(_dot_t(qs, kwin) + bias, vwin), (_dot_t(qs, ck), cv)])
    o_ref[...] = _stacked_heads(q_ref[...], N_HEADS_C, attend).astype(BF16)


def _na_bias_rows(tbl):
    pad = jnp.pad(tbl, ((0, 0), (1, 1), (0, GRID_W - tbl.shape[-1])))
    return jnp.concatenate([pad[:, :-1], pad[:, 1:]], axis=-1)


def _na_lat(qc, kc, vc, cache_k, cache_v, brow, layer):
    nq = NA_Q_ROWS * GRID_W
    seq_blk = pl.BlockSpec((None, DEC_SEQ, WIDTH_C), lambda b, j: (b, 0, 0))
    cache_blk = pl.BlockSpec((None, None, PAST_LEN, WIDTH_C), lambda b, j: (b, layer, 0, 0))
    q_blk = pl.BlockSpec((None, nq, WIDTH_C), lambda b, j: (b, j, 0))
    return pl.pallas_call(
        _na_kernel,
        grid=(DEC_BATCH, DEC_SEQ // nq),
        in_specs=[q_blk, seq_blk, seq_blk, cache_blk, cache_blk, _const_spec(brow.shape)],
        out_specs=q_blk,
        out_shape=jax.ShapeDtypeStruct((DEC_BATCH, DEC_SEQ, WIDTH_C), BF16),
        scratch_shapes=[pltpu.VMEM((N_HEADS_C, 2 * NA_WIN_R, GRID_W, 2 * GRID_W), F32)],
        compiler_params=_params(2),
        name="na_lat",
    )(qc, kc, vc, cache_k, cache_v, brow)


def _scan_kernel(uf_ref, ub_ref, bmat_ref, lam_ref, cmat_ref, h0_ref, yf_ref, yb_ref, hfin_ref,
                 hf_ref, hb_ref, st_ref, *, nb, lane_w):
    steps = SCAN_ROWS // nb

    @pl.when(pl.program_id(0) == 0)
    def _():
        st_ref[...] = h0_ref[...]

    hf_ref[...] = _dot(uf_ref[...].astype(BF16), bmat_ref[0])
    hb_ref[...] = _dot(ub_ref[...].astype(BF16), bmat_ref[1])

    for lb in range(SSM_LANES // lane_w):
        re = slice(lb * lane_w, (lb + 1) * lane_w)
        im = slice(SSM_LANES + lb * lane_w, SSM_LANES + (lb + 1) * lane_w)
        lam = [[jnp.broadcast_to(lam_ref[d, c, :, re], (nb, lane_w)) for c in range(2)] for d in range(2)]
        init = tuple(st_ref[d, c, :, re] for d in range(2) for c in range(2))

        def step(k, carry, re=re, im=im, lam=lam):
            fr, fi, br, bi = carry
            rf = pl.multiple_of(k * nb, nb)
            rb = pl.multiple_of((steps - 1 - k) * nb, nb)
            nfr = lam[0][0] * fr - lam[0][1] * fi + hf_ref[pl.ds(rf, nb), re]
            nfi = lam[0][0] * fi + lam[0][1] * fr + hf_ref[pl.ds(rf, nb), im]
            nbr = lam[1][0] * br - lam[1][1] * bi + hb_ref[pl.ds(rb, nb), re]
            nbi = lam[1][0] * bi + lam[1][1] * br + hb_ref[pl.ds(rb, nb), im]
            hf_ref[pl.ds(rf, nb), re] = nfr
            hf_ref[pl.ds(rf, nb), im] = nfi
            hb_ref[pl.ds(rb, nb), re] = nbr
            hb_ref[pl.ds(rb, nb), im] = nbi
            return nfr, nfi, nbr, nbi

        fin = lax.fori_loop(0, steps, step, init)
        for d in range(2):
            for c in range(2):
                st_ref[d, c, :, re] = fin[2 * d + c]

    yf_ref[...] = _dot(hf_ref[...].astype(BF16), cmat_ref[0])
    yb_ref[...] = _dot(hb_ref[...].astype(BF16), cmat_ref[1])
    hfin_ref[...] = st_ref[...]


def _scan_pair_kernel(uf_ref, ub_ref, bmat_ref, lam_ref, cmat_ref, h0_ref, yf_ref, yb_ref,
                      hf_ref, hb_ref, st_ref, *, lane_w):
    half = DEC_BATCH
    tiles = SCAN_ROWS // (2 * half)

    @pl.when(pl.program_id(0) == 0)
    def _():
        st_ref[...] = h0_ref[...]

    hf_ref[...] = _dot(uf_ref[...].astype(BF16), bmat_ref[0])
    hb_ref[...] = _dot(ub_ref[...].astype(BF16), bmat_ref[1])

    top = lax.broadcasted_iota(jnp.int32, (2 * half, lane_w), 0) < half
    swap = lambda a: pltpu.roll(a, half, 0)
    for lb in range(SSM_LANES // lane_w):
        re = slice(lb * lane_w, (lb + 1) * lane_w)
        im = slice(SSM_LANES + lb * lane_w, SSM_LANES + (lb + 1) * lane_w)
        la_r = lam_ref[0, :, re]
        la_i = lam_ref[1, :, re]
        lb_r = swap(la_r)
        lb_i = swap(la_i)

        def body(m, carry, re=re, im=im, la_r=la_r, la_i=la_i, lb_r=lb_r, lb_i=lb_i):
            sr, si = carry
            rf = pl.multiple_of(m * 2 * half, 2 * half)
            rb = pl.multiple_of((tiles - 1 - m) * 2 * half, 2 * half)
            fr = hf_ref[pl.ds(rf, 2 * half), re]
            fi = hf_ref[pl.ds(rf, 2 * half), im]
            br = hb_ref[pl.ds(rb, 2 * half), re]
            bi = hb_ref[pl.ds(rb, 2 * half), im]
            vr = la_r * sr - la_i * si + jnp.where(top, fr, br)
            vi = la_r * si + la_i * sr + jnp.where(top, fi, bi)
            tr = swap(vr)
            ti = swap(vi)
            wr = lb_r * tr - lb_i * ti + jnp.where(top, br, fr)
            wi = lb_r * ti + lb_i * tr + jnp.where(top, bi, fi)
            hf_ref[pl.ds(rf, 2 * half), re] = jnp.where(top, vr, wr)
            hf_ref[pl.ds(rf, 2 * half), im] = jnp.where(top, vi, wi)
            hb_ref[pl.ds(rb, 2 * half), re] = jnp.where(top, wr, vr)
            hb_ref[pl.ds(rb, 2 * half), im] = jnp.where(top, wi, vi)
            return swap(wr), swap(wi)

        fin = lax.fori_loop(0, tiles, body, (st_ref[0, :, re], st_ref[1, :, re]))
        st_ref[0, :, re] = fin[0]
        st_ref[1, :, re] = fin[1]

    yf_ref[...] = _dot(hf_ref[...].astype(BF16), cmat_ref[0])
    yb_ref[...] = _dot(hb_ref[...].astype(BF16), cmat_ref[1])


def _scan_specs(n_rows):
    n = n_rows // SCAN_ROWS
    fwd = pl.BlockSpec((SCAN_ROWS, SSM_WIDTH), lambda j: (j, 0))
    bwd = pl.BlockSpec((SCAN_ROWS, SSM_WIDTH), lambda j: (n - 1 - j, 0))
    y_shape = jax.ShapeDtypeStruct((n_rows, SSM_WIDTH), F32)
    buf = pltpu.VMEM((SCAN_ROWS, 2 * SSM_LANES), F32)
    return n, fwd, bwd, y_shape, buf


def _scan_ctx(u_rows, bmat, lam, cmat, h0):
    n, fwd, bwd, y_shape, buf = _scan_specs(u_rows.shape[0])
    st_shape = (2, 2, BATCH, SSM_LANES)
    return pl.pallas_call(
        functools.partial(_scan_kernel, nb=BATCH, lane_w=256),
        grid=(n,),
        in_specs=[fwd, bwd, _const_spec(bmat.shape), _const_spec(lam.shape), _const_spec(cmat.shape),
                  _const_spec(st_shape)],
        out_specs=[fwd, bwd, pl.BlockSpec(st_shape, lambda j: (0, 0, 0, 0))],
        out_shape=[y_shape, y_shape, jax.ShapeDtypeStruct(st_shape, F32)],
        scratch_shapes=[buf, buf, pltpu.VMEM(st_shape, F32)],
        compiler_params=_params(1),
        name="scan_ctx",
    )(u_rows, u_rows, bmat, lam, cmat, h0)


def _scan_lat(u_rows, bmat, lam_pair, cmat, h0_pair):
    n, fwd, bwd, y_shape, buf = _scan_specs(u_rows.shape[0])
    st_shape = (2, 2 * DEC_BATCH, SSM_LANES)
    return pl.pallas_call(
        functools.partial(_scan_pair_kernel, lane_w=512),
        grid=(n,),
        in_specs=[fwd, bwd, _const_spec(bmat.shape), _const_spec(st_shape), _const_spec(cmat.shape),
                  _const_spec(st_shape)],
        out_specs=[fwd, bwd],
        out_shape=[y_shape, y_shape],
        scratch_shapes=[buf, buf, pltpu.VMEM(st_shape, F32)],
        compiler_params=_params(1),
        name="scan_lat",
    )(u_rows, u_rows, bmat, lam_pair, cmat, h0_pair)


def _ssm_discretise(lam_re, lam_im, log_step, b_re, b_im, c_re, c_im):
    step = jnp.exp(log_step.astype(F32))[..., None]
    lr, li = lam_re.astype(F32), lam_im.astype(F32)
    mag = jnp.exp(lr * step)
    bar_r = mag * jnp.cos(li * step)
    bar_i = mag * jnp.sin(li * step)
    den = lr * lr + li * li
    coef_r = (((bar_r - 1) * lr + bar_i * li) / den)[..., None]
    coef_i = ((bar_i * lr - (bar_r - 1) * li) / den)[..., None]
    br, bi = b_re.astype(F32), b_im.astype(F32)
    bbar_r = coef_r * br - coef_i * bi
    bbar_i = coef_r * bi + coef_i * br
    eye = jnp.eye(SSM_GROUPS, dtype=F32)
    blk_b = lambda a: jnp.einsum('dgpc,gh->dgchp', a, eye).reshape(2, SSM_WIDTH, SSM_LANES)
    bmat = jnp.concatenate([blk_b(bbar_r), blk_b(bbar_i)], axis=-1)
    blk_c = lambda a: jnp.einsum('dgcp,gh->dgphc', a, eye).reshape(2, SSM_LANES, SSM_WIDTH)
    cmat = jnp.concatenate([blk_c(c_re.astype(F32)), -blk_c(c_im.astype(F32))], axis=1)
    lam_flat = jnp.stack([bar_r, bar_i], axis=1).reshape(2, 2, 1, SSM_LANES)
    return bmat.astype(BF16), lam_flat, cmat.astype(BF16)


def _merge_kernel(x_ref, oa_ref, yf_ref, yb_ref, u_ref, oc_ref, g_ref, mod_ref, d_ref, wglu_ref,
                  wa_ref, wb_ref, wc_ref, wo_ref, g2_ref, wgu_ref, wd_ref, fg_ref, o_ref, slab0, slab1, *, final):
    nb, tt, _ = x_ref.shape
    rows = nb * tt
    flat = lambda ref: ref[...].reshape(rows, ref.shape[-1])
    y = _gelu_tanh(yf_ref[...] + yb_ref[...] + d_ref[...] * u_ref[...])
    ob = y * jax.nn.sigmoid(_dot(y.astype(BF16), wglu_ref[...]))
    ob = _to_batch_major(ob, (slab0, slab1), nb)
    gate = lambda i: jax.nn.sigmoid(g_ref[:, :, i * D_MODEL:(i + 1) * D_MODEL].reshape(rows, D_MODEL))
    merged = (gate(0) * _dot(flat(oa_ref), wa_ref[...])
              + gate(1) * _dot(ob.astype(BF16), wb_ref[...])
              + gate(2) * _dot(flat(oc_ref), wc_ref[...]))
    mod = mod_ref[...]
    gate1 = mod[:, :, 2 * D_MODEL:3 * D_MODEL]
    shift2 = mod[:, :, 3 * D_MODEL:4 * D_MODEL]
    scale2 = mod[:, :, 4 * D_MODEL:5 * D_MODEL]
    gate2 = mod[:, :, 5 * D_MODEL:6 * D_MODEL]
    x1 = x_ref[...] + gate1 * _dot(merged.astype(BF16), wo_ref[...]).reshape(nb, tt, D_MODEL)
    h2 = _rms(x1, g2_ref[...]) * (1 + scale2) + shift2
    gu = _dot(h2.reshape(rows, D_MODEL).astype(BF16), wgu_ref[...])
    act = _silu(gu[:, :D_FF]) * gu[:, D_FF:]
    x2 = x1 + gate2 * _dot(act.astype(BF16), wd_ref[...]).reshape(nb, tt, D_MODEL)
    if final:
        x2 = _rms(x2, fg_ref[...])
    o_ref[...] = x2


def _merge(x, oa, yf, yb, u_rows, oc, g, mod_rows, lw, final_g, *, final, name):
    nb, seq, _ = x.shape
    tt = ROW_TILE // nb
    slab = lambda w: pl.BlockSpec((nb, tt, w), lambda j: (0, j, 0))
    tmaj = pl.BlockSpec((ROW_TILE, SSM_WIDTH), lambda j: (j, 0))
    weights = [lw["ssm_d"], lw["w_glu"], lw["w_br_a"], lw["w_br_b"], lw["w_br_c"], lw["w_out"],
               lw["norm2_g"], lw["w_ffn_gu"], lw["w_ffn_d"], final_g]
    return pl.pallas_call(
        functools.partial(_merge_kernel, final=final),
        grid=(seq // tt,),
        in_specs=[slab(D_MODEL), slab(WIDTH_A), tmaj, tmaj, tmaj, slab(WIDTH_C), slab(N_BRANCH * D_MODEL),
                  _const_spec(mod_rows.shape)] + [_const_spec(w.shape) for w in weights],
        out_specs=slab(D_MODEL),
        out_shape=jax.ShapeDtypeStruct((nb, seq, D_MODEL), F32),
        scratch_shapes=[pltpu.VMEM((ROW_TILE, LANES), F32)] * 2,
        compiler_params=_params(1),
        name=name + ("_final" if final else ""),
    )(x, oa, yf, yb, u_rows, oc, g, mod_rows, *weights)


def _rope_tables():
    t = jnp.arange(DEC_SEQ)
    row = (t // GRID_W).astype(F32)
    col = (t % GRID_W).astype(F32)
    inv = 1.0 / (ROPE_THETA ** (jnp.arange(ROT_FREQS, dtype=F32) / ROT_FREQS))
    ar = row[:, None] * inv[None]
    ac = col[:, None] * inv[None]
    cos = jnp.concatenate([jnp.cos(ar), jnp.cos(ar), jnp.cos(ac), jnp.cos(ac)], axis=-1)
    sin = jnp.concatenate([-jnp.sin(ar), jnp.sin(ar), -jnp.sin(ac), jnp.sin(ac)], axis=-1)
    return jnp.tile(cos, (1, 2)), jnp.tile(sin, (1, 2))


def kernel(x_prompt, x_sample, c, cache_ga_k, cache_ga_v, cache_na_k, cache_na_v, state_ssm, c_ctx, w_mod, b_mod, norm1_g, w_in, qn_g, kn_g, ssm_lam_re, ssm_lam_im, ssm_log_step, ssm_b_re, ssm_b_im, ssm_c_re, ssm_c_im, ssm_d, ssm_w_glu, na_bias, w_br_a, w_br_b, w_br_c, w_out, norm2_g, w_ffn_gu, w_ffn_d, final_g):
    cvec = jnp.concatenate([c_ctx[None, :], c, jnp.zeros((N_MOD_ROWS - 1 - DEC_BATCH, D_MODEL), F32)], axis=0)
    mod = _adaln(cvec, w_mod, b_mod).reshape(DEPTH, N_MOD_ROWS, 1, 6 * D_MODEL)

    seg = jnp.kron(jnp.eye(N_HEADS_A, dtype=F32), jnp.full((HEAD_DIM, HEAD_DIM), 1.0 / HEAD_DIM, F32)).astype(BF16)
    rope_tabs = _rope_tables()
    fg = final_g.reshape(1, D_MODEL)
    ck_a = cache_ga_k.reshape(DEC_BATCH, DEPTH, PAST_LEN, KV_WIDTH_A)
    cv_a = cache_ga_v.reshape(DEC_BATCH, DEPTH, PAST_LEN, KV_WIDTH_A)
    ck_c = cache_na_k.reshape(DEC_BATCH, DEPTH, PAST_LEN, WIDTH_C)
    cv_c = cache_na_v.reshape(DEC_BATCH, DEPTH, PAST_LEN, WIDTH_C)
    zero_state = jnp.zeros((2, 2, BATCH, SSM_LANES), F32)

    xp, xs = x_prompt, x_sample
    ga_k, ga_v, na_k, na_v, ssm_st = [], [], [], [], []
    for l in range(DEPTH):
        lw = dict(ssm_d=ssm_d[l].reshape(1, SSM_WIDTH), w_glu=ssm_w_glu[l].astype(BF16),
                  w_br_a=w_br_a[l].astype(BF16), w_br_b=w_br_b[l].astype(BF16), w_br_c=w_br_c[l].astype(BF16),
                  w_out=w_out[l].astype(BF16), norm2_g=norm2_g[l].reshape(1, D_MODEL),
                  w_ffn_gu=w_ffn_gu[l].astype(BF16), w_ffn_d=w_ffn_d[l].astype(BF16))
        w_in_l = w_in[l].astype(BF16)
        g1 = norm1_g[l].reshape(1, D_MODEL)
        qg = jnp.tile(qn_g[l], N_HEADS_A).reshape(1, WIDTH_A)
        kg = jnp.tile(kn_g[l], N_KV_A).reshape(1, KV_WIDTH_A)
        bmat, lam, cmat = _ssm_discretise(ssm_lam_re[l], ssm_lam_im[l], ssm_log_step[l], ssm_b_re[l], ssm_b_im[l],
                                          ssm_c_re[l], ssm_c_im[l])
        final = l == DEPTH - 1
        mod_ctx = mod[l, 0:1]
        mod_lat = mod[l, 1:1 + DEC_BATCH]

        qa, krep, vrep, ka, va, u_rows, qc, kc, vc, g = _inproj(xp, mod_ctx, g1, w_in_l, seg, qg, kg, None,
                                                                latent=False)
        oa = _attn_a_ctx(qa, krep, vrep)
        oc = _attn_c_ctx(qc, kc, vc)
        yf, yb, hfin = _scan_ctx(u_rows, bmat, lam, cmat, zero_state)
        xp = _merge(xp, oa, yf, yb, u_rows, oc, g, mod_ctx, lw, fg, final=final, name="merge_ctx")
        ga_k.append(ka)
        ga_v.append(va)
        na_k.append(kc)
        na_v.append(vc)
        ssm_st.append(jnp.transpose(hfin, (2, 0, 1, 3)))

        qa, krep, vrep, u_rows, qc, kc, vc, g = _inproj(xs, mod_lat, g1, w_in_l, seg, qg, kg, rope_tabs, latent=True)
        oa = _attn_a_lat(qa, krep, vrep, ck_a, cv_a, l)
        oc = _na_lat(qc, kc, vc, ck_c, cv_c, _na_bias_rows(na_bias[l]), l)
        h0 = jnp.transpose(state_ssm[:, l].reshape(DEC_BATCH, 2, 2, SSM_LANES), (2, 1, 0, 3))
        h0 = h0.reshape(2, 2 * DEC_BATCH, SSM_LANES)
        lam_pair = jnp.broadcast_to(jnp.transpose(lam, (1, 0, 2, 3)), (2, 2, DEC_BATCH, SSM_LANES))
        lam_pair = lam_pair.reshape(2, 2 * DEC_BATCH, SSM_LANES)
        yf, yb = _scan_lat(u_rows, bmat, lam_pair, cmat, h0)
        xs = _merge(xs, oa, yf, yb, u_rows, oc, g, mod_lat, lw, fg, final=final, name="merge_lat")

    stack = lambda parts, tail: jnp.stack(parts, axis=1).reshape((BATCH, DEPTH) + tail)
    return (xp, xs,
            stack(ga_k, (SEQ, N_KV_A, HEAD_DIM)), stack(ga_v, (SEQ, N_KV_A, HEAD_DIM)),
            stack(na_k, (SEQ, N_HEADS_C, HEAD_DIM)), stack(na_v, (SEQ, N_HEADS_C, HEAD_DIM)),
            stack(ssm_st, (2, 2, SSM_GROUPS, SSM_STATE)))
```

```python
import functools
import math

import jax
import jax.numpy as jnp
from jax import lax
from jax.experimental import pallas as pl
from jax.experimental.pallas import tpu as pltpu

D_MODEL = 1024
BATCH = 16
SEQ = 256
DEPTH = 2
DEC_BATCH = 4
DEC_SEQ = 2048
PAST_LEN = 256
GRID_W = 64
GRID_ROWS = DEC_SEQ // GRID_W
HEAD_DIM = 64
N_HEADS_A = 8
N_KV_A = 2
REP_A = N_HEADS_A // N_KV_A
N_HEADS_C = 4
SSM_WIDTH = 256
SSM_GROUP = 16
SSM_GROUPS = SSM_WIDTH // SSM_GROUP
SSM_STATE = 64
SSM_LANES = SSM_GROUPS * SSM_STATE
NA_WIN_R = 8
NA_WIN_C = 16
D_FF = -(-8 * D_MODEL // (3 * 256)) * 256
ROPE_THETA = 10000.0
ROT_HALF = HEAD_DIM // 2
ROT_FREQS = ROT_HALF // 2
WIDTH_A = N_HEADS_A * HEAD_DIM
KV_WIDTH_A = N_KV_A * HEAD_DIM
WIDTH_C = N_HEADS_C * HEAD_DIM
N_BRANCH = 3
IN_WIDTH = WIDTH_A + 2 * KV_WIDTH_A + SSM_WIDTH + 3 * WIDTH_C + N_BRANCH * D_MODEL
EPS = 1e-6

OFF_QA = 0
OFF_KA = OFF_QA + WIDTH_A
OFF_VA = OFF_KA + KV_WIDTH_A
OFF_U = OFF_VA + KV_WIDTH_A
OFF_QC = OFF_U + SSM_WIDTH
OFF_KC = OFF_QC + WIDTH_C
OFF_VC = OFF_KC + WIDTH_C
OFF_G = OFF_VC + WIDTH_C

N_MOD_ROWS = 8
ROW_TILE = 256
Q_TILE_A = 256
NA_Q_ROWS = 2
NA_K_ROWS = 10
NA_R0_MAX = GRID_ROWS - NA_K_ROWS
SCAN_ROWS = 1024
LANES = 128
NEG = -1e30
VMEM_LIMIT_V7X = 56 * 1024 * 1024

F32 = jnp.float32
BF16 = jnp.bfloat16


def _dot(a, b):
    return jnp.dot(a, b, preferred_element_type=F32)


def _dot_t(a, b):
    return lax.dot_general(a, b, (((1,), (1,)), ((), ())), preferred_element_type=F32)


def _params(n_axes):
    return pltpu.CompilerParams(dimension_semantics=("arbitrary",) * n_axes,
                                vmem_limit_bytes=VMEM_LIMIT_V7X)


def _const_spec(shape):
    nd = len(shape)
    return pl.BlockSpec(shape, lambda *_: (0,) * nd, pipeline_mode=pl.Buffered(1))


def _layer_spec(stacked, layer):
    shape = stacked.shape[1:]
    return pl.BlockSpec((None,) + shape, lambda *_: (layer,) + (0,) * len(shape), pipeline_mode=pl.Buffered(1))


def _rms(x, g):
    return x * lax.rsqrt(jnp.mean(x * x, axis=-1, keepdims=True) + EPS) * g


def _silu(x):
    return x * jax.nn.sigmoid(x)


def _gelu_tanh(x):
    c = math.sqrt(2.0 / math.pi)
    return x * (0.5 * (1.0 + jnp.tanh(c * (x + 0.044715 * (x * x * x)))))


def _seg_rms(x, seg, g):
    x2 = x * x
    hi = x2.astype(BF16)
    lo = (x2 - hi.astype(F32)).astype(BF16)
    ms = _dot(hi, seg) + _dot(lo, seg)
    return x * lax.rsqrt(ms + EPS) * g


def _rope(x, cos, sin_signed):
    w = x.shape[-1]
    lane = lax.broadcasted_iota(jnp.int32, x.shape, 1)
    first = (lane & ROT_FREQS) == 0
    partner = jnp.where(first, pltpu.roll(x, w - ROT_FREQS, 1), pltpu.roll(x, ROT_FREQS, 1))
    return x * cos + partner * sin_signed


def _rep_heads(kv):
    lane = lax.broadcasted_iota(jnp.int32, kv.shape, 1)
    swapped = pltpu.roll(kv, HEAD_DIM, 1)
    lo = lane < HEAD_DIM
    h0 = jnp.where(lo, kv, swapped)
    h1 = jnp.where(lo, swapped, kv)
    return jnp.concatenate([h0, h0, h1, h1], axis=1)


def _to_time_major(val, slabs, nb):
    tt = val.shape[0] // nb
    for s, slab in enumerate(slabs):
        for b in range(nb):
            slab[pl.ds(b, tt, stride=nb), :] = val[b * tt:(b + 1) * tt, s * LANES:(s + 1) * LANES]
    return jnp.concatenate([slab[...] for slab in slabs], axis=1)


def _to_batch_major(val, slabs, nb):
    tt = val.shape[0] // nb
    for s, slab in enumerate(slabs):
        slab[...] = val[:, s * LANES:(s + 1) * LANES]
    return jnp.concatenate(
        [jnp.concatenate([slab[pl.ds(b, tt, stride=nb), :] for slab in slabs], axis=1) for b in range(nb)], axis=0)


def _adaln_kernel(c_ref, w_ref, b_ref, o_ref):
    s = _silu(c_ref[...])
    o_ref[...] = _dot(s.astype(BF16), w_ref[...].astype(BF16)) + b_ref[...]


def _adaln(cvec, w_mod, b_mod):
    n_col = 6 * D_MODEL
    tn = n_col // 4
    return pl.pallas_call(
        _adaln_kernel,
        grid=(DEPTH, n_col // tn),
        in_specs=[pl.BlockSpec((N_MOD_ROWS, D_MODEL), lambda l, n: (0, 0)),
                  pl.BlockSpec((None, D_MODEL, tn), lambda l, n: (l, 0, n)),
                  pl.BlockSpec((None, 1, tn), lambda l, n: (l, 0, n))],
        out_specs=pl.BlockSpec((None, N_MOD_ROWS, tn), lambda l, n: (l, 0, n)),
        out_shape=jax.ShapeDtypeStruct((DEPTH, N_MOD_ROWS, n_col), F32),
        compiler_params=_params(2),
        name="adaln",
    )(cvec, w_mod, b_mod.reshape(DEPTH, 1, n_col))


def _inproj_kernel(*refs, latent, n_alias):
    if latent:
        (x_ref, mod_ref, g1_ref, w_ref, seg_ref, qg_ref, kg_ref, cos_ref, sin_ref,
         qa_ref, krep_ref, vrep_ref, u_ref, qc_ref, kc_ref, vc_ref, g_ref, slab0, slab1) = refs
    else:
        x_ref, mod_ref, g1_ref, w_ref, seg_ref, qg_ref, kg_ref = refs[:7]
        (qa_ref, krep_ref, vrep_ref, ka_ref, va_ref, u_ref, qc_ref, kc_ref, vc_ref, g_ref,
         slab0, slab1) = refs[7 + n_alias:]
    nb, tt, _ = x_ref.shape
    rows = nb * tt
    mod = mod_ref[...]
    shift = mod[:, :, 0:D_MODEL]
    scale = mod[:, :, D_MODEL:2 * D_MODEL]
    h = _rms(x_ref[...], g1_ref[...]) * (1 + scale) + shift
    hb = h.reshape(rows, D_MODEL).astype(BF16)
    seg = seg_ref[...]
    q_scale = HEAD_DIM ** -0.5

    def put(ref, val):
        ref[...] = val.reshape(ref.shape).astype(ref.dtype)

    qa = _seg_rms(_dot(hb, w_ref[:, OFF_QA:OFF_KA]), seg, qg_ref[...])
    ka = _seg_rms(_dot(hb, w_ref[:, OFF_KA:OFF_VA]), seg[0:KV_WIDTH_A, 0:KV_WIDTH_A], kg_ref[...])
    va = _dot(hb, w_ref[:, OFF_VA:OFF_U])
    if latent:
        cos = jnp.concatenate([cos_ref[...]] * nb, axis=0)
        sin = jnp.concatenate([sin_ref[...]] * nb, axis=0)
        qa = _rope(qa, jnp.concatenate([cos] * REP_A, axis=1), jnp.concatenate([sin] * REP_A, axis=1))
        ka = _rope(ka, cos, sin)
    else:
        put(ka_ref, ka)
        put(va_ref, va)
    put(qa_ref, qa * q_scale)
    put(krep_ref, _rep_heads(ka))
    put(vrep_ref, _rep_heads(va))

    u_ref[...] = _to_time_major(_dot(hb, w_ref[:, OFF_U:OFF_QC]), (slab0, slab1), nb)
    put(qc_ref, _dot(hb, w_ref[:, OFF_QC:OFF_KC]) * q_scale)
    put(kc_ref, _dot(hb, w_ref[:, OFF_KC:OFF_VC]))
    put(vc_ref, _dot(hb, w_ref[:, OFF_VC:OFF_G]))
    for cidx in range(N_BRANCH):
        lo = OFF_G + cidx * D_MODEL
        g_ref[:, :, cidx * D_MODEL:(cidx + 1) * D_MODEL] = _dot(hb, w_ref[:, lo:lo + D_MODEL]).reshape(
            nb, tt, D_MODEL)


def _inproj(x, mod_rows, g1, w_in, seg, qg, kg, layer, *, rope_tabs=None, prev_cache=()):
    latent = rope_tabs is not None
    nb, seq, _ = x.shape
    tt = ROW_TILE // nb
    slab = lambda w: pl.BlockSpec((nb, tt, w), lambda j: (0, j, 0))
    act = lambda w, dt: jax.ShapeDtypeStruct((nb, seq, w), dt)
    cache_slab = lambda w: pl.BlockSpec((nb, None, tt, w), lambda j: (0, layer, j, 0))
    cache = lambda w: jax.ShapeDtypeStruct((nb, DEPTH, seq, w), F32)
    in_specs = [slab(D_MODEL), _const_spec(mod_rows.shape), _layer_spec(g1, layer), _layer_spec(w_in, layer),
                _const_spec(seg.shape), _layer_spec(qg, layer), _layer_spec(kg, layer)]
    args = [x, mod_rows, g1, w_in, seg, qg, kg]
    out_shape = [act(WIDTH_A, BF16), act(REP_A * KV_WIDTH_A, BF16), act(REP_A * KV_WIDTH_A, BF16)]
    out_specs = [slab(WIDTH_A), slab(REP_A * KV_WIDTH_A), slab(REP_A * KV_WIDTH_A)]
    aliases = {}
    if latent:
        in_specs += [pl.BlockSpec((tt, 2 * HEAD_DIM), lambda j: (j, 0))] * 2
        args += list(rope_tabs)
        kv_c = [act(WIDTH_C, BF16)] * 2
        kv_c_specs = [slab(WIDTH_C)] * 2
    else:
        aliases = {len(args) + i: o for i, o in enumerate((3, 4, 7, 8)[:len(prev_cache)])}
        in_specs += [pl.BlockSpec(memory_space=pl.ANY)] * len(prev_cache)
        args += list(prev_cache)
        out_shape += [cache(KV_WIDTH_A)] * 2
        out_specs += [cache_slab(KV_WIDTH_A)] * 2
        kv_c = [cache(WIDTH_C)] * 2
        kv_c_specs = [cache_slab(WIDTH_C)] * 2
    out_shape += [jax.ShapeDtypeStruct((seq * nb, SSM_WIDTH), F32),
                  act(WIDTH_C, BF16)] + kv_c + [act(N_BRANCH * D_MODEL, F32)]
    out_specs += [pl.BlockSpec((ROW_TILE, SSM_WIDTH), lambda j: (j, 0)),
                  slab(WIDTH_C)] + kv_c_specs + [slab(N_BRANCH * D_MODEL)]
    return pl.pallas_call(
        functools.partial(_inproj_kernel, latent=latent, n_alias=len(prev_cache)),
        grid=(seq // tt,),
        in_specs=in_specs,
        out_specs=out_specs,
        out_shape=out_shape,
        scratch_shapes=[pltpu.VMEM((ROW_TILE, LANES), F32)] * 2,
        input_output_aliases=aliases,
        compiler_params=_params(1),
        name="inproj_lat" if latent else "inproj_ctx",
    )(*args)


def _head_mask(shape, head):
    lane = lax.broadcasted_iota(jnp.int32, shape, 1)
    return (lane // HEAD_DIM) == head


def _softmax_pv(parts):
    m = None
    for s, _ in parts:
        pm = jnp.max(s, axis=-1, keepdims=True)
        m = pm if m is None else jnp.maximum(m, pm)
    l = None
    o = None
    for s, v in parts:
        p = jnp.exp(s - m)
        pl_ = jnp.sum(p, axis=-1, keepdims=True)
        po = _dot(p.astype(BF16), v)
        l = pl_ if l is None else l + pl_
        o = po if o is None else o + po
    return o / l


def _stacked_heads(q, n_heads, attend):
    m_rows = q.shape[0]
    masks = [_head_mask(q.shape, h) for h in range(n_heads)]
    qs = jnp.concatenate([jnp.where(hm, q, jnp.zeros_like(q)) for hm in masks], axis=0)
    o_all = attend(qs)
    acc = jnp.zeros(q.shape, F32)
    for h, hm in enumerate(masks):
        acc = jnp.where(hm, o_all[h * m_rows:(h + 1) * m_rows], acc)
    return acc


def _attn_a_kernel(*refs, has_cache):
    if has_cache:
        q_ref, kn_ref, vn_ref, ck_ref, cv_ref, o_ref, ckrep_ref, cvrep_ref = refs

        @pl.when(pl.program_id(1) == 0)
        def _():
            ckrep_ref[...] = _rep_heads(ck_ref[...]).astype(BF16)
            cvrep_ref[...] = _rep_heads(cv_ref[...]).astype(BF16)
    else:
        q_ref, kn_ref, vn_ref, o_ref = refs
    gw = REP_A * HEAD_DIM
    for g in range(N_KV_A):
        sl = slice(g * gw, (g + 1) * gw)
        qg = q_ref[:, sl]
        acc = jnp.zeros(qg.shape, F32)
        for j in range(REP_A):
            hm = _head_mask(qg.shape, j)
            qm = jnp.where(hm, qg, jnp.zeros_like(qg))
            parts = [(_dot_t(qm, kn_ref[:, sl]), vn_ref[:, sl])]
            if has_cache:
                parts.append((_dot_t(qm, ckrep_ref[:, sl]), cvrep_ref[:, sl]))
            acc = jnp.where(hm, _softmax_pv(parts), acc)
        o_ref[:, sl] = acc.astype(BF16)


def _attn_a_ctx(qa, krep, vrep):
    blk = pl.BlockSpec((None, SEQ, WIDTH_A), lambda b: (b, 0, 0))
    return pl.pallas_call(
        functools.partial(_attn_a_kernel, has_cache=False),
        grid=(BATCH,),
        in_specs=[blk, blk, blk],
        out_specs=blk,
        out_shape=jax.ShapeDtypeStruct((BATCH, SEQ, WIDTH_A), BF16),
        compiler_params=_params(1),
        name="attn_a_ctx",
    )(qa, krep, vrep)


def _attn_a_lat(qa, krep, vrep, cache_k, cache_v, layer):
    seq_blk = pl.BlockSpec((None, DEC_SEQ, WIDTH_A), lambda b, t: (b, 0, 0))
    cache_blk = pl.BlockSpec((None, None, PAST_LEN, KV_WIDTH_A), lambda b, t: (b, layer, 0, 0))
    q_blk = pl.BlockSpec((None, Q_TILE_A, WIDTH_A), lambda b, t: (b, t, 0))
    return pl.pallas_call(
        functools.partial(_attn_a_kernel, has_cache=True),
        grid=(DEC_BATCH, DEC_SEQ // Q_TILE_A),
        in_specs=[q_blk, seq_blk, seq_blk, cache_blk, cache_blk],
        out_specs=q_blk,
        out_shape=jax.ShapeDtypeStruct((DEC_BATCH, DEC_SEQ, WIDTH_A), BF16),
        scratch_shapes=[pltpu.VMEM((PAST_LEN, WIDTH_A), BF16), pltpu.VMEM((PAST_LEN, WIDTH_A), BF16)],
        compiler_params=_params(2),
        name="attn_a_lat",
    )(qa, krep, vrep, cache_k, cache_v)


def _attn_c_kernel(q_ref, k_ref, v_ref, o_ref):
    k = k_ref[...].astype(BF16)
    v = v_ref[...].astype(BF16)
    attend = lambda qs: _softmax_pv([(_dot_t(qs, k), v)])
    o_ref[...] = _stacked_heads(q_ref[...], N_HEADS_C, attend).astype(BF16)


def _attn_c_ctx(qc, kc, vc, layer):
    blk = pl.BlockSpec((None, SEQ, WIDTH_C), lambda b: (b, 0, 0))
    kv_blk = pl.BlockSpec((None, None, SEQ, WIDTH_C), lambda b: (b, layer, 0, 0))
    return pl.pallas_call(
        _attn_c_kernel,
        grid=(BATCH,),
        in_specs=[blk, kv_blk, kv_blk],
        out_specs=blk,
        out_shape=jax.ShapeDtypeStruct((BATCH, SEQ, WIDTH_C), BF16),
        compiler_params=_params(1),
        name="attn_c_ctx",
    )(qc, kc, vc)


def _na_kernel(q_ref, k_ref, v_ref, ck_ref, cv_ref, brow_ref, o_ref, b2_ref):
    nq = NA_Q_ROWS * GRID_W
    nk = NA_K_ROWS * GRID_W
    n_pair = 2 * NA_WIN_R

    @pl.when((pl.program_id(0) == 0) & (pl.program_id(1) == 0))
    def _():
        shp = (GRID_W, 2 * GRID_W)
        c = lax.broadcasted_iota(jnp.int32, shp, 0)
        kc = lax.broadcasted_iota(jnp.int32, shp, 1) & (GRID_W - 1)
        c0 = jnp.clip(c - NA_WIN_C // 2, 0, GRID_W - NA_WIN_C)
        col_bias = jnp.where(kc < c0, NEG, jnp.where(kc >= c0 + NA_WIN_C, NEG, 0.0))
        for h in range(N_HEADS_C):
            for e in range(n_pair):
                row = jnp.broadcast_to(brow_ref[h, e:e + 1, :], shp)
                toep = pltpu.roll(row, 2 * GRID_W - (NA_WIN_C - 1), 1, stride=1, stride_axis=0)
                b2_ref[h, e] = jnp.where(col_bias < 0.0, NEG, toep)

    qrow0 = NA_Q_ROWS * pl.program_id(1)
    row0 = jnp.clip(qrow0 - NA_WIN_R // 2, 0, NA_R0_MAX)
    start = pl.multiple_of(row0 * GRID_W, GRID_W)
    kwin = k_ref[pl.ds(start, nk), :]
    vwin = v_ref[pl.ds(start, nk), :]
    ck = ck_ref[...].astype(BF16)
    cv = cv_ref[...].astype(BF16)

    qrow = qrow0 + lax.broadcasted_iota(jnp.int32, (nq, nk), 0) // GRID_W
    krow = row0 + lax.broadcasted_iota(jnp.int32, (nq, nk), 1) // GRID_W
    win0 = jnp.clip(qrow - NA_WIN_R // 2, 0, GRID_ROWS - NA_WIN_R)
    row_bias = jnp.where(krow < win0, NEG, jnp.where(krow >= win0 + NA_WIN_R, NEG, 0.0))

    blocks = []
    for h in range(N_HEADS_C):
        rows = []
        for i in range(NA_Q_ROWS):
            tiles = []
            for m in range(NA_K_ROWS // 2):
                e = jnp.clip(row0 + 2 * m - (qrow0 + i) + NA_WIN_R, 0, n_pair - 1)
                tiles.append(b2_ref[h, e])
            rows.append(jnp.concatenate(tiles, axis=1))
        blocks.append(jnp.concatenate(rows, axis=0) + row_bias)
    bias = jnp.concatenate(blocks, axis=0)
    attend = lambda qs: _softmax_pv([(_dot_t(qs, kwin) + bias, vwin), (_dot_t(qs, ck), cv)])
    o_ref[...] = _stacked_heads(q_ref[...], N_HEADS_C, attend).astype(BF16)


def _na_bias_rows(tbl):
    pad = jnp.pad(tbl, ((0, 0), (1, 1), (0, GRID_W - tbl.shape[-1])))
    return jnp.concatenate([pad[:, :-1], pad[:, 1:]], axis=-1)


def _na_lat(qc, kc, vc, cache_k, cache_v, brow, layer):
    nq = NA_Q_ROWS * GRID_W
    seq_blk = pl.BlockSpec((None, DEC_SEQ, WIDTH_C), lambda b, j: (b, 0, 0))
    cache_blk = pl.BlockSpec((None, None, PAST_LEN, WIDTH_C), lambda b, j: (b, layer, 0, 0))
    q_blk = pl.BlockSpec((None, nq, WIDTH_C), lambda b, j: (b, j, 0))
    return pl.pallas_call(
        _na_kernel,
        grid=(DEC_BATCH, DEC_SEQ // nq),
        in_specs=[q_blk, seq_blk, seq_blk, cache_blk, cache_blk, _const_spec(brow.shape)],
        out_specs=q_blk,
        out_shape=jax.ShapeDtypeStruct((DEC_BATCH, DEC_SEQ, WIDTH_C), BF16),
        scratch_shapes=[pltpu.VMEM((N_HEADS_C, 2 * NA_WIN_R, GRID_W, 2 * GRID_W), F32)],
        compiler_params=_params(2),
        name="na_lat",
    )(qc, kc, vc, cache_k, cache_v, brow)


def _scan_kernel(uf_ref, ub_ref, bmat_ref, lam_ref, cmat_ref, h0_ref, yf_ref, yb_ref, hfin_ref,
                 hf_ref, hb_ref, st_ref, *, nb, lane_w):
    steps = SCAN_ROWS // nb

    @pl.when(pl.program_id(0) == 0)
    def _():
        st_ref[...] = h0_ref[...]

    hf_ref[...] = _dot(uf_ref[...].astype(BF16), bmat_ref[0])
    hb_ref[...] = _dot(ub_ref[...].astype(BF16), bmat_ref[1])

    for lb in range(SSM_LANES // lane_w):
        re = slice(lb * lane_w, (lb + 1) * lane_w)
        im = slice(SSM_LANES + lb * lane_w, SSM_LANES + (lb + 1) * lane_w)
        lam = [[jnp.broadcast_to(lam_ref[d, c, :, re], (nb, lane_w)) for c in range(2)] for d in range(2)]
        init = tuple(st_ref[d, c, :, re] for d in range(2) for c in range(2))

        def step(k, carry, re=re, im=im, lam=lam):
            fr, fi, br, bi = carry
            rf = pl.multiple_of(k * nb, nb)
            rb = pl.multiple_of((steps - 1 - k) * nb, nb)
            nfr = lam[0][0] * fr - lam[0][1] * fi + hf_ref[pl.ds(rf, nb), re]
            nfi = lam[0][0] * fi + lam[0][1] * fr + hf_ref[pl.ds(rf, nb), im]
            nbr = lam[1][0] * br - lam[1][1] * bi + hb_ref[pl.ds(rb, nb), re]
            nbi = lam[1][0] * bi + lam[1][1] * br + hb_ref[pl.ds(rb, nb), im]
            hf_ref[pl.ds(rf, nb), re] = nfr
            hf_ref[pl.ds(rf, nb), im] = nfi
            hb_ref[pl.ds(rb, nb), re] = nbr
            hb_ref[pl.ds(rb, nb), im] = nbi
            return nfr, nfi, nbr, nbi

        fin = lax.fori_loop(0, steps, step, init)
        for d in range(2):
            for c in range(2):
                st_ref[d, c, :, re] = fin[2 * d + c]

    yf_ref[...] = _dot(hf_ref[...].astype(BF16), cmat_ref[0])
    yb_ref[...] = _dot(hb_ref[...].astype(BF16), cmat_ref[1])
    hfin_ref[...] = st_ref[...]


def _scan_pair_kernel(uf_ref, ub_ref, bmat_ref, lam_ref, cmat_ref, h0_ref, yf_ref, yb_ref,
                      hf_ref, hb_ref, st_ref, *, lane_w):
    half = DEC_BATCH
    tiles = SCAN_ROWS // (2 * half)

    @pl.when(pl.program_id(0) == 0)
    def _():
        st_ref[...] = h0_ref[...]

    hf_ref[...] = _dot(uf_ref[...].astype(BF16), bmat_ref[0])
    hb_ref[...] = _dot(ub_ref[...].astype(BF16), bmat_ref[1])

    top = lax.broadcasted_iota(jnp.int32, (2 * half, lane_w), 0) < half
    swap = lambda a: pltpu.roll(a, half, 0)
    for lb in range(SSM_LANES // lane_w):
        re = slice(lb * lane_w, (lb + 1) * lane_w)
        im = slice(SSM_LANES + lb * lane_w, SSM_LANES + (lb + 1) * lane_w)
        la_r = lam_ref[0, :, re]
        la_i = lam_ref[1, :, re]
        lb_r = swap(la_r)
        lb_i = swap(la_i)

        def body(m, carry, re=re, im=im, la_r=la_r, la_i=la_i, lb_r=lb_r, lb_i=lb_i):
            sr, si = carry
            rf = pl.multiple_of(m * 2 * half, 2 * half)
            rb = pl.multiple_of((tiles - 1 - m) * 2 * half, 2 * half)
            fr = hf_ref[pl.ds(rf, 2 * half), re]
            fi = hf_ref[pl.ds(rf, 2 * half), im]
            br = hb_ref[pl.ds(rb, 2 * half), re]
            bi = hb_ref[pl.ds(rb, 2 * half), im]
            vr = la_r * sr - la_i * si + jnp.where(top, fr, br)
            vi = la_r * si + la_i * sr + jnp.where(top, fi, bi)
            tr = swap(vr)
            ti = swap(vi)
            wr = lb_r * tr - lb_i * ti + jnp.where(top, br, fr)
            wi = lb_r * ti + lb_i * tr + jnp.where(top, bi, fi)
            hf_ref[pl.ds(rf, 2 * half), re] = jnp.where(top, vr, wr)
            hf_ref[pl.ds(rf, 2 * half), im] = jnp.where(top, vi, wi)
            hb_ref[pl.ds(rb, 2 * half), re] = jnp.where(top, wr, vr)
            hb_ref[pl.ds(rb, 2 * half), im] = jnp.where(top, wi, vi)
            return swap(wr), swap(wi)

        fin = lax.fori_loop(0, tiles, body, (st_ref[0, :, re], st_ref[1, :, re]))
        st_ref[0, :, re] = fin[0]
        st_ref[1, :, re] = fin[1]

    yf_ref[...] = _dot(hf_ref[...].astype(BF16), cmat_ref[0])
    yb_ref[...] = _dot(hb_ref[...].astype(BF16), cmat_ref[1])


def _scan_specs(n_rows):
    n = n_rows // SCAN_ROWS
    fwd = pl.BlockSpec((SCAN_ROWS, SSM_WIDTH), lambda j: (j, 0))
    bwd = pl.BlockSpec((SCAN_ROWS, SSM_WIDTH), lambda j: (n - 1 - j, 0))
    y_shape = jax.ShapeDtypeStruct((n_rows, SSM_WIDTH), F32)
    buf = pltpu.VMEM((SCAN_ROWS, 2 * SSM_LANES), F32)
    return n, fwd, bwd, y_shape, buf


def _scan_ctx(u_rows, bmat, lam, cmat, h0):
    n, fwd, bwd, y_shape, buf = _scan_specs(u_rows.shape[0])
    st_shape = (2, 2, BATCH, SSM_LANES)
    return pl.pallas_call(
        functools.partial(_scan_kernel, nb=BATCH, lane_w=256),
        grid=(n,),
        in_specs=[fwd, bwd, _const_spec(bmat.shape), _const_spec(lam.shape), _const_spec(cmat.shape),
                  _const_spec(st_shape)],
        out_specs=[fwd, bwd, pl.BlockSpec(st_shape, lambda j: (0, 0, 0, 0))],
        out_shape=[y_shape, y_shape, jax.ShapeDtypeStruct(st_shape, F32)],
        scratch_shapes=[buf, buf, pltpu.VMEM(st_shape, F32)],
        compiler_params=_params(1),
        name="scan_ctx",
    )(u_rows, u_rows, bmat, lam, cmat, h0)


def _scan_lat(u_rows, bmat, lam_pair, cmat, h0_pair):
    n, fwd, bwd, y_shape, buf = _scan_specs(u_rows.shape[0])
    st_shape = (2, 2 * DEC_BATCH, SSM_LANES)
    return pl.pallas_call(
        functools.partial(_scan_pair_kernel, lane_w=512),
        grid=(n,),
        in_specs=[fwd, bwd, _const_spec(bmat.shape), _const_spec(st_shape), _const_spec(cmat.shape),
                  _const_spec(st_shape)],
        out_specs=[fwd, bwd],
        out_shape=[y_shape, y_shape],
        scratch_shapes=[buf, buf, pltpu.VMEM(st_shape, F32)],
        compiler_params=_params(1),
        name="scan_lat",
    )(u_rows, u_rows, bmat, lam_pair, cmat, h0_pair)


def _ssm_discretise(lam_re, lam_im, log_step, b_re, b_im, c_re, c_im):
    step = jnp.exp(log_step.astype(F32))[..., None]
    lr, li = lam_re.astype(F32), lam_im.astype(F32)
    mag = jnp.exp(lr * step)
    bar_r = mag * jnp.cos(li * step)
    bar_i = mag * jnp.sin(li * step)
    den = lr * lr + li * li
    coef_r = (((bar_r - 1) * lr + bar_i * li) / den)[..., None]
    coef_i = ((bar_i * lr - (bar_r - 1) * li) / den)[..., None]
    br, bi = b_re.astype(F32), b_im.astype(F32)
    bbar_r = coef_r * br - coef_i * bi
    bbar_i = coef_r * bi + coef_i * br
    eye = jnp.eye(SSM_GROUPS, dtype=F32)
    blk_b = lambda a: jnp.einsum('dgpc,gh->dgchp', a, eye).reshape(2, SSM_WIDTH, SSM_LANES)
    bmat = jnp.concatenate([blk_b(bbar_r), blk_b(bbar_i)], axis=-1)
    blk_c = lambda a: jnp.einsum('dgcp,gh->dgphc', a, eye).reshape(2, SSM_LANES, SSM_WIDTH)
    cmat = jnp.concatenate([blk_c(c_re.astype(F32)), -blk_c(c_im.astype(F32))], axis=1)
    lam_flat = jnp.stack([bar_r, bar_i], axis=1).reshape(2, 2, 1, SSM_LANES)
    return bmat.astype(BF16), lam_flat, cmat.astype(BF16)


def _merge_kernel(x_ref, oa_ref, yf_ref, yb_ref, u_ref, oc_ref, g_ref, mod_ref, d_ref, wglu_ref,
                  wa_ref, wb_ref, wc_ref, wo_ref, g2_ref, wgu_ref, wd_ref, fg_ref, o_ref, slab0, slab1, *, final):
    nb, tt, _ = x_ref.shape
    rows = nb * tt
    flat = lambda ref: ref[...].reshape(rows, ref.shape[-1])
    y = _gelu_tanh(yf_ref[...] + yb_ref[...] + d_ref[...] * u_ref[...])
    ob = y * jax.nn.sigmoid(_dot(y.astype(BF16), wglu_ref[...]))
    ob = _to_batch_major(ob, (slab0, slab1), nb)
    gate = lambda i: jax.nn.sigmoid(g_ref[:, :, i * D_MODEL:(i + 1) * D_MODEL].reshape(rows, D_MODEL))
    merged = (gate(0) * _dot(flat(oa_ref), wa_ref[...])
              + gate(1) * _dot(ob.astype(BF16), wb_ref[...])
              + gate(2) * _dot(flat(oc_ref), wc_ref[...]))
    mod = mod_ref[...]
    gate1 = mod[:, :, 2 * D_MODEL:3 * D_MODEL]
    shift2 = mod[:, :, 3 * D_MODEL:4 * D_MODEL]
    scale2 = mod[:, :, 4 * D_MODEL:5 * D_MODEL]
    gate2 = mod[:, :, 5 * D_MODEL:6 * D_MODEL]
    x1 = x_ref[...] + gate1 * _dot(merged.astype(BF16), wo_ref[...]).reshape(nb, tt, D_MODEL)
    h2 = _rms(x1, g2_ref[...]) * (1 + scale2) + shift2
    gu = _dot(h2.reshape(rows, D_MODEL).astype(BF16), wgu_ref[...])
    act = _silu(gu[:, :D_FF]) * gu[:, D_FF:]
    x2 = x1 + gate2 * _dot(act.astype(BF16), wd_ref[...]).reshape(nb, tt, D_MODEL)
    if final:
        x2 = _rms(x2, fg_ref[...])
    o_ref[...] = x2


def _merge(x, oa, yf, yb, u_rows, oc, g, mod_rows, weights, final_g, layer, *, name):
    final = layer == DEPTH - 1
    nb, seq, _ = x.shape
    tt = ROW_TILE // nb
    slab = lambda w: pl.BlockSpec((nb, tt, w), lambda j: (0, j, 0))
    tmaj = pl.BlockSpec((ROW_TILE, SSM_WIDTH), lambda j: (j, 0))
    return pl.pallas_call(
        functools.partial(_merge_kernel, final=final),
        grid=(seq // tt,),
        in_specs=[slab(D_MODEL), slab(WIDTH_A), tmaj, tmaj, tmaj, slab(WIDTH_C), slab(N_BRANCH * D_MODEL),
                  _const_spec(mod_rows.shape)] + [_layer_spec(w, layer) for w in weights]
                 + [_const_spec(final_g.shape)],
        out_specs=slab(D_MODEL),
        out_shape=jax.ShapeDtypeStruct((nb, seq, D_MODEL), F32),
        scratch_shapes=[pltpu.VMEM((ROW_TILE, LANES), F32)] * 2,
        compiler_params=_params(1),
        name=name + ("_final" if final else ""),
    )(x, oa, yf, yb, u_rows, oc, g, mod_rows, *weights, final_g)


def _rope_tables():
    t = jnp.arange(DEC_SEQ)
    row = (t // GRID_W).astype(F32)
    col = (t % GRID_W).astype(F32)
    inv = 1.0 / (ROPE_THETA ** (jnp.arange(ROT_FREQS, dtype=F32) / ROT_FREQS))
    ar = row[:, None] * inv[None]
    ac = col[:, None] * inv[None]
    cos = jnp.concatenate([jnp.cos(ar), jnp.cos(ar), jnp.cos(ac), jnp.cos(ac)], axis=-1)
    sin = jnp.concatenate([-jnp.sin(ar), jnp.sin(ar), -jnp.sin(ac), jnp.sin(ac)], axis=-1)
    return jnp.tile(cos, (1, 2)), jnp.tile(sin, (1, 2))


def kernel(x_prompt, x_sample, c, cache_ga_k, cache_ga_v, cache_na_k, cache_na_v, state_ssm, c_ctx, w_mod, b_mod, norm1_g, w_in, qn_g, kn_g, ssm_lam_re, ssm_lam_im, ssm_log_step, ssm_b_re, ssm_b_im, ssm_c_re, ssm_c_im, ssm_d, ssm_w_glu, na_bias, w_br_a, w_br_b, w_br_c, w_out, norm2_g, w_ffn_gu, w_ffn_d, final_g):
    cvec = jnp.concatenate([c_ctx[None, :], c, jnp.zeros((N_MOD_ROWS - 1 - DEC_BATCH, D_MODEL), F32)], axis=0)
    mod = _adaln(cvec, w_mod, b_mod).reshape(DEPTH, N_MOD_ROWS, 1, 6 * D_MODEL)

    seg = jnp.kron(jnp.eye(N_HEADS_A, dtype=F32), jnp.full((HEAD_DIM, HEAD_DIM), 1.0 / HEAD_DIM, F32)).astype(BF16)
    rope_tabs = _rope_tables()
    fg = final_g.reshape(1, D_MODEL)
    ck_a = cache_ga_k.reshape(DEC_BATCH, DEPTH, PAST_LEN, KV_WIDTH_A)
    cv_a = cache_ga_v.reshape(DEC_BATCH, DEPTH, PAST_LEN, KV_WIDTH_A)
    ck_c = cache_na_k.reshape(DEC_BATCH, DEPTH, PAST_LEN, WIDTH_C)
    cv_c = cache_na_v.reshape(DEC_BATCH, DEPTH, PAST_LEN, WIDTH_C)
    zero_state = jnp.zeros((2, 2, BATCH, SSM_LANES), F32)

    row = lambda p: p.reshape(DEPTH, 1, p.shape[-1])
    g1 = row(norm1_g)
    w_in_b = w_in.astype(BF16)
    qg = row(jnp.tile(qn_g, (1, N_HEADS_A)))
    kg = row(jnp.tile(kn_g, (1, N_KV_A)))
    merge_w = [row(ssm_d), ssm_w_glu.astype(BF16), w_br_a.astype(BF16), w_br_b.astype(BF16), w_br_c.astype(BF16),
               w_out.astype(BF16), row(norm2_g), w_ffn_gu.astype(BF16), w_ffn_d.astype(BF16)]

    xp, xs = x_prompt, x_sample
    cache = tuple(jnp.zeros((BATCH, DEPTH, SEQ, w), F32) for w in (KV_WIDTH_A, KV_WIDTH_A, WIDTH_C, WIDTH_C))
    ssm_st = []
    for l in range(DEPTH):
        bmat, lam, cmat = _ssm_discretise(ssm_lam_re[l], ssm_lam_im[l], ssm_log_step[l], ssm_b_re[l], ssm_b_im[l],
                                          ssm_c_re[l], ssm_c_im[l])
        mod_ctx = mod[l, 0:1]
        mod_lat = mod[l, 1:1 + DEC_BATCH]

        qa, krep, vrep, ka, va, u_rows, qc, kc, vc, g = _inproj(xp, mod_ctx, g1, w_in_b, seg, qg, kg, l,
                                                                prev_cache=cache)
        cache = (ka, va, kc, vc)
        oa = _attn_a_ctx(qa, krep, vrep)
        oc = _attn_c_ctx(qc, kc, vc, l)
        yf, yb, hfin = _scan_ctx(u_rows, bmat, lam, cmat, zero_state)
        xp = _merge(xp, oa, yf, yb, u_rows, oc, g, mod_ctx, merge_w, fg, l, name="merge_ctx")
        ssm_st.append(jnp.transpose(hfin, (2, 0, 1, 3)))

        qa, krep, vrep, u_rows, qc, kc, vc, g = _inproj(xs, mod_lat, g1, w_in_b, seg, qg, kg, l, rope_tabs=rope_tabs)
        oa = _attn_a_lat(qa, krep, vrep, ck_a, cv_a, l)
        oc = _na_lat(qc, kc, vc, ck_c, cv_c, _na_bias_rows(na_bias[l]), l)
        h0 = jnp.transpose(state_ssm[:, l].reshape(DEC_BATCH, 2, 2, SSM_LANES), (2, 1, 0, 3))
        h0 = h0.reshape(2, 2 * DEC_BATCH, SSM_LANES)
        lam_pair = jnp.broadcast_to(jnp.transpose(lam, (1, 0, 2, 3)), (2, 2, DEC_BATCH, SSM_LANES))
        lam_pair = lam_pair.reshape(2, 2 * DEC_BATCH, SSM_LANES)
        yf, yb = _scan_lat(u_rows, bmat, lam_pair, cmat, h0)
        xs = _merge(xs, oa, yf, yb, u_rows, oc, g, mod_lat, merge_w, fg, l, name="merge_lat")

    ga_k, ga_v, na_k, na_v = cache
    new_ssm = jnp.stack(ssm_st, axis=1).reshape(BATCH, DEPTH, 2, 2, SSM_GROUPS, SSM_STATE)
    return (xp, xs,
            ga_k.reshape(BATCH, DEPTH, SEQ, N_KV_A, HEAD_DIM), ga_v.reshape(BATCH, DEPTH, SEQ, N_KV_A, HEAD_DIM),
            na_k.reshape(BATCH, DEPTH, SEQ, N_HEADS_C, HEAD_DIM), na_v.reshape(BATCH, DEPTH, SEQ, N_HEADS_C, HEAD_DIM),
            new_ssm)
```

```python
import functools
import math

import jax
import jax.numpy as jnp
from jax import lax
from jax.experimental import pallas as pl
from jax.experimental.pallas import tpu as pltpu

D_MODEL = 1024
BATCH = 16
SEQ = 256
DEPTH = 2
DEC_BATCH = 4
DEC_SEQ = 2048
PAST_LEN = 256
GRID_W = 64
GRID_ROWS = DEC_SEQ // GRID_W
HEAD_DIM = 64
N_HEADS_A = 8
N_KV_A = 2
REP_A = N_HEADS_A // N_KV_A
N_HEADS_C = 4
SSM_WIDTH = 256
SSM_GROUP = 16
SSM_GROUPS = SSM_WIDTH // SSM_GROUP
SSM_STATE = 64
SSM_LANES = SSM_GROUPS * SSM_STATE
NA_WIN_R = 8
NA_WIN_C = 16
D_FF = -(-8 * D_MODEL // (3 * 256)) * 256
ROPE_THETA = 10000.0
ROT_HALF = HEAD_DIM // 2
ROT_FREQS = ROT_HALF // 2
WIDTH_A = N_HEADS_A * HEAD_DIM
KV_WIDTH_A = N_KV_A * HEAD_DIM
WIDTH_C = N_HEADS_C * HEAD_DIM
N_BRANCH = 3
IN_WIDTH = WIDTH_A + 2 * KV_WIDTH_A + SSM_WIDTH + 3 * WIDTH_C + N_BRANCH * D_MODEL
EPS = 1e-6

OFF_QA = 0
OFF_KA = OFF_QA + WIDTH_A
OFF_VA = OFF_KA + KV_WIDTH_A
OFF_U = OFF_VA + KV_WIDTH_A
OFF_QC = OFF_U + SSM_WIDTH
OFF_KC = OFF_QC + WIDTH_C
OFF_VC = OFF_KC + WIDTH_C
OFF_G = OFF_VC + WIDTH_C

N_MOD_ROWS = 8
ROW_TILE = 256
Q_TILE_A = 256
NA_Q_ROWS = 2
NA_K_ROWS = 10
NA_R0_MAX = GRID_ROWS - NA_K_ROWS
SCAN_ROWS = 1024
LANES = 128
MXU_DIM = 256
NEG = -1e30
VMEM_LIMIT_V7X = 56 * 1024 * 1024

F32 = jnp.float32
BF16 = jnp.bfloat16


def _dot(a, b):
    return jnp.dot(a, b, preferred_element_type=F32)


def _dot_t(a, b):
    return lax.dot_general(a, b, (((1,), (1,)), ((), ())), preferred_element_type=F32)


def _params(n_axes):
    return pltpu.CompilerParams(dimension_semantics=("arbitrary",) * n_axes,
                                vmem_limit_bytes=VMEM_LIMIT_V7X)


def _const_spec(shape):
    nd = len(shape)
    return pl.BlockSpec(shape, lambda *_: (0,) * nd, pipeline_mode=pl.Buffered(1))


def _layer_spec(stacked, layer):
    shape = stacked.shape[1:]
    return pl.BlockSpec((None,) + shape, lambda *_: (layer,) + (0,) * len(shape), pipeline_mode=pl.Buffered(1))


def _rms(x, g):
    return x * lax.rsqrt(jnp.mean(x * x, axis=-1, keepdims=True) + EPS) * g


def _silu(x):
    return x * jax.nn.sigmoid(x)


def _gelu_tanh(x):
    c = math.sqrt(2.0 / math.pi)
    return x * (0.5 * (1.0 + jnp.tanh(c * (x + 0.044715 * (x * x * x)))))


def _seg_rms(x, seg, g):
    w = seg.shape[0]
    x2 = x * x
    hi = x2.astype(BF16)
    lo = (x2 - hi.astype(F32)).astype(BF16)
    ms = jnp.concatenate([_dot(hi[:, c:c + w], seg) + _dot(lo[:, c:c + w], seg)
                          for c in range(0, x.shape[1], w)], axis=1)
    return x * lax.rsqrt(ms + EPS) * g


def _rope(x, cos, sin_signed):
    w = x.shape[-1]
    lane = lax.broadcasted_iota(jnp.int32, x.shape, 1)
    first = (lane & ROT_FREQS) == 0
    partner = jnp.where(first, pltpu.roll(x, w - ROT_FREQS, 1), pltpu.roll(x, ROT_FREQS, 1))
    return x * cos + partner * sin_signed


def _rep_heads(kv):
    lane = lax.broadcasted_iota(jnp.int32, kv.shape, 1)
    swapped = pltpu.roll(kv, HEAD_DIM, 1)
    lo = lane < HEAD_DIM
    h0 = jnp.where(lo, kv, swapped)
    h1 = jnp.where(lo, swapped, kv)
    return jnp.concatenate([h0, h0, h1, h1], axis=1)


def _to_time_major(val, slabs, nb):
    tt = val.shape[0] // nb
    for s, slab in enumerate(slabs):
        for b in range(nb):
            slab[pl.ds(b, tt, stride=nb), :] = val[b * tt:(b + 1) * tt, s * LANES:(s + 1) * LANES]
    return jnp.concatenate([slab[...] for slab in slabs], axis=1)


def _to_batch_major(val, slabs, nb):
    tt = val.shape[0] // nb
    for s, slab in enumerate(slabs):
        slab[...] = val[:, s * LANES:(s + 1) * LANES]
    return jnp.concatenate(
        [jnp.concatenate([slab[pl.ds(b, tt, stride=nb), :] for slab in slabs], axis=1) for b in range(nb)], axis=0)


def _adaln_kernel(c_ref, w_ref, b_ref, o_ref):
    s = _silu(c_ref[...])
    o_ref[...] = _dot(s.astype(BF16), w_ref[...].astype(BF16)) + b_ref[...]


def _adaln(cvec, w_mod, b_mod):
    n_col = 6 * D_MODEL
    tn = n_col // 4
    return pl.pallas_call(
        _adaln_kernel,
        grid=(DEPTH, n_col // tn),
        in_specs=[pl.BlockSpec((N_MOD_ROWS, D_MODEL), lambda l, n: (0, 0)),
                  pl.BlockSpec((None, D_MODEL, tn), lambda l, n: (l, 0, n)),
                  pl.BlockSpec((None, 1, tn), lambda l, n: (l, 0, n))],
        out_specs=pl.BlockSpec((None, N_MOD_ROWS, tn), lambda l, n: (l, 0, n)),
        out_shape=jax.ShapeDtypeStruct((DEPTH, N_MOD_ROWS, n_col), F32),
        compiler_params=_params(2),
        name="adaln",
    )(cvec, w_mod, b_mod.reshape(DEPTH, 1, n_col))


def _inproj_kernel(*refs, latent, n_alias):
    if latent:
        (x_ref, mod_ref, g1_ref, w_ref, seg_ref, qg_ref, kg_ref, cos_ref, sin_ref,
         qa_ref, krep_ref, vrep_ref, u_ref, qc_ref, kc_ref, vc_ref, g_ref, slab0, slab1) = refs
    else:
        x_ref, mod_ref, g1_ref, w_ref, seg_ref, qg_ref, kg_ref = refs[:7]
        (qa_ref, krep_ref, vrep_ref, ka_ref, va_ref, u_ref, qc_ref, kc_ref, vc_ref, g_ref,
         slab0, slab1) = refs[7 + n_alias:]
    nb, tt, _ = x_ref.shape
    rows = nb * tt
    mod = mod_ref[...]
    shift = mod[:, :, 0:D_MODEL]
    scale = mod[:, :, D_MODEL:2 * D_MODEL]
    h = _rms(x_ref[...], g1_ref[...]) * (1 + scale) + shift
    hb = h.reshape(rows, D_MODEL).astype(BF16)
    seg = seg_ref[...]
    q_scale = HEAD_DIM ** -0.5

    def put(ref, val):
        ref[...] = val.reshape(ref.shape).astype(ref.dtype)

    qa = _seg_rms(_dot(hb, w_ref[:, OFF_QA:OFF_KA]), seg, qg_ref[...])
    ka = _seg_rms(_dot(hb, w_ref[:, OFF_KA:OFF_VA]), seg[0:KV_WIDTH_A, 0:KV_WIDTH_A], kg_ref[...])
    va = _dot(hb, w_ref[:, OFF_VA:OFF_U])
    if latent:
        cos = jnp.concatenate([cos_ref[...]] * nb, axis=0)
        sin = jnp.concatenate([sin_ref[...]] * nb, axis=0)
        qa = _rope(qa, jnp.concatenate([cos] * REP_A, axis=1), jnp.concatenate([sin] * REP_A, axis=1))
        ka = _rope(ka, cos, sin)
    else:
        put(ka_ref, ka)
        put(va_ref, va)
    put(qa_ref, qa * q_scale)
    put(krep_ref, _rep_heads(ka))
    put(vrep_ref, _rep_heads(va))

    u_ref[...] = _to_time_major(_dot(hb, w_ref[:, OFF_U:OFF_QC]), (slab0, slab1), nb)
    put(qc_ref, _dot(hb, w_ref[:, OFF_QC:OFF_KC]) * q_scale)
    put(kc_ref, _dot(hb, w_ref[:, OFF_KC:OFF_VC]))
    put(vc_ref, _dot(hb, w_ref[:, OFF_VC:OFF_G]))
    for cidx in range(N_BRANCH):
        lo = OFF_G + cidx * D_MODEL
        g_ref[:, :, cidx * D_MODEL:(cidx + 1) * D_MODEL] = _dot(hb, w_ref[:, lo:lo + D_MODEL]).reshape(
            nb, tt, D_MODEL)


def _inproj(x, mod_rows, g1, w_in, seg, qg, kg, layer, *, rope_tabs=None, prev_cache=()):
    latent = rope_tabs is not None
    nb, seq, _ = x.shape
    tt = ROW_TILE // nb
    slab = lambda w: pl.BlockSpec((nb, tt, w), lambda j: (0, j, 0))
    act = lambda w, dt: jax.ShapeDtypeStruct((nb, seq, w), dt)
    cache_slab = lambda w: pl.BlockSpec((nb, None, tt, w), lambda j: (0, layer, j, 0))
    cache = lambda w: jax.ShapeDtypeStruct((nb, DEPTH, seq, w), F32)
    in_specs = [slab(D_MODEL), _const_spec(mod_rows.shape), _layer_spec(g1, layer), _layer_spec(w_in, layer),
                _const_spec(seg.shape), _layer_spec(qg, layer), _layer_spec(kg, layer)]
    args = [x, mod_rows, g1, w_in, seg, qg, kg]
    out_shape = [act(WIDTH_A, BF16), act(REP_A * KV_WIDTH_A, BF16), act(REP_A * KV_WIDTH_A, BF16)]
    out_specs = [slab(WIDTH_A), slab(REP_A * KV_WIDTH_A), slab(REP_A * KV_WIDTH_A)]
    aliases = {}
    if latent:
        in_specs += [pl.BlockSpec((tt, 2 * HEAD_DIM), lambda j: (j, 0))] * 2
        args += list(rope_tabs)
        kv_c = [act(WIDTH_C, BF16)] * 2
        kv_c_specs = [slab(WIDTH_C)] * 2
    else:
        aliases = {len(args) + i: o for i, o in enumerate((3, 4, 7, 8)[:len(prev_cache)])}
        in_specs += [pl.BlockSpec(memory_space=pl.ANY)] * len(prev_cache)
        args += list(prev_cache)
        out_shape += [cache(KV_WIDTH_A)] * 2
        out_specs += [cache_slab(KV_WIDTH_A)] * 2
        kv_c = [cache(WIDTH_C)] * 2
        kv_c_specs = [cache_slab(WIDTH_C)] * 2
    out_shape += [jax.ShapeDtypeStruct((seq * nb, SSM_WIDTH), F32),
                  act(WIDTH_C, BF16)] + kv_c + [act(N_BRANCH * D_MODEL, F32)]
    out_specs += [pl.BlockSpec((ROW_TILE, SSM_WIDTH), lambda j: (j, 0)),
                  slab(WIDTH_C)] + kv_c_specs + [slab(N_BRANCH * D_MODEL)]
    return pl.pallas_call(
        functools.partial(_inproj_kernel, latent=latent, n_alias=len(prev_cache)),
        grid=(seq // tt,),
        in_specs=in_specs,
        out_specs=out_specs,
        out_shape=out_shape,
        scratch_shapes=[pltpu.VMEM((ROW_TILE, LANES), F32)] * 2,
        input_output_aliases=aliases,
        compiler_params=_params(1),
        name="inproj_lat" if latent else "inproj_ctx",
    )(*args)


def _head_mask(shape, head):
    lane = lax.broadcasted_iota(jnp.int32, shape, 1)
    return (lane // HEAD_DIM) == head


def _softmax_pv(parts):
    m = None
    for s, _ in parts:
        pm = jnp.max(s, axis=-1, keepdims=True)
        m = pm if m is None else jnp.maximum(m, pm)
    l = None
    o = None
    for s, v in parts:
        p = jnp.exp(s - m)
        pl_ = jnp.sum(p, axis=-1, keepdims=True)
        po = _dot(p.astype(BF16), v)
        l = pl_ if l is None else l + pl_
        o = po if o is None else o + po
    return o / l


def _stacked_heads(q, n_heads, attend, per_chain):
    m_rows = q.shape[0]
    acc = jnp.zeros(q.shape, F32)
    for first in range(0, n_heads, per_chain):
        heads = list(range(first, first + per_chain))
        masks = [_head_mask(q.shape, h) for h in heads]
        qs = jnp.concatenate([jnp.where(hm, q, jnp.zeros_like(q)) for hm in masks], axis=0)
        o_all = attend(qs, heads)
        for i, hm in enumerate(masks):
            acc = jnp.where(hm, o_all[i * m_rows:(i + 1) * m_rows], acc)
    return acc


def _attn_a_kernel(*refs, has_cache):
    if has_cache:
        q_ref, kn_ref, vn_ref, ck_ref, cv_ref, o_ref, k_ref, v_ref = refs

        @pl.when(pl.program_id(1) == 0)
        def _():
            k_ref[0:PAST_LEN, :] = _rep_heads(ck_ref[...]).astype(BF16)
            v_ref[0:PAST_LEN, :] = _rep_heads(cv_ref[...]).astype(BF16)
            k_ref[PAST_LEN:, :] = kn_ref[...]
            v_ref[PAST_LEN:, :] = vn_ref[...]
    else:
        q_ref, k_ref, v_ref, o_ref = refs
    gw = REP_A * HEAD_DIM
    for g in range(N_KV_A):
        sl = slice(g * gw, (g + 1) * gw)
        qg = q_ref[:, sl]
        acc = jnp.zeros(qg.shape, F32)
        for j in range(REP_A):
            hm = _head_mask(qg.shape, j)
            qm = jnp.where(hm, qg, jnp.zeros_like(qg))
            acc = jnp.where(hm, _softmax_pv([(_dot_t(qm, k_ref[:, sl]), v_ref[:, sl])]), acc)
        o_ref[:, sl] = acc.astype(BF16)


def _attn_a_ctx(qa, krep, vrep):
    blk = pl.BlockSpec((None, SEQ, WIDTH_A), lambda b: (b, 0, 0))
    return pl.pallas_call(
        functools.partial(_attn_a_kernel, has_cache=False),
        grid=(BATCH,),
        in_specs=[blk, blk, blk],
        out_specs=blk,
        out_shape=jax.ShapeDtypeStruct((BATCH, SEQ, WIDTH_A), BF16),
        compiler_params=_params(1),
        name="attn_a_ctx",
    )(qa, krep, vrep)


def _attn_a_lat(qa, krep, vrep, cache_k, cache_v, layer):
    seq_blk = pl.BlockSpec((None, DEC_SEQ, WIDTH_A), lambda b, t: (b, 0, 0))
    cache_blk = pl.BlockSpec((None, None, PAST_LEN, KV_WIDTH_A), lambda b, t: (b, layer, 0, 0))
    q_blk = pl.BlockSpec((None, Q_TILE_A, WIDTH_A), lambda b, t: (b, t, 0))
    return pl.pallas_call(
        functools.partial(_attn_a_kernel, has_cache=True),
        grid=(DEC_BATCH, DEC_SEQ // Q_TILE_A),
        in_specs=[q_blk, seq_blk, seq_blk, cache_blk, cache_blk],
        out_specs=q_blk,
        out_shape=jax.ShapeDtypeStruct((DEC_BATCH, DEC_SEQ, WIDTH_A), BF16),
        scratch_shapes=[pltpu.VMEM((PAST_LEN + DEC_SEQ, WIDTH_A), BF16)] * 2,
        compiler_params=_params(2),
        name="attn_a_lat",
    )(qa, krep, vrep, cache_k, cache_v)


def _attn_c_kernel(q_ref, k_ref, v_ref, o_ref):
    k = k_ref[...].astype(BF16)
    v = v_ref[...].astype(BF16)
    attend = lambda qs, heads: _softmax_pv([(_dot_t(qs, k), v)])
    o_ref[...] = _stacked_heads(q_ref[...], N_HEADS_C, attend, N_HEADS_C).astype(BF16)


def _attn_c_ctx(qc, kc, vc, layer):
    blk = pl.BlockSpec((None, SEQ, WIDTH_C), lambda b: (b, 0, 0))
    kv_blk = pl.BlockSpec((None, None, SEQ, WIDTH_C), lambda b: (b, layer, 0, 0))
    return pl.pallas_call(
        _attn_c_kernel,
        grid=(BATCH,),
        in_specs=[blk, kv_blk, kv_blk],
        out_specs=blk,
        out_shape=jax.ShapeDtypeStruct((BATCH, SEQ, WIDTH_C), BF16),
        compiler_params=_params(1),
        name="attn_c_ctx",
    )(qc, kc, vc)


def _na_kernel(q_ref, k_ref, v_ref, ck_ref, cv_ref, brow_ref, o_ref, b2_ref):
    nq = NA_Q_ROWS * GRID_W
    nk = NA_K_ROWS * GRID_W
    n_pair = 2 * NA_WIN_R

    @pl.when((pl.program_id(0) == 0) & (pl.program_id(1) == 0))
    def _():
        shp = (GRID_W, 2 * GRID_W)
        c = lax.broadcasted_iota(jnp.int32, shp, 0)
        kc = lax.broadcasted_iota(jnp.int32, shp, 1) & (GRID_W - 1)
        c0 = jnp.clip(c - NA_WIN_C // 2, 0, GRID_W - NA_WIN_C)
        col_bias = jnp.where(kc < c0, NEG, jnp.where(kc >= c0 + NA_WIN_C, NEG, 0.0))
        for h in range(N_HEADS_C):
            for e in range(n_pair):
                row = jnp.broadcast_to(brow_ref[h, e:e + 1, :], shp)
                toep = pltpu.roll(row, 2 * GRID_W - (NA_WIN_C - 1), 1, stride=1, stride_axis=0)
                b2_ref[h, e] = jnp.where(col_bias < 0.0, NEG, toep)

    qrow0 = NA_Q_ROWS * pl.program_id(1)
    row0 = jnp.clip(qrow0 - NA_WIN_R // 2, 0, NA_R0_MAX)
    start = pl.multiple_of(row0 * GRID_W, GRID_W)
    keys = jnp.concatenate([k_ref[pl.ds(start, nk), :], ck_ref[...].astype(BF16)], axis=0)
    vals = jnp.concatenate([v_ref[pl.ds(start, nk), :], cv_ref[...].astype(BF16)], axis=0)

    qrow = qrow0 + lax.broadcasted_iota(jnp.int32, (nq, nk), 0) // GRID_W
    krow = row0 + lax.broadcasted_iota(jnp.int32, (nq, nk), 1) // GRID_W
    win0 = jnp.clip(qrow - NA_WIN_R // 2, 0, GRID_ROWS - NA_WIN_R)
    row_bias = jnp.where(krow < win0, NEG, jnp.where(krow >= win0 + NA_WIN_R, NEG, 0.0))

    no_bias = jnp.zeros((nq, PAST_LEN), F32)

    def head_bias(h):
        rows = []
        for i in range(NA_Q_ROWS):
            tiles = []
            for m in range(NA_K_ROWS // 2):
                e = jnp.clip(row0 + 2 * m - (qrow0 + i) + NA_WIN_R, 0, n_pair - 1)
                tiles.append(b2_ref[h, e])
            rows.append(jnp.concatenate(tiles, axis=1))
        return jnp.concatenate([jnp.concatenate(rows, axis=0) + row_bias, no_bias], axis=1)

    def attend(qs, heads):
        bias = jnp.concatenate([head_bias(h) for h in heads], axis=0)
        return _softmax_pv([(_dot_t(qs, keys) + bias, vals)])

    o_ref[...] = _stacked_heads(q_ref[...], N_HEADS_C, attend, 2).astype(BF16)


def _na_bias_rows(tbl):
    pad = jnp.pad(tbl, ((0, 0), (1, 1), (0, GRID_W - tbl.shape[-1])))
    return jnp.concatenate([pad[:, :-1], pad[:, 1:]], axis=-1)


def _na_lat(qc, kc, vc, cache_k, cache_v, brow, layer):
    nq = NA_Q_ROWS * GRID_W
    seq_blk = pl.BlockSpec((None, DEC_SEQ, WIDTH_C), lambda b, j: (b, 0, 0))
    cache_blk = pl.BlockSpec((None, None, PAST_LEN, WIDTH_C), lambda b, j: (b, layer, 0, 0))
    q_blk = pl.BlockSpec((None, nq, WIDTH_C), lambda b, j: (b, j, 0))
    return pl.pallas_call(
        _na_kernel,
        grid=(DEC_BATCH, DEC_SEQ // nq),
        in_specs=[q_blk, seq_blk, seq_blk, cache_blk, cache_blk, _const_spec(brow.shape)],
        out_specs=q_blk,
        out_shape=jax.ShapeDtypeStruct((DEC_BATCH, DEC_SEQ, WIDTH_C), BF16),
        scratch_shapes=[pltpu.VMEM((N_HEADS_C, 2 * NA_WIN_R, GRID_W, 2 * GRID_W), F32)],
        compiler_params=_params(2),
        name="na_lat",
    )(qc, kc, vc, cache_k, cache_v, brow)


def _scan_kernel(uf_ref, ub_ref, bmat_ref, lam_ref, cmat_ref, h0_ref, yf_ref, yb_ref, hfin_ref,
                 hf_ref, hb_ref, st_ref, *, nb, lane_w):
    steps = SCAN_ROWS // nb

    @pl.when(pl.program_id(0) == 0)
    def _():
        st_ref[...] = h0_ref[...]

    hf_ref[...] = _dot(uf_ref[...].astype(BF16), bmat_ref[0])
    hb_ref[...] = _dot(ub_ref[...].astype(BF16), bmat_ref[1])

    for lb in range(SSM_LANES // lane_w):
        re = slice(lb * lane_w, (lb + 1) * lane_w)
        im = slice(SSM_LANES + lb * lane_w, SSM_LANES + (lb + 1) * lane_w)
        lam = [[jnp.broadcast_to(lam_ref[d, c, :, re], (nb, lane_w)) for c in range(2)] for d in range(2)]
        init = tuple(st_ref[d, c, :, re] for d in range(2) for c in range(2))

        def step(k, carry, re=re, im=im, lam=lam):
            fr, fi, br, bi = carry
            rf = pl.multiple_of(k * nb, nb)
            rb = pl.multiple_of((steps - 1 - k) * nb, nb)
            nfr = lam[0][0] * fr - lam[0][1] * fi + hf_ref[pl.ds(rf, nb), re]
            nfi = lam[0][0] * fi + lam[0][1] * fr + hf_ref[pl.ds(rf, nb), im]
            nbr = lam[1][0] * br - lam[1][1] * bi + hb_ref[pl.ds(rb, nb), re]
            nbi = lam[1][0] * bi + lam[1][1] * br + hb_ref[pl.ds(rb, nb), im]
            hf_ref[pl.ds(rf, nb), re] = nfr
            hf_ref[pl.ds(rf, nb), im] = nfi
            hb_ref[pl.ds(rb, nb), re] = nbr
            hb_ref[pl.ds(rb, nb), im] = nbi
            return nfr, nfi, nbr, nbi

        fin = lax.fori_loop(0, steps, step, init)
        for d in range(2):
            for c in range(2):
                st_ref[d, c, :, re] = fin[2 * d + c]

    yf_ref[...] = _dot(hf_ref[...].astype(BF16), cmat_ref[0])
    yb_ref[...] = _dot(hb_ref[...].astype(BF16), cmat_ref[1])
    hfin_ref[...] = st_ref[...]


def _scan_pair_kernel(uf_ref, ub_ref, bmat_ref, lam_ref, cmat_ref, h0_ref, yf_ref, yb_ref,
                      hf_ref, hb_ref, st_ref, *, lane_w):
    half = DEC_BATCH
    tiles = SCAN_ROWS // (2 * half)

    @pl.when(pl.program_id(0) == 0)
    def _():
        st_ref[...] = h0_ref[...]

    hf_ref[...] = _dot(uf_ref[...].astype(BF16), bmat_ref[0])
    hb_ref[...] = _dot(ub_ref[...].astype(BF16), bmat_ref[1])

    top = lax.broadcasted_iota(jnp.int32, (2 * half, lane_w), 0) < half
    swap = lambda a: pltpu.roll(a, half, 0)
    for lb in range(SSM_LANES // lane_w):
        re = slice(lb * lane_w, (lb + 1) * lane_w)
        im = slice(SSM_LANES + lb * lane_w, SSM_LANES + (lb + 1) * lane_w)
        la_r = lam_ref[0, :, re]
        la_i = lam_ref[1, :, re]
        lb_r = swap(la_r)
        lb_i = swap(la_i)

        def body(m, carry, re=re, im=im, la_r=la_r, la_i=la_i, lb_r=lb_r, lb_i=lb_i):
            sr, si = carry
            rf = pl.multiple_of(m * 2 * half, 2 * half)
            rb = pl.multiple_of((tiles - 1 - m) * 2 * half, 2 * half)
            fr = hf_ref[pl.ds(rf, 2 * half), re]
            fi = hf_ref[pl.ds(rf, 2 * half), im]
            br = hb_ref[pl.ds(rb, 2 * half), re]
            bi = hb_ref[pl.ds(rb, 2 * half), im]
            vr = la_r * sr - la_i * si + jnp.where(top, fr, br)
            vi = la_r * si + la_i * sr + jnp.where(top, fi, bi)
            tr = swap(vr)
            ti = swap(vi)
            wr = lb_r * tr - lb_i * ti + jnp.where(top, br, fr)
            wi = lb_r * ti + lb_i * tr + jnp.where(top, bi, fi)
            hf_ref[pl.ds(rf, 2 * half), re] = jnp.where(top, vr, wr)
            hf_ref[pl.ds(rf, 2 * half), im] = jnp.where(top, vi, wi)
            hb_ref[pl.ds(rb, 2 * half), re] = jnp.where(top, wr, vr)
            hb_ref[pl.ds(rb, 2 * half), im] = jnp.where(top, wi, vi)
            return swap(wr), swap(wi)

        fin = lax.fori_loop(0, tiles, body, (st_ref[0, :, re], st_ref[1, :, re]))
        st_ref[0, :, re] = fin[0]
        st_ref[1, :, re] = fin[1]

    yf_ref[...] = _dot(hf_ref[...].astype(BF16), cmat_ref[0])
    yb_ref[...] = _dot(hb_ref[...].astype(BF16), cmat_ref[1])


def _scan_specs(n_rows):
    n = n_rows // SCAN_ROWS
    fwd = pl.BlockSpec((SCAN_ROWS, SSM_WIDTH), lambda j: (j, 0))
    bwd = pl.BlockSpec((SCAN_ROWS, SSM_WIDTH), lambda j: (n - 1 - j, 0))
    y_shape = jax.ShapeDtypeStruct((n_rows, SSM_WIDTH), F32)
    buf = pltpu.VMEM((SCAN_ROWS, 2 * SSM_LANES), F32)
    return n, fwd, bwd, y_shape, buf


def _scan_ctx(u_rows, bmat, lam, cmat, h0):
    n, fwd, bwd, y_shape, buf = _scan_specs(u_rows.shape[0])
    st_shape = (2, 2, BATCH, SSM_LANES)
    return pl.pallas_call(
        functools.partial(_scan_kernel, nb=BATCH, lane_w=256),
        grid=(n,),
        in_specs=[fwd, bwd, _const_spec(bmat.shape), _const_spec(lam.shape), _const_spec(cmat.shape),
                  _const_spec(st_shape)],
        out_specs=[fwd, bwd, pl.BlockSpec(st_shape, lambda j: (0, 0, 0, 0))],
        out_shape=[y_shape, y_shape, jax.ShapeDtypeStruct(st_shape, F32)],
        scratch_shapes=[buf, buf, pltpu.VMEM(st_shape, F32)],
        compiler_params=_params(1),
        name="scan_ctx",
    )(u_rows, u_rows, bmat, lam, cmat, h0)


def _scan_lat(u_rows, bmat, lam_pair, cmat, h0_pair):
    n, fwd, bwd, y_shape, buf = _scan_specs(u_rows.shape[0])
    st_shape = (2, 2 * DEC_BATCH, SSM_LANES)
    return pl.pallas_call(
        functools.partial(_scan_pair_kernel, lane_w=512),
        grid=(n,),
        in_specs=[fwd, bwd, _const_spec(bmat.shape), _const_spec(st_shape), _const_spec(cmat.shape),
                  _const_spec(st_shape)],
        out_specs=[fwd, bwd],
        out_shape=[y_shape, y_shape],
        scratch_shapes=[buf, buf, pltpu.VMEM(st_shape, F32)],
        compiler_params=_params(1),
        name="scan_lat",
    )(u_rows, u_rows, bmat, lam_pair, cmat, h0_pair)


def _ssm_discretise(lam_re, lam_im, log_step, b_re, b_im, c_re, c_im):
    step = jnp.exp(log_step.astype(F32))[..., None]
    lr, li = lam_re.astype(F32), lam_im.astype(F32)
    mag = jnp.exp(lr * step)
    bar_r = mag * jnp.cos(li * step)
    bar_i = mag * jnp.sin(li * step)
    den = lr * lr + li * li
    coef_r = (((bar_r - 1) * lr + bar_i * li) / den)[..., None]
    coef_i = ((bar_i * lr - (bar_r - 1) * li) / den)[..., None]
    br, bi = b_re.astype(F32), b_im.astype(F32)
    bbar_r = coef_r * br - coef_i * bi
    bbar_i = coef_r * bi + coef_i * br
    eye = jnp.eye(SSM_GROUPS, dtype=F32)
    blk_b = lambda a: jnp.einsum('dgpc,gh->dgchp', a, eye).reshape(2, SSM_WIDTH, SSM_LANES)
    bmat = jnp.concatenate([blk_b(bbar_r), blk_b(bbar_i)], axis=-1)
    blk_c = lambda a: jnp.einsum('dgcp,gh->dgphc', a, eye).reshape(2, SSM_LANES, SSM_WIDTH)
    cmat = jnp.concatenate([blk_c(c_re.astype(F32)), -blk_c(c_im.astype(F32))], axis=1)
    lam_flat = jnp.stack([bar_r, bar_i], axis=1).reshape(2, 2, 1, SSM_LANES)
    return bmat.astype(BF16), lam_flat, cmat.astype(BF16)


def _merge_kernel(x_ref, oa_ref, yf_ref, yb_ref, u_ref, oc_ref, g_ref, mod_ref, d_ref, wglu_ref,
                  wa_ref, wb_ref, wc_ref, wo_ref, g2_ref, wgu_ref, wd_ref, fg_ref, o_ref, slab0, slab1, *, final):
    nb, tt, _ = x_ref.shape
    rows = nb * tt
    flat = lambda ref: ref[...].reshape(rows, ref.shape[-1])
    y = _gelu_tanh(yf_ref[...] + yb_ref[...] + d_ref[...] * u_ref[...])
    ob = y * jax.nn.sigmoid(_dot(y.astype(BF16), wglu_ref[...]))
    ob = _to_batch_major(ob, (slab0, slab1), nb)
    gate = lambda i: jax.nn.sigmoid(g_ref[:, :, i * D_MODEL:(i + 1) * D_MODEL].reshape(rows, D_MODEL))
    merged = (gate(0) * _dot(flat(oa_ref), wa_ref[...])
              + gate(1) * _dot(ob.astype(BF16), wb_ref[...])
              + gate(2) * _dot(flat(oc_ref), wc_ref[...]))
    mod = mod_ref[...]
    gate1 = mod[:, :, 2 * D_MODEL:3 * D_MODEL]
    shift2 = mod[:, :, 3 * D_MODEL:4 * D_MODEL]
    scale2 = mod[:, :, 4 * D_MODEL:5 * D_MODEL]
    gate2 = mod[:, :, 5 * D_MODEL:6 * D_MODEL]
    x1 = x_ref[...] + gate1 * _dot(merged.astype(BF16), wo_ref[...]).reshape(nb, tt, D_MODEL)
    h2 = _rms(x1, g2_ref[...]) * (1 + scale2) + shift2
    gu = _dot(h2.reshape(rows, D_MODEL).astype(BF16), wgu_ref[...])
    act = _silu(gu[:, :D_FF]) * gu[:, D_FF:]
    x2 = x1 + gate2 * _dot(act.astype(BF16), wd_ref[...]).reshape(nb, tt, D_MODEL)
    if final:
        x2 = _rms(x2, fg_ref[...])
    o_ref[...] = x2


def _merge(x, oa, yf, yb, u_rows, oc, g, mod_rows, weights, final_g, layer, *, name):
    final = layer == DEPTH - 1
    nb, seq, _ = x.shape
    tt = ROW_TILE // nb
    slab = lambda w: pl.BlockSpec((nb, tt, w), lambda j: (0, j, 0))
    tmaj = pl.BlockSpec((ROW_TILE, SSM_WIDTH), lambda j: (j, 0))
    return pl.pallas_call(
        functools.partial(_merge_kernel, final=final),
        grid=(seq // tt,),
        in_specs=[slab(D_MODEL), slab(WIDTH_A), tmaj, tmaj, tmaj, slab(WIDTH_C), slab(N_BRANCH * D_MODEL),
                  _const_spec(mod_rows.shape)] + [_layer_spec(w, layer) for w in weights]
                 + [_const_spec(final_g.shape)],
        out_specs=slab(D_MODEL),
        out_shape=jax.ShapeDtypeStruct((nb, seq, D_MODEL), F32),
        scratch_shapes=[pltpu.VMEM((ROW_TILE, LANES), F32)] * 2,
        compiler_params=_params(1),
        name=name + ("_final" if final else ""),
    )(x, oa, yf, yb, u_rows, oc, g, mod_rows, *weights, final_g)


def _rope_tables():
    t = jnp.arange(DEC_SEQ)
    row = (t // GRID_W).astype(F32)
    col = (t % GRID_W).astype(F32)
    inv = 1.0 / (ROPE_THETA ** (jnp.arange(ROT_FREQS, dtype=F32) / ROT_FREQS))
    ar = row[:, None] * inv[None]
    ac = col[:, None] * inv[None]
    cos = jnp.concatenate([jnp.cos(ar), jnp.cos(ar), jnp.cos(ac), jnp.cos(ac)], axis=-1)
    sin = jnp.concatenate([-jnp.sin(ar), jnp.sin(ar), -jnp.sin(ac), jnp.sin(ac)], axis=-1)
    return jnp.tile(cos, (1, 2)), jnp.tile(sin, (1, 2))


def kernel(x_prompt, x_sample, c, cache_ga_k, cache_ga_v, cache_na_k, cache_na_v, state_ssm, c_ctx, w_mod, b_mod, norm1_g, w_in, qn_g, kn_g, ssm_lam_re, ssm_lam_im, ssm_log_step, ssm_b_re, ssm_b_im, ssm_c_re, ssm_c_im, ssm_d, ssm_w_glu, na_bias, w_br_a, w_br_b, w_br_c, w_out, norm2_g, w_ffn_gu, w_ffn_d, final_g):
    cvec = jnp.concatenate([c_ctx[None, :], c, jnp.zeros((N_MOD_ROWS - 1 - DEC_BATCH, D_MODEL), F32)], axis=0)
    mod = _adaln(cvec, w_mod, b_mod).reshape(DEPTH, N_MOD_ROWS, 1, 6 * D_MODEL)

    seg = jnp.kron(jnp.eye(MXU_DIM // HEAD_DIM, dtype=F32),
                   jnp.full((HEAD_DIM, HEAD_DIM), 1.0 / HEAD_DIM, F32)).astype(BF16)
    rope_tabs = _rope_tables()
    fg = final_g.reshape(1, D_MODEL)
    ck_a = cache_ga_k.reshape(DEC_BATCH, DEPTH, PAST_LEN, KV_WIDTH_A)
    cv_a = cache_ga_v.reshape(DEC_BATCH, DEPTH, PAST_LEN, KV_WIDTH_A)
    ck_c = cache_na_k.reshape(DEC_BATCH, DEPTH, PAST_LEN, WIDTH_C)
    cv_c = cache_na_v.reshape(DEC_BATCH, DEPTH, PAST_LEN, WIDTH_C)
    zero_state = jnp.zeros((2, 2, BATCH, SSM_LANES), F32)

    row = lambda p: p.reshape(DEPTH, 1, p.shape[-1])
    g1 = row(norm1_g)
    w_in_b = w_in.astype(BF16)
    qg = row(jnp.tile(qn_g, (1, N_HEADS_A)))
    kg = row(jnp.tile(kn_g, (1, N_KV_A)))
    merge_w = [row(ssm_d), ssm_w_glu.astype(BF16), w_br_a.astype(BF16), w_br_b.astype(BF16), w_br_c.astype(BF16),
               w_out.astype(BF16), row(norm2_g), w_ffn_gu.astype(BF16), w_ffn_d.astype(BF16)]

    xp, xs = x_prompt, x_sample
    cache = ()
    ssm_st = []
    for l in range(DEPTH):
        bmat, lam, cmat = _ssm_discretise(ssm_lam_re[l], ssm_lam_im[l], ssm_log_step[l], ssm_b_re[l], ssm_b_im[l],
                                          ssm_c_re[l], ssm_c_im[l])
        mod_ctx = mod[l, 0:1]
        mod_lat = mod[l, 1:1 + DEC_BATCH]

        qa, krep, vrep, ka, va, u_rows, qc, kc, vc, g = _inproj(xp, mod_ctx, g1, w_in_b, seg, qg, kg, l,
                                                                prev_cache=cache)
        cache = (ka, va, kc, vc)
        oa = _attn_a_ctx(qa, krep, vrep)
        oc = _attn_c_ctx(qc, kc, vc, l)
        yf, yb, hfin = _scan_ctx(u_rows, bmat, lam, cmat, zero_state)
        xp = _merge(xp, oa, yf, yb, u_rows, oc, g, mod_ctx, merge_w, fg, l, name="merge_ctx")
        ssm_st.append(jnp.transpose(hfin, (2, 0, 1, 3)))

        qa, krep, vrep, u_rows, qc, kc, vc, g = _inproj(xs, mod_lat, g1, w_in_b, seg, qg, kg, l, rope_tabs=rope_tabs)
        oa = _attn_a_lat(qa, krep, vrep, ck_a, cv_a, l)
        oc = _na_lat(qc, kc, vc, ck_c, cv_c, _na_bias_rows(na_bias[l]), l)
        h0 = jnp.transpose(state_ssm[:, l].reshape(DEC_BATCH, 2, 2, SSM_LANES), (2, 1, 0, 3))
        h0 = h0.reshape(2, 2 * DEC_BATCH, SSM_LANES)
        lam_pair = jnp.broadcast_to(jnp.transpose(lam, (1, 0, 2, 3)), (2, 2, DEC_BATCH, SSM_LANES))
        lam_pair = lam_pair.reshape(2, 2 * DEC_BATCH, SSM_LANES)
        yf, yb = _scan_lat(u_rows, bmat, lam_pair, cmat, h0)
        xs = _merge(xs, oa, yf, yb, u_rows, oc, g, mod_lat, merge_w, fg, l, name="merge_lat")

    ga_k, ga_v, na_k, na_v = cache
    new_ssm = jnp.stack(ssm_st, axis=1).reshape(BATCH, DEPTH, 2, 2, SSM_GROUPS, SSM_STATE)
    return (xp, xs,
            ga_k.reshape(BATCH, DEPTH, SEQ, N_KV_A, HEAD_DIM), ga_v.reshape(BATCH, DEPTH, SEQ, N_KV_A, HEAD_DIM),
            na_k.reshape(BATCH, DEPTH, SEQ, N_HEADS_C, HEAD_DIM), na_v.reshape(BATCH, DEPTH, SEQ, N_HEADS_C, HEAD_DIM),
            new_ssm)
```

```python
import functools
import math

import jax
import jax.numpy as jnp
from jax import lax
from jax.experimental import pallas as pl
from jax.experimental.pallas import tpu as pltpu

D_MODEL = 1024
BATCH = 16
SEQ = 256
DEPTH = 2
DEC_BATCH = 4
DEC_SEQ = 2048
PAST_LEN = 256
GRID_W = 64
GRID_ROWS = DEC_SEQ // GRID_W
HEAD_DIM = 64
N_HEADS_A = 8
N_KV_A = 2
REP_A = N_HEADS_A // N_KV_A
N_HEADS_C = 4
SSM_WIDTH = 256
SSM_GROUP = 16
SSM_GROUPS = SSM_WIDTH // SSM_GROUP
SSM_STATE = 64
SSM_LANES = SSM_GROUPS * SSM_STATE
NA_WIN_R = 8
NA_WIN_C = 16
D_FF = -(-8 * D_MODEL // (3 * 256)) * 256
ROPE_THETA = 10000.0
ROT_HALF = HEAD_DIM // 2
ROT_FREQS = ROT_HALF // 2
WIDTH_A = N_HEADS_A * HEAD_DIM
KV_WIDTH_A = N_KV_A * HEAD_DIM
WIDTH_C = N_HEADS_C * HEAD_DIM
N_BRANCH = 3
IN_WIDTH = WIDTH_A + 2 * KV_WIDTH_A + SSM_WIDTH + 3 * WIDTH_C + N_BRANCH * D_MODEL
EPS = 1e-6

OFF_QA = 0
OFF_KA = OFF_QA + WIDTH_A
OFF_VA = OFF_KA + KV_WIDTH_A
OFF_U = OFF_VA + KV_WIDTH_A
OFF_QC = OFF_U + SSM_WIDTH
OFF_KC = OFF_QC + WIDTH_C
OFF_VC = OFF_KC + WIDTH_C
OFF_G = OFF_VC + WIDTH_C

N_MOD_ROWS = 8
ROW_TILE = 256
Q_TILE_A = 256
NA_Q_ROWS = 2
NA_K_ROWS = 10
NA_R0_MAX = GRID_ROWS - NA_K_ROWS
ROW_CHAINS = 2
INPROJ_CHAINS = 1
SCAN_ROWS = 1024
LOG2_E = math.log2(math.e)
Q_SCALE = HEAD_DIM ** -0.5 * LOG2_E
LANES = 128
MXU_DIM = 256
NEG = -1e30
VMEM_LIMIT_V7X = 56 * 1024 * 1024

F32 = jnp.float32
BF16 = jnp.bfloat16


def _dot(a, b):
    return jnp.dot(a, b, preferred_element_type=F32)


def _dot_t(a, b):
    return lax.dot_general(a, b, (((1,), (1,)), ((), ())), preferred_element_type=F32)


def _params(n_axes):
    return pltpu.CompilerParams(dimension_semantics=("arbitrary",) * n_axes,
                                vmem_limit_bytes=VMEM_LIMIT_V7X)


def _const_spec(shape):
    nd = len(shape)
    return pl.BlockSpec(shape, lambda *_: (0,) * nd, pipeline_mode=pl.Buffered(1))


def _layer_spec(stacked, layer):
    shape = stacked.shape[1:]
    return pl.BlockSpec((None,) + shape, lambda *_: (layer,) + (0,) * len(shape), pipeline_mode=pl.Buffered(1))


def _rms(x, g):
    return x * lax.rsqrt(jnp.mean(x * x, axis=-1, keepdims=True) + EPS) * g


def _silu(x):
    return x * jax.nn.sigmoid(x)


def _gelu_tanh(x):
    c = math.sqrt(2.0 / math.pi)
    return x * (0.5 * (1.0 + jnp.tanh(c * (x + 0.044715 * (x * x * x)))))


def _seg_rms(x, seg, g):
    w = seg.shape[0]
    x2 = x * x
    hi = x2.astype(BF16)
    lo = (x2 - hi.astype(F32)).astype(BF16)
    ms = jnp.concatenate([_dot(hi[:, c:c + w], seg) + _dot(lo[:, c:c + w], seg)
                          for c in range(0, x.shape[1], w)], axis=1)
    return x * lax.rsqrt(ms + EPS) * g


def _rope(x, cos, sin_signed):
    w = x.shape[-1]
    lane = lax.broadcasted_iota(jnp.int32, x.shape, 1)
    first = (lane & ROT_FREQS) == 0
    partner = jnp.where(first, pltpu.roll(x, w - ROT_FREQS, 1), pltpu.roll(x, ROT_FREQS, 1))
    return x * cos + partner * sin_signed


def _rep_heads(kv):
    lane = lax.broadcasted_iota(jnp.int32, kv.shape, 1)
    swapped = pltpu.roll(kv, HEAD_DIM, 1)
    lo = lane < HEAD_DIM
    h0 = jnp.where(lo, kv, swapped)
    h1 = jnp.where(lo, swapped, kv)
    return jnp.concatenate([h0, h0, h1, h1], axis=1)


def _round_robin(chains, stagger=0):
    done = [False] * len(chains)
    rnd = 0
    while not all(done):
        for i, ch in enumerate(chains):
            if not done[i] and rnd >= i * stagger:
                try:
                    next(ch)
                except StopIteration:
                    done[i] = True
        rnd += 1


def _to_time_major(val, slabs, nb):
    tt = val.shape[0] // nb
    for s, slab in enumerate(slabs):
        for b in range(nb):
            slab[pl.ds(b, tt, stride=nb), :] = val[b * tt:(b + 1) * tt, s * LANES:(s + 1) * LANES]
    return jnp.concatenate([slab[...] for slab in slabs], axis=1)


def _to_batch_major(val, slabs, nb):
    tt = val.shape[0] // nb
    for s, slab in enumerate(slabs):
        slab[...] = val[:, s * LANES:(s + 1) * LANES]
    return jnp.concatenate(
        [jnp.concatenate([slab[pl.ds(b, tt, stride=nb), :] for slab in slabs], axis=1) for b in range(nb)], axis=0)


def _adaln_kernel(c_ref, w_ref, b_ref, o_ref):
    s = _silu(c_ref[...])
    o_ref[...] = _dot(s.astype(BF16), w_ref[...].astype(BF16)) + b_ref[...]


def _adaln(cvec, w_mod, b_mod):
    n_col = 6 * D_MODEL
    tn = n_col // 4
    return pl.pallas_call(
        _adaln_kernel,
        grid=(DEPTH, n_col // tn),
        in_specs=[pl.BlockSpec((N_MOD_ROWS, D_MODEL), lambda l, n: (0, 0)),
                  pl.BlockSpec((None, D_MODEL, tn), lambda l, n: (l, 0, n)),
                  pl.BlockSpec((None, 1, tn), lambda l, n: (l, 0, n))],
        out_specs=pl.BlockSpec((None, N_MOD_ROWS, tn), lambda l, n: (l, 0, n)),
        out_shape=jax.ShapeDtypeStruct((DEPTH, N_MOD_ROWS, n_col), F32),
        compiler_params=_params(2),
        name="adaln",
    )(cvec, w_mod, b_mod.reshape(DEPTH, 1, n_col))


def _inproj_kernel(*refs, latent, n_alias):
    if latent:
        (x_ref, mod_ref, g1_ref, w_ref, seg_ref, qg_ref, kg_ref, cos_ref, sin_ref,
         qa_ref, krep_ref, vrep_ref, u_ref, qc_ref, kc_ref, vc_ref, g_ref) = refs[:17]
        slabs = refs[17:]
    else:
        x_ref, mod_ref, g1_ref, w_ref, seg_ref, qg_ref, kg_ref = refs[:7]
        (qa_ref, krep_ref, vrep_ref, ka_ref, va_ref, u_ref, qc_ref, kc_ref, vc_ref,
         g_ref) = refs[7 + n_alias:17 + n_alias]
        slabs = refs[17 + n_alias:]
    nb, tt, _ = x_ref.shape
    tc = tt // INPROJ_CHAINS
    rows = nb * tc
    mod = mod_ref[...]
    shift = mod[:, :, 0:D_MODEL]
    scale = mod[:, :, D_MODEL:2 * D_MODEL]
    seg = seg_ref[...]
    q_scale = Q_SCALE

    def chain(c):
        ts = slice(c * tc, (c + 1) * tc)

        def put(ref, val):
            ref[:, ts, :] = val.reshape(nb, tc, ref.shape[-1]).astype(ref.dtype)

        h = _rms(x_ref[:, ts, :], g1_ref[...]) * (1 + scale) + shift
        hb = h.reshape(rows, D_MODEL).astype(BF16)
        yield
        qa = _seg_rms(_dot(hb, w_ref[:, OFF_QA:OFF_KA]), seg, qg_ref[...])
        ka = _seg_rms(_dot(hb, w_ref[:, OFF_KA:OFF_VA]), seg[0:KV_WIDTH_A, 0:KV_WIDTH_A], kg_ref[...])
        va = _dot(hb, w_ref[:, OFF_VA:OFF_U])
        if latent:
            cos = jnp.concatenate([cos_ref[ts, :]] * nb, axis=0)
            sin = jnp.concatenate([sin_ref[ts, :]] * nb, axis=0)
            qa = _rope(qa, jnp.concatenate([cos] * REP_A, axis=1), jnp.concatenate([sin] * REP_A, axis=1))
            ka = _rope(ka, cos, sin)
        else:
            put(ka_ref, ka)
            put(va_ref, va)
        put(qa_ref, qa * q_scale)
        put(krep_ref, _rep_heads(ka))
        put(vrep_ref, _rep_heads(va))
        yield
        u_ref[c * rows:(c + 1) * rows, :] = _to_time_major(_dot(hb, w_ref[:, OFF_U:OFF_QC]),
                                                           slabs[2 * c:2 * c + 2], nb)
        put(qc_ref, _dot(hb, w_ref[:, OFF_QC:OFF_KC]) * q_scale)
        put(kc_ref, _dot(hb, w_ref[:, OFF_KC:OFF_VC]))
        put(vc_ref, _dot(hb, w_ref[:, OFF_VC:OFF_G]))
        yield
        for cidx in range(N_BRANCH):
            lo = OFF_G + cidx * D_MODEL
            g_ref[:, ts, cidx * D_MODEL:(cidx + 1) * D_MODEL] = _dot(hb, w_ref[:, lo:lo + D_MODEL]).reshape(
                nb, tc, D_MODEL)

    _round_robin([chain(c) for c in range(INPROJ_CHAINS)])


def _inproj(x, mod_rows, g1, w_in, seg, qg, kg, layer, *, rope_tabs=None, prev_cache=()):
    latent = rope_tabs is not None
    nb, seq, _ = x.shape
    tt = ROW_TILE // nb
    slab = lambda w: pl.BlockSpec((nb, tt, w), lambda j: (0, j, 0))
    act = lambda w, dt: jax.ShapeDtypeStruct((nb, seq, w), dt)
    cache_slab = lambda w: pl.BlockSpec((nb, None, tt, w), lambda j: (0, layer, j, 0))
    cache = lambda w: jax.ShapeDtypeStruct((nb, DEPTH, seq, w), F32)
    in_specs = [slab(D_MODEL), _const_spec(mod_rows.shape), _layer_spec(g1, layer), _layer_spec(w_in, layer),
                _const_spec(seg.shape), _layer_spec(qg, layer), _layer_spec(kg, layer)]
    args = [x, mod_rows, g1, w_in, seg, qg, kg]
    out_shape = [act(WIDTH_A, BF16), act(REP_A * KV_WIDTH_A, BF16), act(REP_A * KV_WIDTH_A, BF16)]
    out_specs = [slab(WIDTH_A), slab(REP_A * KV_WIDTH_A), slab(REP_A * KV_WIDTH_A)]
    aliases = {}
    if latent:
        in_specs += [pl.BlockSpec((tt, 2 * HEAD_DIM), lambda j: (j, 0))] * 2
        args += list(rope_tabs)
        kv_c = [act(WIDTH_C, BF16)] * 2
        kv_c_specs = [slab(WIDTH_C)] * 2
    else:
        aliases = {len(args) + i: o for i, o in enumerate((3, 4, 7, 8)[:len(prev_cache)])}
        in_specs += [pl.BlockSpec(memory_space=pl.ANY)] * len(prev_cache)
        args += list(prev_cache)
        out_shape += [cache(KV_WIDTH_A)] * 2
        out_specs += [cache_slab(KV_WIDTH_A)] * 2
        kv_c = [cache(WIDTH_C)] * 2
        kv_c_specs = [cache_slab(WIDTH_C)] * 2
    out_shape += [jax.ShapeDtypeStruct((seq * nb, SSM_WIDTH), F32),
                  act(WIDTH_C, BF16)] + kv_c + [act(N_BRANCH * D_MODEL, F32)]
    out_specs += [pl.BlockSpec((ROW_TILE, SSM_WIDTH), lambda j: (j, 0)),
                  slab(WIDTH_C)] + kv_c_specs + [slab(N_BRANCH * D_MODEL)]
    return pl.pallas_call(
        functools.partial(_inproj_kernel, latent=latent, n_alias=len(prev_cache)),
        grid=(seq // tt,),
        in_specs=in_specs,
        out_specs=out_specs,
        out_shape=out_shape,
        scratch_shapes=[pltpu.VMEM((ROW_TILE // INPROJ_CHAINS, LANES), F32)] * (2 * INPROJ_CHAINS),
        input_output_aliases=aliases,
        compiler_params=_params(1),
        name="inproj_lat" if latent else "inproj_ctx",
    )(*args)


def _head_mask(shape, head):
    lane = lax.broadcasted_iota(jnp.int32, shape, 1)
    return (lane // HEAD_DIM) == head


def _softmax_pv(parts):
    m = None
    for s, _ in parts:
        pm = jnp.max(s, axis=-1, keepdims=True)
        m = pm if m is None else jnp.maximum(m, pm)
    l = None
    o = None
    for s, v in parts:
        p = jnp.exp2(s - m)
        pl_ = jnp.sum(p, axis=-1, keepdims=True)
        po = _dot(p.astype(BF16), v)
        l = pl_ if l is None else l + pl_
        o = po if o is None else o + po
    return o / l


def _stacked_heads(q, n_heads, attend, per_chain):
    m_rows = q.shape[0]
    acc = jnp.zeros(q.shape, F32)
    for first in range(0, n_heads, per_chain):
        heads = list(range(first, first + per_chain))
        masks = [_head_mask(q.shape, h) for h in heads]
        qs = jnp.concatenate([jnp.where(hm, q, jnp.zeros_like(q)) for hm in masks], axis=0)
        o_all = attend(qs, heads)
        for i, hm in enumerate(masks):
            acc = jnp.where(hm, o_all[i * m_rows:(i + 1) * m_rows], acc)
    return acc


def _attn_a_kernel(*refs, has_cache):
    if has_cache:
        q_ref, kn_ref, vn_ref, ck_ref, cv_ref, o_ref, k_ref, v_ref = refs

        @pl.when(pl.program_id(1) == 0)
        def _():
            k_ref[0:PAST_LEN, :] = _rep_heads(ck_ref[...]).astype(BF16)
            v_ref[0:PAST_LEN, :] = _rep_heads(cv_ref[...]).astype(BF16)
            k_ref[PAST_LEN:, :] = kn_ref[...]
            v_ref[PAST_LEN:, :] = vn_ref[...]
    else:
        q_ref, k_ref, v_ref, o_ref = refs
    gw = REP_A * HEAD_DIM
    acc = [jnp.zeros((q_ref.shape[0], gw), F32) for _ in range(N_KV_A)]

    def head(g, j):
        sl = slice(g * gw, (g + 1) * gw)
        qg = q_ref[:, sl]
        hm = _head_mask(qg.shape, j)
        s = _dot_t(jnp.where(hm, qg, jnp.zeros_like(qg)), k_ref[:, sl])
        yield
        p = jnp.exp2(s - jnp.max(s, axis=-1, keepdims=True))
        l = jnp.sum(p, axis=-1, keepdims=True)
        p = p.astype(BF16)
        yield
        acc[g] = jnp.where(hm, _dot(p, v_ref[:, sl]) / l, acc[g])

    _round_robin([head(g, j) for g in range(N_KV_A) for j in range(REP_A)], stagger=1)
    for g in range(N_KV_A):
        o_ref[:, g * gw:(g + 1) * gw] = acc[g].astype(BF16)


def _attn_a_ctx(qa, krep, vrep):
    blk = pl.BlockSpec((None, SEQ, WIDTH_A), lambda b: (b, 0, 0))
    return pl.pallas_call(
        functools.partial(_attn_a_kernel, has_cache=False),
        grid=(BATCH,),
        in_specs=[blk, blk, blk],
        out_specs=blk,
        out_shape=jax.ShapeDtypeStruct((BATCH, SEQ, WIDTH_A), BF16),
        compiler_params=_params(1),
        name="attn_a_ctx",
    )(qa, krep, vrep)


def _attn_a_lat(qa, krep, vrep, cache_k, cache_v, layer):
    seq_blk = pl.BlockSpec((None, DEC_SEQ, WIDTH_A), lambda b, t: (b, 0, 0))
    cache_blk = pl.BlockSpec((None, None, PAST_LEN, KV_WIDTH_A), lambda b, t: (b, layer, 0, 0))
    q_blk = pl.BlockSpec((None, Q_TILE_A, WIDTH_A), lambda b, t: (b, t, 0))
    return pl.pallas_call(
        functools.partial(_attn_a_kernel, has_cache=True),
        grid=(DEC_BATCH, DEC_SEQ // Q_TILE_A),
        in_specs=[q_blk, seq_blk, seq_blk, cache_blk, cache_blk],
        out_specs=q_blk,
        out_shape=jax.ShapeDtypeStruct((DEC_BATCH, DEC_SEQ, WIDTH_A), BF16),
        scratch_shapes=[pltpu.VMEM((PAST_LEN + DEC_SEQ, WIDTH_A), BF16)] * 2,
        compiler_params=_params(2),
        name="attn_a_lat",
    )(qa, krep, vrep, cache_k, cache_v)


def _attn_c_kernel(q_ref, k_ref, v_ref, o_ref):
    k = k_ref[...].astype(BF16)
    v = v_ref[...].astype(BF16)
    attend = lambda qs, heads: _softmax_pv([(_dot_t(qs, k), v)])
    o_ref[...] = _stacked_heads(q_ref[...], N_HEADS_C, attend, N_HEADS_C).astype(BF16)


def _attn_c_ctx(qc, kc, vc, layer):
    blk = pl.BlockSpec((None, SEQ, WIDTH_C), lambda b: (b, 0, 0))
    kv_blk = pl.BlockSpec((None, None, SEQ, WIDTH_C), lambda b: (b, layer, 0, 0))
    return pl.pallas_call(
        _attn_c_kernel,
        grid=(BATCH,),
        in_specs=[blk, kv_blk, kv_blk],
        out_specs=blk,
        out_shape=jax.ShapeDtypeStruct((BATCH, SEQ, WIDTH_C), BF16),
        compiler_params=_params(1),
        name="attn_c_ctx",
    )(qc, kc, vc)


def _na_kernel(q_ref, k_ref, v_ref, ck_ref, cv_ref, brow_ref, o_ref, b2_ref):
    nq = NA_Q_ROWS * GRID_W
    nk = NA_K_ROWS * GRID_W
    n_pair = 2 * NA_WIN_R

    @pl.when((pl.program_id(0) == 0) & (pl.program_id(1) == 0))
    def _():
        shp = (GRID_W, 2 * GRID_W)
        c = lax.broadcasted_iota(jnp.int32, shp, 0)
        kc = lax.broadcasted_iota(jnp.int32, shp, 1) & (GRID_W - 1)
        c0 = jnp.clip(c - NA_WIN_C // 2, 0, GRID_W - NA_WIN_C)
        col_bias = jnp.where(kc < c0, NEG, jnp.where(kc >= c0 + NA_WIN_C, NEG, 0.0))
        for h in range(N_HEADS_C):
            for e in range(n_pair):
                row = jnp.broadcast_to(brow_ref[h, e:e + 1, :], shp)
                toep = pltpu.roll(row, 2 * GRID_W - (NA_WIN_C - 1), 1, stride=1, stride_axis=0)
                b2_ref[h, e] = jnp.where(col_bias < 0.0, NEG, toep * LOG2_E)

    qrow0 = NA_Q_ROWS * pl.program_id(1)
    row0 = jnp.clip(qrow0 - NA_WIN_R // 2, 0, NA_R0_MAX)
    start = pl.multiple_of(row0 * GRID_W, GRID_W)
    keys = jnp.concatenate([k_ref[pl.ds(start, nk), :], ck_ref[...].astype(BF16)], axis=0)
    vals = jnp.concatenate([v_ref[pl.ds(start, nk), :], cv_ref[...].astype(BF16)], axis=0)

    qrow = qrow0 + lax.broadcasted_iota(jnp.int32, (nq, nk), 0) // GRID_W
    krow = row0 + lax.broadcasted_iota(jnp.int32, (nq, nk), 1) // GRID_W
    win0 = jnp.clip(qrow - NA_WIN_R // 2, 0, GRID_ROWS - NA_WIN_R)
    row_bias = jnp.where(krow < win0, NEG, jnp.where(krow >= win0 + NA_WIN_R, NEG, 0.0))

    no_bias = jnp.zeros((nq, PAST_LEN), F32)

    def head_bias(h):
        rows = []
        for i in range(NA_Q_ROWS):
            tiles = []
            for m in range(NA_K_ROWS // 2):
                e = jnp.clip(row0 + 2 * m - (qrow0 + i) + NA_WIN_R, 0, n_pair - 1)
                tiles.append(b2_ref[h, e])
            rows.append(jnp.concatenate(tiles, axis=1))
        return jnp.concatenate([jnp.concatenate(rows, axis=0) + row_bias, no_bias], axis=1)

    def attend(qs, heads):
        bias = jnp.concatenate([head_bias(h) for h in heads], axis=0)
        return _softmax_pv([(_dot_t(qs, keys) + bias, vals)])

    o_ref[...] = _stacked_heads(q_ref[...], N_HEADS_C, attend, 2).astype(BF16)


def _na_bias_rows(tbl):
    pad = jnp.pad(tbl, ((0, 0), (1, 1), (0, GRID_W - tbl.shape[-1])))
    return jnp.concatenate([pad[:, :-1], pad[:, 1:]], axis=-1)


def _na_lat(qc, kc, vc, cache_k, cache_v, brow, layer):
    nq = NA_Q_ROWS * GRID_W
    seq_blk = pl.BlockSpec((None, DEC_SEQ, WIDTH_C), lambda b, j: (b, 0, 0))
    cache_blk = pl.BlockSpec((None, None, PAST_LEN, WIDTH_C), lambda b, j: (b, layer, 0, 0))
    q_blk = pl.BlockSpec((None, nq, WIDTH_C), lambda b, j: (b, j, 0))
    return pl.pallas_call(
        _na_kernel,
        grid=(DEC_BATCH, DEC_SEQ // nq),
        in_specs=[q_blk, seq_blk, seq_blk, cache_blk, cache_blk, _const_spec(brow.shape)],
        out_specs=q_blk,
        out_shape=jax.ShapeDtypeStruct((DEC_BATCH, DEC_SEQ, WIDTH_C), BF16),
        scratch_shapes=[pltpu.VMEM((N_HEADS_C, 2 * NA_WIN_R, GRID_W, 2 * GRID_W), F32)],
        compiler_params=_params(2),
        name="na_lat",
    )(qc, kc, vc, cache_k, cache_v, brow)


def _scan_kernel(uf_ref, ub_ref, bmat_ref, lam_ref, cmat_ref, h0_ref, yf_ref, yb_ref, hfin_ref,
                 hf_ref, hb_ref, st_ref, *, nb, lane_w):
    steps = SCAN_ROWS // nb

    @pl.when(pl.program_id(0) == 0)
    def _():
        st_ref[...] = h0_ref[...]

    hf_ref[...] = _dot(uf_ref[...].astype(BF16), bmat_ref[0])
    hb_ref[...] = _dot(ub_ref[...].astype(BF16), bmat_ref[1])

    for lb in range(SSM_LANES // lane_w):
        re = slice(lb * lane_w, (lb + 1) * lane_w)
        im = slice(SSM_LANES + lb * lane_w, SSM_LANES + (lb + 1) * lane_w)
        lam = [[jnp.broadcast_to(lam_ref[d, c, :, re], (nb, lane_w)) for c in range(2)] for d in range(2)]
        init = tuple(st_ref[d, c, :, re] for d in range(2) for c in range(2))

        def step(k, carry, re=re, im=im, lam=lam):
            fr, fi, br, bi = carry
            rf = pl.multiple_of(k * nb, nb)
            rb = pl.multiple_of((steps - 1 - k) * nb, nb)
            nfr = lam[0][0] * fr - lam[0][1] * fi + hf_ref[pl.ds(rf, nb), re]
            nfi = lam[0][0] * fi + lam[0][1] * fr + hf_ref[pl.ds(rf, nb), im]
            nbr = lam[1][0] * br - lam[1][1] * bi + hb_ref[pl.ds(rb, nb), re]
            nbi = lam[1][0] * bi + lam[1][1] * br + hb_ref[pl.ds(rb, nb), im]
            hf_ref[pl.ds(rf, nb), re] = nfr
            hf_ref[pl.ds(rf, nb), im] = nfi
            hb_ref[pl.ds(rb, nb), re] = nbr
            hb_ref[pl.ds(rb, nb), im] = nbi
            return nfr, nfi, nbr, nbi

        fin = lax.fori_loop(0, steps, step, init)
        for d in range(2):
            for c in range(2):
                st_ref[d, c, :, re] = fin[2 * d + c]

    yf_ref[...] = _dot(hf_ref[...].astype(BF16), cmat_ref[0])
    yb_ref[...] = _dot(hb_ref[...].astype(BF16), cmat_ref[1])
    hfin_ref[...] = st_ref[...]


def _scan_pair_kernel(uf_ref, ub_ref, bmat_ref, lam_ref, cmat_ref, h0_ref, yf_ref, yb_ref,
                      hf_ref, hb_ref, st_ref, *, lane_w):
    half = DEC_BATCH
    tiles = SCAN_ROWS // (2 * half)

    @pl.when(pl.program_id(0) == 0)
    def _():
        st_ref[...] = h0_ref[...]

    hf_ref[...] = _dot(uf_ref[...].astype(BF16), bmat_ref[0])
    hb_ref[...] = _dot(ub_ref[...].astype(BF16), bmat_ref[1])

    top = lax.broadcasted_iota(jnp.int32, (2 * half, lane_w), 0) < half
    swap = lambda a: pltpu.roll(a, half, 0)
    for lb in range(SSM_LANES // lane_w):
        re = slice(lb * lane_w, (lb + 1) * lane_w)
        im = slice(SSM_LANES + lb * lane_w, SSM_LANES + (lb + 1) * lane_w)
        la_r = lam_ref[0, :, re]
        la_i = lam_ref[1, :, re]
        lb_r = swap(la_r)
        lb_i = swap(la_i)

        def body(m, carry, re=re, im=im, la_r=la_r, la_i=la_i, lb_r=lb_r, lb_i=lb_i):
            sr, si = carry
            rf = pl.multiple_of(m * 2 * half, 2 * half)
            rb = pl.multiple_of((tiles - 1 - m) * 2 * half, 2 * half)
            fr = hf_ref[pl.ds(rf, 2 * half), re]
            fi = hf_ref[pl.ds(rf, 2 * half), im]
            br = hb_ref[pl.ds(rb, 2 * half), re]
            bi = hb_ref[pl.ds(rb, 2 * half), im]
            vr = la_r * sr - la_i * si + jnp.where(top, fr, br)
            vi = la_r * si + la_i * sr + jnp.where(top, fi, bi)
            tr = swap(vr)
            ti = swap(vi)
            wr = lb_r * tr - lb_i * ti + jnp.where(top, br, fr)
            wi = lb_r * ti + lb_i * tr + jnp.where(top, bi, fi)
            hf_ref[pl.ds(rf, 2 * half), re] = jnp.where(top, vr, wr)
            hf_ref[pl.ds(rf, 2 * half), im] = jnp.where(top, vi, wi)
            hb_ref[pl.ds(rb, 2 * half), re] = jnp.where(top, wr, vr)
            hb_ref[pl.ds(rb, 2 * half), im] = jnp.where(top, wi, vi)
            return swap(wr), swap(wi)

        fin = lax.fori_loop(0, tiles, body, (st_ref[0, :, re], st_ref[1, :, re]))
        st_ref[0, :, re] = fin[0]
        st_ref[1, :, re] = fin[1]

    yf_ref[...] = _dot(hf_ref[...].astype(BF16), cmat_ref[0])
    yb_ref[...] = _dot(hb_ref[...].astype(BF16), cmat_ref[1])


def _scan_specs(n_rows):
    n = n_rows // SCAN_ROWS
    fwd = pl.BlockSpec((SCAN_ROWS, SSM_WIDTH), lambda j: (j, 0))
    bwd = pl.BlockSpec((SCAN_ROWS, SSM_WIDTH), lambda j: (n - 1 - j, 0))
    y_shape = jax.ShapeDtypeStruct((n_rows, SSM_WIDTH), F32)
    buf = pltpu.VMEM((SCAN_ROWS, 2 * SSM_LANES), F32)
    return n, fwd, bwd, y_shape, buf


def _scan_ctx(u_rows, bmat, lam, cmat, h0):
    n, fwd, bwd, y_shape, buf = _scan_specs(u_rows.shape[0])
    st_shape = (2, 2, BATCH, SSM_LANES)
    return pl.pallas_call(
        functools.partial(_scan_kernel, nb=BATCH, lane_w=256),
        grid=(n,),
        in_specs=[fwd, bwd, _const_spec(bmat.shape), _const_spec(lam.shape), _const_spec(cmat.shape),
                  _const_spec(st_shape)],
        out_specs=[fwd, bwd, pl.BlockSpec(st_shape, lambda j: (0, 0, 0, 0))],
        out_shape=[y_shape, y_shape, jax.ShapeDtypeStruct(st_shape, F32)],
        scratch_shapes=[buf, buf, pltpu.VMEM(st_shape, F32)],
        compiler_params=_params(1),
        name="scan_ctx",
    )(u_rows, u_rows, bmat, lam, cmat, h0)


def _scan_lat(u_rows, bmat, lam_pair, cmat, h0_pair):
    n, fwd, bwd, y_shape, buf = _scan_specs(u_rows.shape[0])
    st_shape = (2, 2 * DEC_BATCH, SSM_LANES)
    return pl.pallas_call(
        functools.partial(_scan_pair_kernel, lane_w=512),
        grid=(n,),
        in_specs=[fwd, bwd, _const_spec(bmat.shape), _const_spec(st_shape), _const_spec(cmat.shape),
                  _const_spec(st_shape)],
        out_specs=[fwd, bwd],
        out_shape=[y_shape, y_shape],
        scratch_shapes=[buf, buf, pltpu.VMEM(st_shape, F32)],
        compiler_params=_params(1),
        name="scan_lat",
    )(u_rows, u_rows, bmat, lam_pair, cmat, h0_pair)


def _ssm_discretise(lam_re, lam_im, log_step, b_re, b_im, c_re, c_im):
    step = jnp.exp(log_step.astype(F32))[..., None]
    lr, li = lam_re.astype(F32), lam_im.astype(F32)
    mag = jnp.exp(lr * step)
    bar_r = mag * jnp.cos(li * step)
    bar_i = mag * jnp.sin(li * step)
    den = lr * lr + li * li
    coef_r = (((bar_r - 1) * lr + bar_i * li) / den)[..., None]
    coef_i = ((bar_i * lr - (bar_r - 1) * li) / den)[..., None]
    br, bi = b_re.astype(F32), b_im.astype(F32)
    bbar_r = coef_r * br - coef_i * bi
    bbar_i = coef_r * bi + coef_i * br
    eye = jnp.eye(SSM_GROUPS, dtype=F32)
    blk_b = lambda a: jnp.einsum('dgpc,gh->dgchp', a, eye).reshape(2, SSM_WIDTH, SSM_LANES)
    bmat = jnp.concatenate([blk_b(bbar_r), blk_b(bbar_i)], axis=-1)
    blk_c = lambda a: jnp.einsum('dgcp,gh->dgphc', a, eye).reshape(2, SSM_LANES, SSM_WIDTH)
    cmat = jnp.concatenate([blk_c(c_re.astype(F32)), -blk_c(c_im.astype(F32))], axis=1)
    lam_flat = jnp.stack([bar_r, bar_i], axis=1).reshape(2, 2, 1, SSM_LANES)
    return bmat.astype(BF16), lam_flat, cmat.astype(BF16)


def _merge_kernel(x_ref, oa_ref, yf_ref, yb_ref, u_ref, oc_ref, g_ref, mod_ref, d_ref, wglu_ref,
                  wa_ref, wb_ref, wc_ref, wo_ref, g2_ref, wgu_ref, wd_ref, fg_ref, o_ref, *slabs, final):
    nb, tt, _ = x_ref.shape
    tc = tt // ROW_CHAINS
    rows = nb * tc
    mod = mod_ref[...]
    gate1 = mod[:, :, 2 * D_MODEL:3 * D_MODEL]
    shift2 = mod[:, :, 3 * D_MODEL:4 * D_MODEL]
    scale2 = mod[:, :, 4 * D_MODEL:5 * D_MODEL]
    gate2 = mod[:, :, 5 * D_MODEL:6 * D_MODEL]
    def chain(c):
        ts = slice(c * tc, (c + 1) * tc)
        rs = slice(c * rows, (c + 1) * rows)
        flat = lambda ref, lo=0, hi=None: ref[:, ts, lo:hi].reshape(rows, -1)
        y = _gelu_tanh(yf_ref[rs, :] + yb_ref[rs, :] + d_ref[...] * u_ref[rs, :])
        ob = y * jax.nn.sigmoid(_dot(y.astype(BF16), wglu_ref[...]))
        ob = _to_batch_major(ob, slabs[2 * c:2 * c + 2], nb)
        yield
        gate = lambda i: jax.nn.sigmoid(flat(g_ref, i * D_MODEL, (i + 1) * D_MODEL))
        merged = (gate(0) * _dot(flat(oa_ref), wa_ref[...])
                  + gate(1) * _dot(ob.astype(BF16), wb_ref[...])
                  + gate(2) * _dot(flat(oc_ref), wc_ref[...]))
        yield
        x1 = x_ref[:, ts, :] + gate1 * _dot(merged.astype(BF16), wo_ref[...]).reshape(nb, tc, D_MODEL)
        h2 = _rms(x1, g2_ref[...]) * (1 + scale2) + shift2
        yield
        gu = _dot(h2.reshape(rows, D_MODEL).astype(BF16), wgu_ref[...])
        act = _silu(gu[:, :D_FF]) * gu[:, D_FF:]
        yield
        x2 = x1 + gate2 * _dot(act.astype(BF16), wd_ref[...]).reshape(nb, tc, D_MODEL)
        if final:
            x2 = _rms(x2, fg_ref[...])
        o_ref[:, ts, :] = x2

    _round_robin([chain(c) for c in range(ROW_CHAINS)])


def _merge(x, oa, yf, yb, u_rows, oc, g, mod_rows, weights, final_g, layer, *, name):
    final = layer == DEPTH - 1
    nb, seq, _ = x.shape
    tt = ROW_TILE // nb
    slab = lambda w: pl.BlockSpec((nb, tt, w), lambda j: (0, j, 0))
    tmaj = pl.BlockSpec((ROW_TILE, SSM_WIDTH), lambda j: (j, 0))
    return pl.pallas_call(
        functools.partial(_merge_kernel, final=final),
        grid=(seq // tt,),
        in_specs=[slab(D_MODEL), slab(WIDTH_A), tmaj, tmaj, tmaj, slab(WIDTH_C), slab(N_BRANCH * D_MODEL),
                  _const_spec(mod_rows.shape)] + [_layer_spec(w, layer) for w in weights]
                 + [_const_spec(final_g.shape)],
        out_specs=slab(D_MODEL),
        out_shape=jax.ShapeDtypeStruct((nb, seq, D_MODEL), F32),
        scratch_shapes=[pltpu.VMEM((ROW_TILE // ROW_CHAINS, LANES), F32)] * (2 * ROW_CHAINS),
        compiler_params=_params(1),
        name=name + ("_final" if final else ""),
    )(x, oa, yf, yb, u_rows, oc, g, mod_rows, *weights, final_g)


def _rope_tables():
    t = jnp.arange(DEC_SEQ)
    row = (t // GRID_W).astype(F32)
    col = (t % GRID_W).astype(F32)
    inv = 1.0 / (ROPE_THETA ** (jnp.arange(ROT_FREQS, dtype=F32) / ROT_FREQS))
    ar = row[:, None] * inv[None]
    ac = col[:, None] * inv[None]
    cos = jnp.concatenate([jnp.cos(ar), jnp.cos(ar), jnp.cos(ac), jnp.cos(ac)], axis=-1)
    sin = jnp.concatenate([-jnp.sin(ar), jnp.sin(ar), -jnp.sin(ac), jnp.sin(ac)], axis=-1)
    return jnp.tile(cos, (1, 2)), jnp.tile(sin, (1, 2))


def kernel(x_prompt, x_sample, c, cache_ga_k, cache_ga_v, cache_na_k, cache_na_v, state_ssm, c_ctx, w_mod, b_mod, norm1_g, w_in, qn_g, kn_g, ssm_lam_re, ssm_lam_im, ssm_log_step, ssm_b_re, ssm_b_im, ssm_c_re, ssm_c_im, ssm_d, ssm_w_glu, na_bias, w_br_a, w_br_b, w_br_c, w_out, norm2_g, w_ffn_gu, w_ffn_d, final_g):
    cvec = jnp.concatenate([c_ctx[None, :], c, jnp.zeros((N_MOD_ROWS - 1 - DEC_BATCH, D_MODEL), F32)], axis=0)
    mod = _adaln(cvec, w_mod, b_mod).reshape(DEPTH, N_MOD_ROWS, 1, 6 * D_MODEL)

    seg = jnp.kron(jnp.eye(MXU_DIM // HEAD_DIM, dtype=F32),
                   jnp.full((HEAD_DIM, HEAD_DIM), 1.0 / HEAD_DIM, F32)).astype(BF16)
    rope_tabs = _rope_tables()
    fg = final_g.reshape(1, D_MODEL)
    ck_a = cache_ga_k.reshape(DEC_BATCH, DEPTH, PAST_LEN, KV_WIDTH_A)
    cv_a = cache_ga_v.reshape(DEC_BATCH, DEPTH, PAST_LEN, KV_WIDTH_A)
    ck_c = cache_na_k.reshape(DEC_BATCH, DEPTH, PAST_LEN, WIDTH_C)
    cv_c = cache_na_v.reshape(DEC_BATCH, DEPTH, PAST_LEN, WIDTH_C)
    zero_state = jnp.zeros((2, 2, BATCH, SSM_LANES), F32)

    row = lambda p: p.reshape(DEPTH, 1, p.shape[-1])
    g1 = row(norm1_g)
    w_in_b = w_in.astype(BF16)
    qg = row(jnp.tile(qn_g, (1, N_HEADS_A)))
    kg = row(jnp.tile(kn_g, (1, N_KV_A)))
    merge_w = [row(ssm_d), ssm_w_glu.astype(BF16), w_br_a.astype(BF16), w_br_b.astype(BF16), w_br_c.astype(BF16),
               w_out.astype(BF16), row(norm2_g), w_ffn_gu.astype(BF16), w_ffn_d.astype(BF16)]

    xp, xs = x_prompt, x_sample
    cache = ()
    ssm_st = []
    for l in range(DEPTH):
        bmat, lam, cmat = _ssm_discretise(ssm_lam_re[l], ssm_lam_im[l], ssm_log_step[l], ssm_b_re[l], ssm_b_im[l],
                                          ssm_c_re[l], ssm_c_im[l])
        mod_ctx = mod[l, 0:1]
        mod_lat = mod[l, 1:1 + DEC_BATCH]

        qa, krep, vrep, ka, va, u_rows, qc, kc, vc, g = _inproj(xp, mod_ctx, g1, w_in_b, seg, qg, kg, l,
                                                                prev_cache=cache)
        cache = (ka, va, kc, vc)
        oa = _attn_a_ctx(qa, krep, vrep)
        oc = _attn_c_ctx(qc, kc, vc, l)
        yf, yb, hfin = _scan_ctx(u_rows, bmat, lam, cmat, zero_state)
        xp = _merge(xp, oa, yf, yb, u_rows, oc, g, mod_ctx, merge_w, fg, l, name="merge_ctx")
        ssm_st.append(jnp.transpose(hfin, (2, 0, 1, 3)))

        qa, krep, vrep, u_rows, qc, kc, vc, g = _inproj(xs, mod_lat, g1, w_in_b, seg, qg, kg, l, rope_tabs=rope_tabs)
        oa = _attn_a_lat(qa, krep, vrep, ck_a, cv_a, l)
        oc = _na_lat(qc, kc, vc, ck_c, cv_c, _na_bias_rows(na_bias[l]), l)
        h0 = jnp.transpose(state_ssm[:, l].reshape(DEC_BATCH, 2, 2, SSM_LANES), (2, 1, 0, 3))
        h0 = h0.reshape(2, 2 * DEC_BATCH, SSM_LANES)
        lam_pair = jnp.broadcast_to(jnp.transpose(lam, (1, 0, 2, 3)), (2, 2, DEC_BATCH, SSM_LANES))
        lam_pair = lam_pair.reshape(2, 2 * DEC_BATCH, SSM_LANES)
        yf, yb = _scan_lat(u_rows, bmat, lam_pair, cmat, h0)
        xs = _merge(xs, oa, yf, yb, u_rows, oc, g, mod_lat, merge_w, fg, l, name="merge_lat")

    ga_k, ga_v, na_k, na_v = cache
    new_ssm = jnp.stack(ssm_st, axis=1).reshape(BATCH, DEPTH, 2, 2, SSM_GROUPS, SSM_STATE)
    return (xp, xs,
            ga_k.reshape(BATCH, DEPTH, SEQ, N_KV_A, HEAD_DIM), ga_v.reshape(BATCH, DEPTH, SEQ, N_KV_A, HEAD_DIM),
            na_k.reshape(BATCH, DEPTH, SEQ, N_HEADS_C, HEAD_DIM), na_v.reshape(BATCH, DEPTH, SEQ, N_HEADS_C, HEAD_DIM),
            new_ssm)
```

```python
import functools
import math

import jax
import jax.numpy as jnp
from jax import lax
from jax.experimental import pallas as pl
from jax.experimental.pallas import tpu as pltpu

D_MODEL = 1024
BATCH = 16
SEQ = 256
DEPTH = 2
DEC_BATCH = 4
DEC_SEQ = 2048
PAST_LEN = 256
GRID_W = 64
GRID_ROWS = DEC_SEQ // GRID_W
HEAD_DIM = 64
N_HEADS_A = 8
N_KV_A = 2
REP_A = N_HEADS_A // N_KV_A
N_HEADS_C = 4
SSM_WIDTH = 256
SSM_GROUP = 16
SSM_GROUPS = SSM_WIDTH // SSM_GROUP
SSM_STATE = 64
SSM_LANES = SSM_GROUPS * SSM_STATE
NA_WIN_R = 8
NA_WIN_C = 16
D_FF = -(-8 * D_MODEL // (3 * 256)) * 256
ROPE_THETA = 10000.0
ROT_HALF = HEAD_DIM // 2
ROT_FREQS = ROT_HALF // 2
WIDTH_A = N_HEADS_A * HEAD_DIM
KV_WIDTH_A = N_KV_A * HEAD_DIM
WIDTH_C = N_HEADS_C * HEAD_DIM
N_BRANCH = 3
IN_WIDTH = WIDTH_A + 2 * KV_WIDTH_A + SSM_WIDTH + 3 * WIDTH_C + N_BRANCH * D_MODEL
EPS = 1e-6

OFF_QA = 0
OFF_KA = OFF_QA + WIDTH_A
OFF_VA = OFF_KA + KV_WIDTH_A
OFF_U = OFF_VA + KV_WIDTH_A
OFF_QC = OFF_U + SSM_WIDTH
OFF_KC = OFF_QC + WIDTH_C
OFF_VC = OFF_KC + WIDTH_C
OFF_G = OFF_VC + WIDTH_C

N_MOD_ROWS = 8
ROW_TILE = 256
Q_TILE_A = 256
NA_Q_ROWS = 2
NA_K_ROWS = 10
NA_R0_MAX = GRID_ROWS - NA_K_ROWS
NA_STEP_BLOCKS = 2
ROW_CHAINS = 2
SCAN_ROWS = 1024
LOG2_E = math.log2(math.e)
Q_SCALE = HEAD_DIM ** -0.5 * LOG2_E
LANES = 128
MXU_DIM = 256
NEG = -1e30
VMEM_LIMIT_V7X = 56 * 1024 * 1024

F32 = jnp.float32
BF16 = jnp.bfloat16


def _dot(a, b):
    return jnp.dot(a, b, preferred_element_type=F32)


def _dot_t(a, b):
    return lax.dot_general(a, b, (((1,), (1,)), ((), ())), preferred_element_type=F32)


def _params(n_axes):
    return pltpu.CompilerParams(dimension_semantics=("arbitrary",) * n_axes,
                                vmem_limit_bytes=VMEM_LIMIT_V7X)


def _const_spec(shape):
    nd = len(shape)
    return pl.BlockSpec(shape, lambda *_: (0,) * nd, pipeline_mode=pl.Buffered(1))


def _layer_spec(stacked, layer):
    shape = stacked.shape[1:]
    return pl.BlockSpec((None,) + shape, lambda *_: (layer,) + (0,) * len(shape), pipeline_mode=pl.Buffered(1))


def _rms(x, g):
    return x * lax.rsqrt(jnp.mean(x * x, axis=-1, keepdims=True) + EPS) * g


def _silu(x):
    return x * jax.nn.sigmoid(x)


def _gelu_tanh(x):
    c = math.sqrt(2.0 / math.pi)
    return x * (0.5 * (1.0 + jnp.tanh(c * (x + 0.044715 * (x * x * x)))))


def _seg_rms(x, seg, g):
    w = seg.shape[0]
    x2 = x * x
    hi = x2.astype(BF16)
    lo = (x2 - hi.astype(F32)).astype(BF16)
    ms = jnp.concatenate([_dot(hi[:, c:c + w], seg) + _dot(lo[:, c:c + w], seg)
                          for c in range(0, x.shape[1], w)], axis=1)
    return x * lax.rsqrt(ms + EPS) * g


def _rope(x, cos, sin_signed):
    w = x.shape[-1]
    lane = lax.broadcasted_iota(jnp.int32, x.shape, 1)
    first = (lane & ROT_FREQS) == 0
    partner = jnp.where(first, pltpu.roll(x, w - ROT_FREQS, 1), pltpu.roll(x, ROT_FREQS, 1))
    return x * cos + partner * sin_signed


def _rep_heads(kv):
    lane = lax.broadcasted_iota(jnp.int32, kv.shape, 1)
    swapped = pltpu.roll(kv, HEAD_DIM, 1)
    lo = lane < HEAD_DIM
    h0 = jnp.where(lo, kv, swapped)
    h1 = jnp.where(lo, swapped, kv)
    return jnp.concatenate([h0, h0, h1, h1], axis=1)


def _round_robin(chains, stagger=0):
    done = [False] * len(chains)
    rnd = 0
    while not all(done):
        for i, ch in enumerate(chains):
            if not done[i] and rnd >= i * stagger:
                try:
                    next(ch)
                except StopIteration:
                    done[i] = True
        rnd += 1


def _to_time_major(val, slabs, nb):
    tt = val.shape[0] // nb
    for s, slab in enumerate(slabs):
        for b in range(nb):
            slab[pl.ds(b, tt, stride=nb), :] = val[b * tt:(b + 1) * tt, s * LANES:(s + 1) * LANES]
    return jnp.concatenate([slab[...] for slab in slabs], axis=1)


def _to_batch_major(val, slabs, nb):
    tt = val.shape[0] // nb
    for s, slab in enumerate(slabs):
        slab[...] = val[:, s * LANES:(s + 1) * LANES]
    return jnp.concatenate(
        [jnp.concatenate([slab[pl.ds(b, tt, stride=nb), :] for slab in slabs], axis=1) for b in range(nb)], axis=0)


def _adaln_kernel(c_ref, w_ref, b_ref, o_ref):
    s = _silu(c_ref[...])
    o_ref[...] = _dot(s.astype(BF16), w_ref[...].astype(BF16)) + b_ref[...]


def _adaln(cvec, w_mod, b_mod):
    n_col = 6 * D_MODEL
    tn = n_col // 4
    return pl.pallas_call(
        _adaln_kernel,
        grid=(DEPTH, n_col // tn),
        in_specs=[pl.BlockSpec((N_MOD_ROWS, D_MODEL), lambda l, n: (0, 0)),
                  pl.BlockSpec((None, D_MODEL, tn), lambda l, n: (l, 0, n)),
                  pl.BlockSpec((None, 1, tn), lambda l, n: (l, 0, n))],
        out_specs=pl.BlockSpec((None, N_MOD_ROWS, tn), lambda l, n: (l, 0, n)),
        out_shape=jax.ShapeDtypeStruct((DEPTH, N_MOD_ROWS, n_col), F32),
        compiler_params=_params(2),
        name="adaln",
    )(cvec, w_mod, b_mod.reshape(DEPTH, 1, n_col))


def _inproj_kernel(*refs, latent, n_alias):
    if latent:
        (x_ref, xn_ref, mod_ref, g1_ref, w_ref, seg_ref, qg_ref, kg_ref, cos_ref, sin_ref,
         qa_ref, krep_ref, vrep_ref, u_ref, qc_ref, kc_ref, vc_ref, g_ref, hb_ref, slab0, slab1) = refs
    else:
        x_ref, xn_ref, mod_ref, g1_ref, w_ref, seg_ref, qg_ref, kg_ref = refs[:8]
        (qa_ref, krep_ref, vrep_ref, ka_ref, va_ref, u_ref, qc_ref, kc_ref, vc_ref, g_ref,
         hb_ref, slab0, slab1) = refs[8 + n_alias:]
    nb, tt, _ = x_ref.shape
    rows = nb * tt
    mod = mod_ref[...]
    shift = mod[:, :, 0:D_MODEL]
    scale = mod[:, :, D_MODEL:2 * D_MODEL]
    seg = seg_ref[...]
    slot = pl.program_id(0) % 2

    def normed(ref):
        h = _rms(ref[...], g1_ref[...]) * (1 + scale) + shift
        return h.reshape(rows, D_MODEL).astype(BF16)

    def put(ref, val):
        val = val.reshape(nb, tt, ref.shape[-1]).astype(ref.dtype)
        if len(ref.shape) == 3:
            ref[...] = val
        else:
            ref[:, 0] = val
            ref[:, 1:] = jnp.zeros((nb, DEPTH - 1, tt, ref.shape[-1]), ref.dtype)

    @pl.when(pl.program_id(0) == 0)
    def _():
        hb_ref[0] = normed(x_ref)

    def project():
        hb = hb_ref[slot]
        u_ref[...] = _to_time_major(_dot(hb, w_ref[:, OFF_U:OFF_QC]), (slab0, slab1), nb)
        put(qc_ref, _dot(hb, w_ref[:, OFF_QC:OFF_KC]) * Q_SCALE)
        put(kc_ref, _dot(hb, w_ref[:, OFF_KC:OFF_VC]))
        put(vc_ref, _dot(hb, w_ref[:, OFF_VC:OFF_G]))
        yield
        qa = _seg_rms(_dot(hb, w_ref[:, OFF_QA:OFF_KA]), seg, qg_ref[...])
        ka = _seg_rms(_dot(hb, w_ref[:, OFF_KA:OFF_VA]), seg[0:KV_WIDTH_A, 0:KV_WIDTH_A], kg_ref[...])
        va = _dot(hb, w_ref[:, OFF_VA:OFF_U])
        if latent:
            cos = jnp.concatenate([cos_ref[...]] * nb, axis=0)
            sin = jnp.concatenate([sin_ref[...]] * nb, axis=0)
            qa = _rope(qa, jnp.concatenate([cos] * REP_A, axis=1), jnp.concatenate([sin] * REP_A, axis=1))
            ka = _rope(ka, cos, sin)
        else:
            put(ka_ref, ka)
            put(va_ref, va)
        put(qa_ref, qa * Q_SCALE)
        put(krep_ref, _rep_heads(ka))
        put(vrep_ref, _rep_heads(va))
        yield
        for cidx in range(N_BRANCH):
            lo = OFF_G + cidx * D_MODEL
            g_ref[:, :, cidx * D_MODEL:(cidx + 1) * D_MODEL] = _dot(hb, w_ref[:, lo:lo + D_MODEL]).reshape(
                nb, tt, D_MODEL)

    def prepare_next():
        yield
        hb_ref[1 - slot] = normed(xn_ref)

    _round_robin([project(), prepare_next()])


def _inproj(x, mod_rows, g1, w_in, seg, qg, kg, layer, *, rope_tabs=None, prev_cache=()):
    latent = rope_tabs is not None
    nb, seq, _ = x.shape
    tt = ROW_TILE // nb
    slab = lambda w: pl.BlockSpec((nb, tt, w), lambda j: (0, j, 0))
    act = lambda w, dt: jax.ShapeDtypeStruct((nb, seq, w), dt)
    if layer == 0:
        cache_slab = lambda w: pl.BlockSpec((nb, DEPTH, tt, w), lambda j: (0, 0, j, 0))
    else:
        cache_slab = lambda w: pl.BlockSpec((nb, None, tt, w), lambda j: (0, layer, j, 0))
    cache = lambda w: jax.ShapeDtypeStruct((nb, DEPTH, seq, w), F32)
    n_steps = seq // tt
    next_slab = pl.BlockSpec((nb, tt, D_MODEL), lambda j: (0, jnp.minimum(j + 1, n_steps - 1), 0))
    in_specs = [slab(D_MODEL), next_slab, _const_spec(mod_rows.shape), _layer_spec(g1, layer),
                _layer_spec(w_in, layer), _const_spec(seg.shape), _layer_spec(qg, layer), _layer_spec(kg, layer)]
    args = [x, x, mod_rows, g1, w_in, seg, qg, kg]
    out_shape = [act(WIDTH_A, BF16), act(REP_A * KV_WIDTH_A, BF16), act(REP_A * KV_WIDTH_A, BF16)]
    out_specs = [slab(WIDTH_A), slab(REP_A * KV_WIDTH_A), slab(REP_A * KV_WIDTH_A)]
    aliases = {}
    if latent:
        in_specs += [pl.BlockSpec((tt, 2 * HEAD_DIM), lambda j: (j, 0))] * 2
        args += list(rope_tabs)
        kv_c = [act(WIDTH_C, BF16)] * 2
        kv_c_specs = [slab(WIDTH_C)] * 2
    else:
        aliases = {len(args) + i: o for i, o in enumerate((3, 4, 7, 8)[:len(prev_cache)])}
        in_specs += [pl.BlockSpec(memory_space=pl.ANY)] * len(prev_cache)
        args += list(prev_cache)
        out_shape += [cache(KV_WIDTH_A)] * 2
        out_specs += [cache_slab(KV_WIDTH_A)] * 2
        kv_c = [cache(WIDTH_C)] * 2
        kv_c_specs = [cache_slab(WIDTH_C)] * 2
    out_shape += [jax.ShapeDtypeStruct((seq * nb, SSM_WIDTH), F32),
                  act(WIDTH_C, BF16)] + kv_c + [act(N_BRANCH * D_MODEL, F32)]
    out_specs += [pl.BlockSpec((ROW_TILE, SSM_WIDTH), lambda j: (j, 0)),
                  slab(WIDTH_C)] + kv_c_specs + [slab(N_BRANCH * D_MODEL)]
    return pl.pallas_call(
        functools.partial(_inproj_kernel, latent=latent, n_alias=len(prev_cache)),
        grid=(seq // tt,),
        in_specs=in_specs,
        out_specs=out_specs,
        out_shape=out_shape,
        scratch_shapes=[pltpu.VMEM((2, ROW_TILE, D_MODEL), BF16)] + [pltpu.VMEM((ROW_TILE, LANES), F32)] * 2,
        input_output_aliases=aliases,
        compiler_params=_params(1),
        name="inproj_lat" if latent else "inproj_ctx",
    )(*args)


def _head_mask(shape, head):
    lane = lax.broadcasted_iota(jnp.int32, shape, 1)
    return (lane // HEAD_DIM) == head


def _softmax_pv(parts):
    m = None
    for s, _ in parts:
        pm = jnp.max(s, axis=-1, keepdims=True)
        m = pm if m is None else jnp.maximum(m, pm)
    l = None
    o = None
    for s, v in parts:
        p = jnp.exp2(s - m)
        pl_ = jnp.sum(p, axis=-1, keepdims=True)
        po = _dot(p.astype(BF16), v)
        l = pl_ if l is None else l + pl_
        o = po if o is None else o + po
    return o / l


def _stacked_heads(q, n_heads, attend, per_chain):
    m_rows = q.shape[0]
    acc = jnp.zeros(q.shape, F32)
    for first in range(0, n_heads, per_chain):
        heads = list(range(first, first + per_chain))
        masks = [_head_mask(q.shape, h) for h in heads]
        qs = jnp.concatenate([jnp.where(hm, q, jnp.zeros_like(q)) for hm in masks], axis=0)
        o_all = attend(qs, heads)
        for i, hm in enumerate(masks):
            acc = jnp.where(hm, o_all[i * m_rows:(i + 1) * m_rows], acc)
    return acc


def _attn_a_kernel(*refs, has_cache):
    if has_cache:
        q_ref, kn_ref, vn_ref, ck_ref, cv_ref, o_ref, k_ref, v_ref = refs

        @pl.when(pl.program_id(1) == 0)
        def _():
            k_ref[0:PAST_LEN, :] = _rep_heads(ck_ref[...]).astype(BF16)
            v_ref[0:PAST_LEN, :] = _rep_heads(cv_ref[...]).astype(BF16)
            k_ref[PAST_LEN:, :] = kn_ref[...]
            v_ref[PAST_LEN:, :] = vn_ref[...]
    else:
        q_ref, k_ref, v_ref, o_ref = refs
    gw = REP_A * HEAD_DIM
    acc = [jnp.zeros((q_ref.shape[0], gw), F32) for _ in range(N_KV_A)]

    def head(g, j):
        sl = slice(g * gw, (g + 1) * gw)
        qg = q_ref[:, sl]
        hm = _head_mask(qg.shape, j)
        s = _dot_t(jnp.where(hm, qg, jnp.zeros_like(qg)), k_ref[:, sl])
        yield
        p = jnp.exp2(s - jnp.max(s, axis=-1, keepdims=True))
        l = jnp.sum(p, axis=-1, keepdims=True)
        p = p.astype(BF16)
        yield
        acc[g] = jnp.where(hm, _dot(p, v_ref[:, sl]) / l, acc[g])

    _round_robin([head(g, j) for g in range(N_KV_A) for j in range(REP_A)], stagger=1)
    for g in range(N_KV_A):
        o_ref[:, g * gw:(g + 1) * gw] = acc[g].astype(BF16)


def _attn_a_ctx(qa, krep, vrep):
    blk = pl.BlockSpec((None, SEQ, WIDTH_A), lambda b: (b, 0, 0))
    return pl.pallas_call(
        functools.partial(_attn_a_kernel, has_cache=False),
        grid=(BATCH,),
        in_specs=[blk, blk, blk],
        out_specs=blk,
        out_shape=jax.ShapeDtypeStruct((BATCH, SEQ, WIDTH_A), BF16),
        compiler_params=_params(1),
        name="attn_a_ctx",
    )(qa, krep, vrep)


def _attn_a_lat(qa, krep, vrep, cache_k, cache_v, layer):
    seq_blk = pl.BlockSpec((None, DEC_SEQ, WIDTH_A), lambda b, t: (b, 0, 0))
    cache_blk = pl.BlockSpec((None, None, PAST_LEN, KV_WIDTH_A), lambda b, t: (b, layer, 0, 0))
    q_blk = pl.BlockSpec((None, Q_TILE_A, WIDTH_A), lambda b, t: (b, t, 0))
    return pl.pallas_call(
        functools.partial(_attn_a_kernel, has_cache=True),
        grid=(DEC_BATCH, DEC_SEQ // Q_TILE_A),
        in_specs=[q_blk, seq_blk, seq_blk, cache_blk, cache_blk],
        out_specs=q_blk,
        out_shape=jax.ShapeDtypeStruct((DEC_BATCH, DEC_SEQ, WIDTH_A), BF16),
        scratch_shapes=[pltpu.VMEM((PAST_LEN + DEC_SEQ, WIDTH_A), BF16)] * 2,
        compiler_params=_params(2),
        name="attn_a_lat",
    )(qa, krep, vrep, cache_k, cache_v)


def _attn_c_kernel(q_ref, k_ref, v_ref, o_ref):
    k = k_ref[...].astype(BF16)
    v = v_ref[...].astype(BF16)
    attend = lambda qs, heads: _softmax_pv([(_dot_t(qs, k), v)])
    o_ref[...] = _stacked_heads(q_ref[...], N_HEADS_C, attend, N_HEADS_C).astype(BF16)


def _attn_c_ctx(qc, kc, vc, layer):
    blk = pl.BlockSpec((None, SEQ, WIDTH_C), lambda b: (b, 0, 0))
    kv_blk = pl.BlockSpec((None, None, SEQ, WIDTH_C), lambda b: (b, layer, 0, 0))
    return pl.pallas_call(
        _attn_c_kernel,
        grid=(BATCH,),
        in_specs=[blk, kv_blk, kv_blk],
        out_specs=blk,
        out_shape=jax.ShapeDtypeStruct((BATCH, SEQ, WIDTH_C), BF16),
        compiler_params=_params(1),
        name="attn_c_ctx",
    )(qc, kc, vc)


def _na_kernel(q_ref, k_ref, v_ref, ck_ref, cv_ref, brow_ref, o_ref, b2_ref):
    nq = NA_Q_ROWS * GRID_W
    nk = NA_K_ROWS * GRID_W
    n_pair = 2 * NA_WIN_R

    @pl.when((pl.program_id(0) == 0) & (pl.program_id(1) == 0))
    def _():
        shp = (GRID_W, 2 * GRID_W)
        c = lax.broadcasted_iota(jnp.int32, shp, 0)
        kc = lax.broadcasted_iota(jnp.int32, shp, 1) & (GRID_W - 1)
        c0 = jnp.clip(c - NA_WIN_C // 2, 0, GRID_W - NA_WIN_C)
        col_bias = jnp.where(kc < c0, NEG, jnp.where(kc >= c0 + NA_WIN_C, NEG, 0.0))
        for h in range(N_HEADS_C):
            for e in range(n_pair):
                row = jnp.broadcast_to(brow_ref[h, e:e + 1, :], shp)
                toep = pltpu.roll(row, 2 * GRID_W - (NA_WIN_C - 1), 1, stride=1, stride_axis=0)
                b2_ref[h, e] = jnp.where(col_bias < 0.0, NEG, toep * LOG2_E)

    no_bias = jnp.zeros((nq, PAST_LEN), F32)
    ck = ck_ref[...].astype(BF16)
    cv = cv_ref[...].astype(BF16)
    acc = [jnp.zeros((nq, WIDTH_C), F32) for _ in range(NA_STEP_BLOCKS)]

    def chain(blk, heads):
        qrow0 = NA_Q_ROWS * (NA_STEP_BLOCKS * pl.program_id(1) + blk)
        row0 = jnp.clip(qrow0 - NA_WIN_R // 2, 0, NA_R0_MAX)
        start = pl.multiple_of(row0 * GRID_W, GRID_W)
        qrow = qrow0 + lax.broadcasted_iota(jnp.int32, (nq, nk), 0) // GRID_W
        krow = row0 + lax.broadcasted_iota(jnp.int32, (nq, nk), 1) // GRID_W
        win0 = jnp.clip(qrow - NA_WIN_R // 2, 0, GRID_ROWS - NA_WIN_R)
        row_bias = jnp.where(krow < win0, NEG, jnp.where(krow >= win0 + NA_WIN_R, NEG, 0.0))

        def head_bias(h):
            rows = []
            for i in range(NA_Q_ROWS):
                tiles = []
                for m in range(NA_K_ROWS // 2):
                    e = jnp.clip(row0 + 2 * m - (qrow0 + i) + NA_WIN_R, 0, n_pair - 1)
                    tiles.append(b2_ref[h, e])
                rows.append(jnp.concatenate(tiles, axis=1))
            return jnp.concatenate([jnp.concatenate(rows, axis=0) + row_bias, no_bias], axis=1)

        q = q_ref[blk * nq:(blk + 1) * nq, :]
        masks = [_head_mask(q.shape, h) for h in heads]
        qs = jnp.concatenate([jnp.where(hm, q, jnp.zeros_like(q)) for hm in masks], axis=0)
        keys = jnp.concatenate([k_ref[pl.ds(start, nk), :], ck], axis=0)
        s = _dot_t(qs, keys) + jnp.concatenate([head_bias(h) for h in heads], axis=0)
        yield
        p = jnp.exp2(s - jnp.max(s, axis=-1, keepdims=True))
        l = jnp.sum(p, axis=-1, keepdims=True)
        p = p.astype(BF16)
        yield
        o = _dot(p, jnp.concatenate([v_ref[pl.ds(start, nk), :], cv], axis=0)) / l
        for i, hm in enumerate(masks):
            acc[blk] = jnp.where(hm, o[i * nq:(i + 1) * nq], acc[blk])

    pairs = [list(range(h, h + 2)) for h in range(0, N_HEADS_C, 2)]
    _round_robin([chain(blk, heads) for blk in range(NA_STEP_BLOCKS) for heads in pairs], stagger=1)
    for blk in range(NA_STEP_BLOCKS):
        o_ref[blk * nq:(blk + 1) * nq, :] = acc[blk].astype(BF16)


def _na_bias_rows(tbl):
    pad = jnp.pad(tbl, ((0, 0), (1, 1), (0, GRID_W - tbl.shape[-1])))
    return jnp.concatenate([pad[:, :-1], pad[:, 1:]], axis=-1)


def _na_lat(qc, kc, vc, cache_k, cache_v, brow, layer):
    step_rows = NA_STEP_BLOCKS * NA_Q_ROWS * GRID_W
    seq_blk = pl.BlockSpec((None, DEC_SEQ, WIDTH_C), lambda b, j: (b, 0, 0))
    cache_blk = pl.BlockSpec((None, None, PAST_LEN, WIDTH_C), lambda b, j: (b, layer, 0, 0))
    q_blk = pl.BlockSpec((None, step_rows, WIDTH_C), lambda b, j: (b, j, 0))
    return pl.pallas_call(
        _na_kernel,
        grid=(DEC_BATCH, DEC_SEQ // step_rows),
        in_specs=[q_blk, seq_blk, seq_blk, cache_blk, cache_blk, _const_spec(brow.shape)],
        out_specs=q_blk,
        out_shape=jax.ShapeDtypeStruct((DEC_BATCH, DEC_SEQ, WIDTH_C), BF16),
        scratch_shapes=[pltpu.VMEM((N_HEADS_C, 2 * NA_WIN_R, GRID_W, 2 * GRID_W), F32)],
        compiler_params=_params(2),
        name="na_lat",
    )(qc, kc, vc, cache_k, cache_v, brow)


def _scan_kernel(uf_ref, ub_ref, bmat_ref, lam_ref, cmat_ref, h0_ref, yf_ref, yb_ref, hfin_ref,
                 hf_ref, hb_ref, st_ref, *, nb, lane_w):
    steps = SCAN_ROWS // nb

    @pl.when(pl.program_id(0) == 0)
    def _():
        st_ref[...] = h0_ref[...]

    hf_ref[...] = _dot(uf_ref[...].astype(BF16), bmat_ref[0])
    hb_ref[...] = _dot(ub_ref[...].astype(BF16), bmat_ref[1])

    for lb in range(SSM_LANES // lane_w):
        re = slice(lb * lane_w, (lb + 1) * lane_w)
        im = slice(SSM_LANES + lb * lane_w, SSM_LANES + (lb + 1) * lane_w)
        lam = [[jnp.broadcast_to(lam_ref[d, c, :, re], (nb, lane_w)) for c in range(2)] for d in range(2)]
        init = tuple(st_ref[d, c, :, re] for d in range(2) for c in range(2))

        def step(k, carry, re=re, im=im, lam=lam):
            fr, fi, br, bi = carry
            rf = pl.multiple_of(k * nb, nb)
            rb = pl.multiple_of((steps - 1 - k) * nb, nb)
            nfr = lam[0][0] * fr - lam[0][1] * fi + hf_ref[pl.ds(rf, nb), re]
            nfi = lam[0][0] * fi + lam[0][1] * fr + hf_ref[pl.ds(rf, nb), im]
            nbr = lam[1][0] * br - lam[1][1] * bi + hb_ref[pl.ds(rb, nb), re]
            nbi = lam[1][0] * bi + lam[1][1] * br + hb_ref[pl.ds(rb, nb), im]
            hf_ref[pl.ds(rf, nb), re] = nfr
            hf_ref[pl.ds(rf, nb), im] = nfi
            hb_ref[pl.ds(rb, nb), re] = nbr
            hb_ref[pl.ds(rb, nb), im] = nbi
            return nfr, nfi, nbr, nbi

        fin = lax.fori_loop(0, steps, step, init)
        for d in range(2):
            for c in range(2):
                st_ref[d, c, :, re] = fin[2 * d + c]

    yf_ref[...] = _dot(hf_ref[...].astype(BF16), cmat_ref[0])
    yb_ref[...] = _dot(hb_ref[...].astype(BF16), cmat_ref[1])
    hfin_ref[...] = st_ref[...]


def _scan_pair_kernel(uf_ref, ub_ref, bmat_ref, lam_ref, cmat_ref, h0_ref, yf_ref, yb_ref,
                      hf_ref, hb_ref, st_ref, *, lane_w):
    half = DEC_BATCH
    tiles = SCAN_ROWS // (2 * half)

    @pl.when(pl.program_id(0) == 0)
    def _():
        st_ref[...] = h0_ref[...]

    hf_ref[...] = _dot(uf_ref[...].astype(BF16), bmat_ref[0])
    hb_ref[...] = _dot(ub_ref[...].astype(BF16), bmat_ref[1])

    top = lax.broadcasted_iota(jnp.int32, (2 * half, lane_w), 0) < half
    swap = lambda a: pltpu.roll(a, half, 0)
    for lb in range(SSM_LANES // lane_w):
        re = slice(lb * lane_w, (lb + 1) * lane_w)
        im = slice(SSM_LANES + lb * lane_w, SSM_LANES + (lb + 1) * lane_w)
        la_r = lam_ref[0, :, re]
        la_i = lam_ref[1, :, re]
        lb_r = swap(la_r)
        lb_i = swap(la_i)

        def body(m, carry, re=re, im=im, la_r=la_r, la_i=la_i, lb_r=lb_r, lb_i=lb_i):
            sr, si = carry
            rf = pl.multiple_of(m * 2 * half, 2 * half)
            rb = pl.multiple_of((tiles - 1 - m) * 2 * half, 2 * half)
            fr = hf_ref[pl.ds(rf, 2 * half), re]
            fi = hf_ref[pl.ds(rf, 2 * half), im]
            br = hb_ref[pl.ds(rb, 2 * half), re]
            bi = hb_ref[pl.ds(rb, 2 * half), im]
            vr = la_r * sr - la_i * si + jnp.where(top, fr, br)
            vi = la_r * si + la_i * sr + jnp.where(top, fi, bi)
            tr = swap(vr)
            ti = swap(vi)
            wr = lb_r * tr - lb_i * ti + jnp.where(top, br, fr)
            wi = lb_r * ti + lb_i * tr + jnp.where(top, bi, fi)
            hf_ref[pl.ds(rf, 2 * half), re] = jnp.where(top, vr, wr)
            hf_ref[pl.ds(rf, 2 * half), im] = jnp.where(top, vi, wi)
            hb_ref[pl.ds(rb, 2 * half), re] = jnp.where(top, wr, vr)
            hb_ref[pl.ds(rb, 2 * half), im] = jnp.where(top, wi, vi)
            return swap(wr), swap(wi)

        fin = lax.fori_loop(0, tiles, body, (st_ref[0, :, re], st_ref[1, :, re]))
        st_ref[0, :, re] = fin[0]
        st_ref[1, :, re] = fin[1]

    yf_ref[...] = _dot(hf_ref[...].astype(BF16), cmat_ref[0])
    yb_ref[...] = _dot(hb_ref[...].astype(BF16), cmat_ref[1])


def _scan_specs(n_rows):
    n = n_rows // SCAN_ROWS
    fwd = pl.BlockSpec((SCAN_ROWS, SSM_WIDTH), lambda j: (j, 0))
    bwd = pl.BlockSpec((SCAN_ROWS, SSM_WIDTH), lambda j: (n - 1 - j, 0))
    y_shape = jax.ShapeDtypeStruct((n_rows, SSM_WIDTH), F32)
    buf = pltpu.VMEM((SCAN_ROWS, 2 * SSM_LANES), F32)
    return n, fwd, bwd, y_shape, buf


def _scan_ctx(u_rows, bmat, lam, cmat, h0):
    n, fwd, bwd, y_shape, buf = _scan_specs(u_rows.shape[0])
    st_shape = (2, 2, BATCH, SSM_LANES)
    return pl.pallas_call(
        functools.partial(_scan_kernel, nb=BATCH, lane_w=256),
        grid=(n,),
        in_specs=[fwd, bwd, _const_spec(bmat.shape), _const_spec(lam.shape), _const_spec(cmat.shape),
                  _const_spec(st_shape)],
        out_specs=[fwd, bwd, pl.BlockSpec(st_shape, lambda j: (0, 0, 0, 0))],
        out_shape=[y_shape, y_shape, jax.ShapeDtypeStruct(st_shape, F32)],
        scratch_shapes=[buf, buf, pltpu.VMEM(st_shape, F32)],
        compiler_params=_params(1),
        name="scan_ctx",
    )(u_rows, u_rows, bmat, lam, cmat, h0)


def _scan_lat(u_rows, bmat, lam_pair, cmat, h0_pair):
    n, fwd, bwd, y_shape, buf = _scan_specs(u_rows.shape[0])
    st_shape = (2, 2 * DEC_BATCH, SSM_LANES)
    return pl.pallas_call(
        functools.partial(_scan_pair_kernel, lane_w=512),
        grid=(n,),
        in_specs=[fwd, bwd, _const_spec(bmat.shape), _const_spec(st_shape), _const_spec(cmat.shape),
                  _const_spec(st_shape)],
        out_specs=[fwd, bwd],
        out_shape=[y_shape, y_shape],
        scratch_shapes=[buf, buf, pltpu.VMEM(st_shape, F32)],
        compiler_params=_params(1),
        name="scan_lat",
    )(u_rows, u_rows, bmat, lam_pair, cmat, h0_pair)


def _ssm_discretise(lam_re, lam_im, log_step, b_re, b_im, c_re, c_im):
    step = jnp.exp(log_step.astype(F32))[..., None]
    lr, li = lam_re.astype(F32), lam_im.astype(F32)
    mag = jnp.exp(lr * step)
    bar_r = mag * jnp.cos(li * step)
    bar_i = mag * jnp.sin(li * step)
    den = lr * lr + li * li
    coef_r = (((bar_r - 1) * lr + bar_i * li) / den)[..., None]
    coef_i = ((bar_i * lr - (bar_r - 1) * li) / den)[..., None]
    br, bi = b_re.astype(F32), b_im.astype(F32)
    bbar_r = coef_r * br - coef_i * bi
    bbar_i = coef_r * bi + coef_i * br
    eye = jnp.eye(SSM_GROUPS, dtype=F32)
    blk_b = lambda a: jnp.einsum('dgpc,gh->dgchp', a, eye).reshape(2, SSM_WIDTH, SSM_LANES)
    bmat = jnp.concatenate([blk_b(bbar_r), blk_b(bbar_i)], axis=-1)
    blk_c = lambda a: jnp.einsum('dgcp,gh->dgphc', a, eye).reshape(2, SSM_LANES, SSM_WIDTH)
    cmat = jnp.concatenate([blk_c(c_re.astype(F32)), -blk_c(c_im.astype(F32))], axis=1)
    lam_flat = jnp.stack([bar_r, bar_i], axis=1).reshape(2, 2, 1, SSM_LANES)
    return bmat.astype(BF16), lam_flat, cmat.astype(BF16)


def _merge_kernel(x_ref, oa_ref, yf_ref, yb_ref, u_ref, oc_ref, g_ref, mod_ref, d_ref, wglu_ref,
                  wa_ref, wb_ref, wc_ref, wo_ref, g2_ref, wgu_ref, wd_ref, fg_ref, o_ref, *slabs, final):
    nb, tt, _ = x_ref.shape
    tc = tt // ROW_CHAINS
    rows = nb * tc
    mod = mod_ref[...]
    gate1 = mod[:, :, 2 * D_MODEL:3 * D_MODEL]
    shift2 = mod[:, :, 3 * D_MODEL:4 * D_MODEL]
    scale2 = mod[:, :, 4 * D_MODEL:5 * D_MODEL]
    gate2 = mod[:, :, 5 * D_MODEL:6 * D_MODEL]
    def chain(c):
        ts = slice(c * tc, (c + 1) * tc)
        rs = slice(c * rows, (c + 1) * rows)
        flat = lambda ref, lo=0, hi=None: ref[:, ts, lo:hi].reshape(rows, -1)
        y = _gelu_tanh(yf_ref[rs, :] + yb_ref[rs, :] + d_ref[...] * u_ref[rs, :])
        ob = y * jax.nn.sigmoid(_dot(y.astype(BF16), wglu_ref[...]))
        ob = _to_batch_major(ob, slabs[2 * c:2 * c + 2], nb)
        yield
        gate = lambda i: jax.nn.sigmoid(flat(g_ref, i * D_MODEL, (i + 1) * D_MODEL))
        merged = (gate(0) * _dot(flat(oa_ref), wa_ref[...])
                  + gate(1) * _dot(ob.astype(BF16), wb_ref[...])
                  + gate(2) * _dot(flat(oc_ref), wc_ref[...]))
        yield
        x1 = x_ref[:, ts, :] + gate1 * _dot(merged.astype(BF16), wo_ref[...]).reshape(nb, tc, D_MODEL)
        h2 = _rms(x1, g2_ref[...]) * (1 + scale2) + shift2
        yield
        gu = _dot(h2.reshape(rows, D_MODEL).astype(BF16), wgu_ref[...])
        act = _silu(gu[:, :D_FF]) * gu[:, D_FF:]
        yield
        x2 = x1 + gate2 * _dot(act.astype(BF16), wd_ref[...]).reshape(nb, tc, D_MODEL)
        if final:
            x2 = _rms(x2, fg_ref[...])
        o_ref[:, ts, :] = x2

    _round_robin([chain(c) for c in range(ROW_CHAINS)])


def _merge(x, oa, yf, yb, u_rows, oc, g, mod_rows, weights, final_g, layer, *, name):
    final = layer == DEPTH - 1
    nb, seq, _ = x.shape
    tt = ROW_TILE // nb
    slab = lambda w: pl.BlockSpec((nb, tt, w), lambda j: (0, j, 0))
    tmaj = pl.BlockSpec((ROW_TILE, SSM_WIDTH), lambda j: (j, 0))
    return pl.pallas_call(
        functools.partial(_merge_kernel, final=final),
        grid=(seq // tt,),
        in_specs=[slab(D_MODEL), slab(WIDTH_A), tmaj, tmaj, tmaj, slab(WIDTH_C), slab(N_BRANCH * D_MODEL),
                  _const_spec(mod_rows.shape)] + [_layer_spec(w, layer) for w in weights]
                 + [_const_spec(final_g.shape)],
        out_specs=slab(D_MODEL),
        out_shape=jax.ShapeDtypeStruct((nb, seq, D_MODEL), F32),
        scratch_shapes=[pltpu.VMEM((ROW_TILE // ROW_CHAINS, LANES), F32)] * (2 * ROW_CHAINS),
        compiler_params=_params(1),
        name=name + ("_final" if final else ""),
    )(x, oa, yf, yb, u_rows, oc, g, mod_rows, *weights, final_g)


def _rope_tables():
    t = jnp.arange(DEC_SEQ)
    row = (t // GRID_W).astype(F32)
    col = (t % GRID_W).astype(F32)
    inv = 1.0 / (ROPE_THETA ** (jnp.arange(ROT_FREQS, dtype=F32) / ROT_FREQS))
    ar = row[:, None] * inv[None]
    ac = col[:, None] * inv[None]
    cos = jnp.concatenate([jnp.cos(ar), jnp.cos(ar), jnp.cos(ac), jnp.cos(ac)], axis=-1)
    sin = jnp.concatenate([-jnp.sin(ar), jnp.sin(ar), -jnp.sin(ac), jnp.sin(ac)], axis=-1)
    return jnp.tile(cos, (1, 2)), jnp.tile(sin, (1, 2))


def kernel(x_prompt, x_sample, c, cache_ga_k, cache_ga_v, cache_na_k, cache_na_v, state_ssm, c_ctx, w_mod, b_mod, norm1_g, w_in, qn_g, kn_g, ssm_lam_re, ssm_lam_im, ssm_log_step, ssm_b_re, ssm_b_im, ssm_c_re, ssm_c_im, ssm_d, ssm_w_glu, na_bias, w_br_a, w_br_b, w_br_c, w_out, norm2_g, w_ffn_gu, w_ffn_d, final_g):
    cvec = jnp.concatenate([c_ctx[None, :], c, jnp.zeros((N_MOD_ROWS - 1 - DEC_BATCH, D_MODEL), F32)], axis=0)
    mod = _adaln(cvec, w_mod, b_mod).reshape(DEPTH, N_MOD_ROWS, 1, 6 * D_MODEL)

    seg = jnp.kron(jnp.eye(MXU_DIM // HEAD_DIM, dtype=F32),
                   jnp.full((HEAD_DIM, HEAD_DIM), 1.0 / HEAD_DIM, F32)).astype(BF16)
    rope_tabs = _rope_tables()
    fg = final_g.reshape(1, D_MODEL)
    ck_a = cache_ga_k.reshape(DEC_BATCH, DEPTH, PAST_LEN, KV_WIDTH_A)
    cv_a = cache_ga_v.reshape(DEC_BATCH, DEPTH, PAST_LEN, KV_WIDTH_A)
    ck_c = cache_na_k.reshape(DEC_BATCH, DEPTH, PAST_LEN, WIDTH_C)
    cv_c = cache_na_v.reshape(DEC_BATCH, DEPTH, PAST_LEN, WIDTH_C)
    zero_state = jnp.zeros((2, 2, BATCH, SSM_LANES), F32)

    row = lambda p: p.reshape(DEPTH, 1, p.shape[-1])
    g1 = row(norm1_g)
    w_in_b = w_in.astype(BF16)
    qg = row(jnp.tile(qn_g, (1, N_HEADS_A)))
    kg = row(jnp.tile(kn_g, (1, N_KV_A)))
    merge_w = [row(ssm_d), ssm_w_glu.astype(BF16), w_br_a.astype(BF16), w_br_b.astype(BF16), w_br_c.astype(BF16),
               w_out.astype(BF16), row(norm2_g), w_ffn_gu.astype(BF16), w_ffn_d.astype(BF16)]

    xp, xs = x_prompt, x_sample
    cache = ()
    ssm_st = []
    for l in range(DEPTH):
        bmat, lam, cmat = _ssm_discretise(ssm_lam_re[l], ssm_lam_im[l], ssm_log_step[l], ssm_b_re[l], ssm_b_im[l],
                                          ssm_c_re[l], ssm_c_im[l])
        mod_ctx = mod[l, 0:1]
        mod_lat = mod[l, 1:1 + DEC_BATCH]

        qa, krep, vrep, ka, va, u_rows, qc, kc, vc, g = _inproj(xp, mod_ctx, g1, w_in_b, seg, qg, kg, l,
                                                                prev_cache=cache)
        cache = (ka, va, kc, vc)
        oa = _attn_a_ctx(qa, krep, vrep)
        oc = _attn_c_ctx(qc, kc, vc, l)
        yf, yb, hfin = _scan_ctx(u_rows, bmat, lam, cmat, zero_state)
        xp = _merge(xp, oa, yf, yb, u_rows, oc, g, mod_ctx, merge_w, fg, l, name="merge_ctx")
        ssm_st.append(jnp.transpose(hfin, (2, 0, 1, 3)))

        qa, krep, vrep, u_rows, qc, kc, vc, g = _inproj(xs, mod_lat, g1, w_in_b, seg, qg, kg, l, rope_tabs=rope_tabs)
        oa = _attn_a_lat(qa, krep, vrep, ck_a, cv_a, l)
        oc = _na_lat(qc, kc, vc, ck_c, cv_c, _na_bias_rows(na_bias[l]), l)
        h0 = jnp.transpose(state_ssm[:, l].reshape(DEC_BATCH, 2, 2, SSM_LANES), (2, 1, 0, 3))
        h0 = h0.reshape(2, 2 * DEC_BATCH, SSM_LANES)
        lam_pair = jnp.broadcast_to(jnp.transpose(lam, (1, 0, 2, 3)), (2, 2, DEC_BATCH, SSM_LANES))
        lam_pair = lam_pair.reshape(2, 2 * DEC_BATCH, SSM_LANES)
        yf, yb = _scan_lat(u_rows, bmat, lam_pair, cmat, h0)
        xs = _merge(xs, oa, yf, yb, u_rows, oc, g, mod_lat, merge_w, fg, l, name="merge_lat")

    ga_k, ga_v, na_k, na_v = cache
    new_ssm = jnp.stack(ssm_st, axis=1).reshape(BATCH, DEPTH, 2, 2, SSM_GROUPS, SSM_STATE)
    return (xp, xs,
            ga_k.reshape(BATCH, DEPTH, SEQ, N_KV_A, HEAD_DIM), ga_v.reshape(BATCH, DEPTH, SEQ, N_KV_A, HEAD_DIM),
            na_k.reshape(BATCH, DEPTH, SEQ, N_HEADS_C, HEAD_DIM), na_v.reshape(BATCH, DEPTH, SEQ, N_HEADS_C, HEAD_DIM),
            new_ssm)
```

```python
import functools
import math

import jax
import jax.numpy as jnp
from jax import lax
from jax.experimental import pallas as pl
from jax.experimental.pallas import tpu as pltpu

D_MODEL = 1024
BATCH = 16
SEQ = 256
DEPTH = 2
DEC_BATCH = 4
DEC_SEQ = 2048
PAST_LEN = 256
GRID_W = 64
GRID_ROWS = DEC_SEQ // GRID_W
HEAD_DIM = 64
N_HEADS_A = 8
N_KV_A = 2
REP_A = N_HEADS_A // N_KV_A
N_HEADS_C = 4
SSM_WIDTH = 256
SSM_GROUP = 16
SSM_GROUPS = SSM_WIDTH // SSM_GROUP
SSM_STATE = 64
SSM_LANES = SSM_GROUPS * SSM_STATE
NA_WIN_R = 8
NA_WIN_C = 16
D_FF = -(-8 * D_MODEL // (3 * 256)) * 256
ROPE_THETA = 10000.0
ROT_HALF = HEAD_DIM // 2
ROT_FREQS = ROT_HALF // 2
WIDTH_A = N_HEADS_A * HEAD_DIM
KV_WIDTH_A = N_KV_A * HEAD_DIM
WIDTH_C = N_HEADS_C * HEAD_DIM
N_BRANCH = 3
IN_WIDTH = WIDTH_A + 2 * KV_WIDTH_A + SSM_WIDTH + 3 * WIDTH_C + N_BRANCH * D_MODEL
EPS = 1e-6

OFF_QA = 0
OFF_KA = OFF_QA + WIDTH_A
OFF_VA = OFF_KA + KV_WIDTH_A
OFF_U = OFF_VA + KV_WIDTH_A
OFF_QC = OFF_U + SSM_WIDTH
OFF_KC = OFF_QC + WIDTH_C
OFF_VC = OFF_KC + WIDTH_C
OFF_G = OFF_VC + WIDTH_C

N_MOD_ROWS = 8
ROW_TILE = 256
Q_TILE_A = 512
Q_CHAIN_ROWS = 256
NA_Q_ROWS = 2
NA_K_ROWS = 10
NA_R0_MAX = GRID_ROWS - NA_K_ROWS
NA_STEP_BLOCKS = 2
CTX_STEP_SEQS = 2
ROW_CHAINS = 2
SCAN_ROWS = 1024
LOG2_E = math.log2(math.e)
Q_SCALE = HEAD_DIM ** -0.5 * LOG2_E
LANES = 128
MXU_DIM = 256
NEG = -1e30
VMEM_LIMIT_V7X = 56 * 1024 * 1024

F32 = jnp.float32
BF16 = jnp.bfloat16


def _dot(a, b):
    return jnp.dot(a, b, preferred_element_type=F32)


def _dot_t(a, b):
    return lax.dot_general(a, b, (((1,), (1,)), ((), ())), preferred_element_type=F32)


def _params(n_axes):
    return pltpu.CompilerParams(dimension_semantics=("arbitrary",) * n_axes,
                                vmem_limit_bytes=VMEM_LIMIT_V7X)


def _const_spec(shape):
    nd = len(shape)
    return pl.BlockSpec(shape, lambda *_: (0,) * nd, pipeline_mode=pl.Buffered(1))


def _layer_spec(stacked, layer):
    shape = stacked.shape[1:]
    return pl.BlockSpec((None,) + shape, lambda *_: (layer,) + (0,) * len(shape), pipeline_mode=pl.Buffered(1))


def _rms(x, g):
    return x * lax.rsqrt(jnp.mean(x * x, axis=-1, keepdims=True) + EPS) * g


def _silu(x):
    return x * jax.nn.sigmoid(x)


def _gelu_tanh(x):
    c = math.sqrt(2.0 / math.pi)
    return x * (0.5 * (1.0 + jnp.tanh(c * (x + 0.044715 * (x * x * x)))))


def _seg_rms(x, seg, g):
    w = seg.shape[0]
    x2 = x * x
    hi = x2.astype(BF16)
    lo = (x2 - hi.astype(F32)).astype(BF16)
    ms = jnp.concatenate([_dot(hi[:, c:c + w], seg) + _dot(lo[:, c:c + w], seg)
                          for c in range(0, x.shape[1], w)], axis=1)
    return x * lax.rsqrt(ms + EPS) * g


def _rope(x, cos, sin_signed):
    w = x.shape[-1]
    lane = lax.broadcasted_iota(jnp.int32, x.shape, 1)
    first = (lane & ROT_FREQS) == 0
    partner = jnp.where(first, pltpu.roll(x, w - ROT_FREQS, 1), pltpu.roll(x, ROT_FREQS, 1))
    return x * cos + partner * sin_signed


def _rep_heads(kv):
    lane = lax.broadcasted_iota(jnp.int32, kv.shape, 1)
    swapped = pltpu.roll(kv, HEAD_DIM, 1)
    lo = lane < HEAD_DIM
    h0 = jnp.where(lo, kv, swapped)
    h1 = jnp.where(lo, swapped, kv)
    return jnp.concatenate([h0, h0, h1, h1], axis=1)


def _round_robin(chains, stagger=0):
    done = [False] * len(chains)
    rnd = 0
    while not all(done):
        for i, ch in enumerate(chains):
            if not done[i] and rnd >= i * stagger:
                try:
                    next(ch)
                except StopIteration:
                    done[i] = True
        rnd += 1


def _to_time_major(val, slabs, nb):
    tt = val.shape[0] // nb
    for s, slab in enumerate(slabs):
        for b in range(nb):
            slab[pl.ds(b, tt, stride=nb), :] = val[b * tt:(b + 1) * tt, s * LANES:(s + 1) * LANES]
    return jnp.concatenate([slab[...] for slab in slabs], axis=1)


def _to_batch_major(val, slabs, nb):
    tt = val.shape[0] // nb
    for s, slab in enumerate(slabs):
        slab[...] = val[:, s * LANES:(s + 1) * LANES]
    return jnp.concatenate(
        [jnp.concatenate([slab[pl.ds(b, tt, stride=nb), :] for slab in slabs], axis=1) for b in range(nb)], axis=0)


def _adaln_kernel(c_ref, w_ref, b_ref, o_ref):
    s = _silu(c_ref[...])
    o_ref[...] = _dot(s.astype(BF16), w_ref[...].astype(BF16)) + b_ref[...]


def _adaln(cvec, w_mod, b_mod):
    n_col = 6 * D_MODEL
    tn = n_col // 4
    return pl.pallas_call(
        _adaln_kernel,
        grid=(DEPTH, n_col // tn),
        in_specs=[pl.BlockSpec((N_MOD_ROWS, D_MODEL), lambda l, n: (0, 0)),
                  pl.BlockSpec((None, D_MODEL, tn), lambda l, n: (l, 0, n)),
                  pl.BlockSpec((None, 1, tn), lambda l, n: (l, 0, n))],
        out_specs=pl.BlockSpec((None, N_MOD_ROWS, tn), lambda l, n: (l, 0, n)),
        out_shape=jax.ShapeDtypeStruct((DEPTH, N_MOD_ROWS, n_col), F32),
        compiler_params=_params(2),
        name="adaln",
    )(cvec, w_mod, b_mod.reshape(DEPTH, 1, n_col))


def _inproj_kernel(*refs, latent, n_alias):
    if latent:
        (x_ref, xn_ref, mod_ref, g1_ref, w_ref, seg_ref, qg_ref, kg_ref, cos_ref, sin_ref,
         qa_ref, krep_ref, vrep_ref, u_ref, qc_ref, kc_ref, vc_ref, g_ref, hb_ref, slab0, slab1) = refs
    else:
        x_ref, xn_ref, mod_ref, g1_ref, w_ref, seg_ref, qg_ref, kg_ref = refs[:8]
        (qa_ref, krep_ref, vrep_ref, ka_ref, va_ref, u_ref, qc_ref, kc_ref, vc_ref, g_ref,
         hb_ref, slab0, slab1) = refs[8 + n_alias:]
    nb, tt, _ = x_ref.shape
    rows = nb * tt
    mod = mod_ref[...]
    shift = mod[:, :, 0:D_MODEL]
    scale = mod[:, :, D_MODEL:2 * D_MODEL]
    seg = seg_ref[...]
    slot = pl.program_id(0) % 2

    def normed(ref):
        h = _rms(ref[...], g1_ref[...]) * (1 + scale) + shift
        return h.reshape(rows, D_MODEL).astype(BF16)

    def put(ref, val):
        val = val.reshape(nb, tt, ref.shape[-1]).astype(ref.dtype)
        if len(ref.shape) == 3:
            ref[...] = val
        else:
            ref[:, 0] = val
            ref[:, 1:] = jnp.zeros((nb, DEPTH - 1, tt, ref.shape[-1]), ref.dtype)

    @pl.when(pl.program_id(0) == 0)
    def _():
        hb_ref[0] = normed(x_ref)

    def project():
        hb = hb_ref[slot]
        u_ref[...] = _to_time_major(_dot(hb, w_ref[:, OFF_U:OFF_QC]), (slab0, slab1), nb)
        put(qc_ref, _dot(hb, w_ref[:, OFF_QC:OFF_KC]) * Q_SCALE)
        put(kc_ref, _dot(hb, w_ref[:, OFF_KC:OFF_VC]))
        put(vc_ref, _dot(hb, w_ref[:, OFF_VC:OFF_G]))
        yield
        qa = _seg_rms(_dot(hb, w_ref[:, OFF_QA:OFF_KA]), seg, qg_ref[...])
        ka = _seg_rms(_dot(hb, w_ref[:, OFF_KA:OFF_VA]), seg[0:KV_WIDTH_A, 0:KV_WIDTH_A], kg_ref[...])
        va = _dot(hb, w_ref[:, OFF_VA:OFF_U])
        if latent:
            cos = jnp.concatenate([cos_ref[...]] * nb, axis=0)
            sin = jnp.concatenate([sin_ref[...]] * nb, axis=0)
            qa = _rope(qa, jnp.concatenate([cos] * REP_A, axis=1), jnp.concatenate([sin] * REP_A, axis=1))
            ka = _rope(ka, cos, sin)
        else:
            put(ka_ref, ka)
            put(va_ref, va)
        put(qa_ref, qa * Q_SCALE)
        put(krep_ref, _rep_heads(ka))
        put(vrep_ref, _rep_heads(va))
        yield
        for cidx in range(N_BRANCH):
            lo = OFF_G + cidx * D_MODEL
            g_ref[:, :, cidx * D_MODEL:(cidx + 1) * D_MODEL] = _dot(hb, w_ref[:, lo:lo + D_MODEL]).reshape(
                nb, tt, D_MODEL)

    def prepare_next():
        yield
        hb_ref[1 - slot] = normed(xn_ref)

    _round_robin([project(), prepare_next()])


def _inproj(x, mod_rows, g1, w_in, seg, qg, kg, layer, *, rope_tabs=None, prev_cache=()):
    latent = rope_tabs is not None
    nb, seq, _ = x.shape
    tt = ROW_TILE // nb
    slab = lambda w: pl.BlockSpec((nb, tt, w), lambda j: (0, j, 0))
    act = lambda w, dt: jax.ShapeDtypeStruct((nb, seq, w), dt)
    if layer == 0:
        cache_slab = lambda w: pl.BlockSpec((nb, DEPTH, tt, w), lambda j: (0, 0, j, 0))
    else:
        cache_slab = lambda w: pl.BlockSpec((nb, None, tt, w), lambda j: (0, layer, j, 0))
    cache = lambda w: jax.ShapeDtypeStruct((nb, DEPTH, seq, w), F32)
    n_steps = seq // tt
    next_slab = pl.BlockSpec((nb, tt, D_MODEL), lambda j: (0, jnp.minimum(j + 1, n_steps - 1), 0))
    in_specs = [slab(D_MODEL), next_slab, _const_spec(mod_rows.shape), _layer_spec(g1, layer),
                _layer_spec(w_in, layer), _const_spec(seg.shape), _layer_spec(qg, layer), _layer_spec(kg, layer)]
    args = [x, x, mod_rows, g1, w_in, seg, qg, kg]
    out_shape = [act(WIDTH_A, BF16), act(REP_A * KV_WIDTH_A, BF16), act(REP_A * KV_WIDTH_A, BF16)]
    out_specs = [slab(WIDTH_A), slab(REP_A * KV_WIDTH_A), slab(REP_A * KV_WIDTH_A)]
    aliases = {}
    if latent:
        in_specs += [pl.BlockSpec((tt, 2 * HEAD_DIM), lambda j: (j, 0))] * 2
        args += list(rope_tabs)
        kv_c = [act(WIDTH_C, BF16)] * 2
        kv_c_specs = [slab(WIDTH_C)] * 2
    else:
        aliases = {len(args) + i: o for i, o in enumerate((3, 4, 7, 8)[:len(prev_cache)])}
        in_specs += [pl.BlockSpec(memory_space=pl.ANY)] * len(prev_cache)
        args += list(prev_cache)
        out_shape += [cache(KV_WIDTH_A)] * 2
        out_specs += [cache_slab(KV_WIDTH_A)] * 2
        kv_c = [cache(WIDTH_C)] * 2
        kv_c_specs = [cache_slab(WIDTH_C)] * 2
    out_shape += [jax.ShapeDtypeStruct((seq * nb, SSM_WIDTH), F32),
                  act(WIDTH_C, BF16)] + kv_c + [act(N_BRANCH * D_MODEL, F32)]
    out_specs += [pl.BlockSpec((ROW_TILE, SSM_WIDTH), lambda j: (j, 0)),
                  slab(WIDTH_C)] + kv_c_specs + [slab(N_BRANCH * D_MODEL)]
    return pl.pallas_call(
        functools.partial(_inproj_kernel, latent=latent, n_alias=len(prev_cache)),
        grid=(seq // tt,),
        in_specs=in_specs,
        out_specs=out_specs,
        out_shape=out_shape,
        scratch_shapes=[pltpu.VMEM((2, ROW_TILE, D_MODEL), BF16)] + [pltpu.VMEM((ROW_TILE, LANES), F32)] * 2,
        input_output_aliases=aliases,
        compiler_params=_params(1),
        name="inproj_lat" if latent else "inproj_ctx",
    )(*args)


def _head_mask(shape, head):
    lane = lax.broadcasted_iota(jnp.int32, shape, 1)
    return (lane // HEAD_DIM) == head


def _head_chain(q, heads, keys, vals, acc, key, bias=None):
    m_rows = q.shape[0]
    masks = [_head_mask(q.shape, h) for h in heads]
    qs = jnp.concatenate([jnp.where(hm, q, jnp.zeros_like(q)) for hm in masks], axis=0)
    s = _dot_t(qs, keys())
    if bias is not None:
        s = s + bias()
    yield
    p = jnp.exp2(s - jnp.max(s, axis=-1, keepdims=True))
    l = jnp.sum(p, axis=-1, keepdims=True)
    p = p.astype(BF16)
    yield
    o = _dot(p, vals()) / l
    for i, hm in enumerate(masks):
        acc[key] = jnp.where(hm, o[i * m_rows:(i + 1) * m_rows], acc[key])


def _attn_a_kernel(q_ref, kn_ref, vn_ref, ck_ref, cv_ref, o_ref, k_ref, v_ref):
    @pl.when(pl.program_id(1) == 0)
    def _():
        k_ref[0:PAST_LEN, :] = _rep_heads(ck_ref[...]).astype(BF16)
        v_ref[0:PAST_LEN, :] = _rep_heads(cv_ref[...]).astype(BF16)
        k_ref[PAST_LEN:, :] = kn_ref[...]
        v_ref[PAST_LEN:, :] = vn_ref[...]

    gw = REP_A * HEAD_DIM
    n_sub = q_ref.shape[0] // Q_CHAIN_ROWS
    acc = {(t, g): jnp.zeros((Q_CHAIN_ROWS, gw), F32) for t in range(n_sub) for g in range(N_KV_A)}
    chains = []
    for t in range(n_sub):
        rows = slice(t * Q_CHAIN_ROWS, (t + 1) * Q_CHAIN_ROWS)
        for g in range(N_KV_A):
            sl = slice(g * gw, (g + 1) * gw)
            for j in range(REP_A):
                chains.append(_head_chain(q_ref[rows, sl], [j], lambda sl=sl: k_ref[:, sl],
                                          lambda sl=sl: v_ref[:, sl], acc, (t, g)))
    _round_robin(chains, stagger=1)
    for (t, g), val in acc.items():
        o_ref[t * Q_CHAIN_ROWS:(t + 1) * Q_CHAIN_ROWS, g * gw:(g + 1) * gw] = val.astype(BF16)


def _attn_a_lat(qa, krep, vrep, cache_k, cache_v, layer):
    seq_blk = pl.BlockSpec((None, DEC_SEQ, WIDTH_A), lambda b, t: (b, 0, 0))
    cache_blk = pl.BlockSpec((None, None, PAST_LEN, KV_WIDTH_A), lambda b, t: (b, layer, 0, 0))
    q_blk = pl.BlockSpec((None, Q_TILE_A, WIDTH_A), lambda b, t: (b, t, 0))
    return pl.pallas_call(
        _attn_a_kernel,
        grid=(DEC_BATCH, DEC_SEQ // Q_TILE_A),
        in_specs=[q_blk, seq_blk, seq_blk, cache_blk, cache_blk],
        out_specs=q_blk,
        out_shape=jax.ShapeDtypeStruct((DEC_BATCH, DEC_SEQ, WIDTH_A), BF16),
        scratch_shapes=[pltpu.VMEM((PAST_LEN + DEC_SEQ, WIDTH_A), BF16)] * 2,
        compiler_params=_params(2),
        name="attn_a_lat",
    )(qa, krep, vrep, cache_k, cache_v)


def _attn_ctx_kernel(qa_ref, ka_ref, va_ref, qc_ref, kc_ref, vc_ref, oa_ref, oc_ref):
    gw = REP_A * HEAD_DIM
    acc = {}
    chains = []
    for b in range(CTX_STEP_SEQS):
        for g in range(N_KV_A):
            sl = slice(g * gw, (g + 1) * gw)
            acc["a", b, g] = jnp.zeros((SEQ, gw), F32)
            for j in range(REP_A):
                chains.append(_head_chain(qa_ref[b, :, sl], [j], lambda b=b, sl=sl: ka_ref[b, :, sl],
                                          lambda b=b, sl=sl: va_ref[b, :, sl], acc, ("a", b, g)))
        acc["c", b] = jnp.zeros((SEQ, WIDTH_C), F32)
        for h in range(0, N_HEADS_C, 2):
            chains.append(_head_chain(qc_ref[b], [h, h + 1], lambda b=b: kc_ref[b].astype(BF16),
                                      lambda b=b: vc_ref[b].astype(BF16), acc, ("c", b)))
    _round_robin(chains, stagger=1)
    for b in range(CTX_STEP_SEQS):
        for g in range(N_KV_A):
            oa_ref[b, :, g * gw:(g + 1) * gw] = acc["a", b, g].astype(BF16)
        oc_ref[b] = acc["c", b].astype(BF16)


def _attn_ctx(qa, krep, vrep, qc, kc, vc, layer):
    blk = lambda w: pl.BlockSpec((CTX_STEP_SEQS, SEQ, w), lambda i: (i, 0, 0))
    kv_blk = pl.BlockSpec((CTX_STEP_SEQS, None, SEQ, WIDTH_C), lambda i: (i, layer, 0, 0))
    return pl.pallas_call(
        _attn_ctx_kernel,
        grid=(BATCH // CTX_STEP_SEQS,),
        in_specs=[blk(WIDTH_A), blk(WIDTH_A), blk(WIDTH_A), blk(WIDTH_C), kv_blk, kv_blk],
        out_specs=[blk(WIDTH_A), blk(WIDTH_C)],
        out_shape=[jax.ShapeDtypeStruct((BATCH, SEQ, WIDTH_A), BF16),
                   jax.ShapeDtypeStruct((BATCH, SEQ, WIDTH_C), BF16)],
        compiler_params=_params(1),
        name="attn_ctx",
    )(qa, krep, vrep, qc, kc, vc)


def _na_kernel(q_ref, k_ref, v_ref, ck_ref, cv_ref, brow_ref, o_ref, b2_ref):
    nq = NA_Q_ROWS * GRID_W
    nk = NA_K_ROWS * GRID_W
    n_pair = 2 * NA_WIN_R

    @pl.when((pl.program_id(0) == 0) & (pl.program_id(1) == 0))
    def _():
        shp = (GRID_W, 2 * GRID_W)
        c = lax.broadcasted_iota(jnp.int32, shp, 0)
        kc = lax.broadcasted_iota(jnp.int32, shp, 1) & (GRID_W - 1)
        c0 = jnp.clip(c - NA_WIN_C // 2, 0, GRID_W - NA_WIN_C)
        col_bias = jnp.where(kc < c0, NEG, jnp.where(kc >= c0 + NA_WIN_C, NEG, 0.0))
        for h in range(N_HEADS_C):
            for e in range(n_pair):
                row = jnp.broadcast_to(brow_ref[h, e:e + 1, :], shp)
                toep = pltpu.roll(row, 2 * GRID_W - (NA_WIN_C - 1), 1, stride=1, stride_axis=0)
                b2_ref[h, e] = jnp.where(col_bias < 0.0, NEG, toep * LOG2_E)

    no_bias = jnp.zeros((nq, PAST_LEN), F32)
    ck = ck_ref[...].astype(BF16)
    cv = cv_ref[...].astype(BF16)
    acc = [jnp.zeros((nq, WIDTH_C), F32) for _ in range(NA_STEP_BLOCKS)]

    def chain(blk, heads):
        qrow0 = NA_Q_ROWS * (NA_STEP_BLOCKS * pl.program_id(1) + blk)
        row0 = jnp.clip(qrow0 - NA_WIN_R // 2, 0, NA_R0_MAX)
        start = pl.multiple_of(row0 * GRID_W, GRID_W)
        qrow = qrow0 + lax.broadcasted_iota(jnp.int32, (nq, nk), 0) // GRID_W
        krow = row0 + lax.broadcasted_iota(jnp.int32, (nq, nk), 1) // GRID_W
        win0 = jnp.clip(qrow - NA_WIN_R // 2, 0, GRID_ROWS - NA_WIN_R)
        row_bias = jnp.where(krow < win0, NEG, jnp.where(krow >= win0 + NA_WIN_R, NEG, 0.0))

        def head_bias(h):
            rows = []
            for i in range(NA_Q_ROWS):
                tiles = []
                for m in range(NA_K_ROWS // 2):
                    e = jnp.clip(row0 + 2 * m - (qrow0 + i) + NA_WIN_R, 0, n_pair - 1)
                    tiles.append(b2_ref[h, e])
                rows.append(jnp.concatenate(tiles, axis=1))
            return jnp.concatenate([jnp.concatenate(rows, axis=0) + row_bias, no_bias], axis=1)

        return _head_chain(q_ref[blk * nq:(blk + 1) * nq, :], heads,
                           lambda: jnp.concatenate([k_ref[pl.ds(start, nk), :], ck], axis=0),
                           lambda: jnp.concatenate([v_ref[pl.ds(start, nk), :], cv], axis=0), acc, blk,
                           bias=lambda: jnp.concatenate([head_bias(h) for h in heads], axis=0))

    pairs = [[h, h + 1] for h in range(0, N_HEADS_C, 2)]
    _round_robin([chain(blk, heads) for blk in range(NA_STEP_BLOCKS) for heads in pairs], stagger=1)
    for blk in range(NA_STEP_BLOCKS):
        o_ref[blk * nq:(blk + 1) * nq, :] = acc[blk].astype(BF16)


def _na_bias_rows(tbl):
    pad = jnp.pad(tbl, ((0, 0), (1, 1), (0, GRID_W - tbl.shape[-1])))
    return jnp.concatenate([pad[:, :-1], pad[:, 1:]], axis=-1)


def _na_lat(qc, kc, vc, cache_k, cache_v, brow, layer):
    step_rows = NA_STEP_BLOCKS * NA_Q_ROWS * GRID_W
    seq_blk = pl.BlockSpec((None, DEC_SEQ, WIDTH_C), lambda b, j: (b, 0, 0))
    cache_blk = pl.BlockSpec((None, None, PAST_LEN, WIDTH_C), lambda b, j: (b, layer, 0, 0))
    q_blk = pl.BlockSpec((None, step_rows, WIDTH_C), lambda b, j: (b, j, 0))
    return pl.pallas_call(
        _na_kernel,
        grid=(DEC_BATCH, DEC_SEQ // step_rows),
        in_specs=[q_blk, seq_blk, seq_blk, cache_blk, cache_blk, _const_spec(brow.shape)],
        out_specs=q_blk,
        out_shape=jax.ShapeDtypeStruct((DEC_BATCH, DEC_SEQ, WIDTH_C), BF16),
        scratch_shapes=[pltpu.VMEM((N_HEADS_C, 2 * NA_WIN_R, GRID_W, 2 * GRID_W), F32)],
        compiler_params=_params(2),
        name="na_lat",
    )(qc, kc, vc, cache_k, cache_v, brow)


def _scan_kernel(uf_ref, ub_ref, bmat_ref, lam_ref, cmat_ref, h0_ref, yf_ref, yb_ref, hfin_ref,
                 hf_ref, hb_ref, st_ref, *, nb, lane_w):
    steps = SCAN_ROWS // nb

    @pl.when(pl.program_id(0) == 0)
    def _():
        st_ref[...] = h0_ref[...]

    hf_ref[...] = _dot(uf_ref[...].astype(BF16), bmat_ref[0])
    hb_ref[...] = _dot(ub_ref[...].astype(BF16), bmat_ref[1])

    for lb in range(SSM_LANES // lane_w):
        re = slice(lb * lane_w, (lb + 1) * lane_w)
        im = slice(SSM_LANES + lb * lane_w, SSM_LANES + (lb + 1) * lane_w)
        lam = [[jnp.broadcast_to(lam_ref[d, c, :, re], (nb, lane_w)) for c in range(2)] for d in range(2)]
        init = tuple(st_ref[d, c, :, re] for d in range(2) for c in range(2))

        def step(k, carry, re=re, im=im, lam=lam):
            fr, fi, br, bi = carry
            rf = pl.multiple_of(k * nb, nb)
            rb = pl.multiple_of((steps - 1 - k) * nb, nb)
            nfr = lam[0][0] * fr - lam[0][1] * fi + hf_ref[pl.ds(rf, nb), re]
            nfi = lam[0][0] * fi + lam[0][1] * fr + hf_ref[pl.ds(rf, nb), im]
            nbr = lam[1][0] * br - lam[1][1] * bi + hb_ref[pl.ds(rb, nb), re]
            nbi = lam[1][0] * bi + lam[1][1] * br + hb_ref[pl.ds(rb, nb), im]
            hf_ref[pl.ds(rf, nb), re] = nfr
            hf_ref[pl.ds(rf, nb), im] = nfi
            hb_ref[pl.ds(rb, nb), re] = nbr
            hb_ref[pl.ds(rb, nb), im] = nbi
            return nfr, nfi, nbr, nbi

        fin = lax.fori_loop(0, steps, step, init, unroll=2)
        for d in range(2):
            for c in range(2):
                st_ref[d, c, :, re] = fin[2 * d + c]

    yf_ref[...] = _dot(hf_ref[...].astype(BF16), cmat_ref[0])
    yb_ref[...] = _dot(hb_ref[...].astype(BF16), cmat_ref[1])
    hfin_ref[...] = st_ref[...]


def _scan_pair_kernel(uf_ref, ub_ref, bmat_ref, lam_ref, cmat_ref, h0_ref, yf_ref, yb_ref,
                      hf_ref, hb_ref, st_ref, *, lane_w):
    half = DEC_BATCH
    tiles = SCAN_ROWS // (2 * half)

    @pl.when(pl.program_id(0) == 0)
    def _():
        st_ref[...] = h0_ref[...]

    hf_ref[...] = _dot(uf_ref[...].astype(BF16), bmat_ref[0])
    hb_ref[...] = _dot(ub_ref[...].astype(BF16), bmat_ref[1])

    top = lax.broadcasted_iota(jnp.int32, (2 * half, lane_w), 0) < half
    swap = lambda a: pltpu.roll(a, half, 0)
    for lb in range(SSM_LANES // lane_w):
        re = slice(lb * lane_w, (lb + 1) * lane_w)
        im = slice(SSM_LANES + lb * lane_w, SSM_LANES + (lb + 1) * lane_w)
        la_r = lam_ref[0, :, re]
        la_i = lam_ref[1, :, re]
        lb_r = swap(la_r)
        lb_i = swap(la_i)

        def body(m, carry, re=re, im=im, la_r=la_r, la_i=la_i, lb_r=lb_r, lb_i=lb_i):
            sr, si = carry
            rf = pl.multiple_of(m * 2 * half, 2 * half)
            rb = pl.multiple_of((tiles - 1 - m) * 2 * half, 2 * half)
            fr = hf_ref[pl.ds(rf, 2 * half), re]
            fi = hf_ref[pl.ds(rf, 2 * half), im]
            br = hb_ref[pl.ds(rb, 2 * half), re]
            bi = hb_ref[pl.ds(rb, 2 * half), im]
            vr = la_r * sr - la_i * si + jnp.where(top, fr, br)
            vi = la_r * si + la_i * sr + jnp.where(top, fi, bi)
            tr = swap(vr)
            ti = swap(vi)
            wr = lb_r * tr - lb_i * ti + jnp.where(top, br, fr)
            wi = lb_r * ti + lb_i * tr + jnp.where(top, bi, fi)
            hf_ref[pl.ds(rf, 2 * half), re] = jnp.where(top, vr, wr)
            hf_ref[pl.ds(rf, 2 * half), im] = jnp.where(top, vi, wi)
            hb_ref[pl.ds(rb, 2 * half), re] = jnp.where(top, wr, vr)
            hb_ref[pl.ds(rb, 2 * half), im] = jnp.where(top, wi, vi)
            return swap(wr), swap(wi)

        fin = lax.fori_loop(0, tiles, body, (st_ref[0, :, re], st_ref[1, :, re]), unroll=2)
        st_ref[0, :, re] = fin[0]
        st_ref[1, :, re] = fin[1]

    yf_ref[...] = _dot(hf_ref[...].astype(BF16), cmat_ref[0])
    yb_ref[...] = _dot(hb_ref[...].astype(BF16), cmat_ref[1])


def _scan_specs(n_rows):
    n = n_rows // SCAN_ROWS
    fwd = pl.BlockSpec((SCAN_ROWS, SSM_WIDTH), lambda j: (j, 0))
    bwd = pl.BlockSpec((SCAN_ROWS, SSM_WIDTH), lambda j: (n - 1 - j, 0))
    y_shape = jax.ShapeDtypeStruct((n_rows, SSM_WIDTH), F32)
    buf = pltpu.VMEM((SCAN_ROWS, 2 * SSM_LANES), F32)
    return n, fwd, bwd, y_shape, buf


def _scan_ctx(u_rows, bmat, lam, cmat, h0):
    n, fwd, bwd, y_shape, buf = _scan_specs(u_rows.shape[0])
    st_shape = (2, 2, BATCH, SSM_LANES)
    return pl.pallas_call(
        functools.partial(_scan_kernel, nb=BATCH, lane_w=256),
        grid=(n,),
        in_specs=[fwd, bwd, _const_spec(bmat.shape), _const_spec(lam.shape), _const_spec(cmat.shape),
                  _const_spec(st_shape)],
        out_specs=[fwd, bwd, pl.BlockSpec(st_shape, lambda j: (0, 0, 0, 0))],
        out_shape=[y_shape, y_shape, jax.ShapeDtypeStruct(st_shape, F32)],
        scratch_shapes=[buf, buf, pltpu.VMEM(st_shape, F32)],
        compiler_params=_params(1),
        name="scan_ctx",
    )(u_rows, u_rows, bmat, lam, cmat, h0)


def _scan_lat(u_rows, bmat, lam_pair, cmat, h0_pair):
    n, fwd, bwd, y_shape, buf = _scan_specs(u_rows.shape[0])
    st_shape = (2, 2 * DEC_BATCH, SSM_LANES)
    return pl.pallas_call(
        functools.partial(_scan_pair_kernel, lane_w=512),
        grid=(n,),
        in_specs=[fwd, bwd, _const_spec(bmat.shape), _const_spec(st_shape), _const_spec(cmat.shape),
                  _const_spec(st_shape)],
        out_specs=[fwd, bwd],
        out_shape=[y_shape, y_shape],
        scratch_shapes=[buf, buf, pltpu.VMEM(st_shape, F32)],
        compiler_params=_params(1),
        name="scan_lat",
    )(u_rows, u_rows, bmat, lam_pair, cmat, h0_pair)


def _ssm_discretise(lam_re, lam_im, log_step, b_re, b_im, c_re, c_im):
    step = jnp.exp(log_step.astype(F32))[..., None]
    lr, li = lam_re.astype(F32), lam_im.astype(F32)
    mag = jnp.exp(lr * step)
    bar_r = mag * jnp.cos(li * step)
    bar_i = mag * jnp.sin(li * step)
    den = lr * lr + li * li
    coef_r = (((bar_r - 1) * lr + bar_i * li) / den)[..., None]
    coef_i = ((bar_i * lr - (bar_r - 1) * li) / den)[..., None]
    br, bi = b_re.astype(F32), b_im.astype(F32)
    bbar_r = coef_r * br - coef_i * bi
    bbar_i = coef_r * bi + coef_i * br
    eye = jnp.eye(SSM_GROUPS, dtype=F32)
    blk_b = lambda a: jnp.einsum('dgpc,gh->dgchp', a, eye).reshape(2, SSM_WIDTH, SSM_LANES)
    bmat = jnp.concatenate([blk_b(bbar_r), blk_b(bbar_i)], axis=-1)
    blk_c = lambda a: jnp.einsum('dgcp,gh->dgphc', a, eye).reshape(2, SSM_LANES, SSM_WIDTH)
    cmat = jnp.concatenate([blk_c(c_re.astype(F32)), -blk_c(c_im.astype(F32))], axis=1)
    lam_flat = jnp.stack([bar_r, bar_i], axis=1).reshape(2, 2, 1, SSM_LANES)
    return bmat.astype(BF16), lam_flat, cmat.astype(BF16)


def _merge_kernel(x_ref, oa_ref, yf_ref, yb_ref, u_ref, oc_ref, g_ref, mod_ref, d_ref, wglu_ref,
                  wa_ref, wb_ref, wc_ref, wo_ref, g2_ref, wgu_ref, wd_ref, fg_ref, o_ref, *slabs, final):
    nb, tt, _ = x_ref.shape
    tc = tt // ROW_CHAINS
    rows = nb * tc
    mod = mod_ref[...]
    gate1 = mod[:, :, 2 * D_MODEL:3 * D_MODEL]
    shift2 = mod[:, :, 3 * D_MODEL:4 * D_MODEL]
    scale2 = mod[:, :, 4 * D_MODEL:5 * D_MODEL]
    gate2 = mod[:, :, 5 * D_MODEL:6 * D_MODEL]
    def chain(c):
        ts = slice(c * tc, (c + 1) * tc)
        rs = slice(c * rows, (c + 1) * rows)
        flat = lambda ref, lo=0, hi=None: ref[:, ts, lo:hi].reshape(rows, -1)
        y = _gelu_tanh(yf_ref[rs, :] + yb_ref[rs, :] + d_ref[...] * u_ref[rs, :])
        ob = y * jax.nn.sigmoid(_dot(y.astype(BF16), wglu_ref[...]))
        ob = _to_batch_major(ob, slabs[2 * c:2 * c + 2], nb)
        yield
        gate = lambda i: jax.nn.sigmoid(flat(g_ref, i * D_MODEL, (i + 1) * D_MODEL))
        merged = (gate(0) * _dot(flat(oa_ref), wa_ref[...])
                  + gate(1) * _dot(ob.astype(BF16), wb_ref[...])
                  + gate(2) * _dot(flat(oc_ref), wc_ref[...]))
        yield
        x1 = x_ref[:, ts, :] + gate1 * _dot(merged.astype(BF16), wo_ref[...]).reshape(nb, tc, D_MODEL)
        h2 = _rms(x1, g2_ref[...]) * (1 + scale2) + shift2
        yield
        gu = _dot(h2.reshape(rows, D_MODEL).astype(BF16), wgu_ref[...])
        act = _silu(gu[:, :D_FF]) * gu[:, D_FF:]
        yield
        x2 = x1 + gate2 * _dot(act.astype(BF16), wd_ref[...]).reshape(nb, tc, D_MODEL)
        if final:
            x2 = _rms(x2, fg_ref[...])
        o_ref[:, ts, :] = x2

    _round_robin([chain(c) for c in range(ROW_CHAINS)])


def _merge(x, oa, yf, yb, u_rows, oc, g, mod_rows, weights, final_g, layer, *, name):
    final = layer == DEPTH - 1
    nb, seq, _ = x.shape
    tt = ROW_TILE // nb
    slab = lambda w: pl.BlockSpec((nb, tt, w), lambda j: (0, j, 0))
    tmaj = pl.BlockSpec((ROW_TILE, SSM_WIDTH), lambda j: (j, 0))
    return pl.pallas_call(
        functools.partial(_merge_kernel, final=final),
        grid=(seq // tt,),
        in_specs=[slab(D_MODEL), slab(WIDTH_A), tmaj, tmaj, tmaj, slab(WIDTH_C), slab(N_BRANCH * D_MODEL),
                  _const_spec(mod_rows.shape)] + [_layer_spec(w, layer) for w in weights]
                 + [_const_spec(final_g.shape)],
        out_specs=slab(D_MODEL),
        out_shape=jax.ShapeDtypeStruct((nb, seq, D_MODEL), F32),
        scratch_shapes=[pltpu.VMEM((ROW_TILE // ROW_CHAINS, LANES), F32)] * (2 * ROW_CHAINS),
        compiler_params=_params(1),
        name=name + ("_final" if final else ""),
    )(x, oa, yf, yb, u_rows, oc, g, mod_rows, *weights, final_g)


def _rope_tables():
    t = jnp.arange(DEC_SEQ)
    row = (t // GRID_W).astype(F32)
    col = (t % GRID_W).astype(F32)
    inv = 1.0 / (ROPE_THETA ** (jnp.arange(ROT_FREQS, dtype=F32) / ROT_FREQS))
    ar = row[:, None] * inv[None]
    ac = col[:, None] * inv[None]
    cos = jnp.concatenate([jnp.cos(ar), jnp.cos(ar), jnp.cos(ac), jnp.cos(ac)], axis=-1)
    sin = jnp.concatenate([-jnp.sin(ar), jnp.sin(ar), -jnp.sin(ac), jnp.sin(ac)], axis=-1)
    return jnp.tile(cos, (1, 2)), jnp.tile(sin, (1, 2))


def kernel(x_prompt, x_sample, c, cache_ga_k, cache_ga_v, cache_na_k, cache_na_v, state_ssm, c_ctx, w_mod, b_mod, norm1_g, w_in, qn_g, kn_g, ssm_lam_re, ssm_lam_im, ssm_log_step, ssm_b_re, ssm_b_im, ssm_c_re, ssm_c_im, ssm_d, ssm_w_glu, na_bias, w_br_a, w_br_b, w_br_c, w_out, norm2_g, w_ffn_gu, w_ffn_d, final_g):
    cvec = jnp.concatenate([c_ctx[None, :], c, jnp.zeros((N_MOD_ROWS - 1 - DEC_BATCH, D_MODEL), F32)], axis=0)
    mod = _adaln(cvec, w_mod, b_mod).reshape(DEPTH, N_MOD_ROWS, 1, 6 * D_MODEL)

    seg = jnp.kron(jnp.eye(MXU_DIM // HEAD_DIM, dtype=F32),
                   jnp.full((HEAD_DIM, HEAD_DIM), 1.0 / HEAD_DIM, F32)).astype(BF16)
    rope_tabs = _rope_tables()
    fg = final_g.reshape(1, D_MODEL)
    ck_a = cache_ga_k.reshape(DEC_BATCH, DEPTH, PAST_LEN, KV_WIDTH_A)
    cv_a = cache_ga_v.reshape(DEC_BATCH, DEPTH, PAST_LEN, KV_WIDTH_A)
    ck_c = cache_na_k.reshape(DEC_BATCH, DEPTH, PAST_LEN, WIDTH_C)
    cv_c = cache_na_v.reshape(DEC_BATCH, DEPTH, PAST_LEN, WIDTH_C)
    zero_state = jnp.zeros((2, 2, BATCH, SSM_LANES), F32)

    row = lambda p: p.reshape(DEPTH, 1, p.shape[-1])
    g1 = row(norm1_g)
    w_in_b = w_in.astype(BF16)
    qg = row(jnp.tile(qn_g, (1, N_HEADS_A)))
    kg = row(jnp.tile(kn_g, (1, N_KV_A)))
    merge_w = [row(ssm_d), ssm_w_glu.astype(BF16), w_br_a.astype(BF16), w_br_b.astype(BF16), w_br_c.astype(BF16),
               w_out.astype(BF16), row(norm2_g), w_ffn_gu.astype(BF16), w_ffn_d.astype(BF16)]

    xp, xs = x_prompt, x_sample
    cache = ()
    ssm_st = []
    for l in range(DEPTH):
        bmat, lam, cmat = _ssm_discretise(ssm_lam_re[l], ssm_lam_im[l], ssm_log_step[l], ssm_b_re[l], ssm_b_im[l],
                                          ssm_c_re[l], ssm_c_im[l])
        mod_ctx = mod[l, 0:1]
        mod_lat = mod[l, 1:1 + DEC_BATCH]

        qa, krep, vrep, ka, va, u_rows, qc, kc, vc, g = _inproj(xp, mod_ctx, g1, w_in_b, seg, qg, kg, l,
                                                                prev_cache=cache)
        cache = (ka, va, kc, vc)
        oa, oc = _attn_ctx(qa, krep, vrep, qc, kc, vc, l)
        yf, yb, hfin = _scan_ctx(u_rows, bmat, lam, cmat, zero_state)
        xp = _merge(xp, oa, yf, yb, u_rows, oc, g, mod_ctx, merge_w, fg, l, name="merge_ctx")
        ssm_st.append(jnp.transpose(hfin, (2, 0, 1, 3)))

        qa, krep, vrep, u_rows, qc, kc, vc, g = _inproj(xs, mod_lat, g1, w_in_b, seg, qg, kg, l, rope_tabs=rope_tabs)
        oa = _attn_a_lat(qa, krep, vrep, ck_a, cv_a, l)
        oc = _na_lat(qc, kc, vc, ck_c, cv_c, _na_bias_rows(na_bias[l]), l)
        h0 = jnp.transpose(state_ssm[:, l].reshape(DEC_BATCH, 2, 2, SSM_LANES), (2, 1, 0, 3))
        h0 = h0.reshape(2, 2 * DEC_BATCH, SSM_LANES)
        lam_pair = jnp.broadcast_to(jnp.transpose(lam, (1, 0, 2, 3)), (2, 2, DEC_BATCH, SSM_LANES))
        lam_pair = lam_pair.reshape(2, 2 * DEC_BATCH, SSM_LANES)
        yf, yb = _scan_lat(u_rows, bmat, lam_pair, cmat, h0)
        xs = _merge(xs, oa, yf, yb, u_rows, oc, g, mod_lat, merge_w, fg, l, name="merge_lat")

    ga_k, ga_v, na_k, na_v = cache
    new_ssm = jnp.stack(ssm_st, axis=1).reshape(BATCH, DEPTH, 2, 2, SSM_GROUPS, SSM_STATE)
    return (xp, xs,
            ga_k.reshape(BATCH, DEPTH, SEQ, N_KV_A, HEAD_DIM), ga_v.reshape(BATCH, DEPTH, SEQ, N_KV_A, HEAD_DIM),
            na_k.reshape(BATCH, DEPTH, SEQ, N_HEADS_C, HEAD_DIM), na_v.reshape(BATCH, DEPTH, SEQ, N_HEADS_C, HEAD_DIM),
            new_ssm)
```

```python
import functools
import math

import jax
import jax.numpy as jnp
from jax import lax
from jax.experimental import pallas as pl
from jax.experimental.pallas import tpu as pltpu

D_MODEL = 1024
BATCH = 16
SEQ = 256
DEPTH = 2
DEC_BATCH = 4
DEC_SEQ = 2048
PAST_LEN = 256
GRID_W = 64
GRID_ROWS = DEC_SEQ // GRID_W
HEAD_DIM = 64
N_HEADS_A = 8
N_KV_A = 2
REP_A = N_HEADS_A // N_KV_A
N_HEADS_C = 4
SSM_WIDTH = 256
SSM_GROUP = 16
SSM_GROUPS = SSM_WIDTH // SSM_GROUP
SSM_STATE = 64
SSM_LANES = SSM_GROUPS * SSM_STATE
NA_WIN_R = 8
NA_WIN_C = 16
D_FF = -(-8 * D_MODEL // (3 * 256)) * 256
ROPE_THETA = 10000.0
ROT_HALF = HEAD_DIM // 2
ROT_FREQS = ROT_HALF // 2
WIDTH_A = N_HEADS_A * HEAD_DIM
KV_WIDTH_A = N_KV_A * HEAD_DIM
WIDTH_C = N_HEADS_C * HEAD_DIM
N_BRANCH = 3
IN_WIDTH = WIDTH_A + 2 * KV_WIDTH_A + SSM_WIDTH + 3 * WIDTH_C + N_BRANCH * D_MODEL
EPS = 1e-6

OFF_QA = 0
OFF_KA = OFF_QA + WIDTH_A
OFF_VA = OFF_KA + KV_WIDTH_A
OFF_U = OFF_VA + KV_WIDTH_A
OFF_QC = OFF_U + SSM_WIDTH
OFF_KC = OFF_QC + WIDTH_C
OFF_VC = OFF_KC + WIDTH_C
OFF_G = OFF_VC + WIDTH_C

N_MOD_ROWS = 8
ROW_TILE = 256
Q_TILE_A = 512
Q_CHAIN_ROWS = 256
NA_Q_ROWS = 2
NA_K_ROWS = 10
NA_R0_MAX = GRID_ROWS - NA_K_ROWS
NA_STEP_BLOCKS = 2
CTX_STEP_SEQS = 2
ROW_CHAINS = 2
SCAN_ROWS = 1024
SCAN_BLOCK = 256
LOG2_E = math.log2(math.e)
Q_SCALE = HEAD_DIM ** -0.5 * LOG2_E
LANES = 128
MXU_DIM = 256
NEG = -1e30
VMEM_LIMIT_V7X = 56 * 1024 * 1024

F32 = jnp.float32
BF16 = jnp.bfloat16


def _dot(a, b):
    return jnp.dot(a, b, preferred_element_type=F32)


def _dot_t(a, b):
    return lax.dot_general(a, b, (((1,), (1,)), ((), ())), preferred_element_type=F32)


def _params(n_axes):
    return pltpu.CompilerParams(dimension_semantics=("arbitrary",) * n_axes,
                                vmem_limit_bytes=VMEM_LIMIT_V7X)


def _const_spec(shape):
    nd = len(shape)
    return pl.BlockSpec(shape, lambda *_: (0,) * nd, pipeline_mode=pl.Buffered(1))


def _layer_spec(stacked, layer):
    shape = stacked.shape[1:]
    return pl.BlockSpec((None,) + shape, lambda *_: (layer,) + (0,) * len(shape), pipeline_mode=pl.Buffered(1))


def _rms(x, g):
    return x * lax.rsqrt(jnp.mean(x * x, axis=-1, keepdims=True) + EPS) * g


def _silu(x):
    return x * jax.nn.sigmoid(x)


def _gelu_tanh(x):
    c = math.sqrt(2.0 / math.pi)
    return x * (0.5 * (1.0 + jnp.tanh(c * (x + 0.044715 * (x * x * x)))))


def _seg_rms(x, seg, g):
    w = seg.shape[0]
    x2 = x * x
    hi = x2.astype(BF16)
    lo = (x2 - hi.astype(F32)).astype(BF16)
    ms = jnp.concatenate([_dot(hi[:, c:c + w], seg) + _dot(lo[:, c:c + w], seg)
                          for c in range(0, x.shape[1], w)], axis=1)
    return x * lax.rsqrt(ms + EPS) * g


def _rope(x, cos, sin_signed):
    w = x.shape[-1]
    lane = lax.broadcasted_iota(jnp.int32, x.shape, 1)
    first = (lane & ROT_FREQS) == 0
    partner = jnp.where(first, pltpu.roll(x, w - ROT_FREQS, 1), pltpu.roll(x, ROT_FREQS, 1))
    return x * cos + partner * sin_signed


def _rep_heads(kv):
    lane = lax.broadcasted_iota(jnp.int32, kv.shape, 1)
    swapped = pltpu.roll(kv, HEAD_DIM, 1)
    lo = lane < HEAD_DIM
    h0 = jnp.where(lo, kv, swapped)
    h1 = jnp.where(lo, swapped, kv)
    return jnp.concatenate([h0, h0, h1, h1], axis=1)


def _round_robin(chains, stagger=0):
    done = [False] * len(chains)
    rnd = 0
    while not all(done):
        for i, ch in enumerate(chains):
            if not done[i] and rnd >= i * stagger:
                try:
                    next(ch)
                except StopIteration:
                    done[i] = True
        rnd += 1


def _to_time_major(val, slabs, nb):
    tt = val.shape[0] // nb
    for s, slab in enumerate(slabs):
        for b in range(nb):
            slab[pl.ds(b, tt, stride=nb), :] = val[b * tt:(b + 1) * tt, s * LANES:(s + 1) * LANES]
    return jnp.concatenate([slab[...] for slab in slabs], axis=1)


def _to_batch_major(val, slabs, nb):
    tt = val.shape[0] // nb
    for s, slab in enumerate(slabs):
        slab[...] = val[:, s * LANES:(s + 1) * LANES]
    return jnp.concatenate(
        [jnp.concatenate([slab[pl.ds(b, tt, stride=nb), :] for slab in slabs], axis=1) for b in range(nb)], axis=0)


def _adaln_kernel(c_ref, w_ref, b_ref, o_ref):
    s = _silu(c_ref[...])
    o_ref[...] = _dot(s.astype(BF16), w_ref[...].astype(BF16)) + b_ref[...]


def _adaln(cvec, w_mod, b_mod):
    n_col = 6 * D_MODEL
    tn = n_col // 4
    return pl.pallas_call(
        _adaln_kernel,
        grid=(DEPTH, n_col // tn),
        in_specs=[pl.BlockSpec((N_MOD_ROWS, D_MODEL), lambda l, n: (0, 0)),
                  pl.BlockSpec((None, D_MODEL, tn), lambda l, n: (l, 0, n)),
                  pl.BlockSpec((None, 1, tn), lambda l, n: (l, 0, n))],
        out_specs=pl.BlockSpec((None, N_MOD_ROWS, tn), lambda l, n: (l, 0, n)),
        out_shape=jax.ShapeDtypeStruct((DEPTH, N_MOD_ROWS, n_col), F32),
        compiler_params=_params(2),
        name="adaln",
    )(cvec, w_mod, b_mod.reshape(DEPTH, 1, n_col))


def _inproj_kernel(*refs, latent, n_alias):
    if latent:
        (x_ref, xn_ref, mod_ref, g1_ref, w_ref, seg_ref, qg_ref, kg_ref, cos_ref, sin_ref,
         qa_ref, krep_ref, vrep_ref, u_ref, qc_ref, kc_ref, vc_ref, g_ref, hb_ref, slab0, slab1) = refs
    else:
        x_ref, xn_ref, mod_ref, g1_ref, w_ref, seg_ref, qg_ref, kg_ref = refs[:8]
        (qa_ref, krep_ref, vrep_ref, ka_ref, va_ref, u_ref, qc_ref, kc_ref, vc_ref, g_ref,
         hb_ref, slab0, slab1) = refs[8 + n_alias:]
    nb, tt, _ = x_ref.shape
    rows = nb * tt
    mod = mod_ref[...]
    shift = mod[:, :, 0:D_MODEL]
    scale = mod[:, :, D_MODEL:2 * D_MODEL]
    seg = seg_ref[...]
    slot = pl.program_id(0) % 2

    def normed(ref):
        h = _rms(ref[...], g1_ref[...]) * (1 + scale) + shift
        return h.reshape(rows, D_MODEL).astype(BF16)

    def put(ref, val):
        val = val.reshape(nb, tt, ref.shape[-1]).astype(ref.dtype)
        if len(ref.shape) == 3:
            ref[...] = val
        else:
            ref[:, 0] = val
            ref[:, 1:] = jnp.zeros((nb, DEPTH - 1, tt, ref.shape[-1]), ref.dtype)

    @pl.when(pl.program_id(0) == 0)
    def _():
        hb_ref[0] = normed(x_ref)

    def project():
        hb = hb_ref[slot]
        u_ref[...] = _to_time_major(_dot(hb, w_ref[:, OFF_U:OFF_QC]), (slab0, slab1), nb)
        put(qc_ref, _dot(hb, w_ref[:, OFF_QC:OFF_KC]) * Q_SCALE)
        put(kc_ref, _dot(hb, w_ref[:, OFF_KC:OFF_VC]))
        put(vc_ref, _dot(hb, w_ref[:, OFF_VC:OFF_G]))
        yield
        qa = _seg_rms(_dot(hb, w_ref[:, OFF_QA:OFF_KA]), seg, qg_ref[...])
        ka = _seg_rms(_dot(hb, w_ref[:, OFF_KA:OFF_VA]), seg[0:KV_WIDTH_A, 0:KV_WIDTH_A], kg_ref[...])
        va = _dot(hb, w_ref[:, OFF_VA:OFF_U])
        if latent:
            cos = jnp.concatenate([cos_ref[...]] * nb, axis=0)
            sin = jnp.concatenate([sin_ref[...]] * nb, axis=0)
            qa = _rope(qa, jnp.concatenate([cos] * REP_A, axis=1), jnp.concatenate([sin] * REP_A, axis=1))
            ka = _rope(ka, cos, sin)
        else:
            put(ka_ref, ka)
            put(va_ref, va)
        put(qa_ref, qa * Q_SCALE)
        put(krep_ref, _rep_heads(ka))
        put(vrep_ref, _rep_heads(va))
        yield
        for cidx in range(N_BRANCH):
            lo = OFF_G + cidx * D_MODEL
            g_ref[:, :, cidx * D_MODEL:(cidx + 1) * D_MODEL] = _dot(hb, w_ref[:, lo:lo + D_MODEL]).reshape(
                nb, tt, D_MODEL)

    def prepare_next():
        yield
        hb_ref[1 - slot] = normed(xn_ref)

    _round_robin([project(), prepare_next()])


def _inproj(x, mod_rows, g1, w_in, seg, qg, kg, layer, *, rope_tabs=None, prev_cache=()):
    latent = rope_tabs is not None
    nb, seq, _ = x.shape
    tt = ROW_TILE // nb
    slab = lambda w: pl.BlockSpec((nb, tt, w), lambda j: (0, j, 0))
    act = lambda w, dt: jax.ShapeDtypeStruct((nb, seq, w), dt)
    if layer == 0:
        cache_slab = lambda w: pl.BlockSpec((nb, DEPTH, tt, w), lambda j: (0, 0, j, 0))
    else:
        cache_slab = lambda w: pl.BlockSpec((nb, None, tt, w), lambda j: (0, layer, j, 0))
    cache = lambda w: jax.ShapeDtypeStruct((nb, DEPTH, seq, w), F32)
    n_steps = seq // tt
    next_slab = pl.BlockSpec((nb, tt, D_MODEL), lambda j: (0, jnp.minimum(j + 1, n_steps - 1), 0))
    in_specs = [slab(D_MODEL), next_slab, _const_spec(mod_rows.shape), _layer_spec(g1, layer),
                _layer_spec(w_in, layer), _const_spec(seg.shape), _layer_spec(qg, layer), _layer_spec(kg, layer)]
    args = [x, x, mod_rows, g1, w_in, seg, qg, kg]
    out_shape = [act(WIDTH_A, BF16), act(REP_A * KV_WIDTH_A, BF16), act(REP_A * KV_WIDTH_A, BF16)]
    out_specs = [slab(WIDTH_A), slab(REP_A * KV_WIDTH_A), slab(REP_A * KV_WIDTH_A)]
    aliases = {}
    if latent:
        in_specs += [pl.BlockSpec((tt, 2 * HEAD_DIM), lambda j: (j, 0))] * 2
        args += list(rope_tabs)
        kv_c = [act(WIDTH_C, BF16)] * 2
        kv_c_specs = [slab(WIDTH_C)] * 2
    else:
        aliases = {len(args) + i: o for i, o in enumerate((3, 4, 7, 8)[:len(prev_cache)])}
        in_specs += [pl.BlockSpec(memory_space=pl.ANY)] * len(prev_cache)
        args += list(prev_cache)
        out_shape += [cache(KV_WIDTH_A)] * 2
        out_specs += [cache_slab(KV_WIDTH_A)] * 2
        kv_c = [cache(WIDTH_C)] * 2
        kv_c_specs = [cache_slab(WIDTH_C)] * 2
    out_shape += [jax.ShapeDtypeStruct((seq * nb, SSM_WIDTH), F32),
                  act(WIDTH_C, BF16)] + kv_c + [act(N_BRANCH * D_MODEL, F32)]
    out_specs += [pl.BlockSpec((ROW_TILE, SSM_WIDTH), lambda j: (j, 0)),
                  slab(WIDTH_C)] + kv_c_specs + [slab(N_BRANCH * D_MODEL)]
    return pl.pallas_call(
        functools.partial(_inproj_kernel, latent=latent, n_alias=len(prev_cache)),
        grid=(seq // tt,),
        in_specs=in_specs,
        out_specs=out_specs,
        out_shape=out_shape,
        scratch_shapes=[pltpu.VMEM((2, ROW_TILE, D_MODEL), BF16)] + [pltpu.VMEM((ROW_TILE, LANES), F32)] * 2,
        input_output_aliases=aliases,
        compiler_params=_params(1),
        name="inproj_lat" if latent else "inproj_ctx",
    )(*args)


def _head_mask(shape, head):
    lane = lax.broadcasted_iota(jnp.int32, shape, 1)
    return (lane // HEAD_DIM) == head


def _head_chain(q, heads, keys, vals, acc, key, bias=None):
    m_rows = q.shape[0]
    masks = [_head_mask(q.shape, h) for h in heads]
    qs = jnp.concatenate([jnp.where(hm, q, jnp.zeros_like(q)) for hm in masks], axis=0)
    s = _dot_t(qs, keys())
    if bias is not None:
        s = s + bias()
    yield
    p = jnp.exp2(s - jnp.max(s, axis=-1, keepdims=True))
    l = jnp.sum(p, axis=-1, keepdims=True)
    p = p.astype(BF16)
    yield
    o = _dot(p, vals()) / l
    for i, hm in enumerate(masks):
        acc[key] = jnp.where(hm, o[i * m_rows:(i + 1) * m_rows], acc[key])


def _attn_a_kernel(q_ref, kn_ref, vn_ref, ck_ref, cv_ref, o_ref, k_ref, v_ref):
    @pl.when(pl.program_id(1) == 0)
    def _():
        k_ref[0:PAST_LEN, :] = _rep_heads(ck_ref[...]).astype(BF16)
        v_ref[0:PAST_LEN, :] = _rep_heads(cv_ref[...]).astype(BF16)
        k_ref[PAST_LEN:, :] = kn_ref[...]
        v_ref[PAST_LEN:, :] = vn_ref[...]

    gw = REP_A * HEAD_DIM
    n_sub = q_ref.shape[0] // Q_CHAIN_ROWS
    acc = {(t, g): jnp.zeros((Q_CHAIN_ROWS, gw), F32) for t in range(n_sub) for g in range(N_KV_A)}
    chains = []
    for t in range(n_sub):
        rows = slice(t * Q_CHAIN_ROWS, (t + 1) * Q_CHAIN_ROWS)
        for g in range(N_KV_A):
            sl = slice(g * gw, (g + 1) * gw)
            for j in range(REP_A):
                chains.append(_head_chain(q_ref[rows, sl], [j], lambda sl=sl: k_ref[:, sl],
                                          lambda sl=sl: v_ref[:, sl], acc, (t, g)))
    _round_robin(chains, stagger=1)
    for (t, g), val in acc.items():
        o_ref[t * Q_CHAIN_ROWS:(t + 1) * Q_CHAIN_ROWS, g * gw:(g + 1) * gw] = val.astype(BF16)


def _attn_a_lat(qa, krep, vrep, cache_k, cache_v, layer):
    seq_blk = pl.BlockSpec((None, DEC_SEQ, WIDTH_A), lambda b, t: (b, 0, 0))
    cache_blk = pl.BlockSpec((None, None, PAST_LEN, KV_WIDTH_A), lambda b, t: (b, layer, 0, 0))
    q_blk = pl.BlockSpec((None, Q_TILE_A, WIDTH_A), lambda b, t: (b, t, 0))
    return pl.pallas_call(
        _attn_a_kernel,
        grid=(DEC_BATCH, DEC_SEQ // Q_TILE_A),
        in_specs=[q_blk, seq_blk, seq_blk, cache_blk, cache_blk],
        out_specs=q_blk,
        out_shape=jax.ShapeDtypeStruct((DEC_BATCH, DEC_SEQ, WIDTH_A), BF16),
        scratch_shapes=[pltpu.VMEM((PAST_LEN + DEC_SEQ, WIDTH_A), BF16)] * 2,
        compiler_params=_params(2),
        name="attn_a_lat",
    )(qa, krep, vrep, cache_k, cache_v)


def _attn_ctx_kernel(qa_ref, ka_ref, va_ref, qc_ref, kc_ref, vc_ref, oa_ref, oc_ref):
    gw = REP_A * HEAD_DIM
    acc = {}
    chains = []
    for b in range(CTX_STEP_SEQS):
        for g in range(N_KV_A):
            sl = slice(g * gw, (g + 1) * gw)
            acc["a", b, g] = jnp.zeros((SEQ, gw), F32)
            for j in range(REP_A):
                chains.append(_head_chain(qa_ref[b, :, sl], [j], lambda b=b, sl=sl: ka_ref[b, :, sl],
                                          lambda b=b, sl=sl: va_ref[b, :, sl], acc, ("a", b, g)))
        acc["c", b] = jnp.zeros((SEQ, WIDTH_C), F32)
        for h in range(0, N_HEADS_C, 2):
            chains.append(_head_chain(qc_ref[b], [h, h + 1], lambda b=b: kc_ref[b].astype(BF16),
                                      lambda b=b: vc_ref[b].astype(BF16), acc, ("c", b)))
    _round_robin(chains, stagger=1)
    for b in range(CTX_STEP_SEQS):
        for g in range(N_KV_A):
            oa_ref[b, :, g * gw:(g + 1) * gw] = acc["a", b, g].astype(BF16)
        oc_ref[b] = acc["c", b].astype(BF16)


def _attn_ctx(qa, krep, vrep, qc, kc, vc, layer):
    blk = lambda w: pl.BlockSpec((CTX_STEP_SEQS, SEQ, w), lambda i: (i, 0, 0))
    kv_blk = pl.BlockSpec((CTX_STEP_SEQS, None, SEQ, WIDTH_C), lambda i: (i, layer, 0, 0))
    return pl.pallas_call(
        _attn_ctx_kernel,
        grid=(BATCH // CTX_STEP_SEQS,),
        in_specs=[blk(WIDTH_A), blk(WIDTH_A), blk(WIDTH_A), blk(WIDTH_C), kv_blk, kv_blk],
        out_specs=[blk(WIDTH_A), blk(WIDTH_C)],
        out_shape=[jax.ShapeDtypeStruct((BATCH, SEQ, WIDTH_A), BF16),
                   jax.ShapeDtypeStruct((BATCH, SEQ, WIDTH_C), BF16)],
        compiler_params=_params(1),
        name="attn_ctx",
    )(qa, krep, vrep, qc, kc, vc)


def _na_kernel(q_ref, k_ref, v_ref, ck_ref, cv_ref, brow_ref, o_ref, b2_ref):
    nq = NA_Q_ROWS * GRID_W
    nk = NA_K_ROWS * GRID_W
    n_pair = 2 * NA_WIN_R

    @pl.when((pl.program_id(0) == 0) & (pl.program_id(1) == 0))
    def _():
        shp = (GRID_W, 2 * GRID_W)
        c = lax.broadcasted_iota(jnp.int32, shp, 0)
        kc = lax.broadcasted_iota(jnp.int32, shp, 1) & (GRID_W - 1)
        c0 = jnp.clip(c - NA_WIN_C // 2, 0, GRID_W - NA_WIN_C)
        col_bias = jnp.where(kc < c0, NEG, jnp.where(kc >= c0 + NA_WIN_C, NEG, 0.0))
        for h in range(N_HEADS_C):
            for e in range(n_pair):
                row = jnp.broadcast_to(brow_ref[h, e:e + 1, :], shp)
                toep = pltpu.roll(row, 2 * GRID_W - (NA_WIN_C - 1), 1, stride=1, stride_axis=0)
                b2_ref[h, e] = jnp.where(col_bias < 0.0, NEG, toep * LOG2_E)

    no_bias = jnp.zeros((nq, PAST_LEN), F32)
    ck = ck_ref[...].astype(BF16)
    cv = cv_ref[...].astype(BF16)
    acc = [jnp.zeros((nq, WIDTH_C), F32) for _ in range(NA_STEP_BLOCKS)]

    def chain(blk, heads):
        qrow0 = NA_Q_ROWS * (NA_STEP_BLOCKS * pl.program_id(1) + blk)
        row0 = jnp.clip(qrow0 - NA_WIN_R // 2, 0, NA_R0_MAX)
        start = pl.multiple_of(row0 * GRID_W, GRID_W)
        qrow = qrow0 + lax.broadcasted_iota(jnp.int32, (nq, nk), 0) // GRID_W
        krow = row0 + lax.broadcasted_iota(jnp.int32, (nq, nk), 1) // GRID_W
        win0 = jnp.clip(qrow - NA_WIN_R // 2, 0, GRID_ROWS - NA_WIN_R)
        row_bias = jnp.where(krow < win0, NEG, jnp.where(krow >= win0 + NA_WIN_R, NEG, 0.0))

        def head_bias(h):
            rows = []
            for i in range(NA_Q_ROWS):
                tiles = []
                for m in range(NA_K_ROWS // 2):
                    e = jnp.clip(row0 + 2 * m - (qrow0 + i) + NA_WIN_R, 0, n_pair - 1)
                    tiles.append(b2_ref[h, e])
                rows.append(jnp.concatenate(tiles, axis=1))
            return jnp.concatenate([jnp.concatenate(rows, axis=0) + row_bias, no_bias], axis=1)

        return _head_chain(q_ref[blk * nq:(blk + 1) * nq, :], heads,
                           lambda: jnp.concatenate([k_ref[pl.ds(start, nk), :], ck], axis=0),
                           lambda: jnp.concatenate([v_ref[pl.ds(start, nk), :], cv], axis=0), acc, blk,
                           bias=lambda: jnp.concatenate([head_bias(h) for h in heads], axis=0))

    pairs = [[h, h + 1] for h in range(0, N_HEADS_C, 2)]
    _round_robin([chain(blk, heads) for blk in range(NA_STEP_BLOCKS) for heads in pairs], stagger=1)
    for blk in range(NA_STEP_BLOCKS):
        o_ref[blk * nq:(blk + 1) * nq, :] = acc[blk].astype(BF16)


def _na_bias_rows(tbl):
    pad = jnp.pad(tbl, ((0, 0), (1, 1), (0, GRID_W - tbl.shape[-1])))
    return jnp.concatenate([pad[:, :-1], pad[:, 1:]], axis=-1)


def _na_lat(qc, kc, vc, cache_k, cache_v, brow, layer):
    step_rows = NA_STEP_BLOCKS * NA_Q_ROWS * GRID_W
    seq_blk = pl.BlockSpec((None, DEC_SEQ, WIDTH_C), lambda b, j: (b, 0, 0))
    cache_blk = pl.BlockSpec((None, None, PAST_LEN, WIDTH_C), lambda b, j: (b, layer, 0, 0))
    q_blk = pl.BlockSpec((None, step_rows, WIDTH_C), lambda b, j: (b, j, 0))
    return pl.pallas_call(
        _na_kernel,
        grid=(DEC_BATCH, DEC_SEQ // step_rows),
        in_specs=[q_blk, seq_blk, seq_blk, cache_blk, cache_blk, _const_spec(brow.shape)],
        out_specs=q_blk,
        out_shape=jax.ShapeDtypeStruct((DEC_BATCH, DEC_SEQ, WIDTH_C), BF16),
        scratch_shapes=[pltpu.VMEM((N_HEADS_C, 2 * NA_WIN_R, GRID_W, 2 * GRID_W), F32)],
        compiler_params=_params(2),
        name="na_lat",
    )(qc, kc, vc, cache_k, cache_v, brow)


def _scan_kernel(uf_ref, ub_ref, bmat_ref, lam_ref, cmat_ref, h0_ref, yf_ref, yb_ref, hfin_ref,
                 hf_ref, hb_ref, st_ref, *, nb, lane_w):
    n_blk = SCAN_ROWS // SCAN_BLOCK
    steps = SCAN_BLOCK // nb
    n_lb = SSM_LANES // lane_w

    @pl.when(pl.program_id(0) == 0)
    def _():
        st_ref[...] = h0_ref[...]

    lanes = [(slice(lb * lane_w, (lb + 1) * lane_w), slice(SSM_LANES + lb * lane_w, SSM_LANES + (lb + 1) * lane_w))
             for lb in range(n_lb)]
    state = [[st_ref[d, c, :, re] for d in range(2) for c in range(2)] for re, _ in lanes]

    def block(i):
        f0 = i * SCAN_BLOCK
        b0 = (n_blk - 1 - i) * SCAN_BLOCK
        hf_ref[f0:f0 + SCAN_BLOCK, :] = _dot(uf_ref[f0:f0 + SCAN_BLOCK, :].astype(BF16), bmat_ref[0])
        hb_ref[b0:b0 + SCAN_BLOCK, :] = _dot(ub_ref[b0:b0 + SCAN_BLOCK, :].astype(BF16), bmat_ref[1])
        yield
        for lb, (re, im) in enumerate(lanes):
            lam = [[jnp.broadcast_to(lam_ref[d, c, :, re], (nb, lane_w)) for c in range(2)] for d in range(2)]
            fr, fi, br, bi = state[lb]
            for k in range(steps):
                rf = slice(f0 + k * nb, f0 + (k + 1) * nb)
                rb = slice(b0 + (steps - 1 - k) * nb, b0 + (steps - k) * nb)
                fr, fi = (lam[0][0] * fr - lam[0][1] * fi + hf_ref[rf, re],
                          lam[0][0] * fi + lam[0][1] * fr + hf_ref[rf, im])
                br, bi = (lam[1][0] * br - lam[1][1] * bi + hb_ref[rb, re],
                          lam[1][0] * bi + lam[1][1] * br + hb_ref[rb, im])
                hf_ref[rf, re] = fr
                hf_ref[rf, im] = fi
                hb_ref[rb, re] = br
                hb_ref[rb, im] = bi
            state[lb] = [fr, fi, br, bi]
        yield
        yf_ref[f0:f0 + SCAN_BLOCK, :] = _dot(hf_ref[f0:f0 + SCAN_BLOCK, :].astype(BF16), cmat_ref[0])
        yb_ref[b0:b0 + SCAN_BLOCK, :] = _dot(hb_ref[b0:b0 + SCAN_BLOCK, :].astype(BF16), cmat_ref[1])

    _round_robin([block(i) for i in range(n_blk)], stagger=1)
    for (re, _), vals in zip(lanes, state):
        for d in range(2):
            for c in range(2):
                st_ref[d, c, :, re] = vals[2 * d + c]
    hfin_ref[...] = st_ref[...]


def _scan_pair_kernel(uf_ref, ub_ref, bmat_ref, lam_ref, cmat_ref, h0_ref, yf_ref, yb_ref,
                      hf_ref, hb_ref, st_ref, *, lane_w):
    half = DEC_BATCH
    tile = 2 * half
    n_blk = SCAN_ROWS // SCAN_BLOCK
    tiles = SCAN_BLOCK // tile
    n_lb = SSM_LANES // lane_w

    @pl.when(pl.program_id(0) == 0)
    def _():
        st_ref[...] = h0_ref[...]

    top = lax.broadcasted_iota(jnp.int32, (tile, lane_w), 0) < half
    swap = lambda a: pltpu.roll(a, half, 0)
    lanes = [(slice(lb * lane_w, (lb + 1) * lane_w), slice(SSM_LANES + lb * lane_w, SSM_LANES + (lb + 1) * lane_w))
             for lb in range(n_lb)]
    state = [[st_ref[0, :, re], st_ref[1, :, re]] for re, _ in lanes]

    def block(i):
        f0 = i * SCAN_BLOCK
        b0 = (n_blk - 1 - i) * SCAN_BLOCK
        hf_ref[f0:f0 + SCAN_BLOCK, :] = _dot(uf_ref[f0:f0 + SCAN_BLOCK, :].astype(BF16), bmat_ref[0])
        hb_ref[b0:b0 + SCAN_BLOCK, :] = _dot(ub_ref[b0:b0 + SCAN_BLOCK, :].astype(BF16), bmat_ref[1])
        yield
        for lb, (re, im) in enumerate(lanes):
            la_r = lam_ref[0, :, re]
            la_i = lam_ref[1, :, re]
            lb_r = swap(la_r)
            lb_i = swap(la_i)
            sr, si = state[lb]
            for m in range(tiles):
                rf = slice(f0 + m * tile, f0 + (m + 1) * tile)
                rb = slice(b0 + (tiles - 1 - m) * tile, b0 + (tiles - m) * tile)
                fr, fi = hf_ref[rf, re], hf_ref[rf, im]
                br, bi = hb_ref[rb, re], hb_ref[rb, im]
                vr = la_r * sr - la_i * si + jnp.where(top, fr, br)
                vi = la_r * si + la_i * sr + jnp.where(top, fi, bi)
                tr = swap(vr)
                ti = swap(vi)
                wr = lb_r * tr - lb_i * ti + jnp.where(top, br, fr)
                wi = lb_r * ti + lb_i * tr + jnp.where(top, bi, fi)
                hf_ref[rf, re] = jnp.where(top, vr, wr)
                hf_ref[rf, im] = jnp.where(top, vi, wi)
                hb_ref[rb, re] = jnp.where(top, wr, vr)
                hb_ref[rb, im] = jnp.where(top, wi, vi)
                sr, si = swap(wr), swap(wi)
            state[lb] = [sr, si]
        yield
        yf_ref[f0:f0 + SCAN_BLOCK, :] = _dot(hf_ref[f0:f0 + SCAN_BLOCK, :].astype(BF16), cmat_ref[0])
        yb_ref[b0:b0 + SCAN_BLOCK, :] = _dot(hb_ref[b0:b0 + SCAN_BLOCK, :].astype(BF16), cmat_ref[1])

    _round_robin([block(i) for i in range(n_blk)], stagger=1)
    for (re, _), (sr, si) in zip(lanes, state):
        st_ref[0, :, re] = sr
        st_ref[1, :, re] = si


def _scan_specs(n_rows):
    n = n_rows // SCAN_ROWS
    fwd = pl.BlockSpec((SCAN_ROWS, SSM_WIDTH), lambda j: (j, 0))
    bwd = pl.BlockSpec((SCAN_ROWS, SSM_WIDTH), lambda j: (n - 1 - j, 0))
    y_shape = jax.ShapeDtypeStruct((n_rows, SSM_WIDTH), F32)
    buf = pltpu.VMEM((SCAN_ROWS, 2 * SSM_LANES), F32)
    return n, fwd, bwd, y_shape, buf


def _scan_ctx(u_rows, bmat, lam, cmat, h0):
    n, fwd, bwd, y_shape, buf = _scan_specs(u_rows.shape[0])
    st_shape = (2, 2, BATCH, SSM_LANES)
    return pl.pallas_call(
        functools.partial(_scan_kernel, nb=BATCH, lane_w=256),
        grid=(n,),
        in_specs=[fwd, bwd, _const_spec(bmat.shape), _const_spec(lam.shape), _const_spec(cmat.shape),
                  _const_spec(st_shape)],
        out_specs=[fwd, bwd, pl.BlockSpec(st_shape, lambda j: (0, 0, 0, 0))],
        out_shape=[y_shape, y_shape, jax.ShapeDtypeStruct(st_shape, F32)],
        scratch_shapes=[buf, buf, pltpu.VMEM(st_shape, F32)],
        compiler_params=_params(1),
        name="scan_ctx",
    )(u_rows, u_rows, bmat, lam, cmat, h0)


def _scan_lat(u_rows, bmat, lam_pair, cmat, h0_pair):
    n, fwd, bwd, y_shape, buf = _scan_specs(u_rows.shape[0])
    st_shape = (2, 2 * DEC_BATCH, SSM_LANES)
    return pl.pallas_call(
        functools.partial(_scan_pair_kernel, lane_w=512),
        grid=(n,),
        in_specs=[fwd, bwd, _const_spec(bmat.shape), _const_spec(st_shape), _const_spec(cmat.shape),
                  _const_spec(st_shape)],
        out_specs=[fwd, bwd],
        out_shape=[y_shape, y_shape],
        scratch_shapes=[buf, buf, pltpu.VMEM(st_shape, F32)],
        compiler_params=_params(1),
        name="scan_lat",
    )(u_rows, u_rows, bmat, lam_pair, cmat, h0_pair)


def _ssm_discretise(lam_re, lam_im, log_step, b_re, b_im, c_re, c_im):
    step = jnp.exp(log_step.astype(F32))[..., None]
    lr, li = lam_re.astype(F32), lam_im.astype(F32)
    mag = jnp.exp(lr * step)
    bar_r = mag * jnp.cos(li * step)
    bar_i = mag * jnp.sin(li * step)
    den = lr * lr + li * li
    coef_r = (((bar_r - 1) * lr + bar_i * li) / den)[..., None]
    coef_i = ((bar_i * lr - (bar_r - 1) * li) / den)[..., None]
    br, bi = b_re.astype(F32), b_im.astype(F32)
    bbar_r = coef_r * br - coef_i * bi
    bbar_i = coef_r * bi + coef_i * br
    eye = jnp.eye(SSM_GROUPS, dtype=F32)
    blk_b = lambda a: jnp.einsum('dgpc,gh->dgchp', a, eye).reshape(2, SSM_WIDTH, SSM_LANES)
    bmat = jnp.concatenate([blk_b(bbar_r), blk_b(bbar_i)], axis=-1)
    blk_c = lambda a: jnp.einsum('dgcp,gh->dgphc', a, eye).reshape(2, SSM_LANES, SSM_WIDTH)
    cmat = jnp.concatenate([blk_c(c_re.astype(F32)), -blk_c(c_im.astype(F32))], axis=1)
    lam_flat = jnp.stack([bar_r, bar_i], axis=1).reshape(2, 2, 1, SSM_LANES)
    return bmat.astype(BF16), lam_flat, cmat.astype(BF16)


def _merge_kernel(x_ref, oa_ref, yf_ref, yb_ref, u_ref, oc_ref, g_ref, mod_ref, d_ref, wglu_ref,
                  wa_ref, wb_ref, wc_ref, wo_ref, g2_ref, wgu_ref, wd_ref, fg_ref, o_ref, *slabs, final):
    nb, tt, _ = x_ref.shape
    tc = tt // ROW_CHAINS
    rows = nb * tc
    mod = mod_ref[...]
    gate1 = mod[:, :, 2 * D_MODEL:3 * D_MODEL]
    shift2 = mod[:, :, 3 * D_MODEL:4 * D_MODEL]
    scale2 = mod[:, :, 4 * D_MODEL:5 * D_MODEL]
    gate2 = mod[:, :, 5 * D_MODEL:6 * D_MODEL]
    def chain(c):
        ts = slice(c * tc, (c + 1) * tc)
        rs = slice(c * rows, (c + 1) * rows)
        flat = lambda ref, lo=0, hi=None: ref[:, ts, lo:hi].reshape(rows, -1)
        y = _gelu_tanh(yf_ref[rs, :] + yb_ref[rs, :] + d_ref[...] * u_ref[rs, :])
        ob = y * jax.nn.sigmoid(_dot(y.astype(BF16), wglu_ref[...]))
        ob = _to_batch_major(ob, slabs[2 * c:2 * c + 2], nb)
        yield
        gate = lambda i: jax.nn.sigmoid(flat(g_ref, i * D_MODEL, (i + 1) * D_MODEL))
        merged = (gate(0) * _dot(flat(oa_ref), wa_ref[...])
                  + gate(1) * _dot(ob.astype(BF16), wb_ref[...])
                  + gate(2) * _dot(flat(oc_ref), wc_ref[...]))
        yield
        x1 = x_ref[:, ts, :] + gate1 * _dot(merged.astype(BF16), wo_ref[...]).reshape(nb, tc, D_MODEL)
        h2 = _rms(x1, g2_ref[...]) * (1 + scale2) + shift2
        yield
        gu = _dot(h2.reshape(rows, D_MODEL).astype(BF16), wgu_ref[...])
        act = _silu(gu[:, :D_FF]) * gu[:, D_FF:]
        yield
        x2 = x1 + gate2 * _dot(act.astype(BF16), wd_ref[...]).reshape(nb, tc, D_MODEL)
        if final:
            x2 = _rms(x2, fg_ref[...])
        o_ref[:, ts, :] = x2

    _round_robin([chain(c) for c in range(ROW_CHAINS)])


def _merge(x, oa, yf, yb, u_rows, oc, g, mod_rows, weights, final_g, layer, *, name):
    final = layer == DEPTH - 1
    nb, seq, _ = x.shape
    tt = ROW_TILE // nb
    slab = lambda w: pl.BlockSpec((nb, tt, w), lambda j: (0, j, 0))
    tmaj = pl.BlockSpec((ROW_TILE, SSM_WIDTH), lambda j: (j, 0))
    return pl.pallas_call(
        functools.partial(_merge_kernel, final=final),
        grid=(seq // tt,),
        in_specs=[slab(D_MODEL), slab(WIDTH_A), tmaj, tmaj, tmaj, slab(WIDTH_C), slab(N_BRANCH * D_MODEL),
                  _const_spec(mod_rows.shape)] + [_layer_spec(w, layer) for w in weights]
                 + [_const_spec(final_g.shape)],
        out_specs=slab(D_MODEL),
        out_shape=jax.ShapeDtypeStruct((nb, seq, D_MODEL), F32),
        scratch_shapes=[pltpu.VMEM((ROW_TILE // ROW_CHAINS, LANES), F32)] * (2 * ROW_CHAINS),
        compiler_params=_params(1),
        name=name + ("_final" if final else ""),
    )(x, oa, yf, yb, u_rows, oc, g, mod_rows, *weights, final_g)


def _rope_tables():
    t = jnp.arange(DEC_SEQ)
    row = (t // GRID_W).astype(F32)
    col = (t % GRID_W).astype(F32)
    inv = 1.0 / (ROPE_THETA ** (jnp.arange(ROT_FREQS, dtype=F32) / ROT_FREQS))
    ar = row[:, None] * inv[None]
    ac = col[:, None] * inv[None]
    cos = jnp.concatenate([jnp.cos(ar), jnp.cos(ar), jnp.cos(ac), jnp.cos(ac)], axis=-1)
    sin = jnp.concatenate([-jnp.sin(ar), jnp.sin(ar), -jnp.sin(ac), jnp.sin(ac)], axis=-1)
    return jnp.tile(cos, (1, 2)), jnp.tile(sin, (1, 2))


def kernel(x_prompt, x_sample, c, cache_ga_k, cache_ga_v, cache_na_k, cache_na_v, state_ssm, c_ctx, w_mod, b_mod, norm1_g, w_in, qn_g, kn_g, ssm_lam_re, ssm_lam_im, ssm_log_step, ssm_b_re, ssm_b_im, ssm_c_re, ssm_c_im, ssm_d, ssm_w_glu, na_bias, w_br_a, w_br_b, w_br_c, w_out, norm2_g, w_ffn_gu, w_ffn_d, final_g):
    cvec = jnp.concatenate([c_ctx[None, :], c, jnp.zeros((N_MOD_ROWS - 1 - DEC_BATCH, D_MODEL), F32)], axis=0)
    mod = _adaln(cvec, w_mod, b_mod).reshape(DEPTH, N_MOD_ROWS, 1, 6 * D_MODEL)

    seg = jnp.kron(jnp.eye(MXU_DIM // HEAD_DIM, dtype=F32),
                   jnp.full((HEAD_DIM, HEAD_DIM), 1.0 / HEAD_DIM, F32)).astype(BF16)
    rope_tabs = _rope_tables()
    fg = final_g.reshape(1, D_MODEL)
    ck_a = cache_ga_k.reshape(DEC_BATCH, DEPTH, PAST_LEN, KV_WIDTH_A)
    cv_a = cache_ga_v.reshape(DEC_BATCH, DEPTH, PAST_LEN, KV_WIDTH_A)
    ck_c = cache_na_k.reshape(DEC_BATCH, DEPTH, PAST_LEN, WIDTH_C)
    cv_c = cache_na_v.reshape(DEC_BATCH, DEPTH, PAST_LEN, WIDTH_C)
    zero_state = jnp.zeros((2, 2, BATCH, SSM_LANES), F32)

    row = lambda p: p.reshape(DEPTH, 1, p.shape[-1])
    g1 = row(norm1_g)
    w_in_b = w_in.astype(BF16)
    qg = row(jnp.tile(qn_g, (1, N_HEADS_A)))
    kg = row(jnp.tile(kn_g, (1, N_KV_A)))
    merge_w = [row(ssm_d), ssm_w_glu.astype(BF16), w_br_a.astype(BF16), w_br_b.astype(BF16), w_br_c.astype(BF16),
               w_out.astype(BF16), row(norm2_g), w_ffn_gu.astype(BF16), w_ffn_d.astype(BF16)]

    xp, xs = x_prompt, x_sample
    cache = ()
    ssm_st = []
    for l in range(DEPTH):
        bmat, lam, cmat = _ssm_discretise(ssm_lam_re[l], ssm_lam_im[l], ssm_log_step[l], ssm_b_re[l], ssm_b_im[l],
                                          ssm_c_re[l], ssm_c_im[l])
        mod_ctx = mod[l, 0:1]
        mod_lat = mod[l, 1:1 + DEC_BATCH]

        qa, krep, vrep, ka, va, u_rows, qc, kc, vc, g = _inproj(xp, mod_ctx, g1, w_in_b, seg, qg, kg, l,
                                                                prev_cache=cache)
        cache = (ka, va, kc, vc)
        oa, oc = _attn_ctx(qa, krep, vrep, qc, kc, vc, l)
        yf, yb, hfin = _scan_ctx(u_rows, bmat, lam, cmat, zero_state)
        xp = _merge(xp, oa, yf, yb, u_rows, oc, g, mod_ctx, merge_w, fg, l, name="merge_ctx")
        ssm_st.append(jnp.transpose(hfin, (2, 0, 1, 3)))

        qa, krep, vrep, u_rows, qc, kc, vc, g = _inproj(xs, mod_lat, g1, w_in_b, seg, qg, kg, l, rope_tabs=rope_tabs)
        oa = _attn_a_lat(qa, krep, vrep, ck_a, cv_a, l)
        oc = _na_lat(qc, kc, vc, ck_c, cv_c, _na_bias_rows(na_bias[l]), l)
        h0 = jnp.transpose(state_ssm[:, l].reshape(DEC_BATCH, 2, 2, SSM_LANES), (2, 1, 0, 3))
        h0 = h0.reshape(2, 2 * DEC_BATCH, SSM_LANES)
        lam_pair = jnp.broadcast_to(jnp.transpose(lam, (1, 0, 2, 3)), (2, 2, DEC_BATCH, SSM_LANES))
        lam_pair = lam_pair.reshape(2, 2 * DEC_BATCH, SSM_LANES)
        yf, yb = _scan_lat(u_rows, bmat, lam_pair, cmat, h0)
        xs = _merge(xs, oa, yf, yb, u_rows, oc, g, mod_lat, merge_w, fg, l, name="merge_lat")

    ga_k, ga_v, na_k, na_v = cache
    new_ssm = jnp.stack(ssm_st, axis=1).reshape(BATCH, DEPTH, 2, 2, SSM_GROUPS, SSM_STATE)
    return (xp, xs,
            ga_k.reshape(BATCH, DEPTH, SEQ, N_KV_A, HEAD_DIM), ga_v.reshape(BATCH, DEPTH, SEQ, N_KV_A, HEAD_DIM),
            na_k.reshape(BATCH, DEPTH, SEQ, N_HEADS_C, HEAD_DIM), na_v.reshape(BATCH, DEPTH, SEQ, N_HEADS_C, HEAD_DIM),
            new_ssm)
```

```python
import functools
import math

import jax
import jax.numpy as jnp
from jax import lax
from jax.experimental import pallas as pl
from jax.experimental.pallas import tpu as pltpu

D_MODEL = 1024
BATCH = 16
SEQ = 256
DEPTH = 2
DEC_BATCH = 4
DEC_SEQ = 2048
PAST_LEN = 256
GRID_W = 64
GRID_ROWS = DEC_SEQ // GRID_W
HEAD_DIM = 64
N_HEADS_A = 8
N_KV_A = 2
REP_A = N_HEADS_A // N_KV_A
N_HEADS_C = 4
SSM_WIDTH = 256
SSM_GROUP = 16
SSM_GROUPS = SSM_WIDTH // SSM_GROUP
SSM_STATE = 64
SSM_LANES = SSM_GROUPS * SSM_STATE
NA_WIN_R = 8
NA_WIN_C = 16
D_FF = -(-8 * D_MODEL // (3 * 256)) * 256
ROPE_THETA = 10000.0
ROT_HALF = HEAD_DIM // 2
ROT_FREQS = ROT_HALF // 2
WIDTH_A = N_HEADS_A * HEAD_DIM
KV_WIDTH_A = N_KV_A * HEAD_DIM
WIDTH_C = N_HEADS_C * HEAD_DIM
N_BRANCH = 3
IN_WIDTH = WIDTH_A + 2 * KV_WIDTH_A + SSM_WIDTH + 3 * WIDTH_C + N_BRANCH * D_MODEL
EPS = 1e-6

OFF_QA = 0
OFF_KA = OFF_QA + WIDTH_A
OFF_VA = OFF_KA + KV_WIDTH_A
OFF_U = OFF_VA + KV_WIDTH_A
OFF_QC = OFF_U + SSM_WIDTH
OFF_KC = OFF_QC + WIDTH_C
OFF_VC = OFF_KC + WIDTH_C
OFF_G = OFF_VC + WIDTH_C

N_MOD_ROWS = 8
ROW_TILE = 256
Q_TILE_A = 512
Q_CHAIN_ROWS = 256
NA_Q_ROWS = 2
NA_K_ROWS = 10
NA_R0_MAX = GRID_ROWS - NA_K_ROWS
NA_STEP_BLOCKS = 2
CTX_STEP_SEQS = 2
ROW_CHAINS = 2
SCAN_ROWS = 1024
SCAN_BLOCK = 256
LOG2_E = math.log2(math.e)
Q_SCALE = HEAD_DIM ** -0.5 * LOG2_E
LANES = 128
MXU_DIM = 256
NEG = -1e30
VMEM_LIMIT_V7X = 56 * 1024 * 1024

F32 = jnp.float32
BF16 = jnp.bfloat16


def _dot(a, b):
    return jnp.dot(a, b, preferred_element_type=F32)


def _dot_t(a, b):
    return lax.dot_general(a, b, (((1,), (1,)), ((), ())), preferred_element_type=F32)


def _params(n_axes):
    return pltpu.CompilerParams(dimension_semantics=("arbitrary",) * n_axes,
                                vmem_limit_bytes=VMEM_LIMIT_V7X)


def _const_spec(shape):
    nd = len(shape)
    return pl.BlockSpec(shape, lambda *_: (0,) * nd, pipeline_mode=pl.Buffered(1))


def _layer_spec(stacked, layer):
    shape = stacked.shape[1:]
    return pl.BlockSpec((None,) + shape, lambda *_: (layer,) + (0,) * len(shape), pipeline_mode=pl.Buffered(1))


def _rms(x, g):
    return x * lax.rsqrt(jnp.mean(x * x, axis=-1, keepdims=True) + EPS) * g


def _silu(x):
    return x * jax.nn.sigmoid(x)


def _gelu_tanh(x):
    c = math.sqrt(2.0 / math.pi)
    return x * (0.5 * (1.0 + jnp.tanh(c * (x + 0.044715 * (x * x * x)))))


def _seg_rms(x, seg, g):
    w = seg.shape[0]
    x2 = x * x
    hi = x2.astype(BF16)
    lo = (x2 - hi.astype(F32)).astype(BF16)
    ms = jnp.concatenate([_dot(hi[:, c:c + w], seg) + _dot(lo[:, c:c + w], seg)
                          for c in range(0, x.shape[1], w)], axis=1)
    return x * lax.rsqrt(ms + EPS) * g


def _rope(x, cos, sin_signed):
    w = x.shape[-1]
    lane = lax.broadcasted_iota(jnp.int32, x.shape, 1)
    first = (lane & ROT_FREQS) == 0
    partner = jnp.where(first, pltpu.roll(x, w - ROT_FREQS, 1), pltpu.roll(x, ROT_FREQS, 1))
    return x * cos + partner * sin_signed


def _rep_heads(kv):
    lane = lax.broadcasted_iota(jnp.int32, kv.shape, 1)
    swapped = pltpu.roll(kv, HEAD_DIM, 1)
    lo = lane < HEAD_DIM
    h0 = jnp.where(lo, kv, swapped)
    h1 = jnp.where(lo, swapped, kv)
    return jnp.concatenate([h0, h0, h1, h1], axis=1)


def _round_robin(chains, stagger=0):
    done = [False] * len(chains)
    rnd = 0
    while not all(done):
        for i, ch in enumerate(chains):
            if not done[i] and rnd >= i * stagger:
                try:
                    next(ch)
                except StopIteration:
                    done[i] = True
        rnd += 1


def _to_time_major(val, slabs, nb):
    tt = val.shape[0] // nb
    for s, slab in enumerate(slabs):
        for b in range(nb):
            slab[pl.ds(b, tt, stride=nb), :] = val[b * tt:(b + 1) * tt, s * LANES:(s + 1) * LANES]
    return jnp.concatenate([slab[...] for slab in slabs], axis=1)


def _to_batch_major(val, slabs, nb):
    tt = val.shape[0] // nb
    for s, slab in enumerate(slabs):
        slab[...] = val[:, s * LANES:(s + 1) * LANES]
    return jnp.concatenate(
        [jnp.concatenate([slab[pl.ds(b, tt, stride=nb), :] for slab in slabs], axis=1) for b in range(nb)], axis=0)


def _adaln_kernel(c_ref, w_ref, b_ref, o_ref):
    s = _silu(c_ref[...])
    o_ref[...] = _dot(s.astype(BF16), w_ref[...].astype(BF16)) + b_ref[...]


def _adaln(cvec, w_mod, b_mod):
    n_col = 6 * D_MODEL
    tn = n_col // 4
    return pl.pallas_call(
        _adaln_kernel,
        grid=(DEPTH, n_col // tn),
        in_specs=[pl.BlockSpec((N_MOD_ROWS, D_MODEL), lambda l, n: (0, 0)),
                  pl.BlockSpec((None, D_MODEL, tn), lambda l, n: (l, 0, n)),
                  pl.BlockSpec((None, 1, tn), lambda l, n: (l, 0, n))],
        out_specs=pl.BlockSpec((None, N_MOD_ROWS, tn), lambda l, n: (l, 0, n)),
        out_shape=jax.ShapeDtypeStruct((DEPTH, N_MOD_ROWS, n_col), F32),
        compiler_params=_params(2),
        name="adaln",
    )(cvec, w_mod, b_mod.reshape(DEPTH, 1, n_col))


def _inproj_kernel(*refs, latent, n_alias):
    if latent:
        (x_ref, xn_ref, mod_ref, g1_ref, w_ref, seg_ref, qg_ref, kg_ref, cos_ref, sin_ref,
         qa_ref, krep_ref, vrep_ref, u_ref, qc_ref, kc_ref, vc_ref, g_ref, hb_ref, slab0, slab1) = refs
    else:
        x_ref, xn_ref, mod_ref, g1_ref, w_ref, seg_ref, qg_ref, kg_ref = refs[:8]
        (qa_ref, krep_ref, vrep_ref, ka_ref, va_ref, u_ref, qc_ref, kc_ref, vc_ref, g_ref,
         hb_ref, slab0, slab1) = refs[8 + n_alias:]
    nb, tt, _ = x_ref.shape
    rows = nb * tt
    mod = mod_ref[...]
    shift = mod[:, :, 0:D_MODEL]
    scale = mod[:, :, D_MODEL:2 * D_MODEL]
    seg = seg_ref[...]
    slot = pl.program_id(0) % 2

    def normed(ref):
        h = _rms(ref[...], g1_ref[...]) * (1 + scale) + shift
        return h.reshape(rows, D_MODEL).astype(BF16)

    def put(ref, val):
        val = val.reshape(nb, tt, ref.shape[-1]).astype(ref.dtype)
        if len(ref.shape) == 3:
            ref[...] = val
        else:
            ref[:, 0] = val
            ref[:, 1:] = jnp.zeros((nb, DEPTH - 1, tt, ref.shape[-1]), ref.dtype)

    @pl.when(pl.program_id(0) == 0)
    def _():
        hb_ref[0] = normed(x_ref)

    def project():
        hb = hb_ref[slot]
        u_ref[...] = _to_time_major(_dot(hb, w_ref[:, OFF_U:OFF_QC]), (slab0, slab1), nb)
        put(qc_ref, _dot(hb, w_ref[:, OFF_QC:OFF_KC]) * Q_SCALE)
        put(kc_ref, _dot(hb, w_ref[:, OFF_KC:OFF_VC]))
        put(vc_ref, _dot(hb, w_ref[:, OFF_VC:OFF_G]))
        yield
        qa = _seg_rms(_dot(hb, w_ref[:, OFF_QA:OFF_KA]), seg, qg_ref[...])
        ka = _seg_rms(_dot(hb, w_ref[:, OFF_KA:OFF_VA]), seg[0:KV_WIDTH_A, 0:KV_WIDTH_A], kg_ref[...])
        va = _dot(hb, w_ref[:, OFF_VA:OFF_U])
        if latent:
            cos = jnp.concatenate([cos_ref[...]] * nb, axis=0)
            sin = jnp.concatenate([sin_ref[...]] * nb, axis=0)
            qa = _rope(qa, jnp.concatenate([cos] * REP_A, axis=1), jnp.concatenate([sin] * REP_A, axis=1))
            ka = _rope(ka, cos, sin)
        else:
            put(ka_ref, ka)
            put(va_ref, va)
        put(qa_ref, qa * Q_SCALE)
        put(krep_ref, _rep_heads(ka))
        put(vrep_ref, _rep_heads(va))
        yield
        for cidx in range(N_BRANCH):
            lo = OFF_G + cidx * D_MODEL
            g_ref[:, :, cidx * D_MODEL:(cidx + 1) * D_MODEL] = _dot(hb, w_ref[:, lo:lo + D_MODEL]).reshape(
                nb, tt, D_MODEL)

    def prepare_next():
        yield
        hb_ref[1 - slot] = normed(xn_ref)

    _round_robin([project(), prepare_next()])


def _inproj(x, mod_rows, g1, w_in, seg, qg, kg, layer, *, rope_tabs=None, prev_cache=()):
    latent = rope_tabs is not None
    nb, seq, _ = x.shape
    tt = ROW_TILE // nb
    slab = lambda w: pl.BlockSpec((nb, tt, w), lambda j: (0, j, 0))
    act = lambda w, dt: jax.ShapeDtypeStruct((nb, seq, w), dt)
    if layer == 0:
        cache_slab = lambda w: pl.BlockSpec((nb, DEPTH, tt, w), lambda j: (0, 0, j, 0))
    else:
        cache_slab = lambda w: pl.BlockSpec((nb, None, tt, w), lambda j: (0, layer, j, 0))
    cache = lambda w: jax.ShapeDtypeStruct((nb, DEPTH, seq, w), F32)
    n_steps = seq // tt
    next_slab = pl.BlockSpec((nb, tt, D_MODEL), lambda j: (0, jnp.minimum(j + 1, n_steps - 1), 0))
    in_specs = [slab(D_MODEL), next_slab, _const_spec(mod_rows.shape), _layer_spec(g1, layer),
                _layer_spec(w_in, layer), _const_spec(seg.shape), _layer_spec(qg, layer), _layer_spec(kg, layer)]
    args = [x, x, mod_rows, g1, w_in, seg, qg, kg]
    out_shape = [act(WIDTH_A, BF16), act(REP_A * KV_WIDTH_A, BF16), act(REP_A * KV_WIDTH_A, BF16)]
    out_specs = [slab(WIDTH_A), slab(REP_A * KV_WIDTH_A), slab(REP_A * KV_WIDTH_A)]
    aliases = {}
    if latent:
        in_specs += [pl.BlockSpec((tt, 2 * HEAD_DIM), lambda j: (j, 0))] * 2
        args += list(rope_tabs)
        kv_c = [act(WIDTH_C, BF16)] * 2
        kv_c_specs = [slab(WIDTH_C)] * 2
    else:
        aliases = {len(args) + i: o for i, o in enumerate((3, 4, 7, 8)[:len(prev_cache)])}
        in_specs += [pl.BlockSpec(memory_space=pl.ANY)] * len(prev_cache)
        args += list(prev_cache)
        out_shape += [cache(KV_WIDTH_A)] * 2
        out_specs += [cache_slab(KV_WIDTH_A)] * 2
        kv_c = [cache(WIDTH_C)] * 2
        kv_c_specs = [cache_slab(WIDTH_C)] * 2
    out_shape += [jax.ShapeDtypeStruct((seq * nb, SSM_WIDTH), F32),
                  act(WIDTH_C, BF16)] + kv_c + [act(N_BRANCH * D_MODEL, F32)]
    out_specs += [pl.BlockSpec((ROW_TILE, SSM_WIDTH), lambda j: (j, 0)),
                  slab(WIDTH_C)] + kv_c_specs + [slab(N_BRANCH * D_MODEL)]
    return pl.pallas_call(
        functools.partial(_inproj_kernel, latent=latent, n_alias=len(prev_cache)),
        grid=(seq // tt,),
        in_specs=in_specs,
        out_specs=out_specs,
        out_shape=out_shape,
        scratch_shapes=[pltpu.VMEM((2, ROW_TILE, D_MODEL), BF16)] + [pltpu.VMEM((ROW_TILE, LANES), F32)] * 2,
        input_output_aliases=aliases,
        compiler_params=_params(1),
        name="inproj_lat" if latent else "inproj_ctx",
    )(*args)


def _head_mask(shape, head):
    lane = lax.broadcasted_iota(jnp.int32, shape, 1)
    return (lane // HEAD_DIM) == head


def _head_chain(q, heads, keys, vals, acc, key, bias=None):
    m_rows = q.shape[0]
    masks = [_head_mask(q.shape, h) for h in heads]
    qs = jnp.concatenate([jnp.where(hm, q, jnp.zeros_like(q)) for hm in masks], axis=0)
    s = _dot_t(qs, keys())
    if bias is not None:
        s = s + bias()
    yield
    p = jnp.exp2(s - jnp.max(s, axis=-1, keepdims=True))
    l = jnp.sum(p, axis=-1, keepdims=True)
    p = p.astype(BF16)
    yield
    o = _dot(p, vals()) / l
    for i, hm in enumerate(masks):
        acc[key] = jnp.where(hm, o[i * m_rows:(i + 1) * m_rows], acc[key])


def _attn_a_kernel(q_ref, kn_ref, vn_ref, ck_ref, cv_ref, o_ref, k_ref, v_ref):
    @pl.when(pl.program_id(1) == 0)
    def _():
        k_ref[0:PAST_LEN, :] = _rep_heads(ck_ref[...]).astype(BF16)
        v_ref[0:PAST_LEN, :] = _rep_heads(cv_ref[...]).astype(BF16)
        k_ref[PAST_LEN:, :] = kn_ref[...]
        v_ref[PAST_LEN:, :] = vn_ref[...]

    gw = REP_A * HEAD_DIM
    n_sub = q_ref.shape[0] // Q_CHAIN_ROWS
    acc = {(t, g): jnp.zeros((Q_CHAIN_ROWS, gw), F32) for t in range(n_sub) for g in range(N_KV_A)}
    chains = []
    for t in range(n_sub):
        rows = slice(t * Q_CHAIN_ROWS, (t + 1) * Q_CHAIN_ROWS)
        for g in range(N_KV_A):
            sl = slice(g * gw, (g + 1) * gw)
            for j in range(REP_A):
                chains.append(_head_chain(q_ref[rows, sl], [j], lambda sl=sl: k_ref[:, sl],
                                          lambda sl=sl: v_ref[:, sl], acc, (t, g)))
    _round_robin(chains, stagger=1)
    for (t, g), val in acc.items():
        o_ref[t * Q_CHAIN_ROWS:(t + 1) * Q_CHAIN_ROWS, g * gw:(g + 1) * gw] = val.astype(BF16)


def _attn_a_lat(qa, krep, vrep, cache_k, cache_v, layer):
    seq_blk = pl.BlockSpec((None, DEC_SEQ, WIDTH_A), lambda b, t: (b, 0, 0))
    cache_blk = pl.BlockSpec((None, None, PAST_LEN, KV_WIDTH_A), lambda b, t: (b, layer, 0, 0))
    q_blk = pl.BlockSpec((None, Q_TILE_A, WIDTH_A), lambda b, t: (b, t, 0))
    return pl.pallas_call(
        _attn_a_kernel,
        grid=(DEC_BATCH, DEC_SEQ // Q_TILE_A),
        in_specs=[q_blk, seq_blk, seq_blk, cache_blk, cache_blk],
        out_specs=q_blk,
        out_shape=jax.ShapeDtypeStruct((DEC_BATCH, DEC_SEQ, WIDTH_A), BF16),
        scratch_shapes=[pltpu.VMEM((PAST_LEN + DEC_SEQ, WIDTH_A), BF16)] * 2,
        compiler_params=_params(2),
        name="attn_a_lat",
    )(qa, krep, vrep, cache_k, cache_v)


def _attn_ctx_kernel(qa_ref, ka_ref, va_ref, qc_ref, kc_ref, vc_ref, oa_ref, oc_ref):
    gw = REP_A * HEAD_DIM
    acc = {}
    chains = []
    for b in range(CTX_STEP_SEQS):
        for g in range(N_KV_A):
            sl = slice(g * gw, (g + 1) * gw)
            acc["a", b, g] = jnp.zeros((SEQ, gw), F32)
            for j in range(REP_A):
                chains.append(_head_chain(qa_ref[b, :, sl], [j], lambda b=b, sl=sl: ka_ref[b, :, sl],
                                          lambda b=b, sl=sl: va_ref[b, :, sl], acc, ("a", b, g)))
        acc["c", b] = jnp.zeros((SEQ, WIDTH_C), F32)
        for h in range(0, N_HEADS_C, 2):
            chains.append(_head_chain(qc_ref[b], [h, h + 1], lambda b=b: kc_ref[b].astype(BF16),
                                      lambda b=b: vc_ref[b].astype(BF16), acc, ("c", b)))
    _round_robin(chains, stagger=1)
    for b in range(CTX_STEP_SEQS):
        for g in range(N_KV_A):
            oa_ref[b, :, g * gw:(g + 1) * gw] = acc["a", b, g].astype(BF16)
        oc_ref[b] = acc["c", b].astype(BF16)


def _attn_ctx(qa, krep, vrep, qc, kc, vc, layer):
    blk = lambda w: pl.BlockSpec((CTX_STEP_SEQS, SEQ, w), lambda i: (i, 0, 0))
    kv_blk = pl.BlockSpec((CTX_STEP_SEQS, None, SEQ, WIDTH_C), lambda i: (i, layer, 0, 0))
    return pl.pallas_call(
        _attn_ctx_kernel,
        grid=(BATCH // CTX_STEP_SEQS,),
        in_specs=[blk(WIDTH_A), blk(WIDTH_A), blk(WIDTH_A), blk(WIDTH_C), kv_blk, kv_blk],
        out_specs=[blk(WIDTH_A), blk(WIDTH_C)],
        out_shape=[jax.ShapeDtypeStruct((BATCH, SEQ, WIDTH_A), BF16),
                   jax.ShapeDtypeStruct((BATCH, SEQ, WIDTH_C), BF16)],
        compiler_params=_params(1),
        name="attn_ctx",
    )(qa, krep, vrep, qc, kc, vc)


def _na_kernel(q_ref, k_ref, v_ref, ck_ref, cv_ref, brow_ref, o_ref, b2_ref):
    nq = NA_Q_ROWS * GRID_W
    nk = NA_K_ROWS * GRID_W
    n_pair = 2 * NA_WIN_R

    @pl.when((pl.program_id(0) == 0) & (pl.program_id(1) == 0))
    def _():
        shp = (GRID_W, 2 * GRID_W)
        c = lax.broadcasted_iota(jnp.int32, shp, 0)
        kc = lax.broadcasted_iota(jnp.int32, shp, 1) & (GRID_W - 1)
        c0 = jnp.clip(c - NA_WIN_C // 2, 0, GRID_W - NA_WIN_C)
        col_bias = jnp.where(kc < c0, NEG, jnp.where(kc >= c0 + NA_WIN_C, NEG, 0.0))
        for h in range(N_HEADS_C):
            for e in range(n_pair):
                row = jnp.broadcast_to(brow_ref[h, e:e + 1, :], shp)
                toep = pltpu.roll(row, 2 * GRID_W - (NA_WIN_C - 1), 1, stride=1, stride_axis=0)
                b2_ref[h, e] = jnp.where(col_bias < 0.0, NEG, toep * LOG2_E)

    no_bias = jnp.zeros((nq, PAST_LEN), F32)
    ck = ck_ref[...].astype(BF16)
    cv = cv_ref[...].astype(BF16)
    acc = [jnp.zeros((nq, WIDTH_C), F32) for _ in range(NA_STEP_BLOCKS)]

    def chain(blk, heads):
        qrow0 = NA_Q_ROWS * (NA_STEP_BLOCKS * pl.program_id(1) + blk)
        row0 = jnp.clip(qrow0 - NA_WIN_R // 2, 0, NA_R0_MAX)
        start = pl.multiple_of(row0 * GRID_W, GRID_W)
        qrow = qrow0 + lax.broadcasted_iota(jnp.int32, (nq, nk), 0) // GRID_W
        krow = row0 + lax.broadcasted_iota(jnp.int32, (nq, nk), 1) // GRID_W
        win0 = jnp.clip(qrow - NA_WIN_R // 2, 0, GRID_ROWS - NA_WIN_R)
        row_bias = jnp.where(krow < win0, NEG, jnp.where(krow >= win0 + NA_WIN_R, NEG, 0.0))

        def head_bias(h):
            rows = []
            for i in range(NA_Q_ROWS):
                tiles = []
                for m in range(NA_K_ROWS // 2):
                    e = jnp.clip(row0 + 2 * m - (qrow0 + i) + NA_WIN_R, 0, n_pair - 1)
                    tiles.append(b2_ref[h, e])
                rows.append(jnp.concatenate(tiles, axis=1))
            return jnp.concatenate([jnp.concatenate(rows, axis=0) + row_bias, no_bias], axis=1)

        return _head_chain(q_ref[blk * nq:(blk + 1) * nq, :], heads,
                           lambda: jnp.concatenate([k_ref[pl.ds(start, nk), :], ck], axis=0),
                           lambda: jnp.concatenate([v_ref[pl.ds(start, nk), :], cv], axis=0), acc, blk,
                           bias=lambda: jnp.concatenate([head_bias(h) for h in heads], axis=0))

    pairs = [[h, h + 1] for h in range(0, N_HEADS_C, 2)]
    _round_robin([chain(blk, heads) for blk in range(NA_STEP_BLOCKS) for heads in pairs], stagger=1)
    for blk in range(NA_STEP_BLOCKS):
        o_ref[blk * nq:(blk + 1) * nq, :] = acc[blk].astype(BF16)


def _na_bias_rows(tbl):
    pad = jnp.pad(tbl, ((0, 0), (1, 1), (0, GRID_W - tbl.shape[-1])))
    return jnp.concatenate([pad[:, :-1], pad[:, 1:]], axis=-1)


def _na_lat(qc, kc, vc, cache_k, cache_v, brow, layer):
    step_rows = NA_STEP_BLOCKS * NA_Q_ROWS * GRID_W
    seq_blk = pl.BlockSpec((None, DEC_SEQ, WIDTH_C), lambda b, j: (b, 0, 0))
    cache_blk = pl.BlockSpec((None, None, PAST_LEN, WIDTH_C), lambda b, j: (b, layer, 0, 0))
    q_blk = pl.BlockSpec((None, step_rows, WIDTH_C), lambda b, j: (b, j, 0))
    return pl.pallas_call(
        _na_kernel,
        grid=(DEC_BATCH, DEC_SEQ // step_rows),
        in_specs=[q_blk, seq_blk, seq_blk, cache_blk, cache_blk, _const_spec(brow.shape)],
        out_specs=q_blk,
        out_shape=jax.ShapeDtypeStruct((DEC_BATCH, DEC_SEQ, WIDTH_C), BF16),
        scratch_shapes=[pltpu.VMEM((N_HEADS_C, 2 * NA_WIN_R, GRID_W, 2 * GRID_W), F32)],
        compiler_params=_params(2),
        name="na_lat",
    )(qc, kc, vc, cache_k, cache_v, brow)


def _scan_kernel(uf_ref, ub_ref, bmat_ref, lam_ref, cmat_ref, h0_ref, yf_ref, yb_ref, hfin_ref,
                 *scratch, nb, lane_w):
    n_blk = SCAN_ROWS // SCAN_BLOCK
    hf_blk, hb_blk, st_ref = scratch[:n_blk], scratch[n_blk:2 * n_blk], scratch[2 * n_blk]
    steps = SCAN_BLOCK // nb
    n_lb = SSM_LANES // lane_w

    @pl.when(pl.program_id(0) == 0)
    def _():
        st_ref[...] = h0_ref[...]

    lanes = [(slice(lb * lane_w, (lb + 1) * lane_w), slice(SSM_LANES + lb * lane_w, SSM_LANES + (lb + 1) * lane_w))
             for lb in range(n_lb)]
    state = [[st_ref[d, c, :, re] for d in range(2) for c in range(2)] for re, _ in lanes]

    def block(i):
        f0 = i * SCAN_BLOCK
        b0 = (n_blk - 1 - i) * SCAN_BLOCK
        hf_ref, hb_ref = hf_blk[i], hb_blk[i]
        uf = uf_ref[f0:f0 + SCAN_BLOCK, :].astype(BF16)
        ub = ub_ref[b0:b0 + SCAN_BLOCK, :].astype(BF16)
        for cols in lanes:
            for sl in cols:
                hf_ref[:, sl] = _dot(uf, bmat_ref[0, :, sl])
                hb_ref[:, sl] = _dot(ub, bmat_ref[1, :, sl])
            yield
        for lb, (re, im) in enumerate(lanes):
            if lb:
                yield
            lam = [[jnp.broadcast_to(lam_ref[d, c, :, re], (nb, lane_w)) for c in range(2)] for d in range(2)]
            fr, fi, br, bi = state[lb]
            for k in range(steps):
                rf = slice(k * nb, (k + 1) * nb)
                rb = slice((steps - 1 - k) * nb, (steps - k) * nb)
                fr, fi = (lam[0][0] * fr - lam[0][1] * fi + hf_ref[rf, re],
                          lam[0][0] * fi + lam[0][1] * fr + hf_ref[rf, im])
                br, bi = (lam[1][0] * br - lam[1][1] * bi + hb_ref[rb, re],
                          lam[1][0] * bi + lam[1][1] * br + hb_ref[rb, im])
                hf_ref[rf, re] = fr
                hf_ref[rf, im] = fi
                hb_ref[rb, re] = br
                hb_ref[rb, im] = bi
            state[lb] = [fr, fi, br, bi]
        yf = yb = 0.0
        for cols in lanes:
            yield
            for sl in cols:
                yf = yf + _dot(hf_ref[:, sl].astype(BF16), cmat_ref[0, sl, :])
                yb = yb + _dot(hb_ref[:, sl].astype(BF16), cmat_ref[1, sl, :])
        yf_ref[f0:f0 + SCAN_BLOCK, :] = yf
        yb_ref[b0:b0 + SCAN_BLOCK, :] = yb

    _round_robin([block(i) for i in range(n_blk)], stagger=n_lb)
    for (re, _), vals in zip(lanes, state):
        for d in range(2):
            for c in range(2):
                st_ref[d, c, :, re] = vals[2 * d + c]
    hfin_ref[...] = st_ref[...]


def _scan_pair_kernel(uf_ref, ub_ref, bmat_ref, lam_ref, cmat_ref, h0_ref, yf_ref, yb_ref, *scratch, lane_w):
    half = DEC_BATCH
    tile = 2 * half
    n_blk = SCAN_ROWS // SCAN_BLOCK
    hf_blk, hb_blk, st_ref = scratch[:n_blk], scratch[n_blk:2 * n_blk], scratch[2 * n_blk]
    tiles = SCAN_BLOCK // tile
    n_lb = SSM_LANES // lane_w

    @pl.when(pl.program_id(0) == 0)
    def _():
        st_ref[...] = h0_ref[...]

    top = lax.broadcasted_iota(jnp.int32, (tile, lane_w), 0) < half
    swap = lambda a: pltpu.roll(a, half, 0)
    lanes = [(slice(lb * lane_w, (lb + 1) * lane_w), slice(SSM_LANES + lb * lane_w, SSM_LANES + (lb + 1) * lane_w))
             for lb in range(n_lb)]
    state = [[st_ref[0, :, re], st_ref[1, :, re]] for re, _ in lanes]

    def block(i):
        f0 = i * SCAN_BLOCK
        b0 = (n_blk - 1 - i) * SCAN_BLOCK
        hf_ref, hb_ref = hf_blk[i], hb_blk[i]
        uf = uf_ref[f0:f0 + SCAN_BLOCK, :].astype(BF16)
        ub = ub_ref[b0:b0 + SCAN_BLOCK, :].astype(BF16)
        for cols in lanes:
            for sl in cols:
                hf_ref[:, sl] = _dot(uf, bmat_ref[0, :, sl])
                hb_ref[:, sl] = _dot(ub, bmat_ref[1, :, sl])
            yield
        for lb, (re, im) in enumerate(lanes):
            if lb:
                yield
            la_r = lam_ref[0, :, re]
            la_i = lam_ref[1, :, re]
            lb_r = swap(la_r)
            lb_i = swap(la_i)
            sr, si = state[lb]
            for m in range(tiles):
                rf = slice(m * tile, (m + 1) * tile)
                rb = slice((tiles - 1 - m) * tile, (tiles - m) * tile)
                fr, fi = hf_ref[rf, re], hf_ref[rf, im]
                br, bi = hb_ref[rb, re], hb_ref[rb, im]
                vr = la_r * sr - la_i * si + jnp.where(top, fr, br)
                vi = la_r * si + la_i * sr + jnp.where(top, fi, bi)
                tr = swap(vr)
                ti = swap(vi)
                wr = lb_r * tr - lb_i * ti + jnp.where(top, br, fr)
                wi = lb_r * ti + lb_i * tr + jnp.where(top, bi, fi)
                hf_ref[rf, re] = jnp.where(top, vr, wr)
                hf_ref[rf, im] = jnp.where(top, vi, wi)
                hb_ref[rb, re] = jnp.where(top, wr, vr)
                hb_ref[rb, im] = jnp.where(top, wi, vi)
                sr, si = swap(wr), swap(wi)
            state[lb] = [sr, si]
        yf = yb = 0.0
        for cols in lanes:
            yield
            for sl in cols:
                yf = yf + _dot(hf_ref[:, sl].astype(BF16), cmat_ref[0, sl, :])
                yb = yb + _dot(hb_ref[:, sl].astype(BF16), cmat_ref[1, sl, :])
        yf_ref[f0:f0 + SCAN_BLOCK, :] = yf
        yb_ref[b0:b0 + SCAN_BLOCK, :] = yb

    _round_robin([block(i) for i in range(n_blk)], stagger=n_lb)
    for (re, _), (sr, si) in zip(lanes, state):
        st_ref[0, :, re] = sr
        st_ref[1, :, re] = si


def _scan_specs(n_rows):
    n = n_rows // SCAN_ROWS
    fwd = pl.BlockSpec((SCAN_ROWS, SSM_WIDTH), lambda j: (j, 0))
    bwd = pl.BlockSpec((SCAN_ROWS, SSM_WIDTH), lambda j: (n - 1 - j, 0))
    y_shape = jax.ShapeDtypeStruct((n_rows, SSM_WIDTH), F32)
    buf = [pltpu.VMEM((SCAN_BLOCK, 2 * SSM_LANES), F32)] * (2 * (SCAN_ROWS // SCAN_BLOCK))
    return n, fwd, bwd, y_shape, buf


def _scan_ctx(u_rows, bmat, lam, cmat, h0):
    n, fwd, bwd, y_shape, buf = _scan_specs(u_rows.shape[0])
    st_shape = (2, 2, BATCH, SSM_LANES)
    return pl.pallas_call(
        functools.partial(_scan_kernel, nb=BATCH, lane_w=256),
        grid=(n,),
        in_specs=[fwd, bwd, _const_spec(bmat.shape), _const_spec(lam.shape), _const_spec(cmat.shape),
                  _const_spec(st_shape)],
        out_specs=[fwd, bwd, pl.BlockSpec(st_shape, lambda j: (0, 0, 0, 0))],
        out_shape=[y_shape, y_shape, jax.ShapeDtypeStruct(st_shape, F32)],
        scratch_shapes=buf + [pltpu.VMEM(st_shape, F32)],
        compiler_params=_params(1),
        name="scan_ctx",
    )(u_rows, u_rows, bmat, lam, cmat, h0)


def _scan_lat(u_rows, bmat, lam_pair, cmat, h0_pair):
    n, fwd, bwd, y_shape, buf = _scan_specs(u_rows.shape[0])
    st_shape = (2, 2 * DEC_BATCH, SSM_LANES)
    return pl.pallas_call(
        functools.partial(_scan_pair_kernel, lane_w=512),
        grid=(n,),
        in_specs=[fwd, bwd, _const_spec(bmat.shape), _const_spec(st_shape), _const_spec(cmat.shape),
                  _const_spec(st_shape)],
        out_specs=[fwd, bwd],
        out_shape=[y_shape, y_shape],
        scratch_shapes=buf + [pltpu.VMEM(st_shape, F32)],
        compiler_params=_params(1),
        name="scan_lat",
    )(u_rows, u_rows, bmat, lam_pair, cmat, h0_pair)


def _ssm_discretise(lam_re, lam_im, log_step, b_re, b_im, c_re, c_im):
    step = jnp.exp(log_step.astype(F32))[..., None]
    lr, li = lam_re.astype(F32), lam_im.astype(F32)
    mag = jnp.exp(lr * step)
    bar_r = mag * jnp.cos(li * step)
    bar_i = mag * jnp.sin(li * step)
    den = lr * lr + li * li
    coef_r = (((bar_r - 1) * lr + bar_i * li) / den)[..., None]
    coef_i = ((bar_i * lr - (bar_r - 1) * li) / den)[..., None]
    br, bi = b_re.astype(F32), b_im.astype(F32)
    bbar_r = coef_r * br - coef_i * bi
    bbar_i = coef_r * bi + coef_i * br
    eye = jnp.eye(SSM_GROUPS, dtype=F32)
    blk_b = lambda a: jnp.einsum('dgpc,gh->dgchp', a, eye).reshape(2, SSM_WIDTH, SSM_LANES)
    bmat = jnp.concatenate([blk_b(bbar_r), blk_b(bbar_i)], axis=-1)
    blk_c = lambda a: jnp.einsum('dgcp,gh->dgphc', a, eye).reshape(2, SSM_LANES, SSM_WIDTH)
    cmat = jnp.concatenate([blk_c(c_re.astype(F32)), -blk_c(c_im.astype(F32))], axis=1)
    lam_flat = jnp.stack([bar_r, bar_i], axis=1).reshape(2, 2, 1, SSM_LANES)
    return bmat.astype(BF16), lam_flat, cmat.astype(BF16)


def _merge_kernel(x_ref, oa_ref, yf_ref, yb_ref, u_ref, oc_ref, g_ref, mod_ref, d_ref, wglu_ref,
                  wa_ref, wb_ref, wc_ref, wo_ref, g2_ref, wgu_ref, wd_ref, fg_ref, o_ref, *slabs, final):
    nb, tt, _ = x_ref.shape
    tc = tt // ROW_CHAINS
    rows = nb * tc
    mod = mod_ref[...]
    gate1 = mod[:, :, 2 * D_MODEL:3 * D_MODEL]
    shift2 = mod[:, :, 3 * D_MODEL:4 * D_MODEL]
    scale2 = mod[:, :, 4 * D_MODEL:5 * D_MODEL]
    gate2 = mod[:, :, 5 * D_MODEL:6 * D_MODEL]
    def chain(c):
        ts = slice(c * tc, (c + 1) * tc)
        rs = slice(c * rows, (c + 1) * rows)
        flat = lambda ref, lo=0, hi=None: ref[:, ts, lo:hi].reshape(rows, -1)
        y = _gelu_tanh(yf_ref[rs, :] + yb_ref[rs, :] + d_ref[...] * u_ref[rs, :])
        ob = y * jax.nn.sigmoid(_dot(y.astype(BF16), wglu_ref[...]))
        ob = _to_batch_major(ob, slabs[2 * c:2 * c + 2], nb)
        yield
        gate = lambda i: jax.nn.sigmoid(flat(g_ref, i * D_MODEL, (i + 1) * D_MODEL))
        merged = (gate(0) * _dot(flat(oa_ref), wa_ref[...])
                  + gate(1) * _dot(ob.astype(BF16), wb_ref[...])
                  + gate(2) * _dot(flat(oc_ref), wc_ref[...]))
        yield
        x1 = x_ref[:, ts, :] + gate1 * _dot(merged.astype(BF16), wo_ref[...]).reshape(nb, tc, D_MODEL)
        h2 = _rms(x1, g2_ref[...]) * (1 + scale2) + shift2
        yield
        gu = _dot(h2.reshape(rows, D_MODEL).astype(BF16), wgu_ref[...])
        act = _silu(gu[:, :D_FF]) * gu[:, D_FF:]
        yield
        x2 = x1 + gate2 * _dot(act.astype(BF16), wd_ref[...]).reshape(nb, tc, D_MODEL)
        if final:
            x2 = _rms(x2, fg_ref[...])
        o_ref[:, ts, :] = x2

    _round_robin([chain(c) for c in range(ROW_CHAINS)])


def _merge(x, oa, yf, yb, u_rows, oc, g, mod_rows, weights, final_g, layer, *, name):
    final = layer == DEPTH - 1
    nb, seq, _ = x.shape
    tt = ROW_TILE // nb
    slab = lambda w: pl.BlockSpec((nb, tt, w), lambda j: (0, j, 0))
    tmaj = pl.BlockSpec((ROW_TILE, SSM_WIDTH), lambda j: (j, 0))
    return pl.pallas_call(
        functools.partial(_merge_kernel, final=final),
        grid=(seq // tt,),
        in_specs=[slab(D_MODEL), slab(WIDTH_A), tmaj, tmaj, tmaj, slab(WIDTH_C), slab(N_BRANCH * D_MODEL),
                  _const_spec(mod_rows.shape)] + [_layer_spec(w, layer) for w in weights]
                 + [_const_spec(final_g.shape)],
        out_specs=slab(D_MODEL),
        out_shape=jax.ShapeDtypeStruct((nb, seq, D_MODEL), F32),
        scratch_shapes=[pltpu.VMEM((ROW_TILE // ROW_CHAINS, LANES), F32)] * (2 * ROW_CHAINS),
        compiler_params=_params(1),
        name=name + ("_final" if final else ""),
    )(x, oa, yf, yb, u_rows, oc, g, mod_rows, *weights, final_g)


def _rope_tables():
    t = jnp.arange(DEC_SEQ)
    row = (t // GRID_W).astype(F32)
    col = (t % GRID_W).astype(F32)
    inv = 1.0 / (ROPE_THETA ** (jnp.arange(ROT_FREQS, dtype=F32) / ROT_FREQS))
    ar = row[:, None] * inv[None]
    ac = col[:, None] * inv[None]
    cos = jnp.concatenate([jnp.cos(ar), jnp.cos(ar), jnp.cos(ac), jnp.cos(ac)], axis=-1)
    sin = jnp.concatenate([-jnp.sin(ar), jnp.sin(ar), -jnp.sin(ac), jnp.sin(ac)], axis=-1)
    return jnp.tile(cos, (1, 2)), jnp.tile(sin, (1, 2))


def kernel(x_prompt, x_sample, c, cache_ga_k, cache_ga_v, cache_na_k, cache_na_v, state_ssm, c_ctx, w_mod, b_mod, norm1_g, w_in, qn_g, kn_g, ssm_lam_re, ssm_lam_im, ssm_log_step, ssm_b_re, ssm_b_im, ssm_c_re, ssm_c_im, ssm_d, ssm_w_glu, na_bias, w_br_a, w_br_b, w_br_c, w_out, norm2_g, w_ffn_gu, w_ffn_d, final_g):
    cvec = jnp.concatenate([c_ctx[None, :], c, jnp.zeros((N_MOD_ROWS - 1 - DEC_BATCH, D_MODEL), F32)], axis=0)
    mod = _adaln(cvec, w_mod, b_mod).reshape(DEPTH, N_MOD_ROWS, 1, 6 * D_MODEL)

    seg = jnp.kron(jnp.eye(MXU_DIM // HEAD_DIM, dtype=F32),
                   jnp.full((HEAD_DIM, HEAD_DIM), 1.0 / HEAD_DIM, F32)).astype(BF16)
    rope_tabs = _rope_tables()
    fg = final_g.reshape(1, D_MODEL)
    ck_a = cache_ga_k.reshape(DEC_BATCH, DEPTH, PAST_LEN, KV_WIDTH_A)
    cv_a = cache_ga_v.reshape(DEC_BATCH, DEPTH, PAST_LEN, KV_WIDTH_A)
    ck_c = cache_na_k.reshape(DEC_BATCH, DEPTH, PAST_LEN, WIDTH_C)
    cv_c = cache_na_v.reshape(DEC_BATCH, DEPTH, PAST_LEN, WIDTH_C)
    zero_state = jnp.zeros((2, 2, BATCH, SSM_LANES), F32)

    row = lambda p: p.reshape(DEPTH, 1, p.shape[-1])
    g1 = row(norm1_g)
    w_in_b = w_in.astype(BF16)
    qg = row(jnp.tile(qn_g, (1, N_HEADS_A)))
    kg = row(jnp.tile(kn_g, (1, N_KV_A)))
    merge_w = [row(ssm_d), ssm_w_glu.astype(BF16), w_br_a.astype(BF16), w_br_b.astype(BF16), w_br_c.astype(BF16),
               w_out.astype(BF16), row(norm2_g), w_ffn_gu.astype(BF16), w_ffn_d.astype(BF16)]

    xp, xs = x_prompt, x_sample
    cache = ()
    ssm_st = []
    for l in range(DEPTH):
        bmat, lam, cmat = _ssm_discretise(ssm_lam_re[l], ssm_lam_im[l], ssm_log_step[l], ssm_b_re[l], ssm_b_im[l],
                                          ssm_c_re[l], ssm_c_im[l])
        mod_ctx = mod[l, 0:1]
        mod_lat = mod[l, 1:1 + DEC_BATCH]

        qa, krep, vrep, ka, va, u_rows, qc, kc, vc, g = _inproj(xp, mod_ctx, g1, w_in_b, seg, qg, kg, l,
                                                                prev_cache=cache)
        cache = (ka, va, kc, vc)
        oa, oc = _attn_ctx(qa, krep, vrep, qc, kc, vc, l)
        yf, yb, hfin = _scan_ctx(u_rows, bmat, lam, cmat, zero_state)
        xp = _merge(xp, oa, yf, yb, u_rows, oc, g, mod_ctx, merge_w, fg, l, name="merge_ctx")
        ssm_st.append(jnp.transpose(hfin, (2, 0, 1, 3)))

        qa, krep, vrep, u_rows, qc, kc, vc, g = _inproj(xs, mod_lat, g1, w_in_b, seg, qg, kg, l, rope_tabs=rope_tabs)
        oa = _attn_a_lat(qa, krep, vrep, ck_a, cv_a, l)
        oc = _na_lat(qc, kc, vc, ck_c, cv_c, _na_bias_rows(na_bias[l]), l)
        h0 = jnp.transpose(state_ssm[:, l].reshape(DEC_BATCH, 2, 2, SSM_LANES), (2, 1, 0, 3))
        h0 = h0.reshape(2, 2 * DEC_BATCH, SSM_LANES)
        lam_pair = jnp.broadcast_to(jnp.transpose(lam, (1, 0, 2, 3)), (2, 2, DEC_BATCH, SSM_LANES))
        lam_pair = lam_pair.reshape(2, 2 * DEC_BATCH, SSM_LANES)
        yf, yb = _scan_lat(u_rows, bmat, lam_pair, cmat, h0)
        xs = _merge(xs, oa, yf, yb, u_rows, oc, g, mod_lat, merge_w, fg, l, name="merge_lat")

    ga_k, ga_v, na_k, na_v = cache
    new_ssm = jnp.stack(ssm_st, axis=1).reshape(BATCH, DEPTH, 2, 2, SSM_GROUPS, SSM_STATE)
    return (xp, xs,
            ga_k.reshape(BATCH, DEPTH, SEQ, N_KV_A, HEAD_DIM), ga_v.reshape(BATCH, DEPTH, SEQ, N_KV_A, HEAD_DIM),
            na_k.reshape(BATCH, DEPTH, SEQ, N_HEADS_C, HEAD_DIM), na_v.reshape(BATCH, DEPTH, SEQ, N_HEADS_C, HEAD_DIM),
            new_ssm)
```

```python
import functools
import math

import jax
import jax.numpy as jnp
from jax import lax
from jax.experimental import pallas as pl
from jax.experimental.pallas import tpu as pltpu

D_MODEL = 1024
BATCH = 16
SEQ = 256
DEPTH = 2
DEC_BATCH = 4
DEC_SEQ = 2048
PAST_LEN = 256
GRID_W = 64
GRID_ROWS = DEC_SEQ // GRID_W
HEAD_DIM = 64
N_HEADS_A = 8
N_KV_A = 2
REP_A = N_HEADS_A // N_KV_A
N_HEADS_C = 4
SSM_WIDTH = 256
SSM_GROUP = 16
SSM_GROUPS = SSM_WIDTH // SSM_GROUP
SSM_STATE = 64
SSM_LANES = SSM_GROUPS * SSM_STATE
NA_WIN_R = 8
NA_WIN_C = 16
D_FF = -(-8 * D_MODEL // (3 * 256)) * 256
ROPE_THETA = 10000.0
ROT_HALF = HEAD_DIM // 2
ROT_FREQS = ROT_HALF // 2
WIDTH_A = N_HEADS_A * HEAD_DIM
KV_WIDTH_A = N_KV_A * HEAD_DIM
WIDTH_C = N_HEADS_C * HEAD_DIM
N_BRANCH = 3
IN_WIDTH = WIDTH_A + 2 * KV_WIDTH_A + SSM_WIDTH + 3 * WIDTH_C + N_BRANCH * D_MODEL
EPS = 1e-6

OFF_QA = 0
OFF_KA = OFF_QA + WIDTH_A
OFF_VA = OFF_KA + KV_WIDTH_A
OFF_U = OFF_VA + KV_WIDTH_A
OFF_QC = OFF_U + SSM_WIDTH
OFF_KC = OFF_QC + WIDTH_C
OFF_VC = OFF_KC + WIDTH_C
OFF_G = OFF_VC + WIDTH_C

N_MOD_ROWS = 8
ROW_TILE = 256
Q_TILE_A = 512
Q_CHAIN_ROWS = 256
NA_Q_ROWS = 2
NA_K_ROWS = 10
NA_R0_MAX = GRID_ROWS - NA_K_ROWS
NA_STEP_BLOCKS = 2
CTX_STEP_SEQS = 2
ROW_CHAINS = 2
SCAN_ROWS = 1024
SCAN_BLOCK = 256
LOG2_E = math.log2(math.e)
Q_SCALE = HEAD_DIM ** -0.5 * LOG2_E
LANES = 128
MXU_DIM = 256
NEG = -1e30
VMEM_LIMIT_V7X = 56 * 1024 * 1024

F32 = jnp.float32
BF16 = jnp.bfloat16


def _dot(a, b):
    return jnp.dot(a, b, preferred_element_type=F32)


def _dot_t(a, b):
    return lax.dot_general(a, b, (((1,), (1,)), ((), ())), preferred_element_type=F32)


def _params(n_axes):
    return pltpu.CompilerParams(dimension_semantics=("arbitrary",) * n_axes,
                                vmem_limit_bytes=VMEM_LIMIT_V7X)


def _const_spec(shape):
    nd = len(shape)
    return pl.BlockSpec(shape, lambda *_: (0,) * nd, pipeline_mode=pl.Buffered(1))


def _layer_spec(stacked, layer):
    shape = stacked.shape[1:]
    return pl.BlockSpec((None,) + shape, lambda *_: (layer,) + (0,) * len(shape), pipeline_mode=pl.Buffered(1))


def _rms(x, g):
    return x * lax.rsqrt(jnp.mean(x * x, axis=-1, keepdims=True) + EPS) * g


def _silu(x):
    return x * jax.nn.sigmoid(x)


def _gelu_tanh(x):
    c = math.sqrt(2.0 / math.pi)
    return x * (0.5 * (1.0 + jnp.tanh(c * (x + 0.044715 * (x * x * x)))))


def _seg_rms(x, seg, g):
    w = seg.shape[0]
    x2 = x * x
    hi = x2.astype(BF16)
    lo = (x2 - hi.astype(F32)).astype(BF16)
    ms = jnp.concatenate([_dot(hi[:, c:c + w], seg) + _dot(lo[:, c:c + w], seg)
                          for c in range(0, x.shape[1], w)], axis=1)
    return x * lax.rsqrt(ms + EPS) * g


def _rope(x, cos, sin_signed):
    w = x.shape[-1]
    lane = lax.broadcasted_iota(jnp.int32, x.shape, 1)
    first = (lane & ROT_FREQS) == 0
    partner = jnp.where(first, pltpu.roll(x, w - ROT_FREQS, 1), pltpu.roll(x, ROT_FREQS, 1))
    return x * cos + partner * sin_signed


def _rep_heads(kv):
    lane = lax.broadcasted_iota(jnp.int32, kv.shape, 1)
    swapped = pltpu.roll(kv, HEAD_DIM, 1)
    lo = lane < HEAD_DIM
    h0 = jnp.where(lo, kv, swapped)
    h1 = jnp.where(lo, swapped, kv)
    return jnp.concatenate([h0, h0, h1, h1], axis=1)


def _round_robin(chains, stagger=0):
    done = [False] * len(chains)
    rnd = 0
    while not all(done):
        for i, ch in enumerate(chains):
            if not done[i] and rnd >= i * stagger:
                try:
                    next(ch)
                except StopIteration:
                    done[i] = True
        rnd += 1


def _to_time_major(val, slabs, nb):
    tt = val.shape[0] // nb
    for s, slab in enumerate(slabs):
        for b in range(nb):
            slab[pl.ds(b, tt, stride=nb), :] = val[b * tt:(b + 1) * tt, s * LANES:(s + 1) * LANES]
    return jnp.concatenate([slab[...] for slab in slabs], axis=1)


def _to_batch_major(val, slabs, nb):
    tt = val.shape[0] // nb
    for s, slab in enumerate(slabs):
        slab[...] = val[:, s * LANES:(s + 1) * LANES]
    return jnp.concatenate(
        [jnp.concatenate([slab[pl.ds(b, tt, stride=nb), :] for slab in slabs], axis=1) for b in range(nb)], axis=0)


def _adaln_kernel(c_ref, w_ref, b_ref, o_ref):
    s = _silu(c_ref[...])
    o_ref[...] = _dot(s.astype(BF16), w_ref[...].astype(BF16)) + b_ref[...]


def _adaln(cvec, w_mod, b_mod):
    n_col = 6 * D_MODEL
    tn = n_col // 4
    return pl.pallas_call(
        _adaln_kernel,
        grid=(DEPTH, n_col // tn),
        in_specs=[pl.BlockSpec((N_MOD_ROWS, D_MODEL), lambda l, n: (0, 0)),
                  pl.BlockSpec((None, D_MODEL, tn), lambda l, n: (l, 0, n)),
                  pl.BlockSpec((None, 1, tn), lambda l, n: (l, 0, n))],
        out_specs=pl.BlockSpec((None, N_MOD_ROWS, tn), lambda l, n: (l, 0, n)),
        out_shape=jax.ShapeDtypeStruct((DEPTH, N_MOD_ROWS, n_col), F32),
        compiler_params=_params(2),
        name="adaln",
    )(cvec, w_mod, b_mod.reshape(DEPTH, 1, n_col))


def _inproj_kernel(*refs, latent, n_alias):
    if latent:
        (x_ref, xn_ref, mod_ref, g1_ref, w_ref, seg_ref, qg_ref, kg_ref, cos_ref, sin_ref,
         qa_ref, krep_ref, vrep_ref, u_ref, qc_ref, kc_ref, vc_ref, hb_ref, slab0, slab1) = refs
    else:
        x_ref, xn_ref, mod_ref, g1_ref, w_ref, seg_ref, qg_ref, kg_ref = refs[:8]
        (qa_ref, krep_ref, vrep_ref, ka_ref, va_ref, u_ref, qc_ref, kc_ref, vc_ref,
         hb_ref, slab0, slab1) = refs[8 + n_alias:]
    nb, tt, _ = x_ref.shape
    rows = nb * tt
    mod = mod_ref[...]
    shift = mod[:, :, 0:D_MODEL]
    scale = mod[:, :, D_MODEL:2 * D_MODEL]
    seg = seg_ref[...]
    slot = pl.program_id(0) % 2

    def normed(ref):
        h = _rms(ref[...], g1_ref[...]) * (1 + scale) + shift
        return h.reshape(rows, D_MODEL).astype(BF16)

    def put(ref, val):
        val = val.reshape(nb, tt, ref.shape[-1]).astype(ref.dtype)
        if len(ref.shape) == 3:
            ref[...] = val
        else:
            ref[:, 0] = val
            ref[:, 1:] = jnp.zeros((nb, DEPTH - 1, tt, ref.shape[-1]), ref.dtype)

    @pl.when(pl.program_id(0) == 0)
    def _():
        hb_ref[0] = normed(x_ref)

    def project():
        hb = hb_ref[slot]
        u_ref[...] = _to_time_major(_dot(hb, w_ref[:, OFF_U:OFF_QC]), (slab0, slab1), nb)
        put(qc_ref, _dot(hb, w_ref[:, OFF_QC:OFF_KC]) * Q_SCALE)
        put(kc_ref, _dot(hb, w_ref[:, OFF_KC:OFF_VC]))
        put(vc_ref, _dot(hb, w_ref[:, OFF_VC:OFF_G]))
        yield
        qa = _seg_rms(_dot(hb, w_ref[:, OFF_QA:OFF_KA]), seg, qg_ref[...])
        ka = _seg_rms(_dot(hb, w_ref[:, OFF_KA:OFF_VA]), seg[0:KV_WIDTH_A, 0:KV_WIDTH_A], kg_ref[...])
        va = _dot(hb, w_ref[:, OFF_VA:OFF_U])
        if latent:
            cos = jnp.concatenate([cos_ref[...]] * nb, axis=0)
            sin = jnp.concatenate([sin_ref[...]] * nb, axis=0)
            qa = _rope(qa, jnp.concatenate([cos] * REP_A, axis=1), jnp.concatenate([sin] * REP_A, axis=1))
            ka = _rope(ka, cos, sin)
        else:
            put(ka_ref, ka)
            put(va_ref, va)
        put(qa_ref, qa * Q_SCALE)
        put(krep_ref, _rep_heads(ka))
        put(vrep_ref, _rep_heads(va))

    def prepare_next():
        yield
        hb_ref[1 - slot] = normed(xn_ref)

    _round_robin([project(), prepare_next()])


def _inproj(x, mod_rows, g1, w_in, seg, qg, kg, layer, *, rope_tabs=None, prev_cache=()):
    latent = rope_tabs is not None
    nb, seq, _ = x.shape
    tt = ROW_TILE // nb
    slab = lambda w: pl.BlockSpec((nb, tt, w), lambda j: (0, j, 0))
    act = lambda w, dt: jax.ShapeDtypeStruct((nb, seq, w), dt)
    if layer == 0:
        cache_slab = lambda w: pl.BlockSpec((nb, DEPTH, tt, w), lambda j: (0, 0, j, 0))
    else:
        cache_slab = lambda w: pl.BlockSpec((nb, None, tt, w), lambda j: (0, layer, j, 0))
    cache = lambda w: jax.ShapeDtypeStruct((nb, DEPTH, seq, w), F32)
    n_steps = seq // tt
    next_slab = pl.BlockSpec((nb, tt, D_MODEL), lambda j: (0, jnp.minimum(j + 1, n_steps - 1), 0))
    in_specs = [slab(D_MODEL), next_slab, _const_spec(mod_rows.shape), _layer_spec(g1, layer),
                _layer_spec(w_in, layer), _const_spec(seg.shape), _layer_spec(qg, layer), _layer_spec(kg, layer)]
    args = [x, x, mod_rows, g1, w_in, seg, qg, kg]
    out_shape = [act(WIDTH_A, BF16), act(REP_A * KV_WIDTH_A, BF16), act(REP_A * KV_WIDTH_A, BF16)]
    out_specs = [slab(WIDTH_A), slab(REP_A * KV_WIDTH_A), slab(REP_A * KV_WIDTH_A)]
    aliases = {}
    if latent:
        in_specs += [pl.BlockSpec((tt, 2 * HEAD_DIM), lambda j: (j, 0))] * 2
        args += list(rope_tabs)
        kv_c = [act(WIDTH_C, BF16)] * 2
        kv_c_specs = [slab(WIDTH_C)] * 2
    else:
        aliases = {len(args) + i: o for i, o in enumerate((3, 4, 7, 8)[:len(prev_cache)])}
        in_specs += [pl.BlockSpec(memory_space=pl.ANY)] * len(prev_cache)
        args += list(prev_cache)
        out_shape += [cache(KV_WIDTH_A)] * 2
        out_specs += [cache_slab(KV_WIDTH_A)] * 2
        kv_c = [cache(WIDTH_C)] * 2
        kv_c_specs = [cache_slab(WIDTH_C)] * 2
    out_shape += [jax.ShapeDtypeStruct((seq * nb, SSM_WIDTH), F32),
                  act(WIDTH_C, BF16)] + kv_c
    out_specs += [pl.BlockSpec((ROW_TILE, SSM_WIDTH), lambda j: (j, 0)),
                  slab(WIDTH_C)] + kv_c_specs
    return pl.pallas_call(
        functools.partial(_inproj_kernel, latent=latent, n_alias=len(prev_cache)),
        grid=(seq // tt,),
        in_specs=in_specs,
        out_specs=out_specs,
        out_shape=out_shape,
        scratch_shapes=[pltpu.VMEM((2, ROW_TILE, D_MODEL), BF16)] + [pltpu.VMEM((ROW_TILE, LANES), F32)] * 2,
        input_output_aliases=aliases,
        compiler_params=_params(1),
        name="inproj_lat" if latent else "inproj_ctx",
    )(*args)


def _head_mask(shape, head):
    lane = lax.broadcasted_iota(jnp.int32, shape, 1)
    return (lane // HEAD_DIM) == head


def _head_chain(q, heads, keys, vals, acc, key, bias=None):
    m_rows = q.shape[0]
    masks = [_head_mask(q.shape, h) for h in heads]
    qs = jnp.concatenate([jnp.where(hm, q, jnp.zeros_like(q)) for hm in masks], axis=0)
    s = _dot_t(qs, keys())
    if bias is not None:
        s = s + bias()
    yield
    p = jnp.exp2(s - jnp.max(s, axis=-1, keepdims=True))
    l = jnp.sum(p, axis=-1, keepdims=True)
    p = p.astype(BF16)
    yield
    o = _dot(p, vals()) / l
    for i, hm in enumerate(masks):
        acc[key] = jnp.where(hm, o[i * m_rows:(i + 1) * m_rows], acc[key])


def _attn_a_kernel(q_ref, kn_ref, vn_ref, ck_ref, cv_ref, o_ref, k_ref, v_ref):
    @pl.when(pl.program_id(1) == 0)
    def _():
        k_ref[0:PAST_LEN, :] = _rep_heads(ck_ref[...]).astype(BF16)
        v_ref[0:PAST_LEN, :] = _rep_heads(cv_ref[...]).astype(BF16)
        k_ref[PAST_LEN:, :] = kn_ref[...]
        v_ref[PAST_LEN:, :] = vn_ref[...]

    gw = REP_A * HEAD_DIM
    n_sub = q_ref.shape[0] // Q_CHAIN_ROWS
    acc = {(t, g): jnp.zeros((Q_CHAIN_ROWS, gw), F32) for t in range(n_sub) for g in range(N_KV_A)}
    chains = []
    for t in range(n_sub):
        rows = slice(t * Q_CHAIN_ROWS, (t + 1) * Q_CHAIN_ROWS)
        for g in range(N_KV_A):
            sl = slice(g * gw, (g + 1) * gw)
            for j in range(REP_A):
                chains.append(_head_chain(q_ref[rows, sl], [j], lambda sl=sl: k_ref[:, sl],
                                          lambda sl=sl: v_ref[:, sl], acc, (t, g)))
    _round_robin(chains, stagger=1)
    for (t, g), val in acc.items():
        o_ref[t * Q_CHAIN_ROWS:(t + 1) * Q_CHAIN_ROWS, g * gw:(g + 1) * gw] = val.astype(BF16)


def _attn_a_lat(qa, krep, vrep, cache_k, cache_v, layer):
    seq_blk = pl.BlockSpec((None, DEC_SEQ, WIDTH_A), lambda b, t: (b, 0, 0))
    cache_blk = pl.BlockSpec((None, None, PAST_LEN, KV_WIDTH_A), lambda b, t: (b, layer, 0, 0))
    q_blk = pl.BlockSpec((None, Q_TILE_A, WIDTH_A), lambda b, t: (b, t, 0))
    return pl.pallas_call(
        _attn_a_kernel,
        grid=(DEC_BATCH, DEC_SEQ // Q_TILE_A),
        in_specs=[q_blk, seq_blk, seq_blk, cache_blk, cache_blk],
        out_specs=q_blk,
        out_shape=jax.ShapeDtypeStruct((DEC_BATCH, DEC_SEQ, WIDTH_A), BF16),
        scratch_shapes=[pltpu.VMEM((PAST_LEN + DEC_SEQ, WIDTH_A), BF16)] * 2,
        compiler_params=_params(2),
        name="attn_a_lat",
    )(qa, krep, vrep, cache_k, cache_v)


def _attn_ctx_kernel(qa_ref, ka_ref, va_ref, qc_ref, kc_ref, vc_ref, oa_ref, oc_ref):
    gw = REP_A * HEAD_DIM
    acc = {}
    chains = []
    for b in range(CTX_STEP_SEQS):
        for g in range(N_KV_A):
            sl = slice(g * gw, (g + 1) * gw)
            acc["a", b, g] = jnp.zeros((SEQ, gw), F32)
            for j in range(REP_A):
                chains.append(_head_chain(qa_ref[b, :, sl], [j], lambda b=b, sl=sl: ka_ref[b, :, sl],
                                          lambda b=b, sl=sl: va_ref[b, :, sl], acc, ("a", b, g)))
        acc["c", b] = jnp.zeros((SEQ, WIDTH_C), F32)
        for h in range(0, N_HEADS_C, 2):
            chains.append(_head_chain(qc_ref[b], [h, h + 1], lambda b=b: kc_ref[b].astype(BF16),
                                      lambda b=b: vc_ref[b].astype(BF16), acc, ("c", b)))
    _round_robin(chains, stagger=1)
    for b in range(CTX_STEP_SEQS):
        for g in range(N_KV_A):
            oa_ref[b, :, g * gw:(g + 1) * gw] = acc["a", b, g].astype(BF16)
        oc_ref[b] = acc["c", b].astype(BF16)


def _attn_ctx(qa, krep, vrep, qc, kc, vc, layer):
    blk = lambda w: pl.BlockSpec((CTX_STEP_SEQS, SEQ, w), lambda i: (i, 0, 0))
    kv_blk = pl.BlockSpec((CTX_STEP_SEQS, None, SEQ, WIDTH_C), lambda i: (i, layer, 0, 0))
    return pl.pallas_call(
        _attn_ctx_kernel,
        grid=(BATCH // CTX_STEP_SEQS,),
        in_specs=[blk(WIDTH_A), blk(WIDTH_A), blk(WIDTH_A), blk(WIDTH_C), kv_blk, kv_blk],
        out_specs=[blk(WIDTH_A), blk(WIDTH_C)],
        out_shape=[jax.ShapeDtypeStruct((BATCH, SEQ, WIDTH_A), BF16),
                   jax.ShapeDtypeStruct((BATCH, SEQ, WIDTH_C), BF16)],
        compiler_params=_params(1),
        name="attn_ctx",
    )(qa, krep, vrep, qc, kc, vc)


def _na_kernel(q_ref, k_ref, v_ref, ck_ref, cv_ref, brow_ref, o_ref, b2_ref):
    nq = NA_Q_ROWS * GRID_W
    nk = NA_K_ROWS * GRID_W
    n_pair = 2 * NA_WIN_R

    @pl.when((pl.program_id(0) == 0) & (pl.program_id(1) == 0))
    def _():
        shp = (GRID_W, 2 * GRID_W)
        c = lax.broadcasted_iota(jnp.int32, shp, 0)
        kc = lax.broadcasted_iota(jnp.int32, shp, 1) & (GRID_W - 1)
        c0 = jnp.clip(c - NA_WIN_C // 2, 0, GRID_W - NA_WIN_C)
        col_bias = jnp.where(kc < c0, NEG, jnp.where(kc >= c0 + NA_WIN_C, NEG, 0.0))
        for h in range(N_HEADS_C):
            for e in range(n_pair):
                row = jnp.broadcast_to(brow_ref[h, e:e + 1, :], shp)
                toep = pltpu.roll(row, 2 * GRID_W - (NA_WIN_C - 1), 1, stride=1, stride_axis=0)
                b2_ref[h, e] = jnp.where(col_bias < 0.0, NEG, toep * LOG2_E)

    no_bias = jnp.zeros((nq, PAST_LEN), F32)
    ck = ck_ref[...].astype(BF16)
    cv = cv_ref[...].astype(BF16)
    acc = [jnp.zeros((nq, WIDTH_C), F32) for _ in range(NA_STEP_BLOCKS)]

    def chain(blk, heads):
        qrow0 = NA_Q_ROWS * (NA_STEP_BLOCKS * pl.program_id(1) + blk)
        row0 = jnp.clip(qrow0 - NA_WIN_R // 2, 0, NA_R0_MAX)
        start = pl.multiple_of(row0 * GRID_W, GRID_W)
        qrow = qrow0 + lax.broadcasted_iota(jnp.int32, (nq, nk), 0) // GRID_W
        krow = row0 + lax.broadcasted_iota(jnp.int32, (nq, nk), 1) // GRID_W
        win0 = jnp.clip(qrow - NA_WIN_R // 2, 0, GRID_ROWS - NA_WIN_R)
        row_bias = jnp.where(krow < win0, NEG, jnp.where(krow >= win0 + NA_WIN_R, NEG, 0.0))

        def head_bias(h):
            rows = []
            for i in range(NA_Q_ROWS):
                tiles = []
                for m in range(NA_K_ROWS // 2):
                    e = jnp.clip(row0 + 2 * m - (qrow0 + i) + NA_WIN_R, 0, n_pair - 1)
                    tiles.append(b2_ref[h, e])
                rows.append(jnp.concatenate(tiles, axis=1))
            return jnp.concatenate([jnp.concatenate(rows, axis=0) + row_bias, no_bias], axis=1)

        return _head_chain(q_ref[blk * nq:(blk + 1) * nq, :], heads,
                           lambda: jnp.concatenate([k_ref[pl.ds(start, nk), :], ck], axis=0),
                           lambda: jnp.concatenate([v_ref[pl.ds(start, nk), :], cv], axis=0), acc, blk,
                           bias=lambda: jnp.concatenate([head_bias(h) for h in heads], axis=0))

    pairs = [[h, h + 1] for h in range(0, N_HEADS_C, 2)]
    _round_robin([chain(blk, heads) for blk in range(NA_STEP_BLOCKS) for heads in pairs], stagger=1)
    for blk in range(NA_STEP_BLOCKS):
        o_ref[blk * nq:(blk + 1) * nq, :] = acc[blk].astype(BF16)


def _na_bias_rows(tbl):
    pad = jnp.pad(tbl, ((0, 0), (1, 1), (0, GRID_W - tbl.shape[-1])))
    return jnp.concatenate([pad[:, :-1], pad[:, 1:]], axis=-1)


def _na_lat(qc, kc, vc, cache_k, cache_v, brow, layer):
    step_rows = NA_STEP_BLOCKS * NA_Q_ROWS * GRID_W
    seq_blk = pl.BlockSpec((None, DEC_SEQ, WIDTH_C), lambda b, j: (b, 0, 0))
    cache_blk = pl.BlockSpec((None, None, PAST_LEN, WIDTH_C), lambda b, j: (b, layer, 0, 0))
    q_blk = pl.BlockSpec((None, step_rows, WIDTH_C), lambda b, j: (b, j, 0))
    return pl.pallas_call(
        _na_kernel,
        grid=(DEC_BATCH, DEC_SEQ // step_rows),
        in_specs=[q_blk, seq_blk, seq_blk, cache_blk, cache_blk, _const_spec(brow.shape)],
        out_specs=q_blk,
        out_shape=jax.ShapeDtypeStruct((DEC_BATCH, DEC_SEQ, WIDTH_C), BF16),
        scratch_shapes=[pltpu.VMEM((N_HEADS_C, 2 * NA_WIN_R, GRID_W, 2 * GRID_W), F32)],
        compiler_params=_params(2),
        name="na_lat",
    )(qc, kc, vc, cache_k, cache_v, brow)


def _scan_kernel(uf_ref, ub_ref, bmat_ref, lam_ref, cmat_ref, h0_ref, yf_ref, yb_ref, hfin_ref,
                 *scratch, nb, lane_w):
    n_blk = SCAN_ROWS // SCAN_BLOCK
    hf_blk, hb_blk, st_ref = scratch[:n_blk], scratch[n_blk:2 * n_blk], scratch[2 * n_blk]
    steps = SCAN_BLOCK // nb
    n_lb = SSM_LANES // lane_w

    @pl.when(pl.program_id(0) == 0)
    def _():
        st_ref[...] = h0_ref[...]

    lanes = [(slice(lb * lane_w, (lb + 1) * lane_w), slice(SSM_LANES + lb * lane_w, SSM_LANES + (lb + 1) * lane_w))
             for lb in range(n_lb)]
    state = [[st_ref[d, c, :, re] for d in range(2) for c in range(2)] for re, _ in lanes]

    def block(i):
        f0 = i * SCAN_BLOCK
        b0 = (n_blk - 1 - i) * SCAN_BLOCK
        hf_ref, hb_ref = hf_blk[i], hb_blk[i]
        uf = uf_ref[f0:f0 + SCAN_BLOCK, :].astype(BF16)
        ub = ub_ref[b0:b0 + SCAN_BLOCK, :].astype(BF16)
        for cols in lanes:
            for sl in cols:
                hf_ref[:, sl] = _dot(uf, bmat_ref[0, :, sl])
                hb_ref[:, sl] = _dot(ub, bmat_ref[1, :, sl])
            yield
        for lb, (re, im) in enumerate(lanes):
            if lb:
                yield
            lam = [[jnp.broadcast_to(lam_ref[d, c, :, re], (nb, lane_w)) for c in range(2)] for d in range(2)]
            fr, fi, br, bi = state[lb]
            for k in range(steps):
                rf = slice(k * nb, (k + 1) * nb)
                rb = slice((steps - 1 - k) * nb, (steps - k) * nb)
                fr, fi = (lam[0][0] * fr - lam[0][1] * fi + hf_ref[rf, re],
                          lam[0][0] * fi + lam[0][1] * fr + hf_ref[rf, im])
                br, bi = (lam[1][0] * br - lam[1][1] * bi + hb_ref[rb, re],
                          lam[1][0] * bi + lam[1][1] * br + hb_ref[rb, im])
                hf_ref[rf, re] = fr
                hf_ref[rf, im] = fi
                hb_ref[rb, re] = br
                hb_ref[rb, im] = bi
            state[lb] = [fr, fi, br, bi]
        yf = yb = 0.0
        for cols in lanes:
            yield
            for sl in cols:
                yf = yf + _dot(hf_ref[:, sl].astype(BF16), cmat_ref[0, sl, :])
                yb = yb + _dot(hb_ref[:, sl].astype(BF16), cmat_ref[1, sl, :])
        yf_ref[f0:f0 + SCAN_BLOCK, :] = yf
        yb_ref[b0:b0 + SCAN_BLOCK, :] = yb

    _round_robin([block(i) for i in range(n_blk)], stagger=n_lb)
    for (re, _), vals in zip(lanes, state):
        for d in range(2):
            for c in range(2):
                st_ref[d, c, :, re] = vals[2 * d + c]
    hfin_ref[...] = st_ref[...]


def _scan_pair_kernel(uf_ref, ub_ref, bmat_ref, lam_ref, cmat_ref, h0_ref, yf_ref, yb_ref, *scratch, lane_w):
    half = DEC_BATCH
    tile = 2 * half
    n_blk = SCAN_ROWS // SCAN_BLOCK
    hf_blk, hb_blk, st_ref = scratch[:n_blk], scratch[n_blk:2 * n_blk], scratch[2 * n_blk]
    tiles = SCAN_BLOCK // tile
    n_lb = SSM_LANES // lane_w

    @pl.when(pl.program_id(0) == 0)
    def _():
        st_ref[...] = h0_ref[...]

    top = lax.broadcasted_iota(jnp.int32, (tile, lane_w), 0) < half
    swap = lambda a: pltpu.roll(a, half, 0)
    lanes = [(slice(lb * lane_w, (lb + 1) * lane_w), slice(SSM_LANES + lb * lane_w, SSM_LANES + (lb + 1) * lane_w))
             for lb in range(n_lb)]
    state = [[st_ref[0, :, re], st_ref[1, :, re]] for re, _ in lanes]

    def block(i):
        f0 = i * SCAN_BLOCK
        b0 = (n_blk - 1 - i) * SCAN_BLOCK
        hf_ref, hb_ref = hf_blk[i], hb_blk[i]
        uf = uf_ref[f0:f0 + SCAN_BLOCK, :].astype(BF16)
        ub = ub_ref[b0:b0 + SCAN_BLOCK, :].astype(BF16)
        for cols in lanes:
            for sl in cols:
                hf_ref[:, sl] = _dot(uf, bmat_ref[0, :, sl])
                hb_ref[:, sl] = _dot(ub, bmat_ref[1, :, sl])
            yield
        for lb, (re, im) in enumerate(lanes):
            if lb:
                yield
            la_r = lam_ref[0, :, re]
            la_i = lam_ref[1, :, re]
            lb_r = swap(la_r)
            lb_i = swap(la_i)
            sr, si = state[lb]
            for m in range(tiles):
                rf = slice(m * tile, (m + 1) * tile)
                rb = slice((tiles - 1 - m) * tile, (tiles - m) * tile)
                fr, fi = hf_ref[rf, re], hf_ref[rf, im]
                br, bi = hb_ref[rb, re], hb_ref[rb, im]
                vr = la_r * sr - la_i * si + jnp.where(top, fr, br)
                vi = la_r * si + la_i * sr + jnp.where(top, fi, bi)
                tr = swap(vr)
                ti = swap(vi)
                wr = lb_r * tr - lb_i * ti + jnp.where(top, br, fr)
                wi = lb_r * ti + lb_i * tr + jnp.where(top, bi, fi)
                hf_ref[rf, re] = jnp.where(top, vr, wr)
                hf_ref[rf, im] = jnp.where(top, vi, wi)
                hb_ref[rb, re] = jnp.where(top, wr, vr)
                hb_ref[rb, im] = jnp.where(top, wi, vi)
                sr, si = swap(wr), swap(wi)
            state[lb] = [sr, si]
        yf = yb = 0.0
        for cols in lanes:
            yield
            for sl in cols:
                yf = yf + _dot(hf_ref[:, sl].astype(BF16), cmat_ref[0, sl, :])
                yb = yb + _dot(hb_ref[:, sl].astype(BF16), cmat_ref[1, sl, :])
        yf_ref[f0:f0 + SCAN_BLOCK, :] = yf
        yb_ref[b0:b0 + SCAN_BLOCK, :] = yb

    _round_robin([block(i) for i in range(n_blk)], stagger=n_lb)
    for (re, _), (sr, si) in zip(lanes, state):
        st_ref[0, :, re] = sr
        st_ref[1, :, re] = si


def _scan_specs(n_rows):
    n = n_rows // SCAN_ROWS
    fwd = pl.BlockSpec((SCAN_ROWS, SSM_WIDTH), lambda j: (j, 0))
    bwd = pl.BlockSpec((SCAN_ROWS, SSM_WIDTH), lambda j: (n - 1 - j, 0))
    y_shape = jax.ShapeDtypeStruct((n_rows, SSM_WIDTH), F32)
    buf = [pltpu.VMEM((SCAN_BLOCK, 2 * SSM_LANES), F32)] * (2 * (SCAN_ROWS // SCAN_BLOCK))
    return n, fwd, bwd, y_shape, buf


def _scan_ctx(u_rows, bmat, lam, cmat, h0):
    n, fwd, bwd, y_shape, buf = _scan_specs(u_rows.shape[0])
    st_shape = (2, 2, BATCH, SSM_LANES)
    return pl.pallas_call(
        functools.partial(_scan_kernel, nb=BATCH, lane_w=256),
        grid=(n,),
        in_specs=[fwd, bwd, _const_spec(bmat.shape), _const_spec(lam.shape), _const_spec(cmat.shape),
                  _const_spec(st_shape)],
        out_specs=[fwd, bwd, pl.BlockSpec(st_shape, lambda j: (0, 0, 0, 0))],
        out_shape=[y_shape, y_shape, jax.ShapeDtypeStruct(st_shape, F32)],
        scratch_shapes=buf + [pltpu.VMEM(st_shape, F32)],
        compiler_params=_params(1),
        name="scan_ctx",
    )(u_rows, u_rows, bmat, lam, cmat, h0)


def _scan_lat(u_rows, bmat, lam_pair, cmat, h0_pair):
    n, fwd, bwd, y_shape, buf = _scan_specs(u_rows.shape[0])
    st_shape = (2, 2 * DEC_BATCH, SSM_LANES)
    return pl.pallas_call(
        functools.partial(_scan_pair_kernel, lane_w=512),
        grid=(n,),
        in_specs=[fwd, bwd, _const_spec(bmat.shape), _const_spec(st_shape), _const_spec(cmat.shape),
                  _const_spec(st_shape)],
        out_specs=[fwd, bwd],
        out_shape=[y_shape, y_shape],
        scratch_shapes=buf + [pltpu.VMEM(st_shape, F32)],
        compiler_params=_params(1),
        name="scan_lat",
    )(u_rows, u_rows, bmat, lam_pair, cmat, h0_pair)


def _ssm_discretise(lam_re, lam_im, log_step, b_re, b_im, c_re, c_im):
    step = jnp.exp(log_step.astype(F32))[..., None]
    lr, li = lam_re.astype(F32), lam_im.astype(F32)
    mag = jnp.exp(lr * step)
    bar_r = mag * jnp.cos(li * step)
    bar_i = mag * jnp.sin(li * step)
    den = lr * lr + li * li
    coef_r = (((bar_r - 1) * lr + bar_i * li) / den)[..., None]
    coef_i = ((bar_i * lr - (bar_r - 1) * li) / den)[..., None]
    br, bi = b_re.astype(F32), b_im.astype(F32)
    bbar_r = coef_r * br - coef_i * bi
    bbar_i = coef_r * bi + coef_i * br
    eye = jnp.eye(SSM_GROUPS, dtype=F32)
    blk_b = lambda a: jnp.einsum('dgpc,gh->dgchp', a, eye).reshape(2, SSM_WIDTH, SSM_LANES)
    bmat = jnp.concatenate([blk_b(bbar_r), blk_b(bbar_i)], axis=-1)
    blk_c = lambda a: jnp.einsum('dgcp,gh->dgphc', a, eye).reshape(2, SSM_LANES, SSM_WIDTH)
    cmat = jnp.concatenate([blk_c(c_re.astype(F32)), -blk_c(c_im.astype(F32))], axis=1)
    lam_flat = jnp.stack([bar_r, bar_i], axis=1).reshape(2, 2, 1, SSM_LANES)
    return bmat.astype(BF16), lam_flat, cmat.astype(BF16)


def _merge_kernel(x_ref, oa_ref, yf_ref, yb_ref, u_ref, oc_ref, mod_ref, g1_ref, wg_ref, d_ref, wglu_ref,
                  wa_ref, wb_ref, wc_ref, wo_ref, g2_ref, wgu_ref, wd_ref, fg_ref, o_ref, *slabs, final):
    nb, tt, _ = x_ref.shape
    tc = tt // ROW_CHAINS
    rows = nb * tc
    mod = mod_ref[...]
    shift1 = mod[:, :, 0:D_MODEL]
    scale1 = mod[:, :, D_MODEL:2 * D_MODEL]
    gate1 = mod[:, :, 2 * D_MODEL:3 * D_MODEL]
    shift2 = mod[:, :, 3 * D_MODEL:4 * D_MODEL]
    scale2 = mod[:, :, 4 * D_MODEL:5 * D_MODEL]
    gate2 = mod[:, :, 5 * D_MODEL:6 * D_MODEL]
    def chain(c):
        ts = slice(c * tc, (c + 1) * tc)
        rs = slice(c * rows, (c + 1) * rows)
        flat = lambda ref, lo=0, hi=None: ref[:, ts, lo:hi].reshape(rows, -1)
        y = _gelu_tanh(yf_ref[rs, :] + yb_ref[rs, :] + d_ref[...] * u_ref[rs, :])
        ob = y * jax.nn.sigmoid(_dot(y.astype(BF16), wglu_ref[...]))
        ob = _to_batch_major(ob, slabs[2 * c:2 * c + 2], nb)
        x = x_ref[:, ts, :]
        h1 = (_rms(x, g1_ref[...]) * (1 + scale1) + shift1).reshape(rows, D_MODEL).astype(BF16)
        yield
        gate = lambda i: jax.nn.sigmoid(_dot(h1, wg_ref[:, i * D_MODEL:(i + 1) * D_MODEL]))
        merged = (gate(0) * _dot(flat(oa_ref), wa_ref[...])
                  + gate(1) * _dot(ob.astype(BF16), wb_ref[...])
                  + gate(2) * _dot(flat(oc_ref), wc_ref[...]))
        yield
        x1 = x + gate1 * _dot(merged.astype(BF16), wo_ref[...]).reshape(nb, tc, D_MODEL)
        h2 = _rms(x1, g2_ref[...]) * (1 + scale2) + shift2
        yield
        gu = _dot(h2.reshape(rows, D_MODEL).astype(BF16), wgu_ref[...])
        act = _silu(gu[:, :D_FF]) * gu[:, D_FF:]
        yield
        x2 = x1 + gate2 * _dot(act.astype(BF16), wd_ref[...]).reshape(nb, tc, D_MODEL)
        if final:
            x2 = _rms(x2, fg_ref[...])
        o_ref[:, ts, :] = x2

    _round_robin([chain(c) for c in range(ROW_CHAINS)])


def _merge(x, oa, yf, yb, u_rows, oc, mod_rows, weights, final_g, layer, *, name):
    final = layer == DEPTH - 1
    nb, seq, _ = x.shape
    tt = ROW_TILE // nb
    slab = lambda w: pl.BlockSpec((nb, tt, w), lambda j: (0, j, 0))
    tmaj = pl.BlockSpec((ROW_TILE, SSM_WIDTH), lambda j: (j, 0))
    return pl.pallas_call(
        functools.partial(_merge_kernel, final=final),
        grid=(seq // tt,),
        in_specs=[slab(D_MODEL), slab(WIDTH_A), tmaj, tmaj, tmaj, slab(WIDTH_C), _const_spec(mod_rows.shape)]
                 + [_layer_spec(w, layer) for w in weights]
                 + [_const_spec(final_g.shape)],
        out_specs=slab(D_MODEL),
        out_shape=jax.ShapeDtypeStruct((nb, seq, D_MODEL), F32),
        scratch_shapes=[pltpu.VMEM((ROW_TILE // ROW_CHAINS, LANES), F32)] * (2 * ROW_CHAINS),
        compiler_params=_params(1),
        name=name + ("_final" if final else ""),
    )(x, oa, yf, yb, u_rows, oc, mod_rows, *weights, final_g)


def _rope_tables():
    t = jnp.arange(DEC_SEQ)
    row = (t // GRID_W).astype(F32)
    col = (t % GRID_W).astype(F32)
    inv = 1.0 / (ROPE_THETA ** (jnp.arange(ROT_FREQS, dtype=F32) / ROT_FREQS))
    ar = row[:, None] * inv[None]
    ac = col[:, None] * inv[None]
    cos = jnp.concatenate([jnp.cos(ar), jnp.cos(ar), jnp.cos(ac), jnp.cos(ac)], axis=-1)
    sin = jnp.concatenate([-jnp.sin(ar), jnp.sin(ar), -jnp.sin(ac), jnp.sin(ac)], axis=-1)
    return jnp.tile(cos, (1, 2)), jnp.tile(sin, (1, 2))


def kernel(x_prompt, x_sample, c, cache_ga_k, cache_ga_v, cache_na_k, cache_na_v, state_ssm, c_ctx, w_mod, b_mod, norm1_g, w_in, qn_g, kn_g, ssm_lam_re, ssm_lam_im, ssm_log_step, ssm_b_re, ssm_b_im, ssm_c_re, ssm_c_im, ssm_d, ssm_w_glu, na_bias, w_br_a, w_br_b, w_br_c, w_out, norm2_g, w_ffn_gu, w_ffn_d, final_g):
    cvec = jnp.concatenate([c_ctx[None, :], c, jnp.zeros((N_MOD_ROWS - 1 - DEC_BATCH, D_MODEL), F32)], axis=0)
    mod = _adaln(cvec, w_mod, b_mod).reshape(DEPTH, N_MOD_ROWS, 1, 6 * D_MODEL)

    seg = jnp.kron(jnp.eye(MXU_DIM // HEAD_DIM, dtype=F32),
                   jnp.full((HEAD_DIM, HEAD_DIM), 1.0 / HEAD_DIM, F32)).astype(BF16)
    rope_tabs = _rope_tables()
    fg = final_g.reshape(1, D_MODEL)
    ck_a = cache_ga_k.reshape(DEC_BATCH, DEPTH, PAST_LEN, KV_WIDTH_A)
    cv_a = cache_ga_v.reshape(DEC_BATCH, DEPTH, PAST_LEN, KV_WIDTH_A)
    ck_c = cache_na_k.reshape(DEC_BATCH, DEPTH, PAST_LEN, WIDTH_C)
    cv_c = cache_na_v.reshape(DEC_BATCH, DEPTH, PAST_LEN, WIDTH_C)
    zero_state = jnp.zeros((2, 2, BATCH, SSM_LANES), F32)

    row = lambda p: p.reshape(DEPTH, 1, p.shape[-1])
    g1 = row(norm1_g)
    w_in_b = w_in[:, :, :OFF_G].astype(BF16)
    w_gate = w_in[:, :, OFF_G:].astype(BF16)
    qg = row(jnp.tile(qn_g, (1, N_HEADS_A)))
    kg = row(jnp.tile(kn_g, (1, N_KV_A)))
    merge_w = [g1, w_gate, row(ssm_d), ssm_w_glu.astype(BF16), w_br_a.astype(BF16), w_br_b.astype(BF16), w_br_c.astype(BF16),
               w_out.astype(BF16), row(norm2_g), w_ffn_gu.astype(BF16), w_ffn_d.astype(BF16)]

    xp, xs = x_prompt, x_sample
    cache = ()
    ssm_st = []
    for l in range(DEPTH):
        bmat, lam, cmat = _ssm_discretise(ssm_lam_re[l], ssm_lam_im[l], ssm_log_step[l], ssm_b_re[l], ssm_b_im[l],
                                          ssm_c_re[l], ssm_c_im[l])
        mod_ctx = mod[l, 0:1]
        mod_lat = mod[l, 1:1 + DEC_BATCH]

        qa, krep, vrep, ka, va, u_rows, qc, kc, vc = _inproj(xp, mod_ctx, g1, w_in_b, seg, qg, kg, l,
                                                             prev_cache=cache)
        cache = (ka, va, kc, vc)
        oa, oc = _attn_ctx(qa, krep, vrep, qc, kc, vc, l)
        yf, yb, hfin = _scan_ctx(u_rows, bmat, lam, cmat, zero_state)
        xp = _merge(xp, oa, yf, yb, u_rows, oc, mod_ctx, merge_w, fg, l, name="merge_ctx")
        ssm_st.append(jnp.transpose(hfin, (2, 0, 1, 3)))

        qa, krep, vrep, u_rows, qc, kc, vc = _inproj(xs, mod_lat, g1, w_in_b, seg, qg, kg, l, rope_tabs=rope_tabs)
        oa = _attn_a_lat(qa, krep, vrep, ck_a, cv_a, l)
        oc = _na_lat(qc, kc, vc, ck_c, cv_c, _na_bias_rows(na_bias[l]), l)
        h0 = jnp.transpose(state_ssm[:, l].reshape(DEC_BATCH, 2, 2, SSM_LANES), (2, 1, 0, 3))
        h0 = h0.reshape(2, 2 * DEC_BATCH, SSM_LANES)
        lam_pair = jnp.broadcast_to(jnp.transpose(lam, (1, 0, 2, 3)), (2, 2, DEC_BATCH, SSM_LANES))
        lam_pair = lam_pair.reshape(2, 2 * DEC_BATCH, SSM_LANES)
        yf, yb = _scan_lat(u_rows, bmat, lam_pair, cmat, h0)
        xs = _merge(xs, oa, yf, yb, u_rows, oc, mod_lat, merge_w, fg, l, name="merge_lat")

    ga_k, ga_v, na_k, na_v = cache
    new_ssm = jnp.stack(ssm_st, axis=1).reshape(BATCH, DEPTH, 2, 2, SSM_GROUPS, SSM_STATE)
    return (xp, xs,
            ga_k.reshape(BATCH, DEPTH, SEQ, N_KV_A, HEAD_DIM), ga_v.reshape(BATCH, DEPTH, SEQ, N_KV_A, HEAD_DIM),
            na_k.reshape(BATCH, DEPTH, SEQ, N_HEADS_C, HEAD_DIM), na_v.reshape(BATCH, DEPTH, SEQ, N_HEADS_C, HEAD_DIM),
            new_ssm)
```

```python
import functools
import math

import jax
import jax.numpy as jnp
from jax import lax
from jax.experimental import pallas as pl
from jax.experimental.pallas import tpu as pltpu

D_MODEL = 1024
BATCH = 16
SEQ = 256
DEPTH = 2
DEC_BATCH = 4
DEC_SEQ = 2048
PAST_LEN = 256
GRID_W = 64
GRID_ROWS = DEC_SEQ // GRID_W
HEAD_DIM = 64
N_HEADS_A = 8
N_KV_A = 2
REP_A = N_HEADS_A // N_KV_A
N_HEADS_C = 4
SSM_WIDTH = 256
SSM_GROUP = 16
SSM_GROUPS = SSM_WIDTH // SSM_GROUP
SSM_STATE = 64
SSM_LANES = SSM_GROUPS * SSM_STATE
NA_WIN_R = 8
NA_WIN_C = 16
D_FF = -(-8 * D_MODEL // (3 * 256)) * 256
ROPE_THETA = 10000.0
ROT_HALF = HEAD_DIM // 2
ROT_FREQS = ROT_HALF // 2
WIDTH_A = N_HEADS_A * HEAD_DIM
KV_WIDTH_A = N_KV_A * HEAD_DIM
WIDTH_C = N_HEADS_C * HEAD_DIM
N_BRANCH = 3
IN_WIDTH = WIDTH_A + 2 * KV_WIDTH_A + SSM_WIDTH + 3 * WIDTH_C + N_BRANCH * D_MODEL
EPS = 1e-6

OFF_QA = 0
OFF_KA = OFF_QA + WIDTH_A
OFF_VA = OFF_KA + KV_WIDTH_A
OFF_U = OFF_VA + KV_WIDTH_A
OFF_QC = OFF_U + SSM_WIDTH
OFF_KC = OFF_QC + WIDTH_C
OFF_VC = OFF_KC + WIDTH_C
OFF_G = OFF_VC + WIDTH_C

N_MOD_ROWS = 8
ROW_TILE = 256
Q_TILE_A = 512
Q_CHAIN_ROWS = 256
NA_Q_ROWS = 2
NA_K_ROWS = 10
NA_R0_MAX = GRID_ROWS - NA_K_ROWS
NA_STEP_BLOCKS = 4
CTX_STEP_SEQS = 4
ROW_CHAINS = 2
SCAN_ROWS = 1024
SCAN_BLOCK = 256
LOG2_E = math.log2(math.e)
Q_SCALE = HEAD_DIM ** -0.5 * LOG2_E
LANES = 128
MXU_DIM = 256
NEG = -1e30
VMEM_LIMIT_V7X = 56 * 1024 * 1024

F32 = jnp.float32
BF16 = jnp.bfloat16


def _dot(a, b):
    return jnp.dot(a, b, preferred_element_type=F32)


def _dot_t(a, b):
    return lax.dot_general(a, b, (((1,), (1,)), ((), ())), preferred_element_type=F32)


def _params(n_axes):
    return pltpu.CompilerParams(dimension_semantics=("arbitrary",) * n_axes,
                                vmem_limit_bytes=VMEM_LIMIT_V7X)


def _const_spec(shape):
    nd = len(shape)
    return pl.BlockSpec(shape, lambda *_: (0,) * nd, pipeline_mode=pl.Buffered(1))


def _layer_spec(stacked, layer):
    shape = stacked.shape[1:]
    return pl.BlockSpec((None,) + shape, lambda *_: (layer,) + (0,) * len(shape), pipeline_mode=pl.Buffered(1))


def _layer_cols_spec(stacked, layer, start, width):
    k = stacked.shape[1]
    if start == 0:
        return pl.BlockSpec((None, k, width), lambda *_: (layer, 0, 0), pipeline_mode=pl.Buffered(1))
    return pl.BlockSpec((pl.Element(1), pl.Element(k), pl.Element(width)), lambda *_: (layer, 0, start),
                        pipeline_mode=pl.Buffered(1))


def _rms(x, g):
    return x * lax.rsqrt(jnp.mean(x * x, axis=-1, keepdims=True) + EPS) * g


def _silu(x):
    return x * jax.nn.sigmoid(x)


def _gelu_tanh(x):
    c = math.sqrt(2.0 / math.pi)
    return x * (0.5 * (1.0 + jnp.tanh(c * (x + 0.044715 * (x * x * x)))))


def _seg_rms(x, seg, g):
    w = seg.shape[0]
    x2 = x * x
    hi = x2.astype(BF16)
    lo = (x2 - hi.astype(F32)).astype(BF16)
    ms = jnp.concatenate([_dot(hi[:, c:c + w], seg) + _dot(lo[:, c:c + w], seg)
                          for c in range(0, x.shape[1], w)], axis=1)
    return x * lax.rsqrt(ms + EPS) * g


def _rope(x, cos, sin_signed):
    w = x.shape[-1]
    lane = lax.broadcasted_iota(jnp.int32, x.shape, 1)
    first = (lane & ROT_FREQS) == 0
    partner = jnp.where(first, pltpu.roll(x, w - ROT_FREQS, 1), pltpu.roll(x, ROT_FREQS, 1))
    return x * cos + partner * sin_signed


def _rep_heads(kv):
    lane = lax.broadcasted_iota(jnp.int32, kv.shape, 1)
    swapped = pltpu.roll(kv, HEAD_DIM, 1)
    lo = lane < HEAD_DIM
    h0 = jnp.where(lo, kv, swapped)
    h1 = jnp.where(lo, swapped, kv)
    return jnp.concatenate([h0, h0, h1, h1], axis=1)


def _round_robin(chains, stagger=0):
    done = [False] * len(chains)
    rnd = 0
    while not all(done):
        for i, ch in enumerate(chains):
            if not done[i] and rnd >= i * stagger:
                try:
                    next(ch)
                except StopIteration:
                    done[i] = True
        rnd += 1


def _to_time_major(val, slabs, nb):
    tt = val.shape[0] // nb
    for s, slab in enumerate(slabs):
        for b in range(nb):
            slab[pl.ds(b, tt, stride=nb), :] = val[b * tt:(b + 1) * tt, s * LANES:(s + 1) * LANES]
    return jnp.concatenate([slab[...] for slab in slabs], axis=1)


def _to_batch_major(val, slabs, nb):
    tt = val.shape[0] // nb
    for s, slab in enumerate(slabs):
        slab[...] = val[:, s * LANES:(s + 1) * LANES]
    return jnp.concatenate(
        [jnp.concatenate([slab[pl.ds(b, tt, stride=nb), :] for slab in slabs], axis=1) for b in range(nb)], axis=0)


def _adaln_kernel(c_ref, w_ref, b_ref, o_ref):
    s = _silu(c_ref[...])
    o_ref[...] = _dot(s.astype(BF16), w_ref[...].astype(BF16)) + b_ref[...]


def _adaln(cvec, w_mod, b_mod):
    n_col = 6 * D_MODEL
    tn = n_col // 4
    return pl.pallas_call(
        _adaln_kernel,
        grid=(DEPTH, n_col // tn),
        in_specs=[pl.BlockSpec((N_MOD_ROWS, D_MODEL), lambda l, n: (0, 0)),
                  pl.BlockSpec((None, D_MODEL, tn), lambda l, n: (l, 0, n)),
                  pl.BlockSpec((None, 1, tn), lambda l, n: (l, 0, n))],
        out_specs=pl.BlockSpec((None, N_MOD_ROWS, tn), lambda l, n: (l, 0, n)),
        out_shape=jax.ShapeDtypeStruct((DEPTH, N_MOD_ROWS, n_col), F32),
        compiler_params=_params(2),
        name="adaln",
    )(cvec, w_mod, b_mod.reshape(DEPTH, 1, n_col))


def _inproj_kernel(*refs, latent, n_alias):
    if latent:
        (x_ref, xn_ref, mod_ref, g1_ref, w_ref, seg_ref, qg_ref, kg_ref, cos_ref, sin_ref,
         qa_ref, krep_ref, vrep_ref, u_ref, qc_ref, kc_ref, vc_ref, hb_ref, slab0, slab1) = refs
    else:
        x_ref, xn_ref, mod_ref, g1_ref, w_ref, seg_ref, qg_ref, kg_ref = refs[:8]
        (qa_ref, krep_ref, vrep_ref, ka_ref, va_ref, u_ref, qc_ref, kc_ref, vc_ref,
         hb_ref, slab0, slab1) = refs[8 + n_alias:]
    nb, tt, _ = x_ref.shape
    rows = nb * tt
    mod = mod_ref[...]
    shift = mod[:, :, 0:D_MODEL]
    scale = mod[:, :, D_MODEL:2 * D_MODEL]
    seg = seg_ref[...]
    slot = pl.program_id(0) % 2

    def normed(ref):
        h = _rms(ref[...], g1_ref[...]) * (1 + scale) + shift
        return h.reshape(rows, D_MODEL).astype(BF16)

    def put(ref, val):
        val = val.reshape(nb, tt, ref.shape[-1]).astype(ref.dtype)
        if len(ref.shape) == 3:
            ref[...] = val
        else:
            ref[:, 0] = val
            ref[:, 1:] = jnp.zeros((nb, DEPTH - 1, tt, ref.shape[-1]), ref.dtype)

    @pl.when(pl.program_id(0) == 0)
    def _():
        hb_ref[0] = normed(x_ref)

    def project():
        hb = hb_ref[slot]
        u_ref[...] = _to_time_major(_dot(hb, w_ref[:, OFF_U:OFF_QC]), (slab0, slab1), nb)
        put(qc_ref, _dot(hb, w_ref[:, OFF_QC:OFF_KC]) * Q_SCALE)
        put(kc_ref, _dot(hb, w_ref[:, OFF_KC:OFF_VC]))
        put(vc_ref, _dot(hb, w_ref[:, OFF_VC:OFF_G]))
        yield
        qa = _seg_rms(_dot(hb, w_ref[:, OFF_QA:OFF_KA]), seg, qg_ref[...])
        ka = _seg_rms(_dot(hb, w_ref[:, OFF_KA:OFF_VA]), seg[0:KV_WIDTH_A, 0:KV_WIDTH_A], kg_ref[...])
        va = _dot(hb, w_ref[:, OFF_VA:OFF_U])
        if latent:
            cos = jnp.concatenate([cos_ref[...]] * nb, axis=0)
            sin = jnp.concatenate([sin_ref[...]] * nb, axis=0)
            qa = _rope(qa, jnp.concatenate([cos] * REP_A, axis=1), jnp.concatenate([sin] * REP_A, axis=1))
            ka = _rope(ka, cos, sin)
        else:
            put(ka_ref, ka)
            put(va_ref, va)
        put(qa_ref, qa * Q_SCALE)
        put(krep_ref, _rep_heads(ka))
        put(vrep_ref, _rep_heads(va))

    def prepare_next():
        yield
        hb_ref[1 - slot] = normed(xn_ref)

    _round_robin([project(), prepare_next()])


def _inproj(x, mod_rows, g1, w_in, seg, qg, kg, layer, *, rope_tabs=None, prev_cache=()):
    latent = rope_tabs is not None
    nb, seq, _ = x.shape
    tt = ROW_TILE // nb
    slab = lambda w: pl.BlockSpec((nb, tt, w), lambda j: (0, j, 0))
    act = lambda w, dt: jax.ShapeDtypeStruct((nb, seq, w), dt)
    if layer == 0:
        cache_slab = lambda w: pl.BlockSpec((nb, DEPTH, tt, w), lambda j: (0, 0, j, 0))
    else:
        cache_slab = lambda w: pl.BlockSpec((nb, None, tt, w), lambda j: (0, layer, j, 0))
    cache = lambda w: jax.ShapeDtypeStruct((nb, DEPTH, seq, w), F32)
    n_steps = seq // tt
    next_slab = pl.BlockSpec((nb, tt, D_MODEL), lambda j: (0, jnp.minimum(j + 1, n_steps - 1), 0))
    in_specs = [slab(D_MODEL), next_slab, _const_spec(mod_rows.shape), _layer_spec(g1, layer),
                _layer_cols_spec(w_in, layer, 0, OFF_G), _const_spec(seg.shape), _layer_spec(qg, layer),
                _layer_spec(kg, layer)]
    args = [x, x, mod_rows, g1, w_in, seg, qg, kg]
    out_shape = [act(WIDTH_A, BF16), act(REP_A * KV_WIDTH_A, BF16), act(REP_A * KV_WIDTH_A, BF16)]
    out_specs = [slab(WIDTH_A), slab(REP_A * KV_WIDTH_A), slab(REP_A * KV_WIDTH_A)]
    aliases = {}
    if latent:
        in_specs += [pl.BlockSpec((tt, 2 * HEAD_DIM), lambda j: (j, 0))] * 2
        args += list(rope_tabs)
        kv_c = [act(WIDTH_C, BF16)] * 2
        kv_c_specs = [slab(WIDTH_C)] * 2
    else:
        aliases = {len(args) + i: o for i, o in enumerate((3, 4, 7, 8)[:len(prev_cache)])}
        in_specs += [pl.BlockSpec(memory_space=pl.ANY)] * len(prev_cache)
        args += list(prev_cache)
        out_shape += [cache(KV_WIDTH_A)] * 2
        out_specs += [cache_slab(KV_WIDTH_A)] * 2
        kv_c = [cache(WIDTH_C)] * 2
        kv_c_specs = [cache_slab(WIDTH_C)] * 2
    out_shape += [jax.ShapeDtypeStruct((seq * nb, SSM_WIDTH), F32),
                  act(WIDTH_C, BF16)] + kv_c
    out_specs += [pl.BlockSpec((ROW_TILE, SSM_WIDTH), lambda j: (j, 0)),
                  slab(WIDTH_C)] + kv_c_specs
    return pl.pallas_call(
        functools.partial(_inproj_kernel, latent=latent, n_alias=len(prev_cache)),
        grid=(seq // tt,),
        in_specs=in_specs,
        out_specs=out_specs,
        out_shape=out_shape,
        scratch_shapes=[pltpu.VMEM((2, ROW_TILE, D_MODEL), BF16)] + [pltpu.VMEM((ROW_TILE, LANES), F32)] * 2,
        input_output_aliases=aliases,
        compiler_params=_params(1),
        name="inproj_lat" if latent else "inproj_ctx",
    )(*args)


def _head_mask(shape, head):
    lane = lax.broadcasted_iota(jnp.int32, shape, 1)
    return (lane // HEAD_DIM) == head


def _head_chain(q, heads, keys, vals, acc, key, bias=None):
    m_rows = q.shape[0]
    masks = [_head_mask(q.shape, h) for h in heads]
    qs = jnp.concatenate([jnp.where(hm, q, jnp.zeros_like(q)) for hm in masks], axis=0)
    s = _dot_t(qs, keys())
    if bias is not None:
        s = s + bias()
    yield
    p = jnp.exp2(s - jnp.max(s, axis=-1, keepdims=True))
    l = jnp.sum(p, axis=-1, keepdims=True)
    p = p.astype(BF16)
    yield
    o = _dot(p, vals()) / l
    for i, hm in enumerate(masks):
        acc[key] = jnp.where(hm, o[i * m_rows:(i + 1) * m_rows], acc[key])


def _attn_a_kernel(q_ref, kn_ref, vn_ref, ck_ref, cv_ref, o_ref, k_ref, v_ref):
    @pl.when(pl.program_id(1) == 0)
    def _():
        k_ref[0:PAST_LEN, :] = _rep_heads(ck_ref[...]).astype(BF16)
        v_ref[0:PAST_LEN, :] = _rep_heads(cv_ref[...]).astype(BF16)
        k_ref[PAST_LEN:, :] = kn_ref[...]
        v_ref[PAST_LEN:, :] = vn_ref[...]

    gw = REP_A * HEAD_DIM
    n_sub = q_ref.shape[0] // Q_CHAIN_ROWS
    acc = {(t, g): jnp.zeros((Q_CHAIN_ROWS, gw), F32) for t in range(n_sub) for g in range(N_KV_A)}
    chains = []
    for t in range(n_sub):
        rows = slice(t * Q_CHAIN_ROWS, (t + 1) * Q_CHAIN_ROWS)
        for g in range(N_KV_A):
            sl = slice(g * gw, (g + 1) * gw)
            for j in range(REP_A):
                chains.append(_head_chain(q_ref[rows, sl], [j], lambda sl=sl: k_ref[:, sl],
                                          lambda sl=sl: v_ref[:, sl], acc, (t, g)))
    _round_robin(chains, stagger=1)
    for (t, g), val in acc.items():
        o_ref[t * Q_CHAIN_ROWS:(t + 1) * Q_CHAIN_ROWS, g * gw:(g + 1) * gw] = val.astype(BF16)


def _attn_a_lat(qa, krep, vrep, cache_k, cache_v, layer):
    seq_blk = pl.BlockSpec((None, DEC_SEQ, WIDTH_A), lambda b, t: (b, 0, 0))
    cache_blk = pl.BlockSpec((None, None, PAST_LEN, KV_WIDTH_A), lambda b, t: (b, layer, 0, 0))
    q_blk = pl.BlockSpec((None, Q_TILE_A, WIDTH_A), lambda b, t: (b, t, 0))
    return pl.pallas_call(
        _attn_a_kernel,
        grid=(DEC_BATCH, DEC_SEQ // Q_TILE_A),
        in_specs=[q_blk, seq_blk, seq_blk, cache_blk, cache_blk],
        out_specs=q_blk,
        out_shape=jax.ShapeDtypeStruct((DEC_BATCH, DEC_SEQ, WIDTH_A), BF16),
        scratch_shapes=[pltpu.VMEM((PAST_LEN + DEC_SEQ, WIDTH_A), BF16)] * 2,
        compiler_params=_params(2),
        name="attn_a_lat",
    )(qa, krep, vrep, cache_k, cache_v)


def _attn_ctx_kernel(qa_ref, ka_ref, va_ref, qc_ref, kc_ref, vc_ref, oa_ref, oc_ref):
    gw = REP_A * HEAD_DIM
    acc = {}
    chains = []
    for b in range(CTX_STEP_SEQS):
        for g in range(N_KV_A):
            sl = slice(g * gw, (g + 1) * gw)
            acc["a", b, g] = jnp.zeros((SEQ, gw), F32)
            for j in range(REP_A):
                chains.append(_head_chain(qa_ref[b, :, sl], [j], lambda b=b, sl=sl: ka_ref[b, :, sl],
                                          lambda b=b, sl=sl: va_ref[b, :, sl], acc, ("a", b, g)))
        acc["c", b] = jnp.zeros((SEQ, WIDTH_C), F32)
        for h in range(0, N_HEADS_C, 2):
            chains.append(_head_chain(qc_ref[b], [h, h + 1], lambda b=b: kc_ref[b].astype(BF16),
                                      lambda b=b: vc_ref[b].astype(BF16), acc, ("c", b)))
    _round_robin(chains, stagger=1)
    for b in range(CTX_STEP_SEQS):
        for g in range(N_KV_A):
            oa_ref[b, :, g * gw:(g + 1) * gw] = acc["a", b, g].astype(BF16)
        oc_ref[b] = acc["c", b].astype(BF16)


def _attn_ctx(qa, krep, vrep, qc, kc, vc, layer):
    blk = lambda w: pl.BlockSpec((CTX_STEP_SEQS, SEQ, w), lambda i: (i, 0, 0))
    kv_blk = pl.BlockSpec((CTX_STEP_SEQS, None, SEQ, WIDTH_C), lambda i: (i, layer, 0, 0))
    return pl.pallas_call(
        _attn_ctx_kernel,
        grid=(BATCH // CTX_STEP_SEQS,),
        in_specs=[blk(WIDTH_A), blk(WIDTH_A), blk(WIDTH_A), blk(WIDTH_C), kv_blk, kv_blk],
        out_specs=[blk(WIDTH_A), blk(WIDTH_C)],
        out_shape=[jax.ShapeDtypeStruct((BATCH, SEQ, WIDTH_A), BF16),
                   jax.ShapeDtypeStruct((BATCH, SEQ, WIDTH_C), BF16)],
        compiler_params=_params(1),
        name="attn_ctx",
    )(qa, krep, vrep, qc, kc, vc)


def _na_kernel(q_ref, k_ref, v_ref, ck_ref, cv_ref, brow_ref, o_ref, b2_ref):
    nq = NA_Q_ROWS * GRID_W
    nk = NA_K_ROWS * GRID_W
    n_pair = 2 * NA_WIN_R

    @pl.when((pl.program_id(0) == 0) & (pl.program_id(1) == 0))
    def _():
        shp = (GRID_W, 2 * GRID_W)
        c = lax.broadcasted_iota(jnp.int32, shp, 0)
        kc = lax.broadcasted_iota(jnp.int32, shp, 1) & (GRID_W - 1)
        c0 = jnp.clip(c - NA_WIN_C // 2, 0, GRID_W - NA_WIN_C)
        col_bias = jnp.where(kc < c0, NEG, jnp.where(kc >= c0 + NA_WIN_C, NEG, 0.0))
        for h in range(N_HEADS_C):
            for e in range(n_pair):
                row = jnp.broadcast_to(brow_ref[h, e:e + 1, :], shp)
                toep = pltpu.roll(row, 2 * GRID_W - (NA_WIN_C - 1), 1, stride=1, stride_axis=0)
                b2_ref[h, e] = jnp.where(col_bias < 0.0, NEG, toep * LOG2_E)

    no_bias = jnp.zeros((nq, PAST_LEN), F32)
    ck = ck_ref[...].astype(BF16)
    cv = cv_ref[...].astype(BF16)
    acc = [jnp.zeros((nq, WIDTH_C), F32) for _ in range(NA_STEP_BLOCKS)]

    def chain(blk, heads):
        qrow0 = NA_Q_ROWS * (NA_STEP_BLOCKS * pl.program_id(1) + blk)
        row0 = jnp.clip(qrow0 - NA_WIN_R // 2, 0, NA_R0_MAX)
        start = pl.multiple_of(row0 * GRID_W, GRID_W)
        qrow = qrow0 + lax.broadcasted_iota(jnp.int32, (nq, nk), 0) // GRID_W
        krow = row0 + lax.broadcasted_iota(jnp.int32, (nq, nk), 1) // GRID_W
        win0 = jnp.clip(qrow - NA_WIN_R // 2, 0, GRID_ROWS - NA_WIN_R)
        row_bias = jnp.where(krow < win0, NEG, jnp.where(krow >= win0 + NA_WIN_R, NEG, 0.0))

        def head_bias(h):
            rows = []
            for i in range(NA_Q_ROWS):
                tiles = []
                for m in range(NA_K_ROWS // 2):
                    e = jnp.clip(row0 + 2 * m - (qrow0 + i) + NA_WIN_R, 0, n_pair - 1)
                    tiles.append(b2_ref[h, e])
                rows.append(jnp.concatenate(tiles, axis=1))
            return jnp.concatenate([jnp.concatenate(rows, axis=0) + row_bias, no_bias], axis=1)

        return _head_chain(q_ref[blk * nq:(blk + 1) * nq, :], heads,
                           lambda: jnp.concatenate([k_ref[pl.ds(start, nk), :], ck], axis=0),
                           lambda: jnp.concatenate([v_ref[pl.ds(start, nk), :], cv], axis=0), acc, blk,
                           bias=lambda: jnp.concatenate([head_bias(h) for h in heads], axis=0))

    pairs = [[h, h + 1] for h in range(0, N_HEADS_C, 2)]
    _round_robin([chain(blk, heads) for blk in range(NA_STEP_BLOCKS) for heads in pairs], stagger=1)
    for blk in range(NA_STEP_BLOCKS):
        o_ref[blk * nq:(blk + 1) * nq, :] = acc[blk].astype(BF16)


def _na_bias_rows(tbl):
    pad = jnp.pad(tbl, ((0, 0), (1, 1), (0, GRID_W - tbl.shape[-1])))
    return jnp.concatenate([pad[:, :-1], pad[:, 1:]], axis=-1)


def _na_lat(qc, kc, vc, cache_k, cache_v, brow, layer):
    step_rows = NA_STEP_BLOCKS * NA_Q_ROWS * GRID_W
    seq_blk = pl.BlockSpec((None, DEC_SEQ, WIDTH_C), lambda b, j: (b, 0, 0))
    cache_blk = pl.BlockSpec((None, None, PAST_LEN, WIDTH_C), lambda b, j: (b, layer, 0, 0))
    q_blk = pl.BlockSpec((None, step_rows, WIDTH_C), lambda b, j: (b, j, 0))
    return pl.pallas_call(
        _na_kernel,
        grid=(DEC_BATCH, DEC_SEQ // step_rows),
        in_specs=[q_blk, seq_blk, seq_blk, cache_blk, cache_blk, _const_spec(brow.shape)],
        out_specs=q_blk,
        out_shape=jax.ShapeDtypeStruct((DEC_BATCH, DEC_SEQ, WIDTH_C), BF16),
        scratch_shapes=[pltpu.VMEM((N_HEADS_C, 2 * NA_WIN_R, GRID_W, 2 * GRID_W), F32)],
        compiler_params=_params(2),
        name="na_lat",
    )(qc, kc, vc, cache_k, cache_v, brow)


def _scan_kernel(uf_ref, ub_ref, bmat_ref, lam_ref, cmat_ref, h0_ref, yf_ref, yb_ref, hfin_ref,
                 *scratch, nb, lane_w):
    n_blk = SCAN_ROWS // SCAN_BLOCK
    hf_blk, hb_blk, st_ref = scratch[:n_blk], scratch[n_blk:2 * n_blk], scratch[2 * n_blk]
    steps = SCAN_BLOCK // nb
    n_lb = SSM_LANES // lane_w

    @pl.when(pl.program_id(0) == 0)
    def _():
        st_ref[...] = h0_ref[...]

    lanes = [(slice(lb * lane_w, (lb + 1) * lane_w), slice(SSM_LANES + lb * lane_w, SSM_LANES + (lb + 1) * lane_w))
             for lb in range(n_lb)]
    state = [[st_ref[d, c, :, re] for d in range(2) for c in range(2)] for re, _ in lanes]

    def block(i):
        f0 = i * SCAN_BLOCK
        b0 = (n_blk - 1 - i) * SCAN_BLOCK
        hf_ref, hb_ref = hf_blk[i], hb_blk[i]
        uf = uf_ref[f0:f0 + SCAN_BLOCK, :].astype(BF16)
        ub = ub_ref[b0:b0 + SCAN_BLOCK, :].astype(BF16)
        for cols in lanes:
            for sl in cols:
                hf_ref[:, sl] = _dot(uf, bmat_ref[0, :, sl])
                hb_ref[:, sl] = _dot(ub, bmat_ref[1, :, sl])
            yield
        for lb, (re, im) in enumerate(lanes):
            if lb:
                yield
            lam = [[jnp.broadcast_to(lam_ref[d, c, :, re], (nb, lane_w)) for c in range(2)] for d in range(2)]
            fr, fi, br, bi = state[lb]
            for k in range(steps):
                rf = slice(k * nb, (k + 1) * nb)
                rb = slice((steps - 1 - k) * nb, (steps - k) * nb)
                fr, fi = (lam[0][0] * fr - lam[0][1] * fi + hf_ref[rf, re],
                          lam[0][0] * fi + lam[0][1] * fr + hf_ref[rf, im])
                br, bi = (lam[1][0] * br - lam[1][1] * bi + hb_ref[rb, re],
                          lam[1][0] * bi + lam[1][1] * br + hb_ref[rb, im])
                hf_ref[rf, re] = fr
                hf_ref[rf, im] = fi
                hb_ref[rb, re] = br
                hb_ref[rb, im] = bi
            state[lb] = [fr, fi, br, bi]
        yf = yb = 0.0
        for cols in lanes:
            yield
            for sl in cols:
                yf = yf + _dot(hf_ref[:, sl].astype(BF16), cmat_ref[0, sl, :])
                yb = yb + _dot(hb_ref[:, sl].astype(BF16), cmat_ref[1, sl, :])
        yf_ref[f0:f0 + SCAN_BLOCK, :] = yf
        yb_ref[b0:b0 + SCAN_BLOCK, :] = yb

    _round_robin([block(i) for i in range(n_blk)], stagger=n_lb)
    for (re, _), vals in zip(lanes, state):
        for d in range(2):
            for c in range(2):
                st_ref[d, c, :, re] = vals[2 * d + c]
    hfin_ref[...] = st_ref[...]


def _scan_pair_kernel(uf_ref, ub_ref, bmat_ref, lam_ref, cmat_ref, h0_ref, yf_ref, yb_ref, *scratch, lane_w):
    half = DEC_BATCH
    tile = 2 * half
    n_blk = SCAN_ROWS // SCAN_BLOCK
    hf_blk, hb_blk, st_ref = scratch[:n_blk], scratch[n_blk:2 * n_blk], scratch[2 * n_blk]
    tiles = SCAN_BLOCK // tile
    n_lb = SSM_LANES // lane_w

    @pl.when(pl.program_id(0) == 0)
    def _():
        st_ref[...] = h0_ref[...]

    top = lax.broadcasted_iota(jnp.int32, (tile, lane_w), 0) < half
    swap = lambda a: pltpu.roll(a, half, 0)
    lanes = [(slice(lb * lane_w, (lb + 1) * lane_w), slice(SSM_LANES + lb * lane_w, SSM_LANES + (lb + 1) * lane_w))
             for lb in range(n_lb)]
    state = [[st_ref[0, :, re], st_ref[1, :, re]] for re, _ in lanes]

    def block(i):
        f0 = i * SCAN_BLOCK
        b0 = (n_blk - 1 - i) * SCAN_BLOCK
        hf_ref, hb_ref = hf_blk[i], hb_blk[i]
        uf = uf_ref[f0:f0 + SCAN_BLOCK, :].astype(BF16)
        ub = ub_ref[b0:b0 + SCAN_BLOCK, :].astype(BF16)
        for cols in lanes:
            for sl in cols:
                hf_ref[:, sl] = _dot(uf, bmat_ref[0, :, sl])
                hb_ref[:, sl] = _dot(ub, bmat_ref[1, :, sl])
            yield
        for lb, (re, im) in enumerate(lanes):
            if lb:
                yield
            la_r = lam_ref[0, :, re]
            la_i = lam_ref[1, :, re]
            lb_r = swap(la_r)
            lb_i = swap(la_i)
            sr, si = state[lb]
            for m in range(tiles):
                rf = slice(m * tile, (m + 1) * tile)
                rb = slice((tiles - 1 - m) * tile, (tiles - m) * tile)
                fr, fi = hf_ref[rf, re], hf_ref[rf, im]
                br, bi = hb_ref[rb, re], hb_ref[rb, im]
                vr = la_r * sr - la_i * si + jnp.where(top, fr, br)
                vi = la_r * si + la_i * sr + jnp.where(top, fi, bi)
                tr = swap(vr)
                ti = swap(vi)
                wr = lb_r * tr - lb_i * ti + jnp.where(top, br, fr)
                wi = lb_r * ti + lb_i * tr + jnp.where(top, bi, fi)
                hf_ref[rf, re] = jnp.where(top, vr, wr)
                hf_ref[rf, im] = jnp.where(top, vi, wi)
                hb_ref[rb, re] = jnp.where(top, wr, vr)
                hb_ref[rb, im] = jnp.where(top, wi, vi)
                sr, si = swap(wr), swap(wi)
            state[lb] = [sr, si]
        yf = yb = 0.0
        for cols in lanes:
            yield
            for sl in cols:
                yf = yf + _dot(hf_ref[:, sl].astype(BF16), cmat_ref[0, sl, :])
                yb = yb + _dot(hb_ref[:, sl].astype(BF16), cmat_ref[1, sl, :])
        yf_ref[f0:f0 + SCAN_BLOCK, :] = yf
        yb_ref[b0:b0 + SCAN_BLOCK, :] = yb

    _round_robin([block(i) for i in range(n_blk)], stagger=n_lb)
    for (re, _), (sr, si) in zip(lanes, state):
        st_ref[0, :, re] = sr
        st_ref[1, :, re] = si


def _scan_specs(n_rows):
    n = n_rows // SCAN_ROWS
    fwd = pl.BlockSpec((SCAN_ROWS, SSM_WIDTH), lambda j: (j, 0))
    bwd = pl.BlockSpec((SCAN_ROWS, SSM_WIDTH), lambda j: (n - 1 - j, 0))
    y_shape = jax.ShapeDtypeStruct((n_rows, SSM_WIDTH), F32)
    buf = [pltpu.VMEM((SCAN_BLOCK, 2 * SSM_LANES), F32)] * (2 * (SCAN_ROWS // SCAN_BLOCK))
    return n, fwd, bwd, y_shape, buf


def _scan_ctx(u_rows, bmat, lam, cmat, h0):
    n, fwd, bwd, y_shape, buf = _scan_specs(u_rows.shape[0])
    st_shape = (2, 2, BATCH, SSM_LANES)
    return pl.pallas_call(
        functools.partial(_scan_kernel, nb=BATCH, lane_w=256),
        grid=(n,),
        in_specs=[fwd, bwd, _const_spec(bmat.shape), _const_spec(lam.shape), _const_spec(cmat.shape),
                  _const_spec(st_shape)],
        out_specs=[fwd, bwd, pl.BlockSpec(st_shape, lambda j: (0, 0, 0, 0))],
        out_shape=[y_shape, y_shape, jax.ShapeDtypeStruct(st_shape, F32)],
        scratch_shapes=buf + [pltpu.VMEM(st_shape, F32)],
        compiler_params=_params(1),
        name="scan_ctx",
    )(u_rows, u_rows, bmat, lam, cmat, h0)


def _scan_lat(u_rows, bmat, lam_pair, cmat, h0_pair):
    n, fwd, bwd, y_shape, buf = _scan_specs(u_rows.shape[0])
    st_shape = (2, 2 * DEC_BATCH, SSM_LANES)
    return pl.pallas_call(
        functools.partial(_scan_pair_kernel, lane_w=512),
        grid=(n,),
        in_specs=[fwd, bwd, _const_spec(bmat.shape), _const_spec(st_shape), _const_spec(cmat.shape),
                  _const_spec(st_shape)],
        out_specs=[fwd, bwd],
        out_shape=[y_shape, y_shape],
        scratch_shapes=buf + [pltpu.VMEM(st_shape, F32)],
        compiler_params=_params(1),
        name="scan_lat",
    )(u_rows, u_rows, bmat, lam_pair, cmat, h0_pair)


def _ssm_discretise(lam_re, lam_im, log_step, b_re, b_im, c_re, c_im):
    step = jnp.exp(log_step.astype(F32))[..., None]
    lr, li = lam_re.astype(F32), lam_im.astype(F32)
    mag = jnp.exp(lr * step)
    bar_r = mag * jnp.cos(li * step)
    bar_i = mag * jnp.sin(li * step)
    den = lr * lr + li * li
    coef_r = (((bar_r - 1) * lr + bar_i * li) / den)[..., None]
    coef_i = ((bar_i * lr - (bar_r - 1) * li) / den)[..., None]
    br, bi = b_re.astype(F32), b_im.astype(F32)
    bbar_r = coef_r * br - coef_i * bi
    bbar_i = coef_r * bi + coef_i * br
    eye = jnp.eye(SSM_GROUPS, dtype=F32)
    blk_b = lambda a: jnp.einsum('dgpc,gh->dgchp', a, eye).reshape(2, SSM_WIDTH, SSM_LANES)
    bmat = jnp.concatenate([blk_b(bbar_r), blk_b(bbar_i)], axis=-1)
    blk_c = lambda a: jnp.einsum('dgcp,gh->dgphc', a, eye).reshape(2, SSM_LANES, SSM_WIDTH)
    cmat = jnp.concatenate([blk_c(c_re.astype(F32)), -blk_c(c_im.astype(F32))], axis=1)
    lam_flat = jnp.stack([bar_r, bar_i], axis=1).reshape(2, 2, 1, SSM_LANES)
    return bmat.astype(BF16), lam_flat, cmat.astype(BF16)


def _merge_kernel(x_ref, oa_ref, yf_ref, yb_ref, u_ref, oc_ref, mod_ref, g1_ref, wg_ref, d_ref, wglu_ref,
                  wa_ref, wb_ref, wc_ref, wo_ref, g2_ref, wgu_ref, wd_ref, fg_ref, o_ref, *slabs, final):
    nb, tt, _ = x_ref.shape
    tc = tt // ROW_CHAINS
    rows = nb * tc
    mod = mod_ref[...]
    shift1 = mod[:, :, 0:D_MODEL]
    scale1 = mod[:, :, D_MODEL:2 * D_MODEL]
    gate1 = mod[:, :, 2 * D_MODEL:3 * D_MODEL]
    shift2 = mod[:, :, 3 * D_MODEL:4 * D_MODEL]
    scale2 = mod[:, :, 4 * D_MODEL:5 * D_MODEL]
    gate2 = mod[:, :, 5 * D_MODEL:6 * D_MODEL]
    def chain(c):
        ts = slice(c * tc, (c + 1) * tc)
        rs = slice(c * rows, (c + 1) * rows)
        flat = lambda ref, lo=0, hi=None: ref[:, ts, lo:hi].reshape(rows, -1)
        y = _gelu_tanh(yf_ref[rs, :] + yb_ref[rs, :] + d_ref[...] * u_ref[rs, :])
        ob = y * jax.nn.sigmoid(_dot(y.astype(BF16), wglu_ref[...]))
        ob = _to_batch_major(ob, slabs[2 * c:2 * c + 2], nb)
        x = x_ref[:, ts, :]
        h1 = (_rms(x, g1_ref[...]) * (1 + scale1) + shift1).reshape(rows, D_MODEL).astype(BF16)
        yield
        gate = lambda i: jax.nn.sigmoid(_dot(h1, wg_ref[0, :, i * D_MODEL:(i + 1) * D_MODEL]))
        merged = (gate(0) * _dot(flat(oa_ref), wa_ref[...])
                  + gate(1) * _dot(ob.astype(BF16), wb_ref[...])
                  + gate(2) * _dot(flat(oc_ref), wc_ref[...]))
        yield
        x1 = x + gate1 * _dot(merged.astype(BF16), wo_ref[...]).reshape(nb, tc, D_MODEL)
        h2 = _rms(x1, g2_ref[...]) * (1 + scale2) + shift2
        yield
        gu = _dot(h2.reshape(rows, D_MODEL).astype(BF16), wgu_ref[...])
        act = _silu(gu[:, :D_FF]) * gu[:, D_FF:]
        yield
        x2 = x1 + gate2 * _dot(act.astype(BF16), wd_ref[...]).reshape(nb, tc, D_MODEL)
        if final:
            x2 = _rms(x2, fg_ref[...])
        o_ref[:, ts, :] = x2

    _round_robin([chain(c) for c in range(ROW_CHAINS)])


def _merge(x, oa, yf, yb, u_rows, oc, mod_rows, g1, w_in, weights, final_g, layer, *, name):
    final = layer == DEPTH - 1
    nb, seq, _ = x.shape
    tt = ROW_TILE // nb
    slab = lambda w: pl.BlockSpec((nb, tt, w), lambda j: (0, j, 0))
    tmaj = pl.BlockSpec((ROW_TILE, SSM_WIDTH), lambda j: (j, 0))
    return pl.pallas_call(
        functools.partial(_merge_kernel, final=final),
        grid=(seq // tt,),
        in_specs=[slab(D_MODEL), slab(WIDTH_A), tmaj, tmaj, tmaj, slab(WIDTH_C), _const_spec(mod_rows.shape),
                  _layer_spec(g1, layer), _layer_cols_spec(w_in, layer, OFF_G, N_BRANCH * D_MODEL)]
                 + [_layer_spec(w, layer) for w in weights]
                 + [_const_spec(final_g.shape)],
        out_specs=slab(D_MODEL),
        out_shape=jax.ShapeDtypeStruct((nb, seq, D_MODEL), F32),
        scratch_shapes=[pltpu.VMEM((ROW_TILE // ROW_CHAINS, LANES), F32)] * (2 * ROW_CHAINS),
        compiler_params=_params(1),
        name=name + ("_final" if final else ""),
    )(x, oa, yf, yb, u_rows, oc, mod_rows, g1, w_in, *weights, final_g)


def _rope_tables():
    t = jnp.arange(DEC_SEQ)
    row = (t // GRID_W).astype(F32)
    col = (t % GRID_W).astype(F32)
    inv = 1.0 / (ROPE_THETA ** (jnp.arange(ROT_FREQS, dtype=F32) / ROT_FREQS))
    ar = row[:, None] * inv[None]
    ac = col[:, None] * inv[None]
    cos = jnp.concatenate([jnp.cos(ar), jnp.cos(ar), jnp.cos(ac), jnp.cos(ac)], axis=-1)
    sin = jnp.concatenate([-jnp.sin(ar), jnp.sin(ar), -jnp.sin(ac), jnp.sin(ac)], axis=-1)
    return jnp.tile(cos, (1, 2)), jnp.tile(sin, (1, 2))


def kernel(x_prompt, x_sample, c, cache_ga_k, cache_ga_v, cache_na_k, cache_na_v, state_ssm, c_ctx, w_mod, b_mod, norm1_g, w_in, qn_g, kn_g, ssm_lam_re, ssm_lam_im, ssm_log_step, ssm_b_re, ssm_b_im, ssm_c_re, ssm_c_im, ssm_d, ssm_w_glu, na_bias, w_br_a, w_br_b, w_br_c, w_out, norm2_g, w_ffn_gu, w_ffn_d, final_g):
    cvec = jnp.concatenate([c_ctx[None, :], c, jnp.zeros((N_MOD_ROWS - 1 - DEC_BATCH, D_MODEL), F32)], axis=0)
    mod = _adaln(cvec, w_mod, b_mod).reshape(DEPTH, N_MOD_ROWS, 1, 6 * D_MODEL)

    seg = jnp.kron(jnp.eye(MXU_DIM // HEAD_DIM, dtype=F32),
                   jnp.full((HEAD_DIM, HEAD_DIM), 1.0 / HEAD_DIM, F32)).astype(BF16)
    rope_tabs = _rope_tables()
    fg = final_g.reshape(1, D_MODEL)
    ck_a = cache_ga_k.reshape(DEC_BATCH, DEPTH, PAST_LEN, KV_WIDTH_A)
    cv_a = cache_ga_v.reshape(DEC_BATCH, DEPTH, PAST_LEN, KV_WIDTH_A)
    ck_c = cache_na_k.reshape(DEC_BATCH, DEPTH, PAST_LEN, WIDTH_C)
    cv_c = cache_na_v.reshape(DEC_BATCH, DEPTH, PAST_LEN, WIDTH_C)
    zero_state = jnp.zeros((2, 2, BATCH, SSM_LANES), F32)

    row = lambda p: p.reshape(DEPTH, 1, p.shape[-1])
    g1 = row(norm1_g)
    w_in_b = w_in.astype(BF16)
    qg = row(jnp.tile(qn_g, (1, N_HEADS_A)))
    kg = row(jnp.tile(kn_g, (1, N_KV_A)))
    merge_w = [row(ssm_d), ssm_w_glu.astype(BF16), w_br_a.astype(BF16), w_br_b.astype(BF16), w_br_c.astype(BF16),
               w_out.astype(BF16), row(norm2_g), w_ffn_gu.astype(BF16), w_ffn_d.astype(BF16)]

    xp, xs = x_prompt, x_sample
    cache = ()
    ssm_st = []
    for l in range(DEPTH):
        bmat, lam, cmat = _ssm_discretise(ssm_lam_re[l], ssm_lam_im[l], ssm_log_step[l], ssm_b_re[l], ssm_b_im[l],
                                          ssm_c_re[l], ssm_c_im[l])
        mod_ctx = mod[l, 0:1]
        mod_lat = mod[l, 1:1 + DEC_BATCH]

        qa, krep, vrep, ka, va, u_rows, qc, kc, vc = _inproj(xp, mod_ctx, g1, w_in_b, seg, qg, kg, l,
                                                             prev_cache=cache)
        cache = (ka, va, kc, vc)
        oa, oc = _attn_ctx(qa, krep, vrep, qc, kc, vc, l)
        yf, yb, hfin = _scan_ctx(u_rows, bmat, lam, cmat, zero_state)
        xp = _merge(xp, oa, yf, yb, u_rows, oc, mod_ctx, g1, w_in_b, merge_w, fg, l, name="merge_ctx")
        ssm_st.append(jnp.transpose(hfin, (2, 0, 1, 3)))

        qa, krep, vrep, u_rows, qc, kc, vc = _inproj(xs, mod_lat, g1, w_in_b, seg, qg, kg, l, rope_tabs=rope_tabs)
        oa = _attn_a_lat(qa, krep, vrep, ck_a, cv_a, l)
        oc = _na_lat(qc, kc, vc, ck_c, cv_c, _na_bias_rows(na_bias[l]), l)
        h0 = jnp.transpose(state_ssm[:, l].reshape(DEC_BATCH, 2, 2, SSM_LANES), (2, 1, 0, 3))
        h0 = h0.reshape(2, 2 * DEC_BATCH, SSM_LANES)
        lam_pair = jnp.broadcast_to(jnp.transpose(lam, (1, 0, 2, 3)), (2, 2, DEC_BATCH, SSM_LANES))
        lam_pair = lam_pair.reshape(2, 2 * DEC_BATCH, SSM_LANES)
        yf, yb = _scan_lat(u_rows, bmat, lam_pair, cmat, h0)
        xs = _merge(xs, oa, yf, yb, u_rows, oc, mod_lat, g1, w_in_b, merge_w, fg, l, name="merge_lat")

    ga_k, ga_v, na_k, na_v = cache
    new_ssm = jnp.stack(ssm_st, axis=1).reshape(BATCH, DEPTH, 2, 2, SSM_GROUPS, SSM_STATE)
    return (xp, xs,
            ga_k.reshape(BATCH, DEPTH, SEQ, N_KV_A, HEAD_DIM), ga_v.reshape(BATCH, DEPTH, SEQ, N_KV_A, HEAD_DIM),
            na_k.reshape(BATCH, DEPTH, SEQ, N_HEADS_C, HEAD_DIM), na_v.reshape(BATCH, DEPTH, SEQ, N_HEADS_C, HEAD_DIM),
            new_ssm)
```

```python
import functools
import math

import jax
import jax.numpy as jnp
from jax import lax
from jax.experimental import pallas as pl
from jax.experimental.pallas import tpu as pltpu

D_MODEL = 1024
BATCH = 16
SEQ = 256
DEPTH = 2
DEC_BATCH = 4
DEC_SEQ = 2048
PAST_LEN = 256
GRID_W = 64
GRID_ROWS = DEC_SEQ // GRID_W
HEAD_DIM = 64
N_HEADS_A = 8
N_KV_A = 2
REP_A = N_HEADS_A // N_KV_A
N_HEADS_C = 4
SSM_WIDTH = 256
SSM_GROUP = 16
SSM_GROUPS = SSM_WIDTH // SSM_GROUP
SSM_STATE = 64
SSM_LANES = SSM_GROUPS * SSM_STATE
NA_WIN_R = 8
NA_WIN_C = 16
D_FF = -(-8 * D_MODEL // (3 * 256)) * 256
ROPE_THETA = 10000.0
ROT_HALF = HEAD_DIM // 2
ROT_FREQS = ROT_HALF // 2
WIDTH_A = N_HEADS_A * HEAD_DIM
KV_WIDTH_A = N_KV_A * HEAD_DIM
WIDTH_C = N_HEADS_C * HEAD_DIM
N_BRANCH = 3
IN_WIDTH = WIDTH_A + 2 * KV_WIDTH_A + SSM_WIDTH + 3 * WIDTH_C + N_BRANCH * D_MODEL
EPS = 1e-6

OFF_QA = 0
OFF_KA = OFF_QA + WIDTH_A
OFF_VA = OFF_KA + KV_WIDTH_A
OFF_U = OFF_VA + KV_WIDTH_A
OFF_QC = OFF_U + SSM_WIDTH
OFF_KC = OFF_QC + WIDTH_C
OFF_VC = OFF_KC + WIDTH_C
OFF_G = OFF_VC + WIDTH_C

N_MOD_ROWS = 8
ROW_TILE = 256
INPROJ_ROWS = 512
Q_TILE_A = 512
Q_CHAIN_ROWS = 256
NA_Q_ROWS = 2
NA_K_ROWS = 10
NA_R0_MAX = GRID_ROWS - NA_K_ROWS
NA_STEP_BLOCKS = 4
CTX_STEP_SEQS = 4
ROW_CHAINS = 2
SCAN_ROWS = 1024
SCAN_BLOCK = 256
LOG2_E = math.log2(math.e)
Q_SCALE = HEAD_DIM ** -0.5 * LOG2_E
LANES = 128
MXU_DIM = 256
NEG = -1e30
VMEM_LIMIT_V7X = 56 * 1024 * 1024

F32 = jnp.float32
BF16 = jnp.bfloat16


def _dot(a, b):
    return jnp.dot(a, b, preferred_element_type=F32)


def _dot_t(a, b):
    return lax.dot_general(a, b, (((1,), (1,)), ((), ())), preferred_element_type=F32)


def _params(n_axes):
    return pltpu.CompilerParams(dimension_semantics=("arbitrary",) * n_axes,
                                vmem_limit_bytes=VMEM_LIMIT_V7X)


def _const_spec(shape):
    nd = len(shape)
    return pl.BlockSpec(shape, lambda *_: (0,) * nd, pipeline_mode=pl.Buffered(1))


def _layer_spec(stacked, layer):
    shape = stacked.shape[1:]
    return pl.BlockSpec((None,) + shape, lambda *_: (layer,) + (0,) * len(shape), pipeline_mode=pl.Buffered(1))


def _layer_cols_spec(stacked, layer, start, width):
    k = stacked.shape[1]
    if start == 0:
        return pl.BlockSpec((None, k, width), lambda *_: (layer, 0, 0), pipeline_mode=pl.Buffered(1))
    return pl.BlockSpec((pl.Element(1), pl.Element(k), pl.Element(width)), lambda *_: (layer, 0, start),
                        pipeline_mode=pl.Buffered(1))


def _rms(x, g):
    return x * lax.rsqrt(jnp.mean(x * x, axis=-1, keepdims=True) + EPS) * g


def _silu(x):
    return x * jax.nn.sigmoid(x)


def _gelu_tanh(x):
    c = math.sqrt(2.0 / math.pi)
    return x * (0.5 * (1.0 + jnp.tanh(c * (x + 0.044715 * (x * x * x)))))


def _seg_rms(x, seg, g):
    w = seg.shape[0]
    x2 = x * x
    hi = x2.astype(BF16)
    lo = (x2 - hi.astype(F32)).astype(BF16)
    ms = jnp.concatenate([_dot(hi[:, c:c + w], seg) + _dot(lo[:, c:c + w], seg)
                          for c in range(0, x.shape[1], w)], axis=1)
    return x * lax.rsqrt(ms + EPS) * g


def _rope(x, cos, sin_signed):
    w = x.shape[-1]
    lane = lax.broadcasted_iota(jnp.int32, x.shape, 1)
    first = (lane & ROT_FREQS) == 0
    partner = jnp.where(first, pltpu.roll(x, w - ROT_FREQS, 1), pltpu.roll(x, ROT_FREQS, 1))
    return x * cos + partner * sin_signed


def _rep_heads(kv):
    lane = lax.broadcasted_iota(jnp.int32, kv.shape, 1)
    swapped = pltpu.roll(kv, HEAD_DIM, 1)
    lo = lane < HEAD_DIM
    h0 = jnp.where(lo, kv, swapped)
    h1 = jnp.where(lo, swapped, kv)
    return jnp.concatenate([h0, h0, h1, h1], axis=1)


def _round_robin(chains, stagger=0):
    done = [False] * len(chains)
    rnd = 0
    while not all(done):
        for i, ch in enumerate(chains):
            if not done[i] and rnd >= i * stagger:
                try:
                    next(ch)
                except StopIteration:
                    done[i] = True
        rnd += 1


def _to_time_major(val, slabs, nb):
    tt = val.shape[0] // nb
    for s, slab in enumerate(slabs):
        for b in range(nb):
            slab[pl.ds(b, tt, stride=nb), :] = val[b * tt:(b + 1) * tt, s * LANES:(s + 1) * LANES]
    return jnp.concatenate([slab[...] for slab in slabs], axis=1)


def _to_batch_major(val, slabs, nb):
    tt = val.shape[0] // nb
    for s, slab in enumerate(slabs):
        slab[...] = val[:, s * LANES:(s + 1) * LANES]
    return jnp.concatenate(
        [jnp.concatenate([slab[pl.ds(b, tt, stride=nb), :] for slab in slabs], axis=1) for b in range(nb)], axis=0)


def _adaln_kernel(c_ref, w_ref, b_ref, o_ref):
    s = _silu(c_ref[...])
    o_ref[...] = _dot(s.astype(BF16), w_ref[...].astype(BF16)) + b_ref[...]


def _adaln(cvec, w_mod, b_mod):
    n_col = 6 * D_MODEL
    tn = n_col // 4
    return pl.pallas_call(
        _adaln_kernel,
        grid=(DEPTH, n_col // tn),
        in_specs=[pl.BlockSpec((N_MOD_ROWS, D_MODEL), lambda l, n: (0, 0)),
                  pl.BlockSpec((None, D_MODEL, tn), lambda l, n: (l, 0, n)),
                  pl.BlockSpec((None, 1, tn), lambda l, n: (l, 0, n))],
        out_specs=pl.BlockSpec((None, N_MOD_ROWS, tn), lambda l, n: (l, 0, n)),
        out_shape=jax.ShapeDtypeStruct((DEPTH, N_MOD_ROWS, n_col), F32),
        compiler_params=_params(2),
        name="adaln",
    )(cvec, w_mod, b_mod.reshape(DEPTH, 1, n_col))


def _inproj_kernel(*refs, latent, n_alias):
    if latent:
        (x_ref, xn_ref, mod_ref, g1_ref, w_ref, seg_ref, qg_ref, kg_ref, cos_ref, sin_ref,
         qa_ref, krep_ref, vrep_ref, u_ref, qc_ref, kc_ref, vc_ref, hb_ref, slab0, slab1) = refs
    else:
        x_ref, xn_ref, mod_ref, g1_ref, w_ref, seg_ref, qg_ref, kg_ref = refs[:8]
        (qa_ref, krep_ref, vrep_ref, ka_ref, va_ref, u_ref, qc_ref, kc_ref, vc_ref,
         hb_ref, slab0, slab1) = refs[8 + n_alias:]
    nb, tt, _ = x_ref.shape
    rows = nb * tt
    mod = mod_ref[...]
    shift = mod[:, :, 0:D_MODEL]
    scale = mod[:, :, D_MODEL:2 * D_MODEL]
    seg = seg_ref[...]
    slot = pl.program_id(0) % 2

    def normed(ref):
        h = _rms(ref[...], g1_ref[...]) * (1 + scale) + shift
        return h.reshape(rows, D_MODEL).astype(BF16)

    def put(ref, val):
        val = val.reshape(nb, tt, ref.shape[-1]).astype(ref.dtype)
        if len(ref.shape) == 3:
            ref[...] = val
        else:
            ref[:, 0] = val
            ref[:, 1:] = jnp.zeros((nb, DEPTH - 1, tt, ref.shape[-1]), ref.dtype)

    @pl.when(pl.program_id(0) == 0)
    def _():
        hb_ref[0] = normed(x_ref)

    def project():
        hb = hb_ref[slot]
        u_ref[...] = _to_time_major(_dot(hb, w_ref[:, OFF_U:OFF_QC]), (slab0, slab1), nb)
        put(qc_ref, _dot(hb, w_ref[:, OFF_QC:OFF_KC]) * Q_SCALE)
        put(kc_ref, _dot(hb, w_ref[:, OFF_KC:OFF_VC]))
        put(vc_ref, _dot(hb, w_ref[:, OFF_VC:OFF_G]))
        yield
        qa = _seg_rms(_dot(hb, w_ref[:, OFF_QA:OFF_KA]), seg, qg_ref[...])
        ka = _seg_rms(_dot(hb, w_ref[:, OFF_KA:OFF_VA]), seg[0:KV_WIDTH_A, 0:KV_WIDTH_A], kg_ref[...])
        va = _dot(hb, w_ref[:, OFF_VA:OFF_U])
        if latent:
            cos = jnp.concatenate([cos_ref[...]] * nb, axis=0)
            sin = jnp.concatenate([sin_ref[...]] * nb, axis=0)
            qa = _rope(qa, jnp.concatenate([cos] * REP_A, axis=1), jnp.concatenate([sin] * REP_A, axis=1))
            ka = _rope(ka, cos, sin)
        else:
            put(ka_ref, ka)
            put(va_ref, va)
        put(qa_ref, qa * Q_SCALE)
        put(krep_ref, _rep_heads(ka))
        put(vrep_ref, _rep_heads(va))

    def prepare_next():
        yield
        hb_ref[1 - slot] = normed(xn_ref)

    _round_robin([project(), prepare_next()])


def _inproj(x, mod_rows, g1, w_in, seg, qg, kg, layer, *, rope_tabs=None, prev_cache=()):
    latent = rope_tabs is not None
    nb, seq, _ = x.shape
    tt = INPROJ_ROWS // nb
    slab = lambda w: pl.BlockSpec((nb, tt, w), lambda j: (0, j, 0))
    act = lambda w, dt: jax.ShapeDtypeStruct((nb, seq, w), dt)
    if layer == 0:
        cache_slab = lambda w: pl.BlockSpec((nb, DEPTH, tt, w), lambda j: (0, 0, j, 0))
    else:
        cache_slab = lambda w: pl.BlockSpec((nb, None, tt, w), lambda j: (0, layer, j, 0))
    cache = lambda w: jax.ShapeDtypeStruct((nb, DEPTH, seq, w), F32)
    n_steps = seq // tt
    next_slab = pl.BlockSpec((nb, tt, D_MODEL), lambda j: (0, jnp.minimum(j + 1, n_steps - 1), 0))
    first_slab = pl.BlockSpec((nb, tt, D_MODEL), lambda j: (0, 0, 0))
    in_specs = [first_slab, next_slab, _const_spec(mod_rows.shape), _layer_spec(g1, layer),
                _layer_cols_spec(w_in, layer, 0, OFF_G), _const_spec(seg.shape), _layer_spec(qg, layer),
                _layer_spec(kg, layer)]
    args = [x, x, mod_rows, g1, w_in, seg, qg, kg]
    out_shape = [act(WIDTH_A, BF16), act(REP_A * KV_WIDTH_A, BF16), act(REP_A * KV_WIDTH_A, BF16)]
    out_specs = [slab(WIDTH_A), slab(REP_A * KV_WIDTH_A), slab(REP_A * KV_WIDTH_A)]
    aliases = {}
    if latent:
        in_specs += [pl.BlockSpec((tt, 2 * HEAD_DIM), lambda j: (j, 0))] * 2
        args += list(rope_tabs)
        kv_c = [act(WIDTH_C, BF16)] * 2
        kv_c_specs = [slab(WIDTH_C)] * 2
    else:
        aliases = {len(args) + i: o for i, o in enumerate((3, 4, 7, 8)[:len(prev_cache)])}
        in_specs += [pl.BlockSpec(memory_space=pl.ANY)] * len(prev_cache)
        args += list(prev_cache)
        out_shape += [cache(KV_WIDTH_A)] * 2
        out_specs += [cache_slab(KV_WIDTH_A)] * 2
        kv_c = [cache(WIDTH_C)] * 2
        kv_c_specs = [cache_slab(WIDTH_C)] * 2
    out_shape += [jax.ShapeDtypeStruct((seq * nb, SSM_WIDTH), F32),
                  act(WIDTH_C, BF16)] + kv_c
    out_specs += [pl.BlockSpec((INPROJ_ROWS, SSM_WIDTH), lambda j: (j, 0)),
                  slab(WIDTH_C)] + kv_c_specs
    return pl.pallas_call(
        functools.partial(_inproj_kernel, latent=latent, n_alias=len(prev_cache)),
        grid=(seq // tt,),
        in_specs=in_specs,
        out_specs=out_specs,
        out_shape=out_shape,
        scratch_shapes=[pltpu.VMEM((2, INPROJ_ROWS, D_MODEL), BF16)] + [pltpu.VMEM((INPROJ_ROWS, LANES), F32)] * 2,
        input_output_aliases=aliases,
        compiler_params=_params(1),
        name="inproj_lat" if latent else "inproj_ctx",
    )(*args)


def _head_mask(shape, head):
    lane = lax.broadcasted_iota(jnp.int32, shape, 1)
    return (lane // HEAD_DIM) == head


def _head_chain(q, heads, keys, vals, acc, key, bias=None):
    m_rows = q.shape[0]
    masks = [_head_mask(q.shape, h) for h in heads]
    qs = jnp.concatenate([jnp.where(hm, q, jnp.zeros_like(q)) for hm in masks], axis=0)
    s = _dot_t(qs, keys())
    if bias is not None:
        s = s + bias()
    yield
    p = jnp.exp2(s - jnp.max(s, axis=-1, keepdims=True))
    l = jnp.sum(p, axis=-1, keepdims=True)
    p = p.astype(BF16)
    yield
    o = _dot(p, vals()) / l
    for i, hm in enumerate(masks):
        acc[key] = jnp.where(hm, o[i * m_rows:(i + 1) * m_rows], acc[key])


def _attn_a_kernel(q_ref, kn_ref, vn_ref, ck_ref, cv_ref, o_ref, k_ref, v_ref):
    @pl.when(pl.program_id(1) == 0)
    def _():
        k_ref[0:PAST_LEN, :] = _rep_heads(ck_ref[...]).astype(BF16)
        v_ref[0:PAST_LEN, :] = _rep_heads(cv_ref[...]).astype(BF16)
        k_ref[PAST_LEN:, :] = kn_ref[...]
        v_ref[PAST_LEN:, :] = vn_ref[...]

    gw = REP_A * HEAD_DIM
    n_sub = q_ref.shape[0] // Q_CHAIN_ROWS
    acc = {(t, g): jnp.zeros((Q_CHAIN_ROWS, gw), F32) for t in range(n_sub) for g in range(N_KV_A)}
    chains = []
    for t in range(n_sub):
        rows = slice(t * Q_CHAIN_ROWS, (t + 1) * Q_CHAIN_ROWS)
        for g in range(N_KV_A):
            sl = slice(g * gw, (g + 1) * gw)
            for j in range(REP_A):
                chains.append(_head_chain(q_ref[rows, sl], [j], lambda sl=sl: k_ref[:, sl],
                                          lambda sl=sl: v_ref[:, sl], acc, (t, g)))
    _round_robin(chains, stagger=1)
    for (t, g), val in acc.items():
        o_ref[t * Q_CHAIN_ROWS:(t + 1) * Q_CHAIN_ROWS, g * gw:(g + 1) * gw] = val.astype(BF16)


def _attn_a_lat(qa, krep, vrep, cache_k, cache_v, layer):
    seq_blk = pl.BlockSpec((None, DEC_SEQ, WIDTH_A), lambda b, t: (b, 0, 0))
    cache_blk = pl.BlockSpec((None, None, PAST_LEN, KV_WIDTH_A), lambda b, t: (b, layer, 0, 0))
    q_blk = pl.BlockSpec((None, Q_TILE_A, WIDTH_A), lambda b, t: (b, t, 0))
    return pl.pallas_call(
        _attn_a_kernel,
        grid=(DEC_BATCH, DEC_SEQ // Q_TILE_A),
        in_specs=[q_blk, seq_blk, seq_blk, cache_blk, cache_blk],
        out_specs=q_blk,
        out_shape=jax.ShapeDtypeStruct((DEC_BATCH, DEC_SEQ, WIDTH_A), BF16),
        scratch_shapes=[pltpu.VMEM((PAST_LEN + DEC_SEQ, WIDTH_A), BF16)] * 2,
        compiler_params=_params(2),
        name="attn_a_lat",
    )(qa, krep, vrep, cache_k, cache_v)


def _attn_ctx_kernel(qa_ref, ka_ref, va_ref, qc_ref, kc_ref, vc_ref, oa_ref, oc_ref):
    gw = REP_A * HEAD_DIM
    acc = {}
    chains = []
    for b in range(CTX_STEP_SEQS):
        for g in range(N_KV_A):
            sl = slice(g * gw, (g + 1) * gw)
            acc["a", b, g] = jnp.zeros((SEQ, gw), F32)
            for j in range(REP_A):
                chains.append(_head_chain(qa_ref[b, :, sl], [j], lambda b=b, sl=sl: ka_ref[b, :, sl],
                                          lambda b=b, sl=sl: va_ref[b, :, sl], acc, ("a", b, g)))
        acc["c", b] = jnp.zeros((SEQ, WIDTH_C), F32)
        for h in range(0, N_HEADS_C, 2):
            chains.append(_head_chain(qc_ref[b], [h, h + 1], lambda b=b: kc_ref[b].astype(BF16),
                                      lambda b=b: vc_ref[b].astype(BF16), acc, ("c", b)))
    _round_robin(chains, stagger=1)
    for b in range(CTX_STEP_SEQS):
        for g in range(N_KV_A):
            oa_ref[b, :, g * gw:(g + 1) * gw] = acc["a", b, g].astype(BF16)
        oc_ref[b] = acc["c", b].astype(BF16)


def _attn_ctx(qa, krep, vrep, qc, kc, vc, layer):
    blk = lambda w: pl.BlockSpec((CTX_STEP_SEQS, SEQ, w), lambda i: (i, 0, 0))
    kv_blk = pl.BlockSpec((CTX_STEP_SEQS, None, SEQ, WIDTH_C), lambda i: (i, layer, 0, 0))
    return pl.pallas_call(
        _attn_ctx_kernel,
        grid=(BATCH // CTX_STEP_SEQS,),
        in_specs=[blk(WIDTH_A), blk(WIDTH_A), blk(WIDTH_A), blk(WIDTH_C), kv_blk, kv_blk],
        out_specs=[blk(WIDTH_A), blk(WIDTH_C)],
        out_shape=[jax.ShapeDtypeStruct((BATCH, SEQ, WIDTH_A), BF16),
                   jax.ShapeDtypeStruct((BATCH, SEQ, WIDTH_C), BF16)],
        compiler_params=_params(1),
        name="attn_ctx",
    )(qa, krep, vrep, qc, kc, vc)


def _na_kernel(q_ref, k_ref, v_ref, ck_ref, cv_ref, brow_ref, o_ref, b2_ref):
    nq = NA_Q_ROWS * GRID_W
    nk = NA_K_ROWS * GRID_W
    n_pair = 2 * NA_WIN_R

    @pl.when((pl.program_id(0) == 0) & (pl.program_id(1) == 0))
    def _():
        shp = (GRID_W, 2 * GRID_W)
        c = lax.broadcasted_iota(jnp.int32, shp, 0)
        kc = lax.broadcasted_iota(jnp.int32, shp, 1) & (GRID_W - 1)
        c0 = jnp.clip(c - NA_WIN_C // 2, 0, GRID_W - NA_WIN_C)
        col_bias = jnp.where(kc < c0, NEG, jnp.where(kc >= c0 + NA_WIN_C, NEG, 0.0))
        for h in range(N_HEADS_C):
            for e in range(n_pair):
                row = jnp.broadcast_to(brow_ref[h, e:e + 1, :], shp)
                toep = pltpu.roll(row, 2 * GRID_W - (NA_WIN_C - 1), 1, stride=1, stride_axis=0)
                b2_ref[h, e] = jnp.where(col_bias < 0.0, NEG, toep * LOG2_E)

    no_bias = jnp.zeros((nq, PAST_LEN), F32)
    ck = ck_ref[...].astype(BF16)
    cv = cv_ref[...].astype(BF16)
    acc = [jnp.zeros((nq, WIDTH_C), F32) for _ in range(NA_STEP_BLOCKS)]

    def chain(blk, heads):
        qrow0 = NA_Q_ROWS * (NA_STEP_BLOCKS * pl.program_id(1) + blk)
        row0 = jnp.clip(qrow0 - NA_WIN_R // 2, 0, NA_R0_MAX)
        start = pl.multiple_of(row0 * GRID_W, GRID_W)
        qrow = qrow0 + lax.broadcasted_iota(jnp.int32, (nq, nk), 0) // GRID_W
        krow = row0 + lax.broadcasted_iota(jnp.int32, (nq, nk), 1) // GRID_W
        win0 = jnp.clip(qrow - NA_WIN_R // 2, 0, GRID_ROWS - NA_WIN_R)
        row_bias = jnp.where(krow < win0, NEG, jnp.where(krow >= win0 + NA_WIN_R, NEG, 0.0))

        def head_bias(h):
            rows = []
            for i in range(NA_Q_ROWS):
                tiles = []
                for m in range(NA_K_ROWS // 2):
                    e = jnp.clip(row0 + 2 * m - (qrow0 + i) + NA_WIN_R, 0, n_pair - 1)
                    tiles.append(b2_ref[h, e])
                rows.append(jnp.concatenate(tiles, axis=1))
            return jnp.concatenate([jnp.concatenate(rows, axis=0) + row_bias, no_bias], axis=1)

        return _head_chain(q_ref[blk * nq:(blk + 1) * nq, :], heads,
                           lambda: jnp.concatenate([k_ref[pl.ds(start, nk), :], ck], axis=0),
                           lambda: jnp.concatenate([v_ref[pl.ds(start, nk), :], cv], axis=0), acc, blk,
                           bias=lambda: jnp.concatenate([head_bias(h) for h in heads], axis=0))

    pairs = [[h, h + 1] for h in range(0, N_HEADS_C, 2)]
    _round_robin([chain(blk, heads) for blk in range(NA_STEP_BLOCKS) for heads in pairs], stagger=1)
    for blk in range(NA_STEP_BLOCKS):
        o_ref[blk * nq:(blk + 1) * nq, :] = acc[blk].astype(BF16)


def _na_bias_rows(tbl):
    pad = jnp.pad(tbl, ((0, 0), (1, 1), (0, GRID_W - tbl.shape[-1])))
    return jnp.concatenate([pad[:, :-1], pad[:, 1:]], axis=-1)


def _na_lat(qc, kc, vc, cache_k, cache_v, brow, layer):
    step_rows = NA_STEP_BLOCKS * NA_Q_ROWS * GRID_W
    seq_blk = pl.BlockSpec((None, DEC_SEQ, WIDTH_C), lambda b, j: (b, 0, 0))
    cache_blk = pl.BlockSpec((None, None, PAST_LEN, WIDTH_C), lambda b, j: (b, layer, 0, 0))
    q_blk = pl.BlockSpec((None, step_rows, WIDTH_C), lambda b, j: (b, j, 0))
    return pl.pallas_call(
        _na_kernel,
        grid=(DEC_BATCH, DEC_SEQ // step_rows),
        in_specs=[q_blk, seq_blk, seq_blk, cache_blk, cache_blk, _const_spec(brow.shape)],
        out_specs=q_blk,
        out_shape=jax.ShapeDtypeStruct((DEC_BATCH, DEC_SEQ, WIDTH_C), BF16),
        scratch_shapes=[pltpu.VMEM((N_HEADS_C, 2 * NA_WIN_R, GRID_W, 2 * GRID_W), F32)],
        compiler_params=_params(2),
        name="na_lat",
    )(qc, kc, vc, cache_k, cache_v, brow)


def _scan_kernel(uf_ref, ub_ref, bmat_ref, lam_ref, cmat_ref, h0_ref, yf_ref, yb_ref, hfin_ref,
                 *scratch, nb, lane_w):
    n_blk = SCAN_ROWS // SCAN_BLOCK
    hf_blk, hb_blk, st_ref = scratch[:n_blk], scratch[n_blk:2 * n_blk], scratch[2 * n_blk]
    steps = SCAN_BLOCK // nb
    n_lb = SSM_LANES // lane_w

    @pl.when(pl.program_id(0) == 0)
    def _():
        st_ref[...] = h0_ref[...]

    lanes = [(slice(lb * lane_w, (lb + 1) * lane_w), slice(SSM_LANES + lb * lane_w, SSM_LANES + (lb + 1) * lane_w))
             for lb in range(n_lb)]
    state = [[st_ref[d, c, :, re] for d in range(2) for c in range(2)] for re, _ in lanes]

    def block(i):
        f0 = i * SCAN_BLOCK
        b0 = (n_blk - 1 - i) * SCAN_BLOCK
        hf_ref, hb_ref = hf_blk[i], hb_blk[i]
        uf = uf_ref[f0:f0 + SCAN_BLOCK, :].astype(BF16)
        ub = ub_ref[b0:b0 + SCAN_BLOCK, :].astype(BF16)
        for cols in lanes:
            for sl in cols:
                hf_ref[:, sl] = _dot(uf, bmat_ref[0, :, sl])
                hb_ref[:, sl] = _dot(ub, bmat_ref[1, :, sl])
            yield
        for lb, (re, im) in enumerate(lanes):
            if lb:
                yield
            lam = [[jnp.broadcast_to(lam_ref[d, c, :, re], (nb, lane_w)) for c in range(2)] for d in range(2)]
            fr, fi, br, bi = state[lb]
            for k in range(steps):
                rf = slice(k * nb, (k + 1) * nb)
                rb = slice((steps - 1 - k) * nb, (steps - k) * nb)
                fr, fi = (lam[0][0] * fr - lam[0][1] * fi + hf_ref[rf, re],
                          lam[0][0] * fi + lam[0][1] * fr + hf_ref[rf, im])
                br, bi = (lam[1][0] * br - lam[1][1] * bi + hb_ref[rb, re],
                          lam[1][0] * bi + lam[1][1] * br + hb_ref[rb, im])
                hf_ref[rf, re] = fr
                hf_ref[rf, im] = fi
                hb_ref[rb, re] = br
                hb_ref[rb, im] = bi
            state[lb] = [fr, fi, br, bi]
        yf = yb = 0.0
        for cols in lanes:
            yield
            for sl in cols:
                yf = yf + _dot(hf_ref[:, sl].astype(BF16), cmat_ref[0, sl, :])
                yb = yb + _dot(hb_ref[:, sl].astype(BF16), cmat_ref[1, sl, :])
        yf_ref[f0:f0 + SCAN_BLOCK, :] = yf
        yb_ref[b0:b0 + SCAN_BLOCK, :] = yb

    _round_robin([block(i) for i in range(n_blk)], stagger=n_lb)
    for (re, _), vals in zip(lanes, state):
        for d in range(2):
            for c in range(2):
                st_ref[d, c, :, re] = vals[2 * d + c]
    hfin_ref[...] = st_ref[...]


def _scan_pair_kernel(uf_ref, ub_ref, bmat_ref, lam_ref, cmat_ref, h0_ref, yf_ref, yb_ref, *scratch, lane_w):
    half = DEC_BATCH
    tile = 2 * half
    n_blk = SCAN_ROWS // SCAN_BLOCK
    hf_blk, hb_blk, st_ref = scratch[:n_blk], scratch[n_blk:2 * n_blk], scratch[2 * n_blk]
    tiles = SCAN_BLOCK // tile
    n_lb = SSM_LANES // lane_w

    @pl.when(pl.program_id(0) == 0)
    def _():
        st_ref[...] = h0_ref[...]

    top = lax.broadcasted_iota(jnp.int32, (tile, lane_w), 0) < half
    swap = lambda a: pltpu.roll(a, half, 0)
    lanes = [(slice(lb * lane_w, (lb + 1) * lane_w), slice(SSM_LANES + lb * lane_w, SSM_LANES + (lb + 1) * lane_w))
             for lb in range(n_lb)]
    state = [[st_ref[0, :, re], st_ref[1, :, re]] for re, _ in lanes]

    def block(i):
        f0 = i * SCAN_BLOCK
        b0 = (n_blk - 1 - i) * SCAN_BLOCK
        hf_ref, hb_ref = hf_blk[i], hb_blk[i]
        uf = uf_ref[f0:f0 + SCAN_BLOCK, :].astype(BF16)
        ub = ub_ref[b0:b0 + SCAN_BLOCK, :].astype(BF16)
        for cols in lanes:
            for sl in cols:
                hf_ref[:, sl] = _dot(uf, bmat_ref[0, :, sl])
                hb_ref[:, sl] = _dot(ub, bmat_ref[1, :, sl])
            yield
        for lb, (re, im) in enumerate(lanes):
            if lb:
                yield
            la_r = lam_ref[0, :, re]
            la_i = lam_ref[1, :, re]
            lb_r = swap(la_r)
            lb_i = swap(la_i)
            sr, si = state[lb]
            for m in range(tiles):
                rf = slice(m * tile, (m + 1) * tile)
                rb = slice((tiles - 1 - m) * tile, (tiles - m) * tile)
                fr, fi = hf_ref[rf, re], hf_ref[rf, im]
                br, bi = hb_ref[rb, re], hb_ref[rb, im]
                vr = la_r * sr - la_i * si + jnp.where(top, fr, br)
                vi = la_r * si + la_i * sr + jnp.where(top, fi, bi)
                tr = swap(vr)
                ti = swap(vi)
                wr = lb_r * tr - lb_i * ti + jnp.where(top, br, fr)
                wi = lb_r * ti + lb_i * tr + jnp.where(top, bi, fi)
                hf_ref[rf, re] = jnp.where(top, vr, wr)
                hf_ref[rf, im] = jnp.where(top, vi, wi)
                hb_ref[rb, re] = jnp.where(top, wr, vr)
                hb_ref[rb, im] = jnp.where(top, wi, vi)
                sr, si = swap(wr), swap(wi)
            state[lb] = [sr, si]
        yf = yb = 0.0
        for cols in lanes:
            yield
            for sl in cols:
                yf = yf + _dot(hf_ref[:, sl].astype(BF16), cmat_ref[0, sl, :])
                yb = yb + _dot(hb_ref[:, sl].astype(BF16), cmat_ref[1, sl, :])
        yf_ref[f0:f0 + SCAN_BLOCK, :] = yf
        yb_ref[b0:b0 + SCAN_BLOCK, :] = yb

    _round_robin([block(i) for i in range(n_blk)], stagger=n_lb)
    for (re, _), (sr, si) in zip(lanes, state):
        st_ref[0, :, re] = sr
        st_ref[1, :, re] = si


def _scan_specs(n_rows):
    n = n_rows // SCAN_ROWS
    fwd = pl.BlockSpec((SCAN_ROWS, SSM_WIDTH), lambda j: (j, 0))
    bwd = pl.BlockSpec((SCAN_ROWS, SSM_WIDTH), lambda j: (n - 1 - j, 0))
    y_shape = jax.ShapeDtypeStruct((n_rows, SSM_WIDTH), F32)
    buf = [pltpu.VMEM((SCAN_BLOCK, 2 * SSM_LANES), F32)] * (2 * (SCAN_ROWS // SCAN_BLOCK))
    return n, fwd, bwd, y_shape, buf


def _scan_ctx(u_rows, bmat, lam, cmat, h0):
    n, fwd, bwd, y_shape, buf = _scan_specs(u_rows.shape[0])
    st_shape = (2, 2, BATCH, SSM_LANES)
    return pl.pallas_call(
        functools.partial(_scan_kernel, nb=BATCH, lane_w=256),
        grid=(n,),
        in_specs=[fwd, bwd, _const_spec(bmat.shape), _const_spec(lam.shape), _const_spec(cmat.shape),
                  _const_spec(st_shape)],
        out_specs=[fwd, bwd, pl.BlockSpec(st_shape, lambda j: (0, 0, 0, 0))],
        out_shape=[y_shape, y_shape, jax.ShapeDtypeStruct(st_shape, F32)],
        scratch_shapes=buf + [pltpu.VMEM(st_shape, F32)],
        compiler_params=_params(1),
        name="scan_ctx",
    )(u_rows, u_rows, bmat, lam, cmat, h0)


def _scan_lat(u_rows, bmat, lam_pair, cmat, h0_pair):
    n, fwd, bwd, y_shape, buf = _scan_specs(u_rows.shape[0])
    st_shape = (2, 2 * DEC_BATCH, SSM_LANES)
    return pl.pallas_call(
        functools.partial(_scan_pair_kernel, lane_w=512),
        grid=(n,),
        in_specs=[fwd, bwd, _const_spec(bmat.shape), _const_spec(st_shape), _const_spec(cmat.shape),
                  _const_spec(st_shape)],
        out_specs=[fwd, bwd],
        out_shape=[y_shape, y_shape],
        scratch_shapes=buf + [pltpu.VMEM(st_shape, F32)],
        compiler_params=_params(1),
        name="scan_lat",
    )(u_rows, u_rows, bmat, lam_pair, cmat, h0_pair)


def _ssm_discretise(lam_re, lam_im, log_step, b_re, b_im, c_re, c_im):
    step = jnp.exp(log_step.astype(F32))[..., None]
    lr, li = lam_re.astype(F32), lam_im.astype(F32)
    mag = jnp.exp(lr * step)
    bar_r = mag * jnp.cos(li * step)
    bar_i = mag * jnp.sin(li * step)
    den = lr * lr + li * li
    coef_r = (((bar_r - 1) * lr + bar_i * li) / den)[..., None]
    coef_i = ((bar_i * lr - (bar_r - 1) * li) / den)[..., None]
    br, bi = b_re.astype(F32), b_im.astype(F32)
    bbar_r = coef_r * br - coef_i * bi
    bbar_i = coef_r * bi + coef_i * br
    eye = jnp.eye(SSM_GROUPS, dtype=F32)
    blk_b = lambda a: jnp.einsum('dgpc,gh->dgchp', a, eye).reshape(2, SSM_WIDTH, SSM_LANES)
    bmat = jnp.concatenate([blk_b(bbar_r), blk_b(bbar_i)], axis=-1)
    blk_c = lambda a: jnp.einsum('dgcp,gh->dgphc', a, eye).reshape(2, SSM_LANES, SSM_WIDTH)
    cmat = jnp.concatenate([blk_c(c_re.astype(F32)), -blk_c(c_im.astype(F32))], axis=1)
    lam_flat = jnp.stack([bar_r, bar_i], axis=1).reshape(2, 2, 1, SSM_LANES)
    return bmat.astype(BF16), lam_flat, cmat.astype(BF16)


def _merge_kernel(x_ref, oa_ref, yf_ref, yb_ref, u_ref, oc_ref, mod_ref, g1_ref, wg_ref, d_ref, wglu_ref,
                  wa_ref, wb_ref, wc_ref, wo_ref, g2_ref, wgu_ref, wd_ref, fg_ref, o_ref, *slabs, final):
    nb, tt, _ = x_ref.shape
    tc = tt // ROW_CHAINS
    rows = nb * tc
    mod = mod_ref[...]
    shift1 = mod[:, :, 0:D_MODEL]
    scale1 = mod[:, :, D_MODEL:2 * D_MODEL]
    gate1 = mod[:, :, 2 * D_MODEL:3 * D_MODEL]
    shift2 = mod[:, :, 3 * D_MODEL:4 * D_MODEL]
    scale2 = mod[:, :, 4 * D_MODEL:5 * D_MODEL]
    gate2 = mod[:, :, 5 * D_MODEL:6 * D_MODEL]
    def chain(c):
        ts = slice(c * tc, (c + 1) * tc)
        rs = slice(c * rows, (c + 1) * rows)
        flat = lambda ref, lo=0, hi=None: ref[:, ts, lo:hi].reshape(rows, -1)
        y = _gelu_tanh(yf_ref[rs, :] + yb_ref[rs, :] + d_ref[...] * u_ref[rs, :])
        ob = y * jax.nn.sigmoid(_dot(y.astype(BF16), wglu_ref[...]))
        ob = _to_batch_major(ob, slabs[2 * c:2 * c + 2], nb)
        x = x_ref[:, ts, :]
        h1 = (_rms(x, g1_ref[...]) * (1 + scale1) + shift1).reshape(rows, D_MODEL).astype(BF16)
        yield
        gate = lambda i: jax.nn.sigmoid(_dot(h1, wg_ref[0, :, i * D_MODEL:(i + 1) * D_MODEL]))
        merged = (gate(0) * _dot(flat(oa_ref), wa_ref[...])
                  + gate(1) * _dot(ob.astype(BF16), wb_ref[...])
                  + gate(2) * _dot(flat(oc_ref), wc_ref[...]))
        yield
        x1 = x + gate1 * _dot(merged.astype(BF16), wo_ref[...]).reshape(nb, tc, D_MODEL)
        h2 = _rms(x1, g2_ref[...]) * (1 + scale2) + shift2
        yield
        gu = _dot(h2.reshape(rows, D_MODEL).astype(BF16), wgu_ref[...])
        act = _silu(gu[:, :D_FF]) * gu[:, D_FF:]
        yield
        x2 = x1 + gate2 * _dot(act.astype(BF16), wd_ref[...]).reshape(nb, tc, D_MODEL)
        if final:
            x2 = _rms(x2, fg_ref[...])
        o_ref[:, ts, :] = x2

    _round_robin([chain(c) for c in range(ROW_CHAINS)])


def _merge(x, oa, yf, yb, u_rows, oc, mod_rows, g1, w_in, weights, final_g, layer, *, name):
    final = layer == DEPTH - 1
    nb, seq, _ = x.shape
    tt = ROW_TILE // nb
    slab = lambda w: pl.BlockSpec((nb, tt, w), lambda j: (0, j, 0))
    tmaj = pl.BlockSpec((ROW_TILE, SSM_WIDTH), lambda j: (j, 0))
    return pl.pallas_call(
        functools.partial(_merge_kernel, final=final),
        grid=(seq // tt,),
        in_specs=[slab(D_MODEL), slab(WIDTH_A), tmaj, tmaj, tmaj, slab(WIDTH_C), _const_spec(mod_rows.shape),
                  _layer_spec(g1, layer), _layer_cols_spec(w_in, layer, OFF_G, N_BRANCH * D_MODEL)]
                 + [_layer_spec(w, layer) for w in weights]
                 + [_const_spec(final_g.shape)],
        out_specs=slab(D_MODEL),
        out_shape=jax.ShapeDtypeStruct((nb, seq, D_MODEL), F32),
        scratch_shapes=[pltpu.VMEM((ROW_TILE // ROW_CHAINS, LANES), F32)] * (2 * ROW_CHAINS),
        compiler_params=_params(1),
        name=name + ("_final" if final else ""),
    )(x, oa, yf, yb, u_rows, oc, mod_rows, g1, w_in, *weights, final_g)


def _rope_tables():
    t = jnp.arange(DEC_SEQ)
    row = (t // GRID_W).astype(F32)
    col = (t % GRID_W).astype(F32)
    inv = 1.0 / (ROPE_THETA ** (jnp.arange(ROT_FREQS, dtype=F32) / ROT_FREQS))
    ar = row[:, None] * inv[None]
    ac = col[:, None] * inv[None]
    cos = jnp.concatenate([jnp.cos(ar), jnp.cos(ar), jnp.cos(ac), jnp.cos(ac)], axis=-1)
    sin = jnp.concatenate([-jnp.sin(ar), jnp.sin(ar), -jnp.sin(ac), jnp.sin(ac)], axis=-1)
    return jnp.tile(cos, (1, 2)), jnp.tile(sin, (1, 2))


def kernel(x_prompt, x_sample, c, cache_ga_k, cache_ga_v, cache_na_k, cache_na_v, state_ssm, c_ctx, w_mod, b_mod, norm1_g, w_in, qn_g, kn_g, ssm_lam_re, ssm_lam_im, ssm_log_step, ssm_b_re, ssm_b_im, ssm_c_re, ssm_c_im, ssm_d, ssm_w_glu, na_bias, w_br_a, w_br_b, w_br_c, w_out, norm2_g, w_ffn_gu, w_ffn_d, final_g):
    cvec = jnp.concatenate([c_ctx[None, :], c, jnp.zeros((N_MOD_ROWS - 1 - DEC_BATCH, D_MODEL), F32)], axis=0)
    mod = _adaln(cvec, w_mod, b_mod).reshape(DEPTH, N_MOD_ROWS, 1, 6 * D_MODEL)

    seg = jnp.kron(jnp.eye(MXU_DIM // HEAD_DIM, dtype=F32),
                   jnp.full((HEAD_DIM, HEAD_DIM), 1.0 / HEAD_DIM, F32)).astype(BF16)
    rope_tabs = _rope_tables()
    fg = final_g.reshape(1, D_MODEL)
    ck_a = cache_ga_k.reshape(DEC_BATCH, DEPTH, PAST_LEN, KV_WIDTH_A)
    cv_a = cache_ga_v.reshape(DEC_BATCH, DEPTH, PAST_LEN, KV_WIDTH_A)
    ck_c = cache_na_k.reshape(DEC_BATCH, DEPTH, PAST_LEN, WIDTH_C)
    cv_c = cache_na_v.reshape(DEC_BATCH, DEPTH, PAST_LEN, WIDTH_C)
    zero_state = jnp.zeros((2, 2, BATCH, SSM_LANES), F32)

    row = lambda p: p.reshape(DEPTH, 1, p.shape[-1])
    g1 = row(norm1_g)
    w_in_b = w_in.astype(BF16)
    qg = row(jnp.tile(qn_g, (1, N_HEADS_A)))
    kg = row(jnp.tile(kn_g, (1, N_KV_A)))
    merge_w = [row(ssm_d), ssm_w_glu.astype(BF16), w_br_a.astype(BF16), w_br_b.astype(BF16), w_br_c.astype(BF16),
               w_out.astype(BF16), row(norm2_g), w_ffn_gu.astype(BF16), w_ffn_d.astype(BF16)]

    xp, xs = x_prompt, x_sample
    cache = ()
    ssm_st = []
    for l in range(DEPTH):
        bmat, lam, cmat = _ssm_discretise(ssm_lam_re[l], ssm_lam_im[l], ssm_log_step[l], ssm_b_re[l], ssm_b_im[l],
                                          ssm_c_re[l], ssm_c_im[l])
        mod_ctx = mod[l, 0:1]
        mod_lat = mod[l, 1:1 + DEC_BATCH]

        qa, krep, vrep, ka, va, u_rows, qc, kc, vc = _inproj(xp, mod_ctx, g1, w_in_b, seg, qg, kg, l,
                                                             prev_cache=cache)
        cache = (ka, va, kc, vc)
        oa, oc = _attn_ctx(qa, krep, vrep, qc, kc, vc, l)
        yf, yb, hfin = _scan_ctx(u_rows, bmat, lam, cmat, zero_state)
        xp = _merge(xp, oa, yf, yb, u_rows, oc, mod_ctx, g1, w_in_b, merge_w, fg, l, name="merge_ctx")
        ssm_st.append(jnp.transpose(hfin, (2, 0, 1, 3)))

        qa, krep, vrep, u_rows, qc, kc, vc = _inproj(xs, mod_lat, g1, w_in_b, seg, qg, kg, l, rope_tabs=rope_tabs)
        oa = _attn_a_lat(qa, krep, vrep, ck_a, cv_a, l)
        oc = _na_lat(qc, kc, vc, ck_c, cv_c, _na_bias_rows(na_bias[l]), l)
        h0 = jnp.transpose(state_ssm[:, l].reshape(DEC_BATCH, 2, 2, SSM_LANES), (2, 1, 0, 3))
        h0 = h0.reshape(2, 2 * DEC_BATCH, SSM_LANES)
        lam_pair = jnp.broadcast_to(jnp.transpose(lam, (1, 0, 2, 3)), (2, 2, DEC_BATCH, SSM_LANES))
        lam_pair = lam_pair.reshape(2, 2 * DEC_BATCH, SSM_LANES)
        yf, yb = _scan_lat(u_rows, bmat, lam_pair, cmat, h0)
        xs = _merge(xs, oa, yf, yb, u_rows, oc, mod_lat, g1, w_in_b, merge_w, fg, l, name="merge_lat")

    ga_k, ga_v, na_k, na_v = cache
    new_ssm = jnp.stack(ssm_st, axis=1).reshape(BATCH, DEPTH, 2, 2, SSM_GROUPS, SSM_STATE)
    return (xp, xs,
            ga_k.reshape(BATCH, DEPTH, SEQ, N_KV_A, HEAD_DIM), ga_v.reshape(BATCH, DEPTH, SEQ, N_KV_A, HEAD_DIM),
            na_k.reshape(BATCH, DEPTH, SEQ, N_HEADS_C, HEAD_DIM), na_v.reshape(BATCH, DEPTH, SEQ, N_HEADS_C, HEAD_DIM),
            new_ssm)
```

```python
import functools
import math

import jax
import jax.numpy as jnp
from jax import lax
from jax.experimental import pallas as pl
from jax.experimental.pallas import tpu as pltpu

D_MODEL = 1024
BATCH = 16
SEQ = 256
DEPTH = 2
DEC_BATCH = 4
DEC_SEQ = 2048
PAST_LEN = 256
GRID_W = 64
GRID_ROWS = DEC_SEQ // GRID_W
HEAD_DIM = 64
N_HEADS_A = 8
N_KV_A = 2
REP_A = N_HEADS_A // N_KV_A
N_HEADS_C = 4
SSM_WIDTH = 256
SSM_GROUP = 16
SSM_GROUPS = SSM_WIDTH // SSM_GROUP
SSM_STATE = 64
SSM_LANES = SSM_GROUPS * SSM_STATE
NA_WIN_R = 8
NA_WIN_C = 16
D_FF = -(-8 * D_MODEL // (3 * 256)) * 256
ROPE_THETA = 10000.0
ROT_HALF = HEAD_DIM // 2
ROT_FREQS = ROT_HALF // 2
WIDTH_A = N_HEADS_A * HEAD_DIM
KV_WIDTH_A = N_KV_A * HEAD_DIM
WIDTH_C = N_HEADS_C * HEAD_DIM
N_BRANCH = 3
IN_WIDTH = WIDTH_A + 2 * KV_WIDTH_A + SSM_WIDTH + 3 * WIDTH_C + N_BRANCH * D_MODEL
EPS = 1e-6

OFF_QA = 0
OFF_KA = OFF_QA + WIDTH_A
OFF_VA = OFF_KA + KV_WIDTH_A
OFF_U = OFF_VA + KV_WIDTH_A
OFF_QC = OFF_U + SSM_WIDTH
OFF_KC = OFF_QC + WIDTH_C
OFF_VC = OFF_KC + WIDTH_C
OFF_G = OFF_VC + WIDTH_C

N_MOD_ROWS = 8
ROW_TILE = 512
INPROJ_ROWS = 512
Q_TILE_A = 512
Q_CHAIN_ROWS = 256
NA_Q_ROWS = 2
NA_K_ROWS = 10
NA_R0_MAX = GRID_ROWS - NA_K_ROWS
NA_STEP_BLOCKS = 4
CTX_STEP_SEQS = 4
ROW_CHAINS = 2
SCAN_ROWS = 1024
SCAN_BLOCK = 256
LOG2_E = math.log2(math.e)
Q_SCALE = HEAD_DIM ** -0.5 * LOG2_E
LANES = 128
MXU_DIM = 256
NEG = -1e30
VMEM_LIMIT_V7X = 56 * 1024 * 1024

F32 = jnp.float32
BF16 = jnp.bfloat16


def _dot(a, b):
    return jnp.dot(a, b, preferred_element_type=F32)


def _dot_t(a, b):
    return lax.dot_general(a, b, (((1,), (1,)), ((), ())), preferred_element_type=F32)


def _params(n_axes):
    return pltpu.CompilerParams(dimension_semantics=("arbitrary",) * n_axes,
                                vmem_limit_bytes=VMEM_LIMIT_V7X)


def _const_spec(shape):
    nd = len(shape)
    return pl.BlockSpec(shape, lambda *_: (0,) * nd, pipeline_mode=pl.Buffered(1))


def _layer_spec(stacked, layer):
    shape = stacked.shape[1:]
    return pl.BlockSpec((None,) + shape, lambda *_: (layer,) + (0,) * len(shape), pipeline_mode=pl.Buffered(1))


def _layer_cols_spec(stacked, layer, start, width):
    k = stacked.shape[1]
    if start == 0:
        return pl.BlockSpec((None, k, width), lambda *_: (layer, 0, 0), pipeline_mode=pl.Buffered(1))
    return pl.BlockSpec((pl.Element(1), pl.Element(k), pl.Element(width)), lambda *_: (layer, 0, start),
                        pipeline_mode=pl.Buffered(1))


def _rms(x, g):
    return x * lax.rsqrt(jnp.mean(x * x, axis=-1, keepdims=True) + EPS) * g


def _silu(x):
    return x * jax.nn.sigmoid(x)


def _gelu_tanh(x):
    c = math.sqrt(2.0 / math.pi)
    return x * (0.5 * (1.0 + jnp.tanh(c * (x + 0.044715 * (x * x * x)))))


def _seg_rms(x, seg, g):
    w = seg.shape[0]
    x2 = x * x
    hi = x2.astype(BF16)
    lo = (x2 - hi.astype(F32)).astype(BF16)
    ms = jnp.concatenate([_dot(hi[:, c:c + w], seg) + _dot(lo[:, c:c + w], seg)
                          for c in range(0, x.shape[1], w)], axis=1)
    return x * lax.rsqrt(ms + EPS) * g


def _rope(x, cos, sin_signed):
    w = x.shape[-1]
    lane = lax.broadcasted_iota(jnp.int32, x.shape, 1)
    first = (lane & ROT_FREQS) == 0
    partner = jnp.where(first, pltpu.roll(x, w - ROT_FREQS, 1), pltpu.roll(x, ROT_FREQS, 1))
    return x * cos + partner * sin_signed


def _rep_heads(kv):
    lane = lax.broadcasted_iota(jnp.int32, kv.shape, 1)
    swapped = pltpu.roll(kv, HEAD_DIM, 1)
    lo = lane < HEAD_DIM
    h0 = jnp.where(lo, kv, swapped)
    h1 = jnp.where(lo, swapped, kv)
    return jnp.concatenate([h0, h0, h1, h1], axis=1)


def _round_robin(chains, stagger=0):
    done = [False] * len(chains)
    rnd = 0
    while not all(done):
        for i, ch in enumerate(chains):
            if not done[i] and rnd >= i * stagger:
                try:
                    next(ch)
                except StopIteration:
                    done[i] = True
        rnd += 1


def _to_time_major(val, slabs, nb):
    tt = val.shape[0] // nb
    for s, slab in enumerate(slabs):
        for b in range(nb):
            slab[pl.ds(b, tt, stride=nb), :] = val[b * tt:(b + 1) * tt, s * LANES:(s + 1) * LANES]
    return jnp.concatenate([slab[...] for slab in slabs], axis=1)


def _to_batch_major(val, slabs, nb):
    tt = val.shape[0] // nb
    for s, slab in enumerate(slabs):
        slab[...] = val[:, s * LANES:(s + 1) * LANES]
    return jnp.concatenate(
        [jnp.concatenate([slab[pl.ds(b, tt, stride=nb), :] for slab in slabs], axis=1) for b in range(nb)], axis=0)


def _adaln_kernel(c_ref, w_ref, b_ref, o_ref):
    s = _silu(c_ref[...])
    o_ref[...] = _dot(s.astype(BF16), w_ref[...].astype(BF16)) + b_ref[...]


def _adaln(cvec, w_mod, b_mod):
    n_col = 6 * D_MODEL
    tn = n_col // 4
    return pl.pallas_call(
        _adaln_kernel,
        grid=(DEPTH, n_col // tn),
        in_specs=[pl.BlockSpec((N_MOD_ROWS, D_MODEL), lambda l, n: (0, 0)),
                  pl.BlockSpec((None, D_MODEL, tn), lambda l, n: (l, 0, n)),
                  pl.BlockSpec((None, 1, tn), lambda l, n: (l, 0, n))],
        out_specs=pl.BlockSpec((None, N_MOD_ROWS, tn), lambda l, n: (l, 0, n)),
        out_shape=jax.ShapeDtypeStruct((DEPTH, N_MOD_ROWS, n_col), F32),
        compiler_params=_params(2),
        name="adaln",
    )(cvec, w_mod, b_mod.reshape(DEPTH, 1, n_col))


def _inproj_kernel(*refs, latent, n_alias):
    if latent:
        (x_ref, xn_ref, mod_ref, g1_ref, w_ref, seg_ref, qg_ref, kg_ref, cos_ref, sin_ref,
         qa_ref, krep_ref, vrep_ref, u_ref, qc_ref, kc_ref, vc_ref, hb_ref, slab0, slab1) = refs
    else:
        x_ref, xn_ref, mod_ref, g1_ref, w_ref, seg_ref, qg_ref, kg_ref = refs[:8]
        (qa_ref, krep_ref, vrep_ref, ka_ref, va_ref, u_ref, qc_ref, kc_ref, vc_ref,
         hb_ref, slab0, slab1) = refs[8 + n_alias:]
    nb, tt, _ = x_ref.shape
    rows = nb * tt
    mod = mod_ref[...]
    shift = mod[:, :, 0:D_MODEL]
    scale = mod[:, :, D_MODEL:2 * D_MODEL]
    seg = seg_ref[...]
    slot = pl.program_id(0) % 2

    def normed(ref):
        h = _rms(ref[...], g1_ref[...]) * (1 + scale) + shift
        return h.reshape(rows, D_MODEL).astype(BF16)

    def put(ref, val):
        val = val.reshape(nb, tt, ref.shape[-1]).astype(ref.dtype)
        if len(ref.shape) == 3:
            ref[...] = val
        else:
            ref[:, 0] = val
            ref[:, 1:] = jnp.zeros((nb, DEPTH - 1, tt, ref.shape[-1]), ref.dtype)

    @pl.when(pl.program_id(0) == 0)
    def _():
        hb_ref[0] = normed(x_ref)

    def project():
        hb = hb_ref[slot]
        u_ref[...] = _to_time_major(_dot(hb, w_ref[:, OFF_U:OFF_QC]), (slab0, slab1), nb)
        put(qc_ref, _dot(hb, w_ref[:, OFF_QC:OFF_KC]) * Q_SCALE)
        put(kc_ref, _dot(hb, w_ref[:, OFF_KC:OFF_VC]))
        put(vc_ref, _dot(hb, w_ref[:, OFF_VC:OFF_G]))
        yield
        qa = _seg_rms(_dot(hb, w_ref[:, OFF_QA:OFF_KA]), seg, qg_ref[...])
        ka = _seg_rms(_dot(hb, w_ref[:, OFF_KA:OFF_VA]), seg[0:KV_WIDTH_A, 0:KV_WIDTH_A], kg_ref[...])
        va = _dot(hb, w_ref[:, OFF_VA:OFF_U])
        if latent:
            cos = jnp.concatenate([cos_ref[...]] * nb, axis=0)
            sin = jnp.concatenate([sin_ref[...]] * nb, axis=0)
            qa = _rope(qa, jnp.concatenate([cos] * REP_A, axis=1), jnp.concatenate([sin] * REP_A, axis=1))
            ka = _rope(ka, cos, sin)
        else:
            put(ka_ref, ka)
            put(va_ref, va)
        put(qa_ref, qa * Q_SCALE)
        put(krep_ref, _rep_heads(ka))
        put(vrep_ref, _rep_heads(va))

    def prepare_next():
        yield
        hb_ref[1 - slot] = normed(xn_ref)

    _round_robin([project(), prepare_next()])


def _inproj(x, mod_rows, g1, w_in, seg, qg, kg, layer, *, rope_tabs=None, prev_cache=()):
    latent = rope_tabs is not None
    nb, seq, _ = x.shape
    tt = INPROJ_ROWS // nb
    slab = lambda w: pl.BlockSpec((nb, tt, w), lambda j: (0, j, 0))
    act = lambda w, dt: jax.ShapeDtypeStruct((nb, seq, w), dt)
    if layer == 0:
        cache_slab = lambda w: pl.BlockSpec((nb, DEPTH, tt, w), lambda j: (0, 0, j, 0))
    else:
        cache_slab = lambda w: pl.BlockSpec((nb, None, tt, w), lambda j: (0, layer, j, 0))
    cache = lambda w: jax.ShapeDtypeStruct((nb, DEPTH, seq, w), F32)
    n_steps = seq // tt
    next_slab = pl.BlockSpec((nb, tt, D_MODEL), lambda j: (0, jnp.minimum(j + 1, n_steps - 1), 0))
    first_slab = pl.BlockSpec((nb, tt, D_MODEL), lambda j: (0, 0, 0))
    in_specs = [first_slab, next_slab, _const_spec(mod_rows.shape), _layer_spec(g1, layer),
                _layer_cols_spec(w_in, layer, 0, OFF_G), _const_spec(seg.shape), _layer_spec(qg, layer),
                _layer_spec(kg, layer)]
    args = [x, x, mod_rows, g1, w_in, seg, qg, kg]
    out_shape = [act(WIDTH_A, BF16), act(REP_A * KV_WIDTH_A, BF16), act(REP_A * KV_WIDTH_A, BF16)]
    out_specs = [slab(WIDTH_A), slab(REP_A * KV_WIDTH_A), slab(REP_A * KV_WIDTH_A)]
    aliases = {}
    if latent:
        in_specs += [pl.BlockSpec((tt, 2 * HEAD_DIM), lambda j: (j, 0))] * 2
        args += list(rope_tabs)
        kv_c = [act(WIDTH_C, BF16)] * 2
        kv_c_specs = [slab(WIDTH_C)] * 2
    else:
        aliases = {len(args) + i: o for i, o in enumerate((3, 4, 7, 8)[:len(prev_cache)])}
        in_specs += [pl.BlockSpec(memory_space=pl.ANY)] * len(prev_cache)
        args += list(prev_cache)
        out_shape += [cache(KV_WIDTH_A)] * 2
        out_specs += [cache_slab(KV_WIDTH_A)] * 2
        kv_c = [cache(WIDTH_C)] * 2
        kv_c_specs = [cache_slab(WIDTH_C)] * 2
    out_shape += [jax.ShapeDtypeStruct((seq * nb, SSM_WIDTH), F32),
                  act(WIDTH_C, BF16)] + kv_c
    out_specs += [pl.BlockSpec((INPROJ_ROWS, SSM_WIDTH), lambda j: (j, 0)),
                  slab(WIDTH_C)] + kv_c_specs
    return pl.pallas_call(
        functools.partial(_inproj_kernel, latent=latent, n_alias=len(prev_cache)),
        grid=(seq // tt,),
        in_specs=in_specs,
        out_specs=out_specs,
        out_shape=out_shape,
        scratch_shapes=[pltpu.VMEM((2, INPROJ_ROWS, D_MODEL), BF16)] + [pltpu.VMEM((INPROJ_ROWS, LANES), F32)] * 2,
        input_output_aliases=aliases,
        compiler_params=_params(1),
        name="inproj_lat" if latent else "inproj_ctx",
    )(*args)


def _head_mask(shape, head):
    lane = lax.broadcasted_iota(jnp.int32, shape, 1)
    return (lane // HEAD_DIM) == head


def _head_chain(q, heads, keys, vals, acc, key, bias=None):
    m_rows = q.shape[0]
    masks = [_head_mask(q.shape, h) for h in heads]
    qs = jnp.concatenate([jnp.where(hm, q, jnp.zeros_like(q)) for hm in masks], axis=0)
    s = _dot_t(qs, keys())
    if bias is not None:
        s = s + bias()
    yield
    p = jnp.exp2(s - jnp.max(s, axis=-1, keepdims=True))
    l = jnp.sum(p, axis=-1, keepdims=True)
    p = p.astype(BF16)
    yield
    o = _dot(p, vals()) / l
    for i, hm in enumerate(masks):
        acc[key] = jnp.where(hm, o[i * m_rows:(i + 1) * m_rows], acc[key])


def _attn_a_kernel(q_ref, kn_ref, vn_ref, ck_ref, cv_ref, o_ref, k_ref, v_ref):
    @pl.when(pl.program_id(1) == 0)
    def _():
        k_ref[0:PAST_LEN, :] = _rep_heads(ck_ref[...]).astype(BF16)
        v_ref[0:PAST_LEN, :] = _rep_heads(cv_ref[...]).astype(BF16)
        k_ref[PAST_LEN:, :] = kn_ref[...]
        v_ref[PAST_LEN:, :] = vn_ref[...]

    gw = REP_A * HEAD_DIM
    n_sub = q_ref.shape[0] // Q_CHAIN_ROWS
    acc = {(t, g): jnp.zeros((Q_CHAIN_ROWS, gw), F32) for t in range(n_sub) for g in range(N_KV_A)}
    chains = []
    for t in range(n_sub):
        rows = slice(t * Q_CHAIN_ROWS, (t + 1) * Q_CHAIN_ROWS)
        for g in range(N_KV_A):
            sl = slice(g * gw, (g + 1) * gw)
            for j in range(REP_A):
                chains.append(_head_chain(q_ref[rows, sl], [j], lambda sl=sl: k_ref[:, sl],
                                          lambda sl=sl: v_ref[:, sl], acc, (t, g)))
    _round_robin(chains, stagger=1)
    for (t, g), val in acc.items():
        o_ref[t * Q_CHAIN_ROWS:(t + 1) * Q_CHAIN_ROWS, g * gw:(g + 1) * gw] = val.astype(BF16)


def _attn_a_lat(qa, krep, vrep, cache_k, cache_v, layer):
    seq_blk = pl.BlockSpec((None, DEC_SEQ, WIDTH_A), lambda b, t: (b, 0, 0))
    cache_blk = pl.BlockSpec((None, None, PAST_LEN, KV_WIDTH_A), lambda b, t: (b, layer, 0, 0))
    q_blk = pl.BlockSpec((None, Q_TILE_A, WIDTH_A), lambda b, t: (b, t, 0))
    return pl.pallas_call(
        _attn_a_kernel,
        grid=(DEC_BATCH, DEC_SEQ // Q_TILE_A),
        in_specs=[q_blk, seq_blk, seq_blk, cache_blk, cache_blk],
        out_specs=q_blk,
        out_shape=jax.ShapeDtypeStruct((DEC_BATCH, DEC_SEQ, WIDTH_A), BF16),
        scratch_shapes=[pltpu.VMEM((PAST_LEN + DEC_SEQ, WIDTH_A), BF16)] * 2,
        compiler_params=_params(2),
        name="attn_a_lat",
    )(qa, krep, vrep, cache_k, cache_v)


def _attn_ctx_kernel(qa_ref, ka_ref, va_ref, qc_ref, kc_ref, vc_ref, oa_ref, oc_ref):
    gw = REP_A * HEAD_DIM
    acc = {}
    chains = []
    for b in range(CTX_STEP_SEQS):
        for g in range(N_KV_A):
            sl = slice(g * gw, (g + 1) * gw)
            acc["a", b, g] = jnp.zeros((SEQ, gw), F32)
            for j in range(REP_A):
                chains.append(_head_chain(qa_ref[b, :, sl], [j], lambda b=b, sl=sl: ka_ref[b, :, sl],
                                          lambda b=b, sl=sl: va_ref[b, :, sl], acc, ("a", b, g)))
        acc["c", b] = jnp.zeros((SEQ, WIDTH_C), F32)
        for h in range(0, N_HEADS_C, 2):
            chains.append(_head_chain(qc_ref[b], [h, h + 1], lambda b=b: kc_ref[b].astype(BF16),
                                      lambda b=b: vc_ref[b].astype(BF16), acc, ("c", b)))
    _round_robin(chains, stagger=1)
    for b in range(CTX_STEP_SEQS):
        for g in range(N_KV_A):
            oa_ref[b, :, g * gw:(g + 1) * gw] = acc["a", b, g].astype(BF16)
        oc_ref[b] = acc["c", b].astype(BF16)


def _attn_ctx(qa, krep, vrep, qc, kc, vc, layer):
    blk = lambda w: pl.BlockSpec((CTX_STEP_SEQS, SEQ, w), lambda i: (i, 0, 0))
    kv_blk = pl.BlockSpec((CTX_STEP_SEQS, None, SEQ, WIDTH_C), lambda i: (i, layer, 0, 0))
    return pl.pallas_call(
        _attn_ctx_kernel,
        grid=(BATCH // CTX_STEP_SEQS,),
        in_specs=[blk(WIDTH_A), blk(WIDTH_A), blk(WIDTH_A), blk(WIDTH_C), kv_blk, kv_blk],
        out_specs=[blk(WIDTH_A), blk(WIDTH_C)],
        out_shape=[jax.ShapeDtypeStruct((BATCH, SEQ, WIDTH_A), BF16),
                   jax.ShapeDtypeStruct((BATCH, SEQ, WIDTH_C), BF16)],
        compiler_params=_params(1),
        name="attn_ctx",
    )(qa, krep, vrep, qc, kc, vc)


def _na_kernel(q_ref, k_ref, v_ref, ck_ref, cv_ref, brow_ref, o_ref, b2_ref):
    nq = NA_Q_ROWS * GRID_W
    nk = NA_K_ROWS * GRID_W
    n_pair = 2 * NA_WIN_R

    @pl.when((pl.program_id(0) == 0) & (pl.program_id(1) == 0))
    def _():
        shp = (GRID_W, 2 * GRID_W)
        c = lax.broadcasted_iota(jnp.int32, shp, 0)
        kc = lax.broadcasted_iota(jnp.int32, shp, 1) & (GRID_W - 1)
        c0 = jnp.clip(c - NA_WIN_C // 2, 0, GRID_W - NA_WIN_C)
        col_bias = jnp.where(kc < c0, NEG, jnp.where(kc >= c0 + NA_WIN_C, NEG, 0.0))
        for h in range(N_HEADS_C):
            for e in range(n_pair):
                row = jnp.broadcast_to(brow_ref[h, e:e + 1, :], shp)
                toep = pltpu.roll(row, 2 * GRID_W - (NA_WIN_C - 1), 1, stride=1, stride_axis=0)
                b2_ref[h, e] = jnp.where(col_bias < 0.0, NEG, toep * LOG2_E)

    no_bias = jnp.zeros((nq, PAST_LEN), F32)
    ck = ck_ref[...].astype(BF16)
    cv = cv_ref[...].astype(BF16)
    acc = [jnp.zeros((nq, WIDTH_C), F32) for _ in range(NA_STEP_BLOCKS)]

    def chain(blk, heads):
        qrow0 = NA_Q_ROWS * (NA_STEP_BLOCKS * pl.program_id(1) + blk)
        row0 = jnp.clip(qrow0 - NA_WIN_R // 2, 0, NA_R0_MAX)
        start = pl.multiple_of(row0 * GRID_W, GRID_W)
        qrow = qrow0 + lax.broadcasted_iota(jnp.int32, (nq, nk), 0) // GRID_W
        krow = row0 + lax.broadcasted_iota(jnp.int32, (nq, nk), 1) // GRID_W
        win0 = jnp.clip(qrow - NA_WIN_R // 2, 0, GRID_ROWS - NA_WIN_R)
        row_bias = jnp.where(krow < win0, NEG, jnp.where(krow >= win0 + NA_WIN_R, NEG, 0.0))

        def head_bias(h):
            rows = []
            for i in range(NA_Q_ROWS):
                tiles = []
                for m in range(NA_K_ROWS // 2):
                    e = jnp.clip(row0 + 2 * m - (qrow0 + i) + NA_WIN_R, 0, n_pair - 1)
                    tiles.append(b2_ref[h, e])
                rows.append(jnp.concatenate(tiles, axis=1))
            return jnp.concatenate([jnp.concatenate(rows, axis=0) + row_bias, no_bias], axis=1)

        return _head_chain(q_ref[blk * nq:(blk + 1) * nq, :], heads,
                           lambda: jnp.concatenate([k_ref[pl.ds(start, nk), :], ck], axis=0),
                           lambda: jnp.concatenate([v_ref[pl.ds(start, nk), :], cv], axis=0), acc, blk,
                           bias=lambda: jnp.concatenate([head_bias(h) for h in heads], axis=0))

    pairs = [[h, h + 1] for h in range(0, N_HEADS_C, 2)]
    _round_robin([chain(blk, heads) for blk in range(NA_STEP_BLOCKS) for heads in pairs], stagger=1)
    for blk in range(NA_STEP_BLOCKS):
        o_ref[blk * nq:(blk + 1) * nq, :] = acc[blk].astype(BF16)


def _na_bias_rows(tbl):
    pad = jnp.pad(tbl, ((0, 0), (1, 1), (0, GRID_W - tbl.shape[-1])))
    return jnp.concatenate([pad[:, :-1], pad[:, 1:]], axis=-1)


def _na_lat(qc, kc, vc, cache_k, cache_v, brow, layer):
    step_rows = NA_STEP_BLOCKS * NA_Q_ROWS * GRID_W
    seq_blk = pl.BlockSpec((None, DEC_SEQ, WIDTH_C), lambda b, j: (b, 0, 0))
    cache_blk = pl.BlockSpec((None, None, PAST_LEN, WIDTH_C), lambda b, j: (b, layer, 0, 0))
    q_blk = pl.BlockSpec((None, step_rows, WIDTH_C), lambda b, j: (b, j, 0))
    return pl.pallas_call(
        _na_kernel,
        grid=(DEC_BATCH, DEC_SEQ // step_rows),
        in_specs=[q_blk, seq_blk, seq_blk, cache_blk, cache_blk, _const_spec(brow.shape)],
        out_specs=q_blk,
        out_shape=jax.ShapeDtypeStruct((DEC_BATCH, DEC_SEQ, WIDTH_C), BF16),
        scratch_shapes=[pltpu.VMEM((N_HEADS_C, 2 * NA_WIN_R, GRID_W, 2 * GRID_W), F32)],
        compiler_params=_params(2),
        name="na_lat",
    )(qc, kc, vc, cache_k, cache_v, brow)


def _scan_kernel(uf_ref, ub_ref, bmat_ref, lam_ref, cmat_ref, h0_ref, yf_ref, yb_ref, hfin_ref,
                 *scratch, nb, lane_w):
    n_blk = SCAN_ROWS // SCAN_BLOCK
    hf_blk, hb_blk, st_ref = scratch[:n_blk], scratch[n_blk:2 * n_blk], scratch[2 * n_blk]
    steps = SCAN_BLOCK // nb
    n_lb = SSM_LANES // lane_w

    @pl.when(pl.program_id(0) == 0)
    def _():
        st_ref[...] = h0_ref[...]

    lanes = [(slice(lb * lane_w, (lb + 1) * lane_w), slice(SSM_LANES + lb * lane_w, SSM_LANES + (lb + 1) * lane_w))
             for lb in range(n_lb)]
    state = [[st_ref[d, c, :, re] for d in range(2) for c in range(2)] for re, _ in lanes]

    def block(i):
        f0 = i * SCAN_BLOCK
        b0 = (n_blk - 1 - i) * SCAN_BLOCK
        hf_ref, hb_ref = hf_blk[i], hb_blk[i]
        uf = uf_ref[f0:f0 + SCAN_BLOCK, :].astype(BF16)
        ub = ub_ref[b0:b0 + SCAN_BLOCK, :].astype(BF16)
        for cols in lanes:
            for sl in cols:
                hf_ref[:, sl] = _dot(uf, bmat_ref[0, :, sl])
                hb_ref[:, sl] = _dot(ub, bmat_ref[1, :, sl])
            yield
        for lb, (re, im) in enumerate(lanes):
            if lb:
                yield
            lam = [[jnp.broadcast_to(lam_ref[d, c, :, re], (nb, lane_w)) for c in range(2)] for d in range(2)]
            fr, fi, br, bi = state[lb]
            for k in range(steps):
                rf = slice(k * nb, (k + 1) * nb)
                rb = slice((steps - 1 - k) * nb, (steps - k) * nb)
                fr, fi = (lam[0][0] * fr - lam[0][1] * fi + hf_ref[rf, re],
                          lam[0][0] * fi + lam[0][1] * fr + hf_ref[rf, im])
                br, bi = (lam[1][0] * br - lam[1][1] * bi + hb_ref[rb, re],
                          lam[1][0] * bi + lam[1][1] * br + hb_ref[rb, im])
                hf_ref[rf, re] = fr
                hf_ref[rf, im] = fi
                hb_ref[rb, re] = br
                hb_ref[rb, im] = bi
            state[lb] = [fr, fi, br, bi]
        yf = yb = 0.0
        for cols in lanes:
            yield
            for sl in cols:
                yf = yf + _dot(hf_ref[:, sl].astype(BF16), cmat_ref[0, sl, :])
                yb = yb + _dot(hb_ref[:, sl].astype(BF16), cmat_ref[1, sl, :])
        yf_ref[f0:f0 + SCAN_BLOCK, :] = yf
        yb_ref[b0:b0 + SCAN_BLOCK, :] = yb

    _round_robin([block(i) for i in range(n_blk)], stagger=n_lb)
    for (re, _), vals in zip(lanes, state):
        for d in range(2):
            for c in range(2):
                st_ref[d, c, :, re] = vals[2 * d + c]
    hfin_ref[...] = st_ref[...]


def _scan_pair_kernel(uf_ref, ub_ref, bmat_ref, lam_ref, cmat_ref, h0_ref, yf_ref, yb_ref, *scratch, lane_w):
    half = DEC_BATCH
    tile = 2 * half
    n_blk = SCAN_ROWS // SCAN_BLOCK
    hf_blk, hb_blk, st_ref = scratch[:n_blk], scratch[n_blk:2 * n_blk], scratch[2 * n_blk]
    tiles = SCAN_BLOCK // tile
    n_lb = SSM_LANES // lane_w

    @pl.when(pl.program_id(0) == 0)
    def _():
        st_ref[...] = h0_ref[...]

    top = lax.broadcasted_iota(jnp.int32, (tile, lane_w), 0) < half
    swap = lambda a: pltpu.roll(a, half, 0)
    lanes = [(slice(lb * lane_w, (lb + 1) * lane_w), slice(SSM_LANES + lb * lane_w, SSM_LANES + (lb + 1) * lane_w))
             for lb in range(n_lb)]
    state = [[st_ref[0, :, re], st_ref[1, :, re]] for re, _ in lanes]

    def block(i):
        f0 = i * SCAN_BLOCK
        b0 = (n_blk - 1 - i) * SCAN_BLOCK
        hf_ref, hb_ref = hf_blk[i], hb_blk[i]
        uf = uf_ref[f0:f0 + SCAN_BLOCK, :].astype(BF16)
        ub = ub_ref[b0:b0 + SCAN_BLOCK, :].astype(BF16)
        for cols in lanes:
            for sl in cols:
                hf_ref[:, sl] = _dot(uf, bmat_ref[0, :, sl])
                hb_ref[:, sl] = _dot(ub, bmat_ref[1, :, sl])
            yield
        for lb, (re, im) in enumerate(lanes):
            if lb:
                yield
            la_r = lam_ref[0, :, re]
            la_i = lam_ref[1, :, re]
            lb_r = swap(la_r)
            lb_i = swap(la_i)
            sr, si = state[lb]
            for m in range(tiles):
                rf = slice(m * tile, (m + 1) * tile)
                rb = slice((tiles - 1 - m) * tile, (tiles - m) * tile)
                fr, fi = hf_ref[rf, re], hf_ref[rf, im]
                br, bi = hb_ref[rb, re], hb_ref[rb, im]
                vr = la_r * sr - la_i * si + jnp.where(top, fr, br)
                vi = la_r * si + la_i * sr + jnp.where(top, fi, bi)
                tr = swap(vr)
                ti = swap(vi)
                wr = lb_r * tr - lb_i * ti + jnp.where(top, br, fr)
                wi = lb_r * ti + lb_i * tr + jnp.where(top, bi, fi)
                hf_ref[rf, re] = jnp.where(top, vr, wr)
                hf_ref[rf, im] = jnp.where(top, vi, wi)
                hb_ref[rb, re] = jnp.where(top, wr, vr)
                hb_ref[rb, im] = jnp.where(top, wi, vi)
                sr, si = swap(wr), swap(wi)
            state[lb] = [sr, si]
        yf = yb = 0.0
        for cols in lanes:
            yield
            for sl in cols:
                yf = yf + _dot(hf_ref[:, sl].astype(BF16), cmat_ref[0, sl, :])
                yb = yb + _dot(hb_ref[:, sl].astype(BF16), cmat_ref[1, sl, :])
        yf_ref[f0:f0 + SCAN_BLOCK, :] = yf
        yb_ref[b0:b0 + SCAN_BLOCK, :] = yb

    _round_robin([block(i) for i in range(n_blk)], stagger=n_lb)
    for (re, _), (sr, si) in zip(lanes, state):
        st_ref[0, :, re] = sr
        st_ref[1, :, re] = si


def _scan_specs(n_rows):
    n = n_rows // SCAN_ROWS
    fwd = pl.BlockSpec((SCAN_ROWS, SSM_WIDTH), lambda j: (j, 0))
    bwd = pl.BlockSpec((SCAN_ROWS, SSM_WIDTH), lambda j: (n - 1 - j, 0))
    y_shape = jax.ShapeDtypeStruct((n_rows, SSM_WIDTH), F32)
    buf = [pltpu.VMEM((SCAN_BLOCK, 2 * SSM_LANES), F32)] * (2 * (SCAN_ROWS // SCAN_BLOCK))
    return n, fwd, bwd, y_shape, buf


def _scan_ctx(u_rows, bmat, lam, cmat, h0):
    n, fwd, bwd, y_shape, buf = _scan_specs(u_rows.shape[0])
    st_shape = (2, 2, BATCH, SSM_LANES)
    return pl.pallas_call(
        functools.partial(_scan_kernel, nb=BATCH, lane_w=256),
        grid=(n,),
        in_specs=[fwd, bwd, _const_spec(bmat.shape), _const_spec(lam.shape), _const_spec(cmat.shape),
                  _const_spec(st_shape)],
        out_specs=[fwd, bwd, pl.BlockSpec(st_shape, lambda j: (0, 0, 0, 0))],
        out_shape=[y_shape, y_shape, jax.ShapeDtypeStruct(st_shape, F32)],
        scratch_shapes=buf + [pltpu.VMEM(st_shape, F32)],
        compiler_params=_params(1),
        name="scan_ctx",
    )(u_rows, u_rows, bmat, lam, cmat, h0)


def _scan_lat(u_rows, bmat, lam_pair, cmat, h0_pair):
    n, fwd, bwd, y_shape, buf = _scan_specs(u_rows.shape[0])
    st_shape = (2, 2 * DEC_BATCH, SSM_LANES)
    return pl.pallas_call(
        functools.partial(_scan_pair_kernel, lane_w=512),
        grid=(n,),
        in_specs=[fwd, bwd, _const_spec(bmat.shape), _const_spec(st_shape), _const_spec(cmat.shape),
                  _const_spec(st_shape)],
        out_specs=[fwd, bwd],
        out_shape=[y_shape, y_shape],
        scratch_shapes=buf + [pltpu.VMEM(st_shape, F32)],
        compiler_params=_params(1),
        name="scan_lat",
    )(u_rows, u_rows, bmat, lam_pair, cmat, h0_pair)


def _ssm_discretise(lam_re, lam_im, log_step, b_re, b_im, c_re, c_im):
    step = jnp.exp(log_step.astype(F32))[..., None]
    lr, li = lam_re.astype(F32), lam_im.astype(F32)
    mag = jnp.exp(lr * step)
    bar_r = mag * jnp.cos(li * step)
    bar_i = mag * jnp.sin(li * step)
    den = lr * lr + li * li
    coef_r = (((bar_r - 1) * lr + bar_i * li) / den)[..., None]
    coef_i = ((bar_i * lr - (bar_r - 1) * li) / den)[..., None]
    br, bi = b_re.astype(F32), b_im.astype(F32)
    bbar_r = coef_r * br - coef_i * bi
    bbar_i = coef_r * bi + coef_i * br
    eye = jnp.eye(SSM_GROUPS, dtype=F32)
    blk_b = lambda a: jnp.einsum('dgpc,gh->dgchp', a, eye).reshape(2, SSM_WIDTH, SSM_LANES)
    bmat = jnp.concatenate([blk_b(bbar_r), blk_b(bbar_i)], axis=-1)
    blk_c = lambda a: jnp.einsum('dgcp,gh->dgphc', a, eye).reshape(2, SSM_LANES, SSM_WIDTH)
    cmat = jnp.concatenate([blk_c(c_re.astype(F32)), -blk_c(c_im.astype(F32))], axis=1)
    lam_flat = jnp.stack([bar_r, bar_i], axis=1).reshape(2, 2, 1, SSM_LANES)
    return bmat.astype(BF16), lam_flat, cmat.astype(BF16)


def _merge_kernel(x_ref, oa_ref, yf_ref, yb_ref, u_ref, oc_ref, mod_ref, g1_ref, wg_ref, d_ref, wglu_ref,
                  wa_ref, wb_ref, wc_ref, wo_ref, g2_ref, wgu_ref, wd_ref, fg_ref, o_ref, *slabs, final):
    nb, tt, _ = x_ref.shape
    tc = tt // ROW_CHAINS
    rows = nb * tc
    mod = mod_ref[...]
    shift1 = mod[:, :, 0:D_MODEL]
    scale1 = mod[:, :, D_MODEL:2 * D_MODEL]
    gate1 = mod[:, :, 2 * D_MODEL:3 * D_MODEL]
    shift2 = mod[:, :, 3 * D_MODEL:4 * D_MODEL]
    scale2 = mod[:, :, 4 * D_MODEL:5 * D_MODEL]
    gate2 = mod[:, :, 5 * D_MODEL:6 * D_MODEL]
    def chain(c):
        ts = slice(c * tc, (c + 1) * tc)
        rs = slice(c * rows, (c + 1) * rows)
        flat = lambda ref, lo=0, hi=None: ref[:, ts, lo:hi].reshape(rows, -1)
        y = _gelu_tanh(yf_ref[rs, :] + yb_ref[rs, :] + d_ref[...] * u_ref[rs, :])
        ob = y * jax.nn.sigmoid(_dot(y.astype(BF16), wglu_ref[...]))
        ob = _to_batch_major(ob, slabs[2 * c:2 * c + 2], nb)
        x = x_ref[:, ts, :]
        h1 = (_rms(x, g1_ref[...]) * (1 + scale1) + shift1).reshape(rows, D_MODEL).astype(BF16)
        yield
        gate = lambda i: jax.nn.sigmoid(_dot(h1, wg_ref[0, :, i * D_MODEL:(i + 1) * D_MODEL]))
        merged = (gate(0) * _dot(flat(oa_ref), wa_ref[...])
                  + gate(1) * _dot(ob.astype(BF16), wb_ref[...])
                  + gate(2) * _dot(flat(oc_ref), wc_ref[...]))
        yield
        x1 = x + gate1 * _dot(merged.astype(BF16), wo_ref[...]).reshape(nb, tc, D_MODEL)
        h2 = _rms(x1, g2_ref[...]) * (1 + scale2) + shift2
        yield
        gu = _dot(h2.reshape(rows, D_MODEL).astype(BF16), wgu_ref[...])
        act = _silu(gu[:, :D_FF]) * gu[:, D_FF:]
        yield
        x2 = x1 + gate2 * _dot(act.astype(BF16), wd_ref[...]).reshape(nb, tc, D_MODEL)
        if final:
            x2 = _rms(x2, fg_ref[...])
        o_ref[:, ts, :] = x2

    _round_robin([chain(c) for c in range(ROW_CHAINS)])


def _merge(x, oa, yf, yb, u_rows, oc, mod_rows, g1, w_in, weights, final_g, layer, *, name):
    final = layer == DEPTH - 1
    nb, seq, _ = x.shape
    tt = ROW_TILE // nb
    slab = lambda w: pl.BlockSpec((nb, tt, w), lambda j: (0, j, 0))
    tmaj = pl.BlockSpec((ROW_TILE, SSM_WIDTH), lambda j: (j, 0))
    return pl.pallas_call(
        functools.partial(_merge_kernel, final=final),
        grid=(seq // tt,),
        in_specs=[slab(D_MODEL), slab(WIDTH_A), tmaj, tmaj, tmaj, slab(WIDTH_C), _const_spec(mod_rows.shape),
                  _layer_spec(g1, layer), _layer_cols_spec(w_in, layer, OFF_G, N_BRANCH * D_MODEL)]
                 + [_layer_spec(w, layer) for w in weights]
                 + [_const_spec(final_g.shape)],
        out_specs=slab(D_MODEL),
        out_shape=jax.ShapeDtypeStruct((nb, seq, D_MODEL), F32),
        scratch_shapes=[pltpu.VMEM((ROW_TILE // ROW_CHAINS, LANES), F32)] * (2 * ROW_CHAINS),
        compiler_params=_params(1),
        name=name + ("_final" if final else ""),
    )(x, oa, yf, yb, u_rows, oc, mod_rows, g1, w_in, *weights, final_g)


def _rope_tables():
    t = jnp.arange(DEC_SEQ)
    row = (t // GRID_W).astype(F32)
    col = (t % GRID_W).astype(F32)
    inv = 1.0 / (ROPE_THETA ** (jnp.arange(ROT_FREQS, dtype=F32) / ROT_FREQS))
    ar = row[:, None] * inv[None]
    ac = col[:, None] * inv[None]
    cos = jnp.concatenate([jnp.cos(ar), jnp.cos(ar), jnp.cos(ac), jnp.cos(ac)], axis=-1)
    sin = jnp.concatenate([-jnp.sin(ar), jnp.sin(ar), -jnp.sin(ac), jnp.sin(ac)], axis=-1)
    return jnp.tile(cos, (1, 2)), jnp.tile(sin, (1, 2))


def kernel(x_prompt, x_sample, c, cache_ga_k, cache_ga_v, cache_na_k, cache_na_v, state_ssm, c_ctx, w_mod, b_mod, norm1_g, w_in, qn_g, kn_g, ssm_lam_re, ssm_lam_im, ssm_log_step, ssm_b_re, ssm_b_im, ssm_c_re, ssm_c_im, ssm_d, ssm_w_glu, na_bias, w_br_a, w_br_b, w_br_c, w_out, norm2_g, w_ffn_gu, w_ffn_d, final_g):
    cvec = jnp.concatenate([c_ctx[None, :], c, jnp.zeros((N_MOD_ROWS - 1 - DEC_BATCH, D_MODEL), F32)], axis=0)
    mod = _adaln(cvec, w_mod, b_mod).reshape(DEPTH, N_MOD_ROWS, 1, 6 * D_MODEL)

    seg = jnp.kron(jnp.eye(MXU_DIM // HEAD_DIM, dtype=F32),
                   jnp.full((HEAD_DIM, HEAD_DIM), 1.0 / HEAD_DIM, F32)).astype(BF16)
    rope_tabs = _rope_tables()
    fg = final_g.reshape(1, D_MODEL)
    ck_a = cache_ga_k.reshape(DEC_BATCH, DEPTH, PAST_LEN, KV_WIDTH_A)
    cv_a = cache_ga_v.reshape(DEC_BATCH, DEPTH, PAST_LEN, KV_WIDTH_A)
    ck_c = cache_na_k.reshape(DEC_BATCH, DEPTH, PAST_LEN, WIDTH_C)
    cv_c = cache_na_v.reshape(DEC_BATCH, DEPTH, PAST_LEN, WIDTH_C)
    zero_state = jnp.zeros((2, 2, BATCH, SSM_LANES), F32)

    row = lambda p: p.reshape(DEPTH, 1, p.shape[-1])
    g1 = row(norm1_g)
    w_in_b = w_in.astype(BF16)
    qg = row(jnp.tile(qn_g, (1, N_HEADS_A)))
    kg = row(jnp.tile(kn_g, (1, N_KV_A)))
    merge_w = [row(ssm_d), ssm_w_glu.astype(BF16), w_br_a.astype(BF16), w_br_b.astype(BF16), w_br_c.astype(BF16),
               w_out.astype(BF16), row(norm2_g), w_ffn_gu.astype(BF16), w_ffn_d.astype(BF16)]

    xp, xs = x_prompt, x_sample
    cache = ()
    ssm_st = []
    for l in range(DEPTH):
        bmat, lam, cmat = _ssm_discretise(ssm_lam_re[l], ssm_lam_im[l], ssm_log_step[l], ssm_b_re[l], ssm_b_im[l],
                                          ssm_c_re[l], ssm_c_im[l])
        mod_ctx = mod[l, 0:1]
        mod_lat = mod[l, 1:1 + DEC_BATCH]

        qa, krep, vrep, ka, va, u_rows, qc, kc, vc = _inproj(xp, mod_ctx, g1, w_in_b, seg, qg, kg, l,
                                                             prev_cache=cache)
        cache = (ka, va, kc, vc)
        oa, oc = _attn_ctx(qa, krep, vrep, qc, kc, vc, l)
        yf, yb, hfin = _scan_ctx(u_rows, bmat, lam, cmat, zero_state)
        xp = _merge(xp, oa, yf, yb, u_rows, oc, mod_ctx, g1, w_in_b, merge_w, fg, l, name="merge_ctx")
        ssm_st.append(jnp.transpose(hfin, (2, 0, 1, 3)))

        qa, krep, vrep, u_rows, qc, kc, vc = _inproj(xs, mod_lat, g1, w_in_b, seg, qg, kg, l, rope_tabs=rope_tabs)
        oa = _attn_a_lat(qa, krep, vrep, ck_a, cv_a, l)
        oc = _na_lat(qc, kc, vc, ck_c, cv_c, _na_bias_rows(na_bias[l]), l)
        h0 = jnp.transpose(state_ssm[:, l].reshape(DEC_BATCH, 2, 2, SSM_LANES), (2, 1, 0, 3))
        h0 = h0.reshape(2, 2 * DEC_BATCH, SSM_LANES)
        lam_pair = jnp.broadcast_to(jnp.transpose(lam, (1, 0, 2, 3)), (2, 2, DEC_BATCH, SSM_LANES))
        lam_pair = lam_pair.reshape(2, 2 * DEC_BATCH, SSM_LANES)
        yf, yb = _scan_lat(u_rows, bmat, lam_pair, cmat, h0)
        xs = _merge(xs, oa, yf, yb, u_rows, oc, mod_lat, g1, w_in_b, merge_w, fg, l, name="merge_lat")

    ga_k, ga_v, na_k, na_v = cache
    new_ssm = jnp.stack(ssm_st, axis=1).reshape(BATCH, DEPTH, 2, 2, SSM_GROUPS, SSM_STATE)
    return (xp, xs,
            ga_k.reshape(BATCH, DEPTH, SEQ, N_KV_A, HEAD_DIM), ga_v.reshape(BATCH, DEPTH, SEQ, N_KV_A, HEAD_DIM),
            na_k.reshape(BATCH, DEPTH, SEQ, N_HEADS_C, HEAD_DIM), na_v.reshape(BATCH, DEPTH, SEQ, N_HEADS_C, HEAD_DIM),
            new_ssm)
```

```python
import functools
import math

import jax
import jax.numpy as jnp
from jax import lax
from jax.experimental import pallas as pl
from jax.experimental.pallas import tpu as pltpu

D_MODEL = 1024
BATCH = 16
SEQ = 256
DEPTH = 2
DEC_BATCH = 4
DEC_SEQ = 2048
PAST_LEN = 256
GRID_W = 64
GRID_ROWS = DEC_SEQ // GRID_W
HEAD_DIM = 64
N_HEADS_A = 8
N_KV_A = 2
REP_A = N_HEADS_A // N_KV_A
N_HEADS_C = 4
SSM_WIDTH = 256
SSM_GROUP = 16
SSM_GROUPS = SSM_WIDTH // SSM_GROUP
SSM_STATE = 64
SSM_LANES = SSM_GROUPS * SSM_STATE
NA_WIN_R = 8
NA_WIN_C = 16
D_FF = -(-8 * D_MODEL // (3 * 256)) * 256
ROPE_THETA = 10000.0
ROT_HALF = HEAD_DIM // 2
ROT_FREQS = ROT_HALF // 2
WIDTH_A = N_HEADS_A * HEAD_DIM
KV_WIDTH_A = N_KV_A * HEAD_DIM
WIDTH_C = N_HEADS_C * HEAD_DIM
N_BRANCH = 3
IN_WIDTH = WIDTH_A + 2 * KV_WIDTH_A + SSM_WIDTH + 3 * WIDTH_C + N_BRANCH * D_MODEL
EPS = 1e-6

OFF_QA = 0
OFF_KA = OFF_QA + WIDTH_A
OFF_VA = OFF_KA + KV_WIDTH_A
OFF_U = OFF_VA + KV_WIDTH_A
OFF_QC = OFF_U + SSM_WIDTH
OFF_KC = OFF_QC + WIDTH_C
OFF_VC = OFF_KC + WIDTH_C
OFF_G = OFF_VC + WIDTH_C

N_MOD_ROWS = 8
ROW_TILE = 512
INPROJ_ROWS = 512
Q_TILE_A = 512
Q_CHAIN_ROWS = 256
NA_Q_ROWS = 2
NA_K_ROWS = 10
NA_R0_MAX = GRID_ROWS - NA_K_ROWS
NA_STEP_BLOCKS = 4
CTX_STEP_SEQS = 4
ROW_CHAINS = 2
SCAN_ROWS = 2048
SCAN_BLOCK = 256
SCAN_RING = 3
LOG2_E = math.log2(math.e)
Q_SCALE = HEAD_DIM ** -0.5 * LOG2_E
LANES = 128
MXU_DIM = 256
NEG = -1e30
VMEM_LIMIT_V7X = 56 * 1024 * 1024

F32 = jnp.float32
BF16 = jnp.bfloat16


def _dot(a, b):
    return jnp.dot(a, b, preferred_element_type=F32)


def _dot_t(a, b):
    return lax.dot_general(a, b, (((1,), (1,)), ((), ())), preferred_element_type=F32)


def _params(n_axes):
    return pltpu.CompilerParams(dimension_semantics=("arbitrary",) * n_axes,
                                vmem_limit_bytes=VMEM_LIMIT_V7X)


def _const_spec(shape):
    nd = len(shape)
    return pl.BlockSpec(shape, lambda *_: (0,) * nd, pipeline_mode=pl.Buffered(1))


def _layer_spec(stacked, layer):
    shape = stacked.shape[1:]
    return pl.BlockSpec((None,) + shape, lambda *_: (layer,) + (0,) * len(shape), pipeline_mode=pl.Buffered(1))


def _layer_cols_spec(stacked, layer, start, width):
    k = stacked.shape[1]
    if start == 0:
        return pl.BlockSpec((None, k, width), lambda *_: (layer, 0, 0), pipeline_mode=pl.Buffered(1))
    return pl.BlockSpec((pl.Element(1), pl.Element(k), pl.Element(width)), lambda *_: (layer, 0, start),
                        pipeline_mode=pl.Buffered(1))


def _rms(x, g):
    return x * lax.rsqrt(jnp.mean(x * x, axis=-1, keepdims=True) + EPS) * g


def _silu(x):
    return x * jax.nn.sigmoid(x)


def _gelu_tanh(x):
    c = math.sqrt(2.0 / math.pi)
    return x * (0.5 * (1.0 + jnp.tanh(c * (x + 0.044715 * (x * x * x)))))


def _seg_rms(x, seg, g):
    w = seg.shape[0]
    x2 = x * x
    hi = x2.astype(BF16)
    lo = (x2 - hi.astype(F32)).astype(BF16)
    ms = jnp.concatenate([_dot(hi[:, c:c + w], seg) + _dot(lo[:, c:c + w], seg)
                          for c in range(0, x.shape[1], w)], axis=1)
    return x * lax.rsqrt(ms + EPS) * g


def _rope(x, cos, sin_signed):
    w = x.shape[-1]
    lane = lax.broadcasted_iota(jnp.int32, x.shape, 1)
    first = (lane & ROT_FREQS) == 0
    partner = jnp.where(first, pltpu.roll(x, w - ROT_FREQS, 1), pltpu.roll(x, ROT_FREQS, 1))
    return x * cos + partner * sin_signed


def _rep_heads(kv):
    lane = lax.broadcasted_iota(jnp.int32, kv.shape, 1)
    swapped = pltpu.roll(kv, HEAD_DIM, 1)
    lo = lane < HEAD_DIM
    h0 = jnp.where(lo, kv, swapped)
    h1 = jnp.where(lo, swapped, kv)
    return jnp.concatenate([h0, h0, h1, h1], axis=1)


def _round_robin(chains, stagger=0):
    done = [False] * len(chains)
    rnd = 0
    while not all(done):
        for i, ch in enumerate(chains):
            if not done[i] and rnd >= i * stagger:
                try:
                    next(ch)
                except StopIteration:
                    done[i] = True
        rnd += 1


def _to_time_major(val, slabs, nb):
    tt = val.shape[0] // nb
    for s, slab in enumerate(slabs):
        for b in range(nb):
            slab[pl.ds(b, tt, stride=nb), :] = val[b * tt:(b + 1) * tt, s * LANES:(s + 1) * LANES]
    return jnp.concatenate([slab[...] for slab in slabs], axis=1)


def _to_batch_major(val, slabs, nb):
    tt = val.shape[0] // nb
    for s, slab in enumerate(slabs):
        slab[...] = val[:, s * LANES:(s + 1) * LANES]
    return jnp.concatenate(
        [jnp.concatenate([slab[pl.ds(b, tt, stride=nb), :] for slab in slabs], axis=1) for b in range(nb)], axis=0)


def _adaln_kernel(c_ref, w_ref, b_ref, o_ref):
    s = _silu(c_ref[...])
    o_ref[...] = _dot(s.astype(BF16), w_ref[...].astype(BF16)) + b_ref[...]


def _adaln(cvec, w_mod, b_mod):
    n_col = 6 * D_MODEL
    tn = n_col // 4
    return pl.pallas_call(
        _adaln_kernel,
        grid=(DEPTH, n_col // tn),
        in_specs=[pl.BlockSpec((N_MOD_ROWS, D_MODEL), lambda l, n: (0, 0)),
                  pl.BlockSpec((None, D_MODEL, tn), lambda l, n: (l, 0, n)),
                  pl.BlockSpec((None, 1, tn), lambda l, n: (l, 0, n))],
        out_specs=pl.BlockSpec((None, N_MOD_ROWS, tn), lambda l, n: (l, 0, n)),
        out_shape=jax.ShapeDtypeStruct((DEPTH, N_MOD_ROWS, n_col), F32),
        compiler_params=_params(2),
        name="adaln",
    )(cvec, w_mod, b_mod.reshape(DEPTH, 1, n_col))


def _inproj_kernel(*refs, latent, n_alias):
    if latent:
        (x_ref, xn_ref, mod_ref, g1_ref, w_ref, seg_ref, qg_ref, kg_ref, cos_ref, sin_ref,
         qa_ref, krep_ref, vrep_ref, u_ref, qc_ref, kc_ref, vc_ref, hb_ref, slab0, slab1) = refs
    else:
        x_ref, xn_ref, mod_ref, g1_ref, w_ref, seg_ref, qg_ref, kg_ref = refs[:8]
        (qa_ref, krep_ref, vrep_ref, ka_ref, va_ref, u_ref, qc_ref, kc_ref, vc_ref,
         hb_ref, slab0, slab1) = refs[8 + n_alias:]
    nb, tt, _ = x_ref.shape
    rows = nb * tt
    mod = mod_ref[...]
    shift = mod[:, :, 0:D_MODEL]
    scale = mod[:, :, D_MODEL:2 * D_MODEL]
    seg = seg_ref[...]
    slot = pl.program_id(0) % 2

    def normed(ref):
        h = _rms(ref[...], g1_ref[...]) * (1 + scale) + shift
        return h.reshape(rows, D_MODEL).astype(BF16)

    def put(ref, val):
        val = val.reshape(nb, tt, ref.shape[-1]).astype(ref.dtype)
        if len(ref.shape) == 3:
            ref[...] = val
        else:
            ref[:, 0] = val
            ref[:, 1:] = jnp.zeros((nb, DEPTH - 1, tt, ref.shape[-1]), ref.dtype)

    @pl.when(pl.program_id(0) == 0)
    def _():
        hb_ref[0] = normed(x_ref)

    def project():
        hb = hb_ref[slot]
        u_ref[...] = _to_time_major(_dot(hb, w_ref[:, OFF_U:OFF_QC]), (slab0, slab1), nb)
        put(qc_ref, _dot(hb, w_ref[:, OFF_QC:OFF_KC]) * Q_SCALE)
        put(kc_ref, _dot(hb, w_ref[:, OFF_KC:OFF_VC]))
        put(vc_ref, _dot(hb, w_ref[:, OFF_VC:OFF_G]))
        yield
        qa = _seg_rms(_dot(hb, w_ref[:, OFF_QA:OFF_KA]), seg, qg_ref[...])
        ka = _seg_rms(_dot(hb, w_ref[:, OFF_KA:OFF_VA]), seg[0:KV_WIDTH_A, 0:KV_WIDTH_A], kg_ref[...])
        va = _dot(hb, w_ref[:, OFF_VA:OFF_U])
        if latent:
            cos = jnp.concatenate([cos_ref[...]] * nb, axis=0)
            sin = jnp.concatenate([sin_ref[...]] * nb, axis=0)
            qa = _rope(qa, jnp.concatenate([cos] * REP_A, axis=1), jnp.concatenate([sin] * REP_A, axis=1))
            ka = _rope(ka, cos, sin)
        else:
            put(ka_ref, ka)
            put(va_ref, va)
        put(qa_ref, qa * Q_SCALE)
        put(krep_ref, _rep_heads(ka))
        put(vrep_ref, _rep_heads(va))

    def prepare_next():
        yield
        hb_ref[1 - slot] = normed(xn_ref)

    _round_robin([project(), prepare_next()])


def _inproj(x, mod_rows, g1, w_in, seg, qg, kg, layer, *, rope_tabs=None, prev_cache=()):
    latent = rope_tabs is not None
    nb, seq, _ = x.shape
    tt = INPROJ_ROWS // nb
    slab = lambda w: pl.BlockSpec((nb, tt, w), lambda j: (0, j, 0))
    act = lambda w, dt: jax.ShapeDtypeStruct((nb, seq, w), dt)
    if layer == 0:
        cache_slab = lambda w: pl.BlockSpec((nb, DEPTH, tt, w), lambda j: (0, 0, j, 0))
    else:
        cache_slab = lambda w: pl.BlockSpec((nb, None, tt, w), lambda j: (0, layer, j, 0))
    cache = lambda w: jax.ShapeDtypeStruct((nb, DEPTH, seq, w), F32)
    n_steps = seq // tt
    next_slab = pl.BlockSpec((nb, tt, D_MODEL), lambda j: (0, jnp.minimum(j + 1, n_steps - 1), 0))
    first_slab = pl.BlockSpec((nb, tt, D_MODEL), lambda j: (0, 0, 0))
    in_specs = [first_slab, next_slab, _const_spec(mod_rows.shape), _layer_spec(g1, layer),
                _layer_cols_spec(w_in, layer, 0, OFF_G), _const_spec(seg.shape), _layer_spec(qg, layer),
                _layer_spec(kg, layer)]
    args = [x, x, mod_rows, g1, w_in, seg, qg, kg]
    out_shape = [act(WIDTH_A, BF16), act(REP_A * KV_WIDTH_A, BF16), act(REP_A * KV_WIDTH_A, BF16)]
    out_specs = [slab(WIDTH_A), slab(REP_A * KV_WIDTH_A), slab(REP_A * KV_WIDTH_A)]
    aliases = {}
    if latent:
        in_specs += [pl.BlockSpec((tt, 2 * HEAD_DIM), lambda j: (j, 0))] * 2
        args += list(rope_tabs)
        kv_c = [act(WIDTH_C, BF16)] * 2
        kv_c_specs = [slab(WIDTH_C)] * 2
    else:
        aliases = {len(args) + i: o for i, o in enumerate((3, 4, 7, 8)[:len(prev_cache)])}
        in_specs += [pl.BlockSpec(memory_space=pl.ANY)] * len(prev_cache)
        args += list(prev_cache)
        out_shape += [cache(KV_WIDTH_A)] * 2
        out_specs += [cache_slab(KV_WIDTH_A)] * 2
        kv_c = [cache(WIDTH_C)] * 2
        kv_c_specs = [cache_slab(WIDTH_C)] * 2
    out_shape += [jax.ShapeDtypeStruct((seq * nb, SSM_WIDTH), F32),
                  act(WIDTH_C, BF16)] + kv_c
    out_specs += [pl.BlockSpec((INPROJ_ROWS, SSM_WIDTH), lambda j: (j, 0)),
                  slab(WIDTH_C)] + kv_c_specs
    return pl.pallas_call(
        functools.partial(_inproj_kernel, latent=latent, n_alias=len(prev_cache)),
        grid=(seq // tt,),
        in_specs=in_specs,
        out_specs=out_specs,
        out_shape=out_shape,
        scratch_shapes=[pltpu.VMEM((2, INPROJ_ROWS, D_MODEL), BF16)] + [pltpu.VMEM((INPROJ_ROWS, LANES), F32)] * 2,
        input_output_aliases=aliases,
        compiler_params=_params(1),
        name="inproj_lat" if latent else "inproj_ctx",
    )(*args)


def _head_mask(shape, head):
    lane = lax.broadcasted_iota(jnp.int32, shape, 1)
    return (lane // HEAD_DIM) == head


def _head_chain(q, heads, keys, vals, acc, key, bias=None):
    m_rows = q.shape[0]
    masks = [_head_mask(q.shape, h) for h in heads]
    qs = jnp.concatenate([jnp.where(hm, q, jnp.zeros_like(q)) for hm in masks], axis=0)
    s = _dot_t(qs, keys())
    if bias is not None:
        s = s + bias()
    yield
    p = jnp.exp2(s - jnp.max(s, axis=-1, keepdims=True))
    l = jnp.sum(p, axis=-1, keepdims=True)
    p = p.astype(BF16)
    yield
    o = _dot(p, vals()) / l
    for i, hm in enumerate(masks):
        acc[key] = jnp.where(hm, o[i * m_rows:(i + 1) * m_rows], acc[key])


def _attn_a_kernel(q_ref, kn_ref, vn_ref, ck_ref, cv_ref, o_ref, k_ref, v_ref):
    @pl.when(pl.program_id(1) == 0)
    def _():
        k_ref[0:PAST_LEN, :] = _rep_heads(ck_ref[...]).astype(BF16)
        v_ref[0:PAST_LEN, :] = _rep_heads(cv_ref[...]).astype(BF16)
        k_ref[PAST_LEN:, :] = kn_ref[...]
        v_ref[PAST_LEN:, :] = vn_ref[...]

    gw = REP_A * HEAD_DIM
    n_sub = q_ref.shape[0] // Q_CHAIN_ROWS
    acc = {(t, g): jnp.zeros((Q_CHAIN_ROWS, gw), F32) for t in range(n_sub) for g in range(N_KV_A)}
    chains = []
    for t in range(n_sub):
        rows = slice(t * Q_CHAIN_ROWS, (t + 1) * Q_CHAIN_ROWS)
        for g in range(N_KV_A):
            sl = slice(g * gw, (g + 1) * gw)
            for j in range(REP_A):
                chains.append(_head_chain(q_ref[rows, sl], [j], lambda sl=sl: k_ref[:, sl],
                                          lambda sl=sl: v_ref[:, sl], acc, (t, g)))
    _round_robin(chains, stagger=1)
    for (t, g), val in acc.items():
        o_ref[t * Q_CHAIN_ROWS:(t + 1) * Q_CHAIN_ROWS, g * gw:(g + 1) * gw] = val.astype(BF16)


def _attn_a_lat(qa, krep, vrep, cache_k, cache_v, layer):
    seq_blk = pl.BlockSpec((None, DEC_SEQ, WIDTH_A), lambda b, t: (b, 0, 0))
    cache_blk = pl.BlockSpec((None, None, PAST_LEN, KV_WIDTH_A), lambda b, t: (b, layer, 0, 0))
    q_blk = pl.BlockSpec((None, Q_TILE_A, WIDTH_A), lambda b, t: (b, t, 0))
    return pl.pallas_call(
        _attn_a_kernel,
        grid=(DEC_BATCH, DEC_SEQ // Q_TILE_A),
        in_specs=[q_blk, seq_blk, seq_blk, cache_blk, cache_blk],
        out_specs=q_blk,
        out_shape=jax.ShapeDtypeStruct((DEC_BATCH, DEC_SEQ, WIDTH_A), BF16),
        scratch_shapes=[pltpu.VMEM((PAST_LEN + DEC_SEQ, WIDTH_A), BF16)] * 2,
        compiler_params=_params(2),
        name="attn_a_lat",
    )(qa, krep, vrep, cache_k, cache_v)


def _attn_ctx_kernel(qa_ref, ka_ref, va_ref, qc_ref, kc_ref, vc_ref, oa_ref, oc_ref):
    gw = REP_A * HEAD_DIM
    acc = {}
    chains = []
    for b in range(CTX_STEP_SEQS):
        for g in range(N_KV_A):
            sl = slice(g * gw, (g + 1) * gw)
            acc["a", b, g] = jnp.zeros((SEQ, gw), F32)
            for j in range(REP_A):
                chains.append(_head_chain(qa_ref[b, :, sl], [j], lambda b=b, sl=sl: ka_ref[b, :, sl],
                                          lambda b=b, sl=sl: va_ref[b, :, sl], acc, ("a", b, g)))
        acc["c", b] = jnp.zeros((SEQ, WIDTH_C), F32)
        for h in range(0, N_HEADS_C, 2):
            chains.append(_head_chain(qc_ref[b], [h, h + 1], lambda b=b: kc_ref[b].astype(BF16),
                                      lambda b=b: vc_ref[b].astype(BF16), acc, ("c", b)))
    _round_robin(chains, stagger=1)
    for b in range(CTX_STEP_SEQS):
        for g in range(N_KV_A):
            oa_ref[b, :, g * gw:(g + 1) * gw] = acc["a", b, g].astype(BF16)
        oc_ref[b] = acc["c", b].astype(BF16)


def _attn_ctx(qa, krep, vrep, qc, kc, vc, layer):
    blk = lambda w: pl.BlockSpec((CTX_STEP_SEQS, SEQ, w), lambda i: (i, 0, 0))
    kv_blk = pl.BlockSpec((CTX_STEP_SEQS, None, SEQ, WIDTH_C), lambda i: (i, layer, 0, 0))
    return pl.pallas_call(
        _attn_ctx_kernel,
        grid=(BATCH // CTX_STEP_SEQS,),
        in_specs=[blk(WIDTH_A), blk(WIDTH_A), blk(WIDTH_A), blk(WIDTH_C), kv_blk, kv_blk],
        out_specs=[blk(WIDTH_A), blk(WIDTH_C)],
        out_shape=[jax.ShapeDtypeStruct((BATCH, SEQ, WIDTH_A), BF16),
                   jax.ShapeDtypeStruct((BATCH, SEQ, WIDTH_C), BF16)],
        compiler_params=_params(1),
        name="attn_ctx",
    )(qa, krep, vrep, qc, kc, vc)


def _na_kernel(q_ref, k_ref, v_ref, ck_ref, cv_ref, brow_ref, o_ref, b2_ref):
    nq = NA_Q_ROWS * GRID_W
    nk = NA_K_ROWS * GRID_W
    n_pair = 2 * NA_WIN_R

    @pl.when((pl.program_id(0) == 0) & (pl.program_id(1) == 0))
    def _():
        shp = (GRID_W, 2 * GRID_W)
        c = lax.broadcasted_iota(jnp.int32, shp, 0)
        kc = lax.broadcasted_iota(jnp.int32, shp, 1) & (GRID_W - 1)
        c0 = jnp.clip(c - NA_WIN_C // 2, 0, GRID_W - NA_WIN_C)
        col_bias = jnp.where(kc < c0, NEG, jnp.where(kc >= c0 + NA_WIN_C, NEG, 0.0))
        for h in range(N_HEADS_C):
            for e in range(n_pair):
                row = jnp.broadcast_to(brow_ref[h, e:e + 1, :], shp)
                toep = pltpu.roll(row, 2 * GRID_W - (NA_WIN_C - 1), 1, stride=1, stride_axis=0)
                b2_ref[h, e] = jnp.where(col_bias < 0.0, NEG, toep * LOG2_E)

    no_bias = jnp.zeros((nq, PAST_LEN), F32)
    ck = ck_ref[...].astype(BF16)
    cv = cv_ref[...].astype(BF16)
    acc = [jnp.zeros((nq, WIDTH_C), F32) for _ in range(NA_STEP_BLOCKS)]

    def chain(blk, heads):
        qrow0 = NA_Q_ROWS * (NA_STEP_BLOCKS * pl.program_id(1) + blk)
        row0 = jnp.clip(qrow0 - NA_WIN_R // 2, 0, NA_R0_MAX)
        start = pl.multiple_of(row0 * GRID_W, GRID_W)
        qrow = qrow0 + lax.broadcasted_iota(jnp.int32, (nq, nk), 0) // GRID_W
        krow = row0 + lax.broadcasted_iota(jnp.int32, (nq, nk), 1) // GRID_W
        win0 = jnp.clip(qrow - NA_WIN_R // 2, 0, GRID_ROWS - NA_WIN_R)
        row_bias = jnp.where(krow < win0, NEG, jnp.where(krow >= win0 + NA_WIN_R, NEG, 0.0))

        def head_bias(h):
            rows = []
            for i in range(NA_Q_ROWS):
                tiles = []
                for m in range(NA_K_ROWS // 2):
                    e = jnp.clip(row0 + 2 * m - (qrow0 + i) + NA_WIN_R, 0, n_pair - 1)
                    tiles.append(b2_ref[h, e])
                rows.append(jnp.concatenate(tiles, axis=1))
            return jnp.concatenate([jnp.concatenate(rows, axis=0) + row_bias, no_bias], axis=1)

        return _head_chain(q_ref[blk * nq:(blk + 1) * nq, :], heads,
                           lambda: jnp.concatenate([k_ref[pl.ds(start, nk), :], ck], axis=0),
                           lambda: jnp.concatenate([v_ref[pl.ds(start, nk), :], cv], axis=0), acc, blk,
                           bias=lambda: jnp.concatenate([head_bias(h) for h in heads], axis=0))

    pairs = [[h, h + 1] for h in range(0, N_HEADS_C, 2)]
    _round_robin([chain(blk, heads) for blk in range(NA_STEP_BLOCKS) for heads in pairs], stagger=1)
    for blk in range(NA_STEP_BLOCKS):
        o_ref[blk * nq:(blk + 1) * nq, :] = acc[blk].astype(BF16)


def _na_bias_rows(tbl):
    pad = jnp.pad(tbl, ((0, 0),) * (tbl.ndim - 2) + ((1, 1), (0, GRID_W - tbl.shape[-1])))
    return jnp.concatenate([pad[..., :-1, :], pad[..., 1:, :]], axis=-1)


def _na_lat(qc, kc, vc, cache_k, cache_v, brow, layer):
    step_rows = NA_STEP_BLOCKS * NA_Q_ROWS * GRID_W
    seq_blk = pl.BlockSpec((None, DEC_SEQ, WIDTH_C), lambda b, j: (b, 0, 0))
    cache_blk = pl.BlockSpec((None, None, PAST_LEN, WIDTH_C), lambda b, j: (b, layer, 0, 0))
    q_blk = pl.BlockSpec((None, step_rows, WIDTH_C), lambda b, j: (b, j, 0))
    return pl.pallas_call(
        _na_kernel,
        grid=(DEC_BATCH, DEC_SEQ // step_rows),
        in_specs=[q_blk, seq_blk, seq_blk, cache_blk, cache_blk, _layer_spec(brow, layer)],
        out_specs=q_blk,
        out_shape=jax.ShapeDtypeStruct((DEC_BATCH, DEC_SEQ, WIDTH_C), BF16),
        scratch_shapes=[pltpu.VMEM((N_HEADS_C, 2 * NA_WIN_R, GRID_W, 2 * GRID_W), F32)],
        compiler_params=_params(2),
        name="na_lat",
    )(qc, kc, vc, cache_k, cache_v, brow)


def _scan_kernel(uf_ref, ub_ref, bmat_ref, lam_ref, cmat_ref, h0_ref, yf_ref, yb_ref, hfin_ref,
                 *scratch, nb, lane_w):
    n_blk = SCAN_ROWS // SCAN_BLOCK
    hf_blk, hb_blk, st_ref = scratch[:SCAN_RING], scratch[SCAN_RING:2 * SCAN_RING], scratch[2 * SCAN_RING]
    steps = SCAN_BLOCK // nb
    n_lb = SSM_LANES // lane_w

    @pl.when(pl.program_id(0) == 0)
    def _():
        st_ref[...] = h0_ref[...]

    lanes = [(slice(lb * lane_w, (lb + 1) * lane_w), slice(SSM_LANES + lb * lane_w, SSM_LANES + (lb + 1) * lane_w))
             for lb in range(n_lb)]
    state = [[st_ref[d, c, :, re] for d in range(2) for c in range(2)] for re, _ in lanes]

    def block(i):
        f0 = i * SCAN_BLOCK
        b0 = (n_blk - 1 - i) * SCAN_BLOCK
        hf_ref, hb_ref = hf_blk[i % SCAN_RING], hb_blk[i % SCAN_RING]
        uf = uf_ref[f0:f0 + SCAN_BLOCK, :].astype(BF16)
        ub = ub_ref[b0:b0 + SCAN_BLOCK, :].astype(BF16)
        for cols in lanes:
            for sl in cols:
                hf_ref[:, sl] = _dot(uf, bmat_ref[0, :, sl])
                hb_ref[:, sl] = _dot(ub, bmat_ref[1, :, sl])
            yield
        for lb, (re, im) in enumerate(lanes):
            if lb:
                yield
            lam = [[jnp.broadcast_to(lam_ref[d, c, :, re], (nb, lane_w)) for c in range(2)] for d in range(2)]
            fr, fi, br, bi = state[lb]
            for k in range(steps):
                rf = slice(k * nb, (k + 1) * nb)
                rb = slice((steps - 1 - k) * nb, (steps - k) * nb)
                fr, fi = (lam[0][0] * fr - lam[0][1] * fi + hf_ref[rf, re],
                          lam[0][0] * fi + lam[0][1] * fr + hf_ref[rf, im])
                br, bi = (lam[1][0] * br - lam[1][1] * bi + hb_ref[rb, re],
                          lam[1][0] * bi + lam[1][1] * br + hb_ref[rb, im])
                hf_ref[rf, re] = fr
                hf_ref[rf, im] = fi
                hb_ref[rb, re] = br
                hb_ref[rb, im] = bi
            state[lb] = [fr, fi, br, bi]
        yf = yb = 0.0
        for cols in lanes:
            yield
            for sl in cols:
                yf = yf + _dot(hf_ref[:, sl].astype(BF16), cmat_ref[0, sl, :])
                yb = yb + _dot(hb_ref[:, sl].astype(BF16), cmat_ref[1, sl, :])
        yf_ref[f0:f0 + SCAN_BLOCK, :] = yf
        yb_ref[b0:b0 + SCAN_BLOCK, :] = yb

    _round_robin([block(i) for i in range(n_blk)], stagger=n_lb)
    for (re, _), vals in zip(lanes, state):
        for d in range(2):
            for c in range(2):
                st_ref[d, c, :, re] = vals[2 * d + c]
    hfin_ref[...] = st_ref[...]


def _scan_pair_kernel(uf_ref, ub_ref, bmat_ref, lam_ref, cmat_ref, h0_ref, yf_ref, yb_ref, *scratch, lane_w):
    half = DEC_BATCH
    tile = 2 * half
    n_blk = SCAN_ROWS // SCAN_BLOCK
    hf_blk, hb_blk, st_ref = scratch[:SCAN_RING], scratch[SCAN_RING:2 * SCAN_RING], scratch[2 * SCAN_RING]
    tiles = SCAN_BLOCK // tile
    n_lb = SSM_LANES // lane_w

    @pl.when(pl.program_id(0) == 0)
    def _():
        st_ref[...] = h0_ref[...]

    top = lax.broadcasted_iota(jnp.int32, (tile, lane_w), 0) < half
    swap = lambda a: pltpu.roll(a, half, 0)
    lanes = [(slice(lb * lane_w, (lb + 1) * lane_w), slice(SSM_LANES + lb * lane_w, SSM_LANES + (lb + 1) * lane_w))
             for lb in range(n_lb)]
    state = [[st_ref[0, :, re], st_ref[1, :, re]] for re, _ in lanes]

    def block(i):
        f0 = i * SCAN_BLOCK
        b0 = (n_blk - 1 - i) * SCAN_BLOCK
        hf_ref, hb_ref = hf_blk[i % SCAN_RING], hb_blk[i % SCAN_RING]
        uf = uf_ref[f0:f0 + SCAN_BLOCK, :].astype(BF16)
        ub = ub_ref[b0:b0 + SCAN_BLOCK, :].astype(BF16)
        for cols in lanes:
            for sl in cols:
                hf_ref[:, sl] = _dot(uf, bmat_ref[0, :, sl])
                hb_ref[:, sl] = _dot(ub, bmat_ref[1, :, sl])
            yield
        for lb, (re, im) in enumerate(lanes):
            if lb:
                yield
            la_r = lam_ref[0, :, re]
            la_i = lam_ref[1, :, re]
            lb_r = swap(la_r)
            lb_i = swap(la_i)
            sr, si = state[lb]
            for m in range(tiles):
                rf = slice(m * tile, (m + 1) * tile)
                rb = slice((tiles - 1 - m) * tile, (tiles - m) * tile)
                fr, fi = hf_ref[rf, re], hf_ref[rf, im]
                br, bi = hb_ref[rb, re], hb_ref[rb, im]
                vr = la_r * sr - la_i * si + jnp.where(top, fr, br)
                vi = la_r * si + la_i * sr + jnp.where(top, fi, bi)
                tr = swap(vr)
                ti = swap(vi)
                wr = lb_r * tr - lb_i * ti + jnp.where(top, br, fr)
                wi = lb_r * ti + lb_i * tr + jnp.where(top, bi, fi)
                hf_ref[rf, re] = jnp.where(top, vr, wr)
                hf_ref[rf, im] = jnp.where(top, vi, wi)
                hb_ref[rb, re] = jnp.where(top, wr, vr)
                hb_ref[rb, im] = jnp.where(top, wi, vi)
                sr, si = swap(wr), swap(wi)
            state[lb] = [sr, si]
        yf = yb = 0.0
        for cols in lanes:
            yield
            for sl in cols:
                yf = yf + _dot(hf_ref[:, sl].astype(BF16), cmat_ref[0, sl, :])
                yb = yb + _dot(hb_ref[:, sl].astype(BF16), cmat_ref[1, sl, :])
        yf_ref[f0:f0 + SCAN_BLOCK, :] = yf
        yb_ref[b0:b0 + SCAN_BLOCK, :] = yb

    _round_robin([block(i) for i in range(n_blk)], stagger=n_lb)
    for (re, _), (sr, si) in zip(lanes, state):
        st_ref[0, :, re] = sr
        st_ref[1, :, re] = si


def _scan_specs(n_rows):
    n = n_rows // SCAN_ROWS
    fwd = pl.BlockSpec((SCAN_ROWS, SSM_WIDTH), lambda j: (j, 0))
    bwd = pl.BlockSpec((SCAN_ROWS, SSM_WIDTH), lambda j: (n - 1 - j, 0))
    y_shape = jax.ShapeDtypeStruct((n_rows, SSM_WIDTH), F32)
    buf = [pltpu.VMEM((SCAN_BLOCK, 2 * SSM_LANES), F32)] * (2 * SCAN_RING)
    return n, fwd, bwd, y_shape, buf


def _scan_ctx(u_rows, bmat, lam, cmat, h0, layer):
    n, fwd, bwd, y_shape, buf = _scan_specs(u_rows.shape[0])
    st_shape = (2, 2, BATCH, SSM_LANES)
    return pl.pallas_call(
        functools.partial(_scan_kernel, nb=BATCH, lane_w=256),
        grid=(n,),
        in_specs=[fwd, bwd, _layer_spec(bmat, layer), _layer_spec(lam, layer), _layer_spec(cmat, layer),
                  _const_spec(st_shape)],
        out_specs=[fwd, bwd, pl.BlockSpec(st_shape, lambda j: (0, 0, 0, 0))],
        out_shape=[y_shape, y_shape, jax.ShapeDtypeStruct(st_shape, F32)],
        scratch_shapes=buf + [pltpu.VMEM(st_shape, F32)],
        compiler_params=_params(1),
        name="scan_ctx",
    )(u_rows, u_rows, bmat, lam, cmat, h0)


def _scan_lat(u_rows, bmat, lam_pair, cmat, h0_pair, layer):
    n, fwd, bwd, y_shape, buf = _scan_specs(u_rows.shape[0])
    st_shape = (2, 2 * DEC_BATCH, SSM_LANES)
    return pl.pallas_call(
        functools.partial(_scan_pair_kernel, lane_w=512),
        grid=(n,),
        in_specs=[fwd, bwd, _layer_spec(bmat, layer), _layer_spec(lam_pair, layer), _layer_spec(cmat, layer),
                  _layer_spec(h0_pair, layer)],
        out_specs=[fwd, bwd],
        out_shape=[y_shape, y_shape],
        scratch_shapes=buf + [pltpu.VMEM(st_shape, F32)],
        compiler_params=_params(1),
        name="scan_lat",
    )(u_rows, u_rows, bmat, lam_pair, cmat, h0_pair)


def _ssm_discretise(lam_re, lam_im, log_step, b_re, b_im, c_re, c_im):
    lead = log_step.shape[:-1]
    step = jnp.exp(log_step.astype(F32))[..., None]
    lr, li = lam_re.astype(F32), lam_im.astype(F32)
    mag = jnp.exp(lr * step)
    bar_r = mag * jnp.cos(li * step)
    bar_i = mag * jnp.sin(li * step)
    den = lr * lr + li * li
    coef_r = (((bar_r - 1) * lr + bar_i * li) / den)[..., None]
    coef_i = ((bar_i * lr - (bar_r - 1) * li) / den)[..., None]
    br, bi = b_re.astype(F32), b_im.astype(F32)
    bbar_r = coef_r * br - coef_i * bi
    bbar_i = coef_r * bi + coef_i * br
    eye = jnp.eye(SSM_GROUPS, dtype=F32)
    blk_b = lambda a: jnp.einsum('...gpc,gh->...gchp', a, eye).reshape(lead + (SSM_WIDTH, SSM_LANES))
    bmat = jnp.concatenate([blk_b(bbar_r), blk_b(bbar_i)], axis=-1)
    blk_c = lambda a: jnp.einsum('...gcp,gh->...gphc', a, eye).reshape(lead + (SSM_LANES, SSM_WIDTH))
    cmat = jnp.concatenate([blk_c(c_re.astype(F32)), -blk_c(c_im.astype(F32))], axis=-2)
    lam_flat = jnp.stack([bar_r, bar_i], axis=len(lead)).reshape(lead + (2, 1, SSM_LANES))
    return bmat.astype(BF16), lam_flat, cmat.astype(BF16)


def _merge_kernel(x_ref, oa_ref, yf_ref, yb_ref, u_ref, oc_ref, mod_ref, g1_ref, wg_ref, d_ref, wglu_ref,
                  wa_ref, wb_ref, wc_ref, wo_ref, g2_ref, wgu_ref, wd_ref, fg_ref, o_ref, *slabs, final):
    nb, tt, _ = x_ref.shape
    tc = tt // ROW_CHAINS
    rows = nb * tc
    mod = mod_ref[...]
    shift1 = mod[:, :, 0:D_MODEL]
    scale1 = mod[:, :, D_MODEL:2 * D_MODEL]
    gate1 = mod[:, :, 2 * D_MODEL:3 * D_MODEL]
    shift2 = mod[:, :, 3 * D_MODEL:4 * D_MODEL]
    scale2 = mod[:, :, 4 * D_MODEL:5 * D_MODEL]
    gate2 = mod[:, :, 5 * D_MODEL:6 * D_MODEL]
    def chain(c):
        ts = slice(c * tc, (c + 1) * tc)
        rs = slice(c * rows, (c + 1) * rows)
        flat = lambda ref, lo=0, hi=None: ref[:, ts, lo:hi].reshape(rows, -1)
        y = _gelu_tanh(yf_ref[rs, :] + yb_ref[rs, :] + d_ref[...] * u_ref[rs, :])
        ob = y * jax.nn.sigmoid(_dot(y.astype(BF16), wglu_ref[...]))
        ob = _to_batch_major(ob, slabs[2 * c:2 * c + 2], nb)
        x = x_ref[:, ts, :]
        h1 = (_rms(x, g1_ref[...]) * (1 + scale1) + shift1).reshape(rows, D_MODEL).astype(BF16)
        yield
        gate = lambda i: jax.nn.sigmoid(_dot(h1, wg_ref[0, :, i * D_MODEL:(i + 1) * D_MODEL]))
        merged = (gate(0) * _dot(flat(oa_ref), wa_ref[...])
                  + gate(1) * _dot(ob.astype(BF16), wb_ref[...])
                  + gate(2) * _dot(flat(oc_ref), wc_ref[...]))
        yield
        x1 = x + gate1 * _dot(merged.astype(BF16), wo_ref[...]).reshape(nb, tc, D_MODEL)
        h2 = _rms(x1, g2_ref[...]) * (1 + scale2) + shift2
        yield
        gu = _dot(h2.reshape(rows, D_MODEL).astype(BF16), wgu_ref[...])
        act = _silu(gu[:, :D_FF]) * gu[:, D_FF:]
        yield
        x2 = x1 + gate2 * _dot(act.astype(BF16), wd_ref[...]).reshape(nb, tc, D_MODEL)
        if final:
            x2 = _rms(x2, fg_ref[...])
        o_ref[:, ts, :] = x2

    _round_robin([chain(c) for c in range(ROW_CHAINS)])


def _merge(x, oa, yf, yb, u_rows, oc, mod_rows, g1, w_in, weights, final_g, layer, *, name):
    final = layer == DEPTH - 1
    nb, seq, _ = x.shape
    tt = ROW_TILE // nb
    slab = lambda w: pl.BlockSpec((nb, tt, w), lambda j: (0, j, 0))
    tmaj = pl.BlockSpec((ROW_TILE, SSM_WIDTH), lambda j: (j, 0))
    return pl.pallas_call(
        functools.partial(_merge_kernel, final=final),
        grid=(seq // tt,),
        in_specs=[slab(D_MODEL), slab(WIDTH_A), tmaj, tmaj, tmaj, slab(WIDTH_C), _const_spec(mod_rows.shape),
                  _layer_spec(g1, layer), _layer_cols_spec(w_in, layer, OFF_G, N_BRANCH * D_MODEL)]
                 + [_layer_spec(w, layer) for w in weights]
                 + [_const_spec(final_g.shape)],
        out_specs=slab(D_MODEL),
        out_shape=jax.ShapeDtypeStruct((nb, seq, D_MODEL), F32),
        scratch_shapes=[pltpu.VMEM((ROW_TILE // ROW_CHAINS, LANES), F32)] * (2 * ROW_CHAINS),
        compiler_params=_params(1),
        name=name + ("_final" if final else ""),
    )(x, oa, yf, yb, u_rows, oc, mod_rows, g1, w_in, *weights, final_g)


def _rope_tables():
    t = jnp.arange(DEC_SEQ)
    row = (t // GRID_W).astype(F32)
    col = (t % GRID_W).astype(F32)
    inv = 1.0 / (ROPE_THETA ** (jnp.arange(ROT_FREQS, dtype=F32) / ROT_FREQS))
    ar = row[:, None] * inv[None]
    ac = col[:, None] * inv[None]
    cos = jnp.concatenate([jnp.cos(ar), jnp.cos(ar), jnp.cos(ac), jnp.cos(ac)], axis=-1)
    sin = jnp.concatenate([-jnp.sin(ar), jnp.sin(ar), -jnp.sin(ac), jnp.sin(ac)], axis=-1)
    return jnp.tile(cos, (1, 2)), jnp.tile(sin, (1, 2))


def kernel(x_prompt, x_sample, c, cache_ga_k, cache_ga_v, cache_na_k, cache_na_v, state_ssm, c_ctx, w_mod, b_mod, norm1_g, w_in, qn_g, kn_g, ssm_lam_re, ssm_lam_im, ssm_log_step, ssm_b_re, ssm_b_im, ssm_c_re, ssm_c_im, ssm_d, ssm_w_glu, na_bias, w_br_a, w_br_b, w_br_c, w_out, norm2_g, w_ffn_gu, w_ffn_d, final_g):
    cvec = jnp.concatenate([c_ctx[None, :], c, jnp.zeros((N_MOD_ROWS - 1 - DEC_BATCH, D_MODEL), F32)], axis=0)
    mod = _adaln(cvec, w_mod, b_mod).reshape(DEPTH, N_MOD_ROWS, 1, 6 * D_MODEL)

    seg = jnp.kron(jnp.eye(MXU_DIM // HEAD_DIM, dtype=F32),
                   jnp.full((HEAD_DIM, HEAD_DIM), 1.0 / HEAD_DIM, F32)).astype(BF16)
    rope_tabs = _rope_tables()
    fg = final_g.reshape(1, D_MODEL)
    ck_a = cache_ga_k.reshape(DEC_BATCH, DEPTH, PAST_LEN, KV_WIDTH_A)
    cv_a = cache_ga_v.reshape(DEC_BATCH, DEPTH, PAST_LEN, KV_WIDTH_A)
    ck_c = cache_na_k.reshape(DEC_BATCH, DEPTH, PAST_LEN, WIDTH_C)
    cv_c = cache_na_v.reshape(DEC_BATCH, DEPTH, PAST_LEN, WIDTH_C)
    zero_state = jnp.zeros((2, 2, BATCH, SSM_LANES), F32)

    row = lambda p: p.reshape(DEPTH, 1, p.shape[-1])
    g1 = row(norm1_g)
    w_in_b = w_in.astype(BF16)
    qg = row(jnp.tile(qn_g, (1, N_HEADS_A)))
    kg = row(jnp.tile(kn_g, (1, N_KV_A)))
    merge_w = [row(ssm_d), ssm_w_glu.astype(BF16), w_br_a.astype(BF16), w_br_b.astype(BF16), w_br_c.astype(BF16),
               w_out.astype(BF16), row(norm2_g), w_ffn_gu.astype(BF16), w_ffn_d.astype(BF16)]

    bmat, lam, cmat = _ssm_discretise(ssm_lam_re, ssm_lam_im, ssm_log_step, ssm_b_re, ssm_b_im, ssm_c_re, ssm_c_im)
    pair = lambda a: jnp.transpose(a, (0, 2, 1, 3, 4)).reshape(DEPTH, 2, 2 * DEC_BATCH, SSM_LANES)
    lam_pair = pair(jnp.broadcast_to(lam, (DEPTH, 2, 2, DEC_BATCH, SSM_LANES)))
    h0_pair = pair(jnp.transpose(state_ssm.reshape(DEC_BATCH, DEPTH, 2, 2, SSM_LANES), (1, 2, 3, 0, 4)))
    brow = _na_bias_rows(na_bias)

    xp, xs = x_prompt, x_sample
    cache = ()
    ssm_st = []
    for l in range(DEPTH):
        mod_ctx = mod[l, 0:1]
        mod_lat = mod[l, 1:1 + DEC_BATCH]

        qa, krep, vrep, ka, va, u_rows, qc, kc, vc = _inproj(xp, mod_ctx, g1, w_in_b, seg, qg, kg, l,
                                                             prev_cache=cache)
        cache = (ka, va, kc, vc)
        oa, oc = _attn_ctx(qa, krep, vrep, qc, kc, vc, l)
        yf, yb, hfin = _scan_ctx(u_rows, bmat, lam, cmat, zero_state, l)
        xp = _merge(xp, oa, yf, yb, u_rows, oc, mod_ctx, g1, w_in_b, merge_w, fg, l, name="merge_ctx")
        ssm_st.append(jnp.transpose(hfin, (2, 0, 1, 3)))

        qa, krep, vrep, u_rows, qc, kc, vc = _inproj(xs, mod_lat, g1, w_in_b, seg, qg, kg, l, rope_tabs=rope_tabs)
        oa = _attn_a_lat(qa, krep, vrep, ck_a, cv_a, l)
        oc = _na_lat(qc, kc, vc, ck_c, cv_c, brow, l)
        yf, yb = _scan_lat(u_rows, bmat, lam_pair, cmat, h0_pair, l)
        xs = _merge(xs, oa, yf, yb, u_rows, oc, mod_lat, g1, w_in_b, merge_w, fg, l, name="merge_lat")

    ga_k, ga_v, na_k, na_v = cache
    new_ssm = jnp.stack(ssm_st, axis=1).reshape(BATCH, DEPTH, 2, 2, SSM_GROUPS, SSM_STATE)
    return (xp, xs,
            ga_k.reshape(BATCH, DEPTH, SEQ, N_KV_A, HEAD_DIM), ga_v.reshape(BATCH, DEPTH, SEQ, N_KV_A, HEAD_DIM),
            na_k.reshape(BATCH, DEPTH, SEQ, N_HEADS_C, HEAD_DIM), na_v.reshape(BATCH, DEPTH, SEQ, N_HEADS_C, HEAD_DIM),
            new_ssm)
```

```python
import functools
import math

import jax
import jax.numpy as jnp
from jax import lax
from jax.experimental import pallas as pl
from jax.experimental.pallas import tpu as pltpu

D_MODEL = 1024
BATCH = 16
SEQ = 256
DEPTH = 2
DEC_BATCH = 4
DEC_SEQ = 2048
PAST_LEN = 256
GRID_W = 64
GRID_ROWS = DEC_SEQ // GRID_W
HEAD_DIM = 64
N_HEADS_A = 8
N_KV_A = 2
REP_A = N_HEADS_A // N_KV_A
N_HEADS_C = 4
SSM_WIDTH = 256
SSM_GROUP = 16
SSM_GROUPS = SSM_WIDTH // SSM_GROUP
SSM_STATE = 64
SSM_LANES = SSM_GROUPS * SSM_STATE
NA_WIN_R = 8
NA_WIN_C = 16
D_FF = -(-8 * D_MODEL // (3 * 256)) * 256
ROPE_THETA = 10000.0
ROT_HALF = HEAD_DIM // 2
ROT_FREQS = ROT_HALF // 2
WIDTH_A = N_HEADS_A * HEAD_DIM
KV_WIDTH_A = N_KV_A * HEAD_DIM
WIDTH_C = N_HEADS_C * HEAD_DIM
N_BRANCH = 3
IN_WIDTH = WIDTH_A + 2 * KV_WIDTH_A + SSM_WIDTH + 3 * WIDTH_C + N_BRANCH * D_MODEL
EPS = 1e-6

OFF_QA = 0
OFF_KA = OFF_QA + WIDTH_A
OFF_VA = OFF_KA + KV_WIDTH_A
OFF_U = OFF_VA + KV_WIDTH_A
OFF_QC = OFF_U + SSM_WIDTH
OFF_KC = OFF_QC + WIDTH_C
OFF_VC = OFF_KC + WIDTH_C
OFF_G = OFF_VC + WIDTH_C

N_MOD_ROWS = 8
ROW_TILE = 512
INPROJ_ROWS = 512
Q_TILE_A = 512
Q_CHAIN_ROWS = 256
NA_Q_ROWS = 2
NA_K_ROWS = 10
NA_R0_MAX = GRID_ROWS - NA_K_ROWS
NA_STEP_BLOCKS = 4
CTX_STEP_SEQS = 4
ROW_CHAINS = 2
SCAN_ROWS = 2048
SCAN_BLOCK = 256
SCAN_RING = 3
LOG2_E = math.log2(math.e)
Q_SCALE = HEAD_DIM ** -0.5 * LOG2_E
LANES = 128
MXU_DIM = 256
NEG = -1e30
VMEM_LIMIT_V7X = 56 * 1024 * 1024

F32 = jnp.float32
BF16 = jnp.bfloat16


def _dot(a, b):
    return jnp.dot(a, b, preferred_element_type=F32)


def _dot_t(a, b):
    return lax.dot_general(a, b, (((1,), (1,)), ((), ())), preferred_element_type=F32)


def _params(n_axes):
    return pltpu.CompilerParams(dimension_semantics=("arbitrary",) * n_axes,
                                vmem_limit_bytes=VMEM_LIMIT_V7X)


def _const_spec(shape):
    nd = len(shape)
    return pl.BlockSpec(shape, lambda *_: (0,) * nd, pipeline_mode=pl.Buffered(1))


def _layer_spec(stacked, layer):
    shape = stacked.shape[1:]
    return pl.BlockSpec((None,) + shape, lambda *_: (layer,) + (0,) * len(shape), pipeline_mode=pl.Buffered(1))


def _layer_cols_spec(stacked, layer, start, width):
    k = stacked.shape[1]
    if start == 0:
        return pl.BlockSpec((None, k, width), lambda *_: (layer, 0, 0), pipeline_mode=pl.Buffered(1))
    return pl.BlockSpec((pl.Element(1), pl.Element(k), pl.Element(width)), lambda *_: (layer, 0, start),
                        pipeline_mode=pl.Buffered(1))


def _rms(x, g):
    return x * lax.rsqrt(jnp.mean(x * x, axis=-1, keepdims=True) + EPS) * g


def _silu(x):
    return x * jax.nn.sigmoid(x)


def _gelu_tanh(x):
    c = math.sqrt(2.0 / math.pi)
    return x * (0.5 * (1.0 + jnp.tanh(c * (x + 0.044715 * (x * x * x)))))


def _seg_rms(x, seg, g):
    w = seg.shape[0]
    x2 = x * x
    hi = x2.astype(BF16)
    lo = (x2 - hi.astype(F32)).astype(BF16)
    ms = jnp.concatenate([_dot(hi[:, c:c + w], seg) + _dot(lo[:, c:c + w], seg)
                          for c in range(0, x.shape[1], w)], axis=1)
    return x * lax.rsqrt(ms + EPS) * g


def _rope(x, cos, sin_signed):
    w = x.shape[-1]
    lane = lax.broadcasted_iota(jnp.int32, x.shape, 1)
    first = (lane & ROT_FREQS) == 0
    partner = jnp.where(first, pltpu.roll(x, w - ROT_FREQS, 1), pltpu.roll(x, ROT_FREQS, 1))
    return x * cos + partner * sin_signed


def _rep_heads(kv):
    lane = lax.broadcasted_iota(jnp.int32, kv.shape, 1)
    swapped = pltpu.roll(kv, HEAD_DIM, 1)
    lo = lane < HEAD_DIM
    h0 = jnp.where(lo, kv, swapped)
    h1 = jnp.where(lo, swapped, kv)
    return jnp.concatenate([h0, h0, h1, h1], axis=1)


def _round_robin(chains, stagger=0):
    done = [False] * len(chains)
    rnd = 0
    while not all(done):
        for i, ch in enumerate(chains):
            if not done[i] and rnd >= i * stagger:
                try:
                    next(ch)
                except StopIteration:
                    done[i] = True
        rnd += 1


def _to_time_major(val, slabs, nb):
    tt = val.shape[0] // nb
    for s, slab in enumerate(slabs):
        for b in range(nb):
            slab[pl.ds(b, tt, stride=nb), :] = val[b * tt:(b + 1) * tt, s * LANES:(s + 1) * LANES]
    return jnp.concatenate([slab[...] for slab in slabs], axis=1)


def _to_batch_major(val, slabs, nb):
    tt = val.shape[0] // nb
    for s, slab in enumerate(slabs):
        slab[...] = val[:, s * LANES:(s + 1) * LANES]
    return jnp.concatenate(
        [jnp.concatenate([slab[pl.ds(b, tt, stride=nb), :] for slab in slabs], axis=1) for b in range(nb)], axis=0)


def _adaln_kernel(c_ref, w_ref, b_ref, o_ref):
    s = _silu(c_ref[...])
    o_ref[...] = _dot(s.astype(BF16), w_ref[...].astype(BF16)) + b_ref[...]


def _adaln(cvec, w_mod, b_mod):
    n_col = 6 * D_MODEL
    tn = n_col // 4
    return pl.pallas_call(
        _adaln_kernel,
        grid=(DEPTH, n_col // tn),
        in_specs=[pl.BlockSpec((N_MOD_ROWS, D_MODEL), lambda l, n: (0, 0)),
                  pl.BlockSpec((None, D_MODEL, tn), lambda l, n: (l, 0, n)),
                  pl.BlockSpec((None, 1, tn), lambda l, n: (l, 0, n))],
        out_specs=pl.BlockSpec((None, N_MOD_ROWS, tn), lambda l, n: (l, 0, n)),
        out_shape=jax.ShapeDtypeStruct((DEPTH, N_MOD_ROWS, n_col), F32),
        compiler_params=_params(2),
        name="adaln",
    )(cvec, w_mod, b_mod.reshape(DEPTH, 1, n_col))


def _inproj_kernel(*refs, latent, n_alias):
    if latent:
        (x_ref, xn_ref, mod_ref, g1_ref, w_ref, seg_ref, qg_ref, kg_ref, cos_ref, sin_ref,
         qa_ref, krep_ref, vrep_ref, u_ref, qc_ref, kc_ref, vc_ref, hb_ref, slab0, slab1) = refs
    else:
        x_ref, xn_ref, mod_ref, g1_ref, w_ref, seg_ref, qg_ref, kg_ref = refs[:8]
        (qa_ref, krep_ref, vrep_ref, ka_ref, va_ref, u_ref, qc_ref, kc_ref, vc_ref,
         hb_ref, slab0, slab1) = refs[8 + n_alias:]
    nb, tt, _ = x_ref.shape
    rows = nb * tt
    mod = mod_ref[...]
    shift = mod[:, :, 0:D_MODEL]
    scale = mod[:, :, D_MODEL:2 * D_MODEL]
    seg = seg_ref[...]
    slot = pl.program_id(0) % 2

    def normed(ref):
        h = _rms(ref[...], g1_ref[...]) * (1 + scale) + shift
        return h.reshape(rows, D_MODEL).astype(BF16)

    def put(ref, val):
        val = val.reshape(nb, tt, ref.shape[-1]).astype(ref.dtype)
        if len(ref.shape) == 3:
            ref[...] = val
        else:
            ref[:, 0] = val
            ref[:, 1:] = jnp.zeros((nb, DEPTH - 1, tt, ref.shape[-1]), ref.dtype)

    @pl.when(pl.program_id(0) == 0)
    def _():
        hb_ref[0] = normed(x_ref)

    def project():
        hb = hb_ref[slot]
        u_ref[...] = _to_time_major(_dot(hb, w_ref[:, OFF_U:OFF_QC]), (slab0, slab1), nb)
        put(qc_ref, _dot(hb, w_ref[:, OFF_QC:OFF_KC]) * Q_SCALE)
        put(kc_ref, _dot(hb, w_ref[:, OFF_KC:OFF_VC]))
        put(vc_ref, _dot(hb, w_ref[:, OFF_VC:OFF_G]))
        yield
        qa = _seg_rms(_dot(hb, w_ref[:, OFF_QA:OFF_KA]), seg, qg_ref[...])
        ka = _seg_rms(_dot(hb, w_ref[:, OFF_KA:OFF_VA]), seg[0:KV_WIDTH_A, 0:KV_WIDTH_A], kg_ref[...])
        va = _dot(hb, w_ref[:, OFF_VA:OFF_U])
        if latent:
            cos = jnp.concatenate([cos_ref[...]] * nb, axis=0)
            sin = jnp.concatenate([sin_ref[...]] * nb, axis=0)
            qa = _rope(qa, jnp.concatenate([cos] * REP_A, axis=1), jnp.concatenate([sin] * REP_A, axis=1))
            ka = _rope(ka, cos, sin)
        else:
            put(ka_ref, ka)
            put(va_ref, va)
        put(qa_ref, qa * Q_SCALE)
        put(krep_ref, _rep_heads(ka))
        put(vrep_ref, _rep_heads(va))

    def prepare_next():
        yield
        hb_ref[1 - slot] = normed(xn_ref)

    _round_robin([project(), prepare_next()])


def _inproj(x, mod_rows, g1, w_in, seg, qg, kg, layer, *, rope_tabs=None, prev_cache=()):
    latent = rope_tabs is not None
    nb, seq, _ = x.shape
    tt = INPROJ_ROWS // nb
    slab = lambda w: pl.BlockSpec((nb, tt, w), lambda j: (0, j, 0))
    act = lambda w, dt: jax.ShapeDtypeStruct((nb, seq, w), dt)
    if layer == 0:
        cache_slab = lambda w: pl.BlockSpec((nb, DEPTH, tt, w), lambda j: (0, 0, j, 0))
    else:
        cache_slab = lambda w: pl.BlockSpec((nb, None, tt, w), lambda j: (0, layer, j, 0))
    cache = lambda w: jax.ShapeDtypeStruct((nb, DEPTH, seq, w), F32)
    n_steps = seq // tt
    next_slab = pl.BlockSpec((nb, tt, D_MODEL), lambda j: (0, jnp.minimum(j + 1, n_steps - 1), 0))
    first_slab = pl.BlockSpec((nb, tt, D_MODEL), lambda j: (0, 0, 0))
    in_specs = [first_slab, next_slab, _const_spec(mod_rows.shape), _layer_spec(g1, layer),
                _layer_cols_spec(w_in, layer, 0, OFF_G), _const_spec(seg.shape), _layer_spec(qg, layer),
                _layer_spec(kg, layer)]
    args = [x, x, mod_rows, g1, w_in, seg, qg, kg]
    out_shape = [act(WIDTH_A, BF16), act(REP_A * KV_WIDTH_A, BF16), act(REP_A * KV_WIDTH_A, BF16)]
    out_specs = [slab(WIDTH_A), slab(REP_A * KV_WIDTH_A), slab(REP_A * KV_WIDTH_A)]
    aliases = {}
    if latent:
        in_specs += [pl.BlockSpec((tt, 2 * HEAD_DIM), lambda j: (j, 0))] * 2
        args += list(rope_tabs)
        kv_c = [act(WIDTH_C, BF16)] * 2
        kv_c_specs = [slab(WIDTH_C)] * 2
    else:
        aliases = {len(args) + i: o for i, o in enumerate((3, 4, 7, 8)[:len(prev_cache)])}
        in_specs += [pl.BlockSpec(memory_space=pl.ANY)] * len(prev_cache)
        args += list(prev_cache)
        out_shape += [cache(KV_WIDTH_A)] * 2
        out_specs += [cache_slab(KV_WIDTH_A)] * 2
        kv_c = [cache(WIDTH_C)] * 2
        kv_c_specs = [cache_slab(WIDTH_C)] * 2
    out_shape += [jax.ShapeDtypeStruct((seq * nb, SSM_WIDTH), F32),
                  act(WIDTH_C, BF16)] + kv_c
    out_specs += [pl.BlockSpec((INPROJ_ROWS, SSM_WIDTH), lambda j: (j, 0)),
                  slab(WIDTH_C)] + kv_c_specs
    return pl.pallas_call(
        functools.partial(_inproj_kernel, latent=latent, n_alias=len(prev_cache)),
        grid=(seq // tt,),
        in_specs=in_specs,
        out_specs=out_specs,
        out_shape=out_shape,
        scratch_shapes=[pltpu.VMEM((2, INPROJ_ROWS, D_MODEL), BF16)] + [pltpu.VMEM((INPROJ_ROWS, LANES), F32)] * 2,
        input_output_aliases=aliases,
        compiler_params=_params(1),
        name="inproj_lat" if latent else "inproj_ctx",
    )(*args)


def _head_mask(shape, head):
    lane = lax.broadcasted_iota(jnp.int32, shape, 1)
    return (lane // HEAD_DIM) == head


def _head_chain(q, heads, keys, vals, acc, key, bias=None):
    m_rows = q.shape[0]
    masks = [_head_mask(q.shape, h) for h in heads]
    qs = jnp.concatenate([jnp.where(hm, q, jnp.zeros_like(q)) for hm in masks], axis=0)
    s = _dot_t(qs, keys())
    if bias is not None:
        s = s + bias()
    yield
    p = jnp.exp2(s - jnp.max(s, axis=-1, keepdims=True))
    l = jnp.sum(p, axis=-1, keepdims=True)
    p = p.astype(BF16)
    yield
    o = _dot(p, vals()) / l
    for i, hm in enumerate(masks):
        acc[key] = jnp.where(hm, o[i * m_rows:(i + 1) * m_rows], acc[key])


def _attn_a_kernel(q_ref, kn_ref, vn_ref, ck_ref, cv_ref, o_ref, k_ref, v_ref):
    @pl.when(pl.program_id(1) == 0)
    def _():
        k_ref[0:PAST_LEN, :] = _rep_heads(ck_ref[...]).astype(BF16)
        v_ref[0:PAST_LEN, :] = _rep_heads(cv_ref[...]).astype(BF16)
        k_ref[PAST_LEN:, :] = kn_ref[...]
        v_ref[PAST_LEN:, :] = vn_ref[...]

    gw = REP_A * HEAD_DIM
    n_sub = q_ref.shape[0] // Q_CHAIN_ROWS
    acc = {(t, g): jnp.zeros((Q_CHAIN_ROWS, gw), F32) for t in range(n_sub) for g in range(N_KV_A)}
    chains = []
    for t in range(n_sub):
        rows = slice(t * Q_CHAIN_ROWS, (t + 1) * Q_CHAIN_ROWS)
        for g in range(N_KV_A):
            sl = slice(g * gw, (g + 1) * gw)
            for j in range(REP_A):
                chains.append(_head_chain(q_ref[rows, sl], [j], lambda sl=sl: k_ref[:, sl],
                                          lambda sl=sl: v_ref[:, sl], acc, (t, g)))
    _round_robin(chains, stagger=1)
    for (t, g), val in acc.items():
        o_ref[t * Q_CHAIN_ROWS:(t + 1) * Q_CHAIN_ROWS, g * gw:(g + 1) * gw] = val.astype(BF16)


def _attn_a_lat(qa, krep, vrep, cache_k, cache_v, layer):
    seq_blk = pl.BlockSpec((None, DEC_SEQ, WIDTH_A), lambda b, t: (b, 0, 0))
    cache_blk = pl.BlockSpec((None, None, PAST_LEN, KV_WIDTH_A), lambda b, t: (b, layer, 0, 0))
    q_blk = pl.BlockSpec((None, Q_TILE_A, WIDTH_A), lambda b, t: (b, t, 0))
    return pl.pallas_call(
        _attn_a_kernel,
        grid=(DEC_BATCH, DEC_SEQ // Q_TILE_A),
        in_specs=[q_blk, seq_blk, seq_blk, cache_blk, cache_blk],
        out_specs=q_blk,
        out_shape=jax.ShapeDtypeStruct((DEC_BATCH, DEC_SEQ, WIDTH_A), BF16),
        scratch_shapes=[pltpu.VMEM((PAST_LEN + DEC_SEQ, WIDTH_A), BF16)] * 2,
        compiler_params=_params(2),
        name="attn_a_lat",
    )(qa, krep, vrep, cache_k, cache_v)


def _attn_ctx_kernel(qa_ref, ka_ref, va_ref, qc_ref, kc_ref, vc_ref, oa_ref, oc_ref):
    gw = REP_A * HEAD_DIM
    acc = {}
    chains = []
    for b in range(CTX_STEP_SEQS):
        for g in range(N_KV_A):
            sl = slice(g * gw, (g + 1) * gw)
            acc["a", b, g] = jnp.zeros((SEQ, gw), F32)
            for j in range(REP_A):
                chains.append(_head_chain(qa_ref[b, :, sl], [j], lambda b=b, sl=sl: ka_ref[b, :, sl],
                                          lambda b=b, sl=sl: va_ref[b, :, sl], acc, ("a", b, g)))
        acc["c", b] = jnp.zeros((SEQ, WIDTH_C), F32)
        for h in range(0, N_HEADS_C, 2):
            chains.append(_head_chain(qc_ref[b], [h, h + 1], lambda b=b: kc_ref[b].astype(BF16),
                                      lambda b=b: vc_ref[b].astype(BF16), acc, ("c", b)))
    _round_robin(chains, stagger=1)
    for b in range(CTX_STEP_SEQS):
        for g in range(N_KV_A):
            oa_ref[b, :, g * gw:(g + 1) * gw] = acc["a", b, g].astype(BF16)
        oc_ref[b] = acc["c", b].astype(BF16)


def _attn_ctx(qa, krep, vrep, qc, kc, vc, layer):
    blk = lambda w: pl.BlockSpec((CTX_STEP_SEQS, SEQ, w), lambda i: (i, 0, 0))
    kv_blk = pl.BlockSpec((CTX_STEP_SEQS, None, SEQ, WIDTH_C), lambda i: (i, layer, 0, 0))
    return pl.pallas_call(
        _attn_ctx_kernel,
        grid=(BATCH // CTX_STEP_SEQS,),
        in_specs=[blk(WIDTH_A), blk(WIDTH_A), blk(WIDTH_A), blk(WIDTH_C), kv_blk, kv_blk],
        out_specs=[blk(WIDTH_A), blk(WIDTH_C)],
        out_shape=[jax.ShapeDtypeStruct((BATCH, SEQ, WIDTH_A), BF16),
                   jax.ShapeDtypeStruct((BATCH, SEQ, WIDTH_C), BF16)],
        compiler_params=_params(1),
        name="attn_ctx",
    )(qa, krep, vrep, qc, kc, vc)


def _na_kernel(q_ref, k_ref, v_ref, ck_ref, cv_ref, brow_ref, o_ref, b2_ref):
    nq = NA_Q_ROWS * GRID_W
    nk = NA_K_ROWS * GRID_W
    n_pair = 2 * NA_WIN_R

    @pl.when((pl.program_id(0) == 0) & (pl.program_id(1) == 0))
    def _():
        shp = (GRID_W, 2 * GRID_W)
        c = lax.broadcasted_iota(jnp.int32, shp, 0)
        kc = lax.broadcasted_iota(jnp.int32, shp, 1) & (GRID_W - 1)
        c0 = jnp.clip(c - NA_WIN_C // 2, 0, GRID_W - NA_WIN_C)
        col_bias = jnp.where(kc < c0, NEG, jnp.where(kc >= c0 + NA_WIN_C, NEG, 0.0))
        for h in range(N_HEADS_C):
            for e in range(n_pair):
                row = jnp.broadcast_to(brow_ref[h, e:e + 1, :], shp)
                toep = pltpu.roll(row, 2 * GRID_W - (NA_WIN_C - 1), 1, stride=1, stride_axis=0)
                b2_ref[h, e] = jnp.where(col_bias < 0.0, NEG, toep * LOG2_E)

    no_bias = jnp.zeros((nq, PAST_LEN), F32)
    ck = ck_ref[...].astype(BF16)
    cv = cv_ref[...].astype(BF16)
    acc = [jnp.zeros((nq, WIDTH_C), F32) for _ in range(NA_STEP_BLOCKS)]

    def chain(blk, heads):
        qrow0 = NA_Q_ROWS * (NA_STEP_BLOCKS * pl.program_id(1) + blk)
        row0 = jnp.clip(qrow0 - NA_WIN_R // 2, 0, NA_R0_MAX)
        start = pl.multiple_of(row0 * GRID_W, GRID_W)
        qrow = qrow0 + lax.broadcasted_iota(jnp.int32, (nq, nk), 0) // GRID_W
        krow = row0 + lax.broadcasted_iota(jnp.int32, (nq, nk), 1) // GRID_W
        win0 = jnp.clip(qrow - NA_WIN_R // 2, 0, GRID_ROWS - NA_WIN_R)
        row_bias = jnp.where(krow < win0, NEG, jnp.where(krow >= win0 + NA_WIN_R, NEG, 0.0))

        def head_bias(h):
            rows = []
            for i in range(NA_Q_ROWS):
                tiles = []
                for m in range(NA_K_ROWS // 2):
                    e = jnp.clip(row0 + 2 * m - (qrow0 + i) + NA_WIN_R, 0, n_pair - 1)
                    tiles.append(b2_ref[h, e])
                rows.append(jnp.concatenate(tiles, axis=1))
            return jnp.concatenate([jnp.concatenate(rows, axis=0) + row_bias, no_bias], axis=1)

        return _head_chain(q_ref[blk * nq:(blk + 1) * nq, :], heads,
                           lambda: jnp.concatenate([k_ref[pl.ds(start, nk), :], ck], axis=0),
                           lambda: jnp.concatenate([v_ref[pl.ds(start, nk), :], cv], axis=0), acc, blk,
                           bias=lambda: jnp.concatenate([head_bias(h) for h in heads], axis=0))

    pairs = [[h, h + 1] for h in range(0, N_HEADS_C, 2)]
    _round_robin([chain(blk, heads) for blk in range(NA_STEP_BLOCKS) for heads in pairs], stagger=1)
    for blk in range(NA_STEP_BLOCKS):
        o_ref[blk * nq:(blk + 1) * nq, :] = acc[blk].astype(BF16)


def _na_bias_rows(tbl):
    pad = jnp.pad(tbl, ((0, 0), (1, 1), (0, GRID_W - tbl.shape[-1])))
    return jnp.concatenate([pad[:, :-1], pad[:, 1:]], axis=-1)


def _na_lat(qc, kc, vc, cache_k, cache_v, brow, layer):
    step_rows = NA_STEP_BLOCKS * NA_Q_ROWS * GRID_W
    seq_blk = pl.BlockSpec((None, DEC_SEQ, WIDTH_C), lambda b, j: (b, 0, 0))
    cache_blk = pl.BlockSpec((None, None, PAST_LEN, WIDTH_C), lambda b, j: (b, layer, 0, 0))
    q_blk = pl.BlockSpec((None, step_rows, WIDTH_C), lambda b, j: (b, j, 0))
    return pl.pallas_call(
        _na_kernel,
        grid=(DEC_BATCH, DEC_SEQ // step_rows),
        in_specs=[q_blk, seq_blk, seq_blk, cache_blk, cache_blk, _const_spec(brow.shape)],
        out_specs=q_blk,
        out_shape=jax.ShapeDtypeStruct((DEC_BATCH, DEC_SEQ, WIDTH_C), BF16),
        scratch_shapes=[pltpu.VMEM((N_HEADS_C, 2 * NA_WIN_R, GRID_W, 2 * GRID_W), F32)],
        compiler_params=_params(2),
        name="na_lat",
    )(qc, kc, vc, cache_k, cache_v, brow)


def _scan_kernel(uf_ref, ub_ref, bmat_ref, lam_ref, cmat_ref, h0_ref, yf_ref, yb_ref, hfin_ref,
                 *scratch, nb, lane_w):
    n_blk = SCAN_ROWS // SCAN_BLOCK
    hf_blk, hb_blk, st_ref = scratch[:SCAN_RING], scratch[SCAN_RING:2 * SCAN_RING], scratch[2 * SCAN_RING]
    steps = SCAN_BLOCK // nb
    n_lb = SSM_LANES // lane_w

    @pl.when(pl.program_id(0) == 0)
    def _():
        st_ref[...] = h0_ref[...]

    lanes = [(slice(lb * lane_w, (lb + 1) * lane_w), slice(SSM_LANES + lb * lane_w, SSM_LANES + (lb + 1) * lane_w))
             for lb in range(n_lb)]
    state = [[st_ref[d, c, :, re] for d in range(2) for c in range(2)] for re, _ in lanes]

    def block(i):
        f0 = i * SCAN_BLOCK
        b0 = (n_blk - 1 - i) * SCAN_BLOCK
        hf_ref, hb_ref = hf_blk[i % SCAN_RING], hb_blk[i % SCAN_RING]
        uf = uf_ref[f0:f0 + SCAN_BLOCK, :].astype(BF16)
        ub = ub_ref[b0:b0 + SCAN_BLOCK, :].astype(BF16)
        for cols in lanes:
            for sl in cols:
                hf_ref[:, sl] = _dot(uf, bmat_ref[0, :, sl])
                hb_ref[:, sl] = _dot(ub, bmat_ref[1, :, sl])
            yield
        for lb, (re, im) in enumerate(lanes):
            if lb:
                yield
            lam = [[jnp.broadcast_to(lam_ref[d, c, :, re], (nb, lane_w)) for c in range(2)] for d in range(2)]
            fr, fi, br, bi = state[lb]
            for k in range(steps):
                rf = slice(k * nb, (k + 1) * nb)
                rb = slice((steps - 1 - k) * nb, (steps - k) * nb)
                fr, fi = (lam[0][0] * fr - lam[0][1] * fi + hf_ref[rf, re],
                          lam[0][0] * fi + lam[0][1] * fr + hf_ref[rf, im])
                br, bi = (lam[1][0] * br - lam[1][1] * bi + hb_ref[rb, re],
                          lam[1][0] * bi + lam[1][1] * br + hb_ref[rb, im])
                hf_ref[rf, re] = fr
                hf_ref[rf, im] = fi
                hb_ref[rb, re] = br
                hb_ref[rb, im] = bi
            state[lb] = [fr, fi, br, bi]
        yf = yb = 0.0
        for cols in lanes:
            yield
            for sl in cols:
                yf = yf + _dot(hf_ref[:, sl].astype(BF16), cmat_ref[0, sl, :])
                yb = yb + _dot(hb_ref[:, sl].astype(BF16), cmat_ref[1, sl, :])
        yf_ref[f0:f0 + SCAN_BLOCK, :] = yf
        yb_ref[b0:b0 + SCAN_BLOCK, :] = yb

    _round_robin([block(i) for i in range(n_blk)], stagger=n_lb)
    for (re, _), vals in zip(lanes, state):
        for d in range(2):
            for c in range(2):
                st_ref[d, c, :, re] = vals[2 * d + c]
    hfin_ref[...] = st_ref[...]


def _scan_pair_kernel(uf_ref, ub_ref, bmat_ref, lam_ref, cmat_ref, h0_ref, yf_ref, yb_ref, *scratch, lane_w):
    half = DEC_BATCH
    tile = 2 * half
    n_blk = SCAN_ROWS // SCAN_BLOCK
    hf_blk, hb_blk, st_ref = scratch[:SCAN_RING], scratch[SCAN_RING:2 * SCAN_RING], scratch[2 * SCAN_RING]
    tiles = SCAN_BLOCK // tile
    n_lb = SSM_LANES // lane_w

    @pl.when(pl.program_id(0) == 0)
    def _():
        st_ref[...] = h0_ref[...]

    top = lax.broadcasted_iota(jnp.int32, (tile, lane_w), 0) < half
    swap = lambda a: pltpu.roll(a, half, 0)
    lanes = [(slice(lb * lane_w, (lb + 1) * lane_w), slice(SSM_LANES + lb * lane_w, SSM_LANES + (lb + 1) * lane_w))
             for lb in range(n_lb)]
    state = [[st_ref[0, :, re], st_ref[1, :, re]] for re, _ in lanes]

    def block(i):
        f0 = i * SCAN_BLOCK
        b0 = (n_blk - 1 - i) * SCAN_BLOCK
        hf_ref, hb_ref = hf_blk[i % SCAN_RING], hb_blk[i % SCAN_RING]
        uf = uf_ref[f0:f0 + SCAN_BLOCK, :].astype(BF16)
        ub = ub_ref[b0:b0 + SCAN_BLOCK, :].astype(BF16)
        for cols in lanes:
            for sl in cols:
                hf_ref[:, sl] = _dot(uf, bmat_ref[0, :, sl])
                hb_ref[:, sl] = _dot(ub, bmat_ref[1, :, sl])
            yield
        for lb, (re, im) in enumerate(lanes):
            if lb:
                yield
            la_r = lam_ref[0, :, re]
            la_i = lam_ref[1, :, re]
            lb_r = swap(la_r)
            lb_i = swap(la_i)
            sr, si = state[lb]
            for m in range(tiles):
                rf = slice(m * tile, (m + 1) * tile)
                rb = slice((tiles - 1 - m) * tile, (tiles - m) * tile)
                fr, fi = hf_ref[rf, re], hf_ref[rf, im]
                br, bi = hb_ref[rb, re], hb_ref[rb, im]
                vr = la_r * sr - la_i * si + jnp.where(top, fr, br)
                vi = la_r * si + la_i * sr + jnp.where(top, fi, bi)
                tr = swap(vr)
                ti = swap(vi)
                wr = lb_r * tr - lb_i * ti + jnp.where(top, br, fr)
                wi = lb_r * ti + lb_i * tr + jnp.where(top, bi, fi)
                hf_ref[rf, re] = jnp.where(top, vr, wr)
                hf_ref[rf, im] = jnp.where(top, vi, wi)
                hb_ref[rb, re] = jnp.where(top, wr, vr)
                hb_ref[rb, im] = jnp.where(top, wi, vi)
                sr, si = swap(wr), swap(wi)
            state[lb] = [sr, si]
        yf = yb = 0.0
        for cols in lanes:
            yield
            for sl in cols:
                yf = yf + _dot(hf_ref[:, sl].astype(BF16), cmat_ref[0, sl, :])
                yb = yb + _dot(hb_ref[:, sl].astype(BF16), cmat_ref[1, sl, :])
        yf_ref[f0:f0 + SCAN_BLOCK, :] = yf
        yb_ref[b0:b0 + SCAN_BLOCK, :] = yb

    _round_robin([block(i) for i in range(n_blk)], stagger=n_lb)
    for (re, _), (sr, si) in zip(lanes, state):
        st_ref[0, :, re] = sr
        st_ref[1, :, re] = si


def _scan_specs(n_rows):
    n = n_rows // SCAN_ROWS
    fwd = pl.BlockSpec((SCAN_ROWS, SSM_WIDTH), lambda j: (j, 0))
    bwd = pl.BlockSpec((SCAN_ROWS, SSM_WIDTH), lambda j: (n - 1 - j, 0))
    y_shape = jax.ShapeDtypeStruct((n_rows, SSM_WIDTH), F32)
    buf = [pltpu.VMEM((SCAN_BLOCK, 2 * SSM_LANES), F32)] * (2 * SCAN_RING)
    return n, fwd, bwd, y_shape, buf


def _scan_ctx(u_rows, bmat, lam, cmat, h0):
    n, fwd, bwd, y_shape, buf = _scan_specs(u_rows.shape[0])
    st_shape = (2, 2, BATCH, SSM_LANES)
    return pl.pallas_call(
        functools.partial(_scan_kernel, nb=BATCH, lane_w=256),
        grid=(n,),
        in_specs=[fwd, bwd, _const_spec(bmat.shape), _const_spec(lam.shape), _const_spec(cmat.shape),
                  _const_spec(st_shape)],
        out_specs=[fwd, bwd, pl.BlockSpec(st_shape, lambda j: (0, 0, 0, 0))],
        out_shape=[y_shape, y_shape, jax.ShapeDtypeStruct(st_shape, F32)],
        scratch_shapes=buf + [pltpu.VMEM(st_shape, F32)],
        compiler_params=_params(1),
        name="scan_ctx",
    )(u_rows, u_rows, bmat, lam, cmat, h0)


def _scan_lat(u_rows, bmat, lam_pair, cmat, h0_pair):
    n, fwd, bwd, y_shape, buf = _scan_specs(u_rows.shape[0])
    st_shape = (2, 2 * DEC_BATCH, SSM_LANES)
    return pl.pallas_call(
        functools.partial(_scan_pair_kernel, lane_w=512),
        grid=(n,),
        in_specs=[fwd, bwd, _const_spec(bmat.shape), _const_spec(st_shape), _const_spec(cmat.shape),
                  _const_spec(st_shape)],
        out_specs=[fwd, bwd],
        out_shape=[y_shape, y_shape],
        scratch_shapes=buf + [pltpu.VMEM(st_shape, F32)],
        compiler_params=_params(1),
        name="scan_lat",
    )(u_rows, u_rows, bmat, lam_pair, cmat, h0_pair)


def _ssm_discretise(lam_re, lam_im, log_step, b_re, b_im, c_re, c_im):
    step = jnp.exp(log_step.astype(F32))[..., None]
    lr, li = lam_re.astype(F32), lam_im.astype(F32)
    mag = jnp.exp(lr * step)
    bar_r = mag * jnp.cos(li * step)
    bar_i = mag * jnp.sin(li * step)
    den = lr * lr + li * li
    coef_r = (((bar_r - 1) * lr + bar_i * li) / den)[..., None]
    coef_i = ((bar_i * lr - (bar_r - 1) * li) / den)[..., None]
    br, bi = b_re.astype(F32), b_im.astype(F32)
    bbar_r = coef_r * br - coef_i * bi
    bbar_i = coef_r * bi + coef_i * br
    eye = jnp.eye(SSM_GROUPS, dtype=F32)
    blk_b = lambda a: jnp.einsum('dgpc,gh->dgchp', a, eye).reshape(2, SSM_WIDTH, SSM_LANES)
    bmat = jnp.concatenate([blk_b(bbar_r), blk_b(bbar_i)], axis=-1)
    blk_c = lambda a: jnp.einsum('dgcp,gh->dgphc', a, eye).reshape(2, SSM_LANES, SSM_WIDTH)
    cmat = jnp.concatenate([blk_c(c_re.astype(F32)), -blk_c(c_im.astype(F32))], axis=1)
    lam_flat = jnp.stack([bar_r, bar_i], axis=1).reshape(2, 2, 1, SSM_LANES)
    return bmat.astype(BF16), lam_flat, cmat.astype(BF16)


def _merge_kernel(x_ref, oa_ref, yf_ref, yb_ref, u_ref, oc_ref, mod_ref, g1_ref, wg_ref, d_ref, wglu_ref,
                  wa_ref, wb_ref, wc_ref, wo_ref, g2_ref, wgu_ref, wd_ref, fg_ref, o_ref, *slabs, final):
    nb, tt, _ = x_ref.shape
    tc = tt // ROW_CHAINS
    rows = nb * tc
    mod = mod_ref[...]
    shift1 = mod[:, :, 0:D_MODEL]
    scale1 = mod[:, :, D_MODEL:2 * D_MODEL]
    gate1 = mod[:, :, 2 * D_MODEL:3 * D_MODEL]
    shift2 = mod[:, :, 3 * D_MODEL:4 * D_MODEL]
    scale2 = mod[:, :, 4 * D_MODEL:5 * D_MODEL]
    gate2 = mod[:, :, 5 * D_MODEL:6 * D_MODEL]
    def chain(c):
        ts = slice(c * tc, (c + 1) * tc)
        rs = slice(c * rows, (c + 1) * rows)
        flat = lambda ref, lo=0, hi=None: ref[:, ts, lo:hi].reshape(rows, -1)
        y = _gelu_tanh(yf_ref[rs, :] + yb_ref[rs, :] + d_ref[...] * u_ref[rs, :])
        ob = y * jax.nn.sigmoid(_dot(y.astype(BF16), wglu_ref[...]))
        ob = _to_batch_major(ob, slabs[2 * c:2 * c + 2], nb)
        x = x_ref[:, ts, :]
        h1 = (_rms(x, g1_ref[...]) * (1 + scale1) + shift1).reshape(rows, D_MODEL).astype(BF16)
        yield
        gate = lambda i: jax.nn.sigmoid(_dot(h1, wg_ref[0, :, i * D_MODEL:(i + 1) * D_MODEL]))
        merged = (gate(0) * _dot(flat(oa_ref), wa_ref[...])
                  + gate(1) * _dot(ob.astype(BF16), wb_ref[...])
                  + gate(2) * _dot(flat(oc_ref), wc_ref[...]))
        yield
        x1 = x + gate1 * _dot(merged.astype(BF16), wo_ref[...]).reshape(nb, tc, D_MODEL)
        h2 = _rms(x1, g2_ref[...]) * (1 + scale2) + shift2
        yield
        gu = _dot(h2.reshape(rows, D_MODEL).astype(BF16), wgu_ref[...])
        act = _silu(gu[:, :D_FF]) * gu[:, D_FF:]
        yield
        x2 = x1 + gate2 * _dot(act.astype(BF16), wd_ref[...]).reshape(nb, tc, D_MODEL)
        if final:
            x2 = _rms(x2, fg_ref[...])
        o_ref[:, ts, :] = x2

    _round_robin([chain(c) for c in range(ROW_CHAINS)])


def _merge(x, oa, yf, yb, u_rows, oc, mod_rows, g1, w_in, weights, final_g, layer, *, name):
    final = layer == DEPTH - 1
    nb, seq, _ = x.shape
    tt = ROW_TILE // nb
    slab = lambda w: pl.BlockSpec((nb, tt, w), lambda j: (0, j, 0))
    tmaj = pl.BlockSpec((ROW_TILE, SSM_WIDTH), lambda j: (j, 0))
    return pl.pallas_call(
        functools.partial(_merge_kernel, final=final),
        grid=(seq // tt,),
        in_specs=[slab(D_MODEL), slab(WIDTH_A), tmaj, tmaj, tmaj, slab(WIDTH_C), _const_spec(mod_rows.shape),
                  _layer_spec(g1, layer), _layer_cols_spec(w_in, layer, OFF_G, N_BRANCH * D_MODEL)]
                 + [_layer_spec(w, layer) for w in weights]
                 + [_const_spec(final_g.shape)],
        out_specs=slab(D_MODEL),
        out_shape=jax.ShapeDtypeStruct((nb, seq, D_MODEL), F32),
        scratch_shapes=[pltpu.VMEM((ROW_TILE // ROW_CHAINS, LANES), F32)] * (2 * ROW_CHAINS),
        compiler_params=_params(1),
        name=name + ("_final" if final else ""),
    )(x, oa, yf, yb, u_rows, oc, mod_rows, g1, w_in, *weights, final_g)


def _rope_tables():
    t = jnp.arange(DEC_SEQ)
    row = (t // GRID_W).astype(F32)
    col = (t % GRID_W).astype(F32)
    inv = 1.0 / (ROPE_THETA ** (jnp.arange(ROT_FREQS, dtype=F32) / ROT_FREQS))
    ar = row[:, None] * inv[None]
    ac = col[:, None] * inv[None]
    cos = jnp.concatenate([jnp.cos(ar), jnp.cos(ar), jnp.cos(ac), jnp.cos(ac)], axis=-1)
    sin = jnp.concatenate([-jnp.sin(ar), jnp.sin(ar), -jnp.sin(ac), jnp.sin(ac)], axis=-1)
    return jnp.tile(cos, (1, 2)), jnp.tile(sin, (1, 2))


def kernel(x_prompt, x_sample, c, cache_ga_k, cache_ga_v, cache_na_k, cache_na_v, state_ssm, c_ctx, w_mod, b_mod, norm1_g, w_in, qn_g, kn_g, ssm_lam_re, ssm_lam_im, ssm_log_step, ssm_b_re, ssm_b_im, ssm_c_re, ssm_c_im, ssm_d, ssm_w_glu, na_bias, w_br_a, w_br_b, w_br_c, w_out, norm2_g, w_ffn_gu, w_ffn_d, final_g):
    cvec = jnp.concatenate([c_ctx[None, :], c, jnp.zeros((N_MOD_ROWS - 1 - DEC_BATCH, D_MODEL), F32)], axis=0)
    mod = _adaln(cvec, w_mod, b_mod).reshape(DEPTH, N_MOD_ROWS, 1, 6 * D_MODEL)

    seg = jnp.kron(jnp.eye(MXU_DIM // HEAD_DIM, dtype=F32),
                   jnp.full((HEAD_DIM, HEAD_DIM), 1.0 / HEAD_DIM, F32)).astype(BF16)
    rope_tabs = _rope_tables()
    fg = final_g.reshape(1, D_MODEL)
    ck_a = cache_ga_k.reshape(DEC_BATCH, DEPTH, PAST_LEN, KV_WIDTH_A)
    cv_a = cache_ga_v.reshape(DEC_BATCH, DEPTH, PAST_LEN, KV_WIDTH_A)
    ck_c = cache_na_k.reshape(DEC_BATCH, DEPTH, PAST_LEN, WIDTH_C)
    cv_c = cache_na_v.reshape(DEC_BATCH, DEPTH, PAST_LEN, WIDTH_C)
    zero_state = jnp.zeros((2, 2, BATCH, SSM_LANES), F32)

    row = lambda p: p.reshape(DEPTH, 1, p.shape[-1])
    g1 = row(norm1_g)
    w_in_b = w_in.astype(BF16)
    qg = row(jnp.tile(qn_g, (1, N_HEADS_A)))
    kg = row(jnp.tile(kn_g, (1, N_KV_A)))
    merge_w = [row(ssm_d), ssm_w_glu.astype(BF16), w_br_a.astype(BF16), w_br_b.astype(BF16), w_br_c.astype(BF16),
               w_out.astype(BF16), row(norm2_g), w_ffn_gu.astype(BF16), w_ffn_d.astype(BF16)]

    xp, xs = x_prompt, x_sample
    cache = ()
    ssm_st = []
    for l in range(DEPTH):
        bmat, lam, cmat = _ssm_discretise(ssm_lam_re[l], ssm_lam_im[l], ssm_log_step[l], ssm_b_re[l], ssm_b_im[l],
                                          ssm_c_re[l], ssm_c_im[l])
        mod_ctx = mod[l, 0:1]
        mod_lat = mod[l, 1:1 + DEC_BATCH]

        qa, krep, vrep, ka, va, u_rows, qc, kc, vc = _inproj(xp, mod_ctx, g1, w_in_b, seg, qg, kg, l,
                                                             prev_cache=cache)
        cache = (ka, va, kc, vc)
        oa, oc = _attn_ctx(qa, krep, vrep, qc, kc, vc, l)
        yf, yb, hfin = _scan_ctx(u_rows, bmat, lam, cmat, zero_state)
        xp = _merge(xp, oa, yf, yb, u_rows, oc, mod_ctx, g1, w_in_b, merge_w, fg, l, name="merge_ctx")
        ssm_st.append(jnp.transpose(hfin, (2, 0, 1, 3)))

        qa, krep, vrep, u_rows, qc, kc, vc = _inproj(xs, mod_lat, g1, w_in_b, seg, qg, kg, l, rope_tabs=rope_tabs)
        oa = _attn_a_lat(qa, krep, vrep, ck_a, cv_a, l)
        oc = _na_lat(qc, kc, vc, ck_c, cv_c, _na_bias_rows(na_bias[l]), l)
        h0 = jnp.transpose(state_ssm[:, l].reshape(DEC_BATCH, 2, 2, SSM_LANES), (2, 1, 0, 3))
        h0 = h0.reshape(2, 2 * DEC_BATCH, SSM_LANES)
        lam_pair = jnp.broadcast_to(jnp.transpose(lam, (1, 0, 2, 3)), (2, 2, DEC_BATCH, SSM_LANES))
        lam_pair = lam_pair.reshape(2, 2 * DEC_BATCH, SSM_LANES)
        yf, yb = _scan_lat(u_rows, bmat, lam_pair, cmat, h0)
        xs = _merge(xs, oa, yf, yb, u_rows, oc, mod_lat, g1, w_in_b, merge_w, fg, l, name="merge_lat")

    ga_k, ga_v, na_k, na_v = cache
    new_ssm = jnp.stack(ssm_st, axis=1).reshape(BATCH, DEPTH, 2, 2, SSM_GROUPS, SSM_STATE)
    return (xp, xs,
            ga_k.reshape(BATCH, DEPTH, SEQ, N_KV_A, HEAD_DIM), ga_v.reshape(BATCH, DEPTH, SEQ, N_KV_A, HEAD_DIM),
            na_k.reshape(BATCH, DEPTH, SEQ, N_HEADS_C, HEAD_DIM), na_v.reshape(BATCH, DEPTH, SEQ, N_HEADS_C, HEAD_DIM),
            new_ssm)
```

```python
import functools
import math

import jax
import jax.numpy as jnp
from jax import lax
from jax.experimental import pallas as pl
from jax.experimental.pallas import tpu as pltpu

D_MODEL = 1024
BATCH = 16
SEQ = 256
DEPTH = 2
DEC_BATCH = 4
DEC_SEQ = 2048
PAST_LEN = 256
GRID_W = 64
GRID_ROWS = DEC_SEQ // GRID_W
HEAD_DIM = 64
N_HEADS_A = 8
N_KV_A = 2
REP_A = N_HEADS_A // N_KV_A
N_HEADS_C = 4
SSM_WIDTH = 256
SSM_GROUP = 16
SSM_GROUPS = SSM_WIDTH // SSM_GROUP
SSM_STATE = 64
SSM_LANES = SSM_GROUPS * SSM_STATE
NA_WIN_R = 8
NA_WIN_C = 16
D_FF = -(-8 * D_MODEL // (3 * 256)) * 256
ROPE_THETA = 10000.0
ROT_HALF = HEAD_DIM // 2
ROT_FREQS = ROT_HALF // 2
WIDTH_A = N_HEADS_A * HEAD_DIM
KV_WIDTH_A = N_KV_A * HEAD_DIM
WIDTH_C = N_HEADS_C * HEAD_DIM
N_BRANCH = 3
IN_WIDTH = WIDTH_A + 2 * KV_WIDTH_A + SSM_WIDTH + 3 * WIDTH_C + N_BRANCH * D_MODEL
EPS = 1e-6

OFF_QA = 0
OFF_KA = OFF_QA + WIDTH_A
OFF_VA = OFF_KA + KV_WIDTH_A
OFF_U = OFF_VA + KV_WIDTH_A
OFF_QC = OFF_U + SSM_WIDTH
OFF_KC = OFF_QC + WIDTH_C
OFF_VC = OFF_KC + WIDTH_C
OFF_G = OFF_VC + WIDTH_C

N_MOD_ROWS = 8
ROW_TILE = 512
INPROJ_ROWS = 512
Q_TILE_A = 512
Q_CHAIN_ROWS = 256
NA_Q_ROWS = 2
NA_K_ROWS = 10
NA_R0_MAX = GRID_ROWS - NA_K_ROWS
NA_STEP_BLOCKS = 8
CTX_STEP_SEQS = 4
ROW_CHAINS = 2
SCAN_ROWS = 2048
SCAN_BLOCK = 256
SCAN_RING = 3
LOG2_E = math.log2(math.e)
Q_SCALE = HEAD_DIM ** -0.5 * LOG2_E
LANES = 128
MXU_DIM = 256
NEG = -1e30
VMEM_LIMIT_V7X = 56 * 1024 * 1024

F32 = jnp.float32
BF16 = jnp.bfloat16


def _dot(a, b):
    return jnp.dot(a, b, preferred_element_type=F32)


def _dot_t(a, b):
    return lax.dot_general(a, b, (((1,), (1,)), ((), ())), preferred_element_type=F32)


def _params(n_axes):
    return pltpu.CompilerParams(dimension_semantics=("arbitrary",) * n_axes,
                                vmem_limit_bytes=VMEM_LIMIT_V7X)


def _const_spec(shape):
    nd = len(shape)
    return pl.BlockSpec(shape, lambda *_: (0,) * nd, pipeline_mode=pl.Buffered(1))


def _layer_spec(stacked, layer):
    shape = stacked.shape[1:]
    return pl.BlockSpec((None,) + shape, lambda *_: (layer,) + (0,) * len(shape), pipeline_mode=pl.Buffered(1))


def _layer_cols_spec(stacked, layer, start, width):
    k = stacked.shape[1]
    if start == 0:
        return pl.BlockSpec((None, k, width), lambda *_: (layer, 0, 0), pipeline_mode=pl.Buffered(1))
    return pl.BlockSpec((pl.Element(1), pl.Element(k), pl.Element(width)), lambda *_: (layer, 0, start),
                        pipeline_mode=pl.Buffered(1))


def _rms(x, g):
    return x * lax.rsqrt(jnp.mean(x * x, axis=-1, keepdims=True) + EPS) * g


def _silu(x):
    return x * jax.nn.sigmoid(x)


def _gelu_tanh(x):
    c = math.sqrt(2.0 / math.pi)
    return x * (0.5 * (1.0 + jnp.tanh(c * (x + 0.044715 * (x * x * x)))))


def _seg_rms(x, seg, g):
    w = seg.shape[0]
    x2 = x * x
    hi = x2.astype(BF16)
    lo = (x2 - hi.astype(F32)).astype(BF16)
    ms = jnp.concatenate([_dot(hi[:, c:c + w], seg) + _dot(lo[:, c:c + w], seg)
                          for c in range(0, x.shape[1], w)], axis=1)
    return x * lax.rsqrt(ms + EPS) * g


def _rope(x, cos, sin_signed):
    w = x.shape[-1]
    lane = lax.broadcasted_iota(jnp.int32, x.shape, 1)
    first = (lane & ROT_FREQS) == 0
    partner = jnp.where(first, pltpu.roll(x, w - ROT_FREQS, 1), pltpu.roll(x, ROT_FREQS, 1))
    return x * cos + partner * sin_signed


def _rep_heads(kv):
    lane = lax.broadcasted_iota(jnp.int32, kv.shape, 1)
    swapped = pltpu.roll(kv, HEAD_DIM, 1)
    lo = lane < HEAD_DIM
    h0 = jnp.where(lo, kv, swapped)
    h1 = jnp.where(lo, swapped, kv)
    return jnp.concatenate([h0, h0, h1, h1], axis=1)


def _round_robin(chains, stagger=0):
    done = [False] * len(chains)
    rnd = 0
    while not all(done):
        for i, ch in enumerate(chains):
            if not done[i] and rnd >= i * stagger:
                try:
                    next(ch)
                except StopIteration:
                    done[i] = True
        rnd += 1


def _to_time_major(val, slabs, nb):
    tt = val.shape[0] // nb
    for s, slab in enumerate(slabs):
        for b in range(nb):
            slab[pl.ds(b, tt, stride=nb), :] = val[b * tt:(b + 1) * tt, s * LANES:(s + 1) * LANES]
    return jnp.concatenate([slab[...] for slab in slabs], axis=1)


def _to_batch_major(val, slabs, nb):
    tt = val.shape[0] // nb
    for s, slab in enumerate(slabs):
        slab[...] = val[:, s * LANES:(s + 1) * LANES]
    return jnp.concatenate(
        [jnp.concatenate([slab[pl.ds(b, tt, stride=nb), :] for slab in slabs], axis=1) for b in range(nb)], axis=0)


def _adaln_kernel(c_ref, w_ref, b_ref, o_ref):
    s = _silu(c_ref[...])
    o_ref[...] = _dot(s.astype(BF16), w_ref[...].astype(BF16)) + b_ref[...]


def _adaln(cvec, w_mod, b_mod):
    n_col = 6 * D_MODEL
    tn = n_col // 4
    return pl.pallas_call(
        _adaln_kernel,
        grid=(DEPTH, n_col // tn),
        in_specs=[pl.BlockSpec((N_MOD_ROWS, D_MODEL), lambda l, n: (0, 0)),
                  pl.BlockSpec((None, D_MODEL, tn), lambda l, n: (l, 0, n)),
                  pl.BlockSpec((None, 1, tn), lambda l, n: (l, 0, n))],
        out_specs=pl.BlockSpec((None, N_MOD_ROWS, tn), lambda l, n: (l, 0, n)),
        out_shape=jax.ShapeDtypeStruct((DEPTH, N_MOD_ROWS, n_col), F32),
        compiler_params=_params(2),
        name="adaln",
    )(cvec, w_mod, b_mod.reshape(DEPTH, 1, n_col))


def _inproj_kernel(*refs, latent, n_alias):
    if latent:
        (x_ref, xn_ref, mod_ref, g1_ref, w_ref, seg_ref, qg_ref, kg_ref, cos_ref, sin_ref,
         qa_ref, krep_ref, vrep_ref, u_ref, qc_ref, kc_ref, vc_ref, hb_ref, slab0, slab1) = refs
    else:
        x_ref, xn_ref, mod_ref, g1_ref, w_ref, seg_ref, qg_ref, kg_ref = refs[:8]
        (qa_ref, krep_ref, vrep_ref, ka_ref, va_ref, u_ref, qc_ref, kc_ref, vc_ref,
         hb_ref, slab0, slab1) = refs[8 + n_alias:]
    nb, tt, _ = x_ref.shape
    rows = nb * tt
    mod = mod_ref[...]
    shift = mod[:, :, 0:D_MODEL]
    scale = mod[:, :, D_MODEL:2 * D_MODEL]
    seg = seg_ref[...]
    slot = pl.program_id(0) % 2

    def normed(ref):
        h = _rms(ref[...], g1_ref[...]) * (1 + scale) + shift
        return h.reshape(rows, D_MODEL).astype(BF16)

    def put(ref, val):
        val = val.reshape(nb, tt, ref.shape[-1]).astype(ref.dtype)
        if len(ref.shape) == 3:
            ref[...] = val
        else:
            ref[:, 0] = val
            ref[:, 1:] = jnp.zeros((nb, DEPTH - 1, tt, ref.shape[-1]), ref.dtype)

    @pl.when(pl.program_id(0) == 0)
    def _():
        hb_ref[0] = normed(x_ref)

    def project():
        hb = hb_ref[slot]
        u_ref[...] = _to_time_major(_dot(hb, w_ref[:, OFF_U:OFF_QC]), (slab0, slab1), nb)
        put(qc_ref, _dot(hb, w_ref[:, OFF_QC:OFF_KC]) * Q_SCALE)
        put(kc_ref, _dot(hb, w_ref[:, OFF_KC:OFF_VC]))
        put(vc_ref, _dot(hb, w_ref[:, OFF_VC:OFF_G]))
        yield
        qa = _seg_rms(_dot(hb, w_ref[:, OFF_QA:OFF_KA]), seg, qg_ref[...])
        ka = _seg_rms(_dot(hb, w_ref[:, OFF_KA:OFF_VA]), seg[0:KV_WIDTH_A, 0:KV_WIDTH_A], kg_ref[...])
        va = _dot(hb, w_ref[:, OFF_VA:OFF_U])
        if latent:
            cos = jnp.concatenate([cos_ref[...]] * nb, axis=0)
            sin = jnp.concatenate([sin_ref[...]] * nb, axis=0)
            qa = _rope(qa, jnp.concatenate([cos] * REP_A, axis=1), jnp.concatenate([sin] * REP_A, axis=1))
            ka = _rope(ka, cos, sin)
        else:
            put(ka_ref, ka)
            put(va_ref, va)
        put(qa_ref, qa * Q_SCALE)
        put(krep_ref, _rep_heads(ka))
        put(vrep_ref, _rep_heads(va))

    def prepare_next():
        yield
        hb_ref[1 - slot] = normed(xn_ref)

    _round_robin([project(), prepare_next()])


def _inproj(x, mod_rows, g1, w_in, seg, qg, kg, layer, *, rope_tabs=None, prev_cache=()):
    latent = rope_tabs is not None
    nb, seq, _ = x.shape
    tt = INPROJ_ROWS // nb
    slab = lambda w: pl.BlockSpec((nb, tt, w), lambda j: (0, j, 0))
    act = lambda w, dt: jax.ShapeDtypeStruct((nb, seq, w), dt)
    if layer == 0:
        cache_slab = lambda w: pl.BlockSpec((nb, DEPTH, tt, w), lambda j: (0, 0, j, 0))
    else:
        cache_slab = lambda w: pl.BlockSpec((nb, None, tt, w), lambda j: (0, layer, j, 0))
    cache = lambda w: jax.ShapeDtypeStruct((nb, DEPTH, seq, w), F32)
    n_steps = seq // tt
    next_slab = pl.BlockSpec((nb, tt, D_MODEL), lambda j: (0, jnp.minimum(j + 1, n_steps - 1), 0))
    first_slab = pl.BlockSpec((nb, tt, D_MODEL), lambda j: (0, 0, 0))
    in_specs = [first_slab, next_slab, _const_spec(mod_rows.shape), _layer_spec(g1, layer),
                _layer_cols_spec(w_in, layer, 0, OFF_G), _const_spec(seg.shape), _layer_spec(qg, layer),
                _layer_spec(kg, layer)]
    args = [x, x, mod_rows, g1, w_in, seg, qg, kg]
    out_shape = [act(WIDTH_A, BF16), act(REP_A * KV_WIDTH_A, BF16), act(REP_A * KV_WIDTH_A, BF16)]
    out_specs = [slab(WIDTH_A), slab(REP_A * KV_WIDTH_A), slab(REP_A * KV_WIDTH_A)]
    aliases = {}
    if latent:
        in_specs += [pl.BlockSpec((tt, 2 * HEAD_DIM), lambda j: (j, 0))] * 2
        args += list(rope_tabs)
        kv_c = [act(WIDTH_C, BF16)] * 2
        kv_c_specs = [slab(WIDTH_C)] * 2
    else:
        aliases = {len(args) + i: o for i, o in enumerate((3, 4, 7, 8)[:len(prev_cache)])}
        in_specs += [pl.BlockSpec(memory_space=pl.ANY)] * len(prev_cache)
        args += list(prev_cache)
        out_shape += [cache(KV_WIDTH_A)] * 2
        out_specs += [cache_slab(KV_WIDTH_A)] * 2
        kv_c = [cache(WIDTH_C)] * 2
        kv_c_specs = [cache_slab(WIDTH_C)] * 2
    out_shape += [jax.ShapeDtypeStruct((seq * nb, SSM_WIDTH), F32),
                  act(WIDTH_C, BF16)] + kv_c
    out_specs += [pl.BlockSpec((INPROJ_ROWS, SSM_WIDTH), lambda j: (j, 0)),
                  slab(WIDTH_C)] + kv_c_specs
    return pl.pallas_call(
        functools.partial(_inproj_kernel, latent=latent, n_alias=len(prev_cache)),
        grid=(seq // tt,),
        in_specs=in_specs,
        out_specs=out_specs,
        out_shape=out_shape,
        scratch_shapes=[pltpu.VMEM((2, INPROJ_ROWS, D_MODEL), BF16)] + [pltpu.VMEM((INPROJ_ROWS, LANES), F32)] * 2,
        input_output_aliases=aliases,
        compiler_params=_params(1),
        name="inproj_lat" if latent else "inproj_ctx",
    )(*args)


def _head_mask(shape, head):
    lane = lax.broadcasted_iota(jnp.int32, shape, 1)
    return (lane // HEAD_DIM) == head


def _head_chain(q, heads, keys, vals, acc, key, bias=None):
    m_rows = q.shape[0]
    masks = [_head_mask(q.shape, h) for h in heads]
    qs = jnp.concatenate([jnp.where(hm, q, jnp.zeros_like(q)) for hm in masks], axis=0)
    s = _dot_t(qs, keys())
    if bias is not None:
        s = s + bias()
    yield
    p = jnp.exp2(s - jnp.max(s, axis=-1, keepdims=True))
    l = jnp.sum(p, axis=-1, keepdims=True)
    p = p.astype(BF16)
    yield
    o = _dot(p, vals()) / l
    for i, hm in enumerate(masks):
        acc[key] = jnp.where(hm, o[i * m_rows:(i + 1) * m_rows], acc[key])


def _attn_a_kernel(q_ref, kn_ref, vn_ref, ck_ref, cv_ref, o_ref, k_ref, v_ref):
    @pl.when(pl.program_id(1) == 0)
    def _():
        k_ref[0:PAST_LEN, :] = _rep_heads(ck_ref[...]).astype(BF16)
        v_ref[0:PAST_LEN, :] = _rep_heads(cv_ref[...]).astype(BF16)
        k_ref[PAST_LEN:, :] = kn_ref[...]
        v_ref[PAST_LEN:, :] = vn_ref[...]

    gw = REP_A * HEAD_DIM
    n_sub = q_ref.shape[0] // Q_CHAIN_ROWS
    acc = {(t, g): jnp.zeros((Q_CHAIN_ROWS, gw), F32) for t in range(n_sub) for g in range(N_KV_A)}
    chains = []
    for t in range(n_sub):
        rows = slice(t * Q_CHAIN_ROWS, (t + 1) * Q_CHAIN_ROWS)
        for g in range(N_KV_A):
            sl = slice(g * gw, (g + 1) * gw)
            for j in range(REP_A):
                chains.append(_head_chain(q_ref[rows, sl], [j], lambda sl=sl: k_ref[:, sl],
                                          lambda sl=sl: v_ref[:, sl], acc, (t, g)))
    _round_robin(chains, stagger=1)
    for (t, g), val in acc.items():
        o_ref[t * Q_CHAIN_ROWS:(t + 1) * Q_CHAIN_ROWS, g * gw:(g + 1) * gw] = val.astype(BF16)


def _attn_a_lat(qa, krep, vrep, cache_k, cache_v, layer):
    seq_blk = pl.BlockSpec((None, DEC_SEQ, WIDTH_A), lambda b, t: (b, 0, 0))
    cache_blk = pl.BlockSpec((None, None, PAST_LEN, KV_WIDTH_A), lambda b, t: (b, layer, 0, 0))
    q_blk = pl.BlockSpec((None, Q_TILE_A, WIDTH_A), lambda b, t: (b, t, 0))
    return pl.pallas_call(
        _attn_a_kernel,
        grid=(DEC_BATCH, DEC_SEQ // Q_TILE_A),
        in_specs=[q_blk, seq_blk, seq_blk, cache_blk, cache_blk],
        out_specs=q_blk,
        out_shape=jax.ShapeDtypeStruct((DEC_BATCH, DEC_SEQ, WIDTH_A), BF16),
        scratch_shapes=[pltpu.VMEM((PAST_LEN + DEC_SEQ, WIDTH_A), BF16)] * 2,
        compiler_params=_params(2),
        name="attn_a_lat",
    )(qa, krep, vrep, cache_k, cache_v)


def _attn_ctx_kernel(qa_ref, ka_ref, va_ref, qc_ref, kc_ref, vc_ref, oa_ref, oc_ref):
    gw = REP_A * HEAD_DIM
    acc = {}
    chains = []
    for b in range(CTX_STEP_SEQS):
        for g in range(N_KV_A):
            sl = slice(g * gw, (g + 1) * gw)
            acc["a", b, g] = jnp.zeros((SEQ, gw), F32)
            for j in range(REP_A):
                chains.append(_head_chain(qa_ref[b, :, sl], [j], lambda b=b, sl=sl: ka_ref[b, :, sl],
                                          lambda b=b, sl=sl: va_ref[b, :, sl], acc, ("a", b, g)))
        acc["c", b] = jnp.zeros((SEQ, WIDTH_C), F32)
        for h in range(0, N_HEADS_C, 2):
            chains.append(_head_chain(qc_ref[b], [h, h + 1], lambda b=b: kc_ref[b].astype(BF16),
                                      lambda b=b: vc_ref[b].astype(BF16), acc, ("c", b)))
    _round_robin(chains, stagger=1)
    for b in range(CTX_STEP_SEQS):
        for g in range(N_KV_A):
            oa_ref[b, :, g * gw:(g + 1) * gw] = acc["a", b, g].astype(BF16)
        oc_ref[b] = acc["c", b].astype(BF16)


def _attn_ctx(qa, krep, vrep, qc, kc, vc, layer):
    blk = lambda w: pl.BlockSpec((CTX_STEP_SEQS, SEQ, w), lambda i: (i, 0, 0))
    kv_blk = pl.BlockSpec((CTX_STEP_SEQS, None, SEQ, WIDTH_C), lambda i: (i, layer, 0, 0))
    return pl.pallas_call(
        _attn_ctx_kernel,
        grid=(BATCH // CTX_STEP_SEQS,),
        in_specs=[blk(WIDTH_A), blk(WIDTH_A), blk(WIDTH_A), blk(WIDTH_C), kv_blk, kv_blk],
        out_specs=[blk(WIDTH_A), blk(WIDTH_C)],
        out_shape=[jax.ShapeDtypeStruct((BATCH, SEQ, WIDTH_A), BF16),
                   jax.ShapeDtypeStruct((BATCH, SEQ, WIDTH_C), BF16)],
        compiler_params=_params(1),
        name="attn_ctx",
    )(qa, krep, vrep, qc, kc, vc)


def _na_kernel(q_ref, k_ref, v_ref, ck_ref, cv_ref, brow_ref, o_ref, b2_ref):
    nq = NA_Q_ROWS * GRID_W
    nk = NA_K_ROWS * GRID_W
    n_pair = 2 * NA_WIN_R

    @pl.when((pl.program_id(0) == 0) & (pl.program_id(1) == 0))
    def _():
        shp = (GRID_W, 2 * GRID_W)
        c = lax.broadcasted_iota(jnp.int32, shp, 0)
        kc = lax.broadcasted_iota(jnp.int32, shp, 1) & (GRID_W - 1)
        c0 = jnp.clip(c - NA_WIN_C // 2, 0, GRID_W - NA_WIN_C)
        col_bias = jnp.where(kc < c0, NEG, jnp.where(kc >= c0 + NA_WIN_C, NEG, 0.0))
        for h in range(N_HEADS_C):
            for e in range(n_pair):
                row = jnp.broadcast_to(brow_ref[h, e:e + 1, :], shp)
                toep = pltpu.roll(row, 2 * GRID_W - (NA_WIN_C - 1), 1, stride=1, stride_axis=0)
                b2_ref[h, e] = jnp.where(col_bias < 0.0, NEG, toep * LOG2_E)

    no_bias = jnp.zeros((nq, PAST_LEN), F32)
    ck = ck_ref[...].astype(BF16)
    cv = cv_ref[...].astype(BF16)
    acc = [jnp.zeros((nq, WIDTH_C), F32) for _ in range(NA_STEP_BLOCKS)]

    def chain(blk, heads):
        qrow0 = NA_Q_ROWS * (NA_STEP_BLOCKS * pl.program_id(1) + blk)
        row0 = jnp.clip(qrow0 - NA_WIN_R // 2, 0, NA_R0_MAX)
        start = pl.multiple_of(row0 * GRID_W, GRID_W)
        qrow = qrow0 + lax.broadcasted_iota(jnp.int32, (nq, nk), 0) // GRID_W
        krow = row0 + lax.broadcasted_iota(jnp.int32, (nq, nk), 1) // GRID_W
        win0 = jnp.clip(qrow - NA_WIN_R // 2, 0, GRID_ROWS - NA_WIN_R)
        row_bias = jnp.where(krow < win0, NEG, jnp.where(krow >= win0 + NA_WIN_R, NEG, 0.0))

        def head_bias(h):
            rows = []
            for i in range(NA_Q_ROWS):
                tiles = []
                for m in range(NA_K_ROWS // 2):
                    e = jnp.clip(row0 + 2 * m - (qrow0 + i) + NA_WIN_R, 0, n_pair - 1)
                    tiles.append(b2_ref[h, e])
                rows.append(jnp.concatenate(tiles, axis=1))
            return jnp.concatenate([jnp.concatenate(rows, axis=0) + row_bias, no_bias], axis=1)

        return _head_chain(q_ref[blk * nq:(blk + 1) * nq, :], heads,
                           lambda: jnp.concatenate([k_ref[pl.ds(start, nk), :], ck], axis=0),
                           lambda: jnp.concatenate([v_ref[pl.ds(start, nk), :], cv], axis=0), acc, blk,
                           bias=lambda: jnp.concatenate([head_bias(h) for h in heads], axis=0))

    pairs = [[h, h + 1] for h in range(0, N_HEADS_C, 2)]
    _round_robin([chain(blk, heads) for blk in range(NA_STEP_BLOCKS) for heads in pairs], stagger=1)
    for blk in range(NA_STEP_BLOCKS):
        o_ref[blk * nq:(blk + 1) * nq, :] = acc[blk].astype(BF16)


def _na_bias_rows(tbl):
    pad = jnp.pad(tbl, ((0, 0), (1, 1), (0, GRID_W - tbl.shape[-1])))
    return jnp.concatenate([pad[:, :-1], pad[:, 1:]], axis=-1)


def _na_lat(qc, kc, vc, cache_k, cache_v, brow, layer):
    step_rows = NA_STEP_BLOCKS * NA_Q_ROWS * GRID_W
    seq_blk = pl.BlockSpec((None, DEC_SEQ, WIDTH_C), lambda b, j: (b, 0, 0))
    cache_blk = pl.BlockSpec((None, None, PAST_LEN, WIDTH_C), lambda b, j: (b, layer, 0, 0))
    q_blk = pl.BlockSpec((None, step_rows, WIDTH_C), lambda b, j: (b, j, 0))
    return pl.pallas_call(
        _na_kernel,
        grid=(DEC_BATCH, DEC_SEQ // step_rows),
        in_specs=[q_blk, seq_blk, seq_blk, cache_blk, cache_blk, _const_spec(brow.shape)],
        out_specs=q_blk,
        out_shape=jax.ShapeDtypeStruct((DEC_BATCH, DEC_SEQ, WIDTH_C), BF16),
        scratch_shapes=[pltpu.VMEM((N_HEADS_C, 2 * NA_WIN_R, GRID_W, 2 * GRID_W), F32)],
        compiler_params=_params(2),
        name="na_lat",
    )(qc, kc, vc, cache_k, cache_v, brow)


def _scan_kernel(uf_ref, ub_ref, bmat_ref, lam_ref, cmat_ref, h0_ref, yf_ref, yb_ref, hfin_ref,
                 *scratch, nb, lane_w):
    n_blk = SCAN_ROWS // SCAN_BLOCK
    hf_blk, hb_blk, st_ref = scratch[:SCAN_RING], scratch[SCAN_RING:2 * SCAN_RING], scratch[2 * SCAN_RING]
    steps = SCAN_BLOCK // nb
    n_lb = SSM_LANES // lane_w

    @pl.when(pl.program_id(0) == 0)
    def _():
        st_ref[...] = h0_ref[...]

    lanes = [(slice(lb * lane_w, (lb + 1) * lane_w), slice(SSM_LANES + lb * lane_w, SSM_LANES + (lb + 1) * lane_w))
             for lb in range(n_lb)]
    state = [[st_ref[d, c, :, re] for d in range(2) for c in range(2)] for re, _ in lanes]

    def block(i):
        f0 = i * SCAN_BLOCK
        b0 = (n_blk - 1 - i) * SCAN_BLOCK
        hf_ref, hb_ref = hf_blk[i % SCAN_RING], hb_blk[i % SCAN_RING]
        uf = uf_ref[f0:f0 + SCAN_BLOCK, :].astype(BF16)
        ub = ub_ref[b0:b0 + SCAN_BLOCK, :].astype(BF16)
        for cols in lanes:
            for sl in cols:
                hf_ref[:, sl] = _dot(uf, bmat_ref[0, :, sl])
                hb_ref[:, sl] = _dot(ub, bmat_ref[1, :, sl])
            yield
        for lb, (re, im) in enumerate(lanes):
            if lb:
                yield
            lam = [[jnp.broadcast_to(lam_ref[d, c, :, re], (nb, lane_w)) for c in range(2)] for d in range(2)]
            fr, fi, br, bi = state[lb]
            for k in range(steps):
                rf = slice(k * nb, (k + 1) * nb)
                rb = slice((steps - 1 - k) * nb, (steps - k) * nb)
                fr, fi = (lam[0][0] * fr - lam[0][1] * fi + hf_ref[rf, re],
                          lam[0][0] * fi + lam[0][1] * fr + hf_ref[rf, im])
                br, bi = (lam[1][0] * br - lam[1][1] * bi + hb_ref[rb, re],
                          lam[1][0] * bi + lam[1][1] * br + hb_ref[rb, im])
                hf_ref[rf, re] = fr
                hf_ref[rf, im] = fi
                hb_ref[rb, re] = br
                hb_ref[rb, im] = bi
            state[lb] = [fr, fi, br, bi]
        yf = yb = 0.0
        for cols in lanes:
            yield
            for sl in cols:
                yf = yf + _dot(hf_ref[:, sl].astype(BF16), cmat_ref[0, sl, :])
                yb = yb + _dot(hb_ref[:, sl].astype(BF16), cmat_ref[1, sl, :])
        yf_ref[f0:f0 + SCAN_BLOCK, :] = yf
        yb_ref[b0:b0 + SCAN_BLOCK, :] = yb

    _round_robin([block(i) for i in range(n_blk)], stagger=n_lb)
    for (re, _), vals in zip(lanes, state):
        for d in range(2):
            for c in range(2):
                st_ref[d, c, :, re] = vals[2 * d + c]
    hfin_ref[...] = st_ref[...]


def _scan_pair_kernel(uf_ref, ub_ref, bmat_ref, lam_ref, cmat_ref, h0_ref, yf_ref, yb_ref, *scratch, lane_w):
    half = DEC_BATCH
    tile = 2 * half
    n_blk = SCAN_ROWS // SCAN_BLOCK
    hf_blk, hb_blk, st_ref = scratch[:SCAN_RING], scratch[SCAN_RING:2 * SCAN_RING], scratch[2 * SCAN_RING]
    tiles = SCAN_BLOCK // tile
    n_lb = SSM_LANES // lane_w

    @pl.when(pl.program_id(0) == 0)
    def _():
        st_ref[...] = h0_ref[...]

    top = lax.broadcasted_iota(jnp.int32, (tile, lane_w), 0) < half
    swap = lambda a: pltpu.roll(a, half, 0)
    lanes = [(slice(lb * lane_w, (lb + 1) * lane_w), slice(SSM_LANES + lb * lane_w, SSM_LANES + (lb + 1) * lane_w))
             for lb in range(n_lb)]
    state = [[st_ref[0, :, re], st_ref[1, :, re]] for re, _ in lanes]

    def block(i):
        f0 = i * SCAN_BLOCK
        b0 = (n_blk - 1 - i) * SCAN_BLOCK
        hf_ref, hb_ref = hf_blk[i % SCAN_RING], hb_blk[i % SCAN_RING]
        uf = uf_ref[f0:f0 + SCAN_BLOCK, :].astype(BF16)
        ub = ub_ref[b0:b0 + SCAN_BLOCK, :].astype(BF16)
        for cols in lanes:
            for sl in cols:
                hf_ref[:, sl] = _dot(uf, bmat_ref[0, :, sl])
                hb_ref[:, sl] = _dot(ub, bmat_ref[1, :, sl])
            yield
        for lb, (re, im) in enumerate(lanes):
            if lb:
                yield
            la_r = lam_ref[0, :, re]
            la_i = lam_ref[1, :, re]
            lb_r = swap(la_r)
            lb_i = swap(la_i)
            sr, si = state[lb]
            for m in range(tiles):
                rf = slice(m * tile, (m + 1) * tile)
                rb = slice((tiles - 1 - m) * tile, (tiles - m) * tile)
                fr, fi = hf_ref[rf, re], hf_ref[rf, im]
                br, bi = hb_ref[rb, re], hb_ref[rb, im]
                vr = la_r * sr - la_i * si + jnp.where(top, fr, br)
                vi = la_r * si + la_i * sr + jnp.where(top, fi, bi)
                tr = swap(vr)
                ti = swap(vi)
                wr = lb_r * tr - lb_i * ti + jnp.where(top, br, fr)
                wi = lb_r * ti + lb_i * tr + jnp.where(top, bi, fi)
                hf_ref[rf, re] = jnp.where(top, vr, wr)
                hf_ref[rf, im] = jnp.where(top, vi, wi)
                hb_ref[rb, re] = jnp.where(top, wr, vr)
                hb_ref[rb, im] = jnp.where(top, wi, vi)
                sr, si = swap(wr), swap(wi)
            state[lb] = [sr, si]
        yf = yb = 0.0
        for cols in lanes:
            yield
            for sl in cols:
                yf = yf + _dot(hf_ref[:, sl].astype(BF16), cmat_ref[0, sl, :])
                yb = yb + _dot(hb_ref[:, sl].astype(BF16), cmat_ref[1, sl, :])
        yf_ref[f0:f0 + SCAN_BLOCK, :] = yf
        yb_ref[b0:b0 + SCAN_BLOCK, :] = yb

    _round_robin([block(i) for i in range(n_blk)], stagger=n_lb)
    for (re, _), (sr, si) in zip(lanes, state):
        st_ref[0, :, re] = sr
        st_ref[1, :, re] = si


def _scan_specs(n_rows):
    n = n_rows // SCAN_ROWS
    fwd = pl.BlockSpec((SCAN_ROWS, SSM_WIDTH), lambda j: (j, 0))
    bwd = pl.BlockSpec((SCAN_ROWS, SSM_WIDTH), lambda j: (n - 1 - j, 0))
    y_shape = jax.ShapeDtypeStruct((n_rows, SSM_WIDTH), F32)
    buf = [pltpu.VMEM((SCAN_BLOCK, 2 * SSM_LANES), F32)] * (2 * SCAN_RING)
    return n, fwd, bwd, y_shape, buf


def _scan_ctx(u_rows, bmat, lam, cmat, h0):
    n, fwd, bwd, y_shape, buf = _scan_specs(u_rows.shape[0])
    st_shape = (2, 2, BATCH, SSM_LANES)
    return pl.pallas_call(
        functools.partial(_scan_kernel, nb=BATCH, lane_w=256),
        grid=(n,),
        in_specs=[fwd, bwd, _const_spec(bmat.shape), _const_spec(lam.shape), _const_spec(cmat.shape),
                  _const_spec(st_shape)],
        out_specs=[fwd, bwd, pl.BlockSpec(st_shape, lambda j: (0, 0, 0, 0))],
        out_shape=[y_shape, y_shape, jax.ShapeDtypeStruct(st_shape, F32)],
        scratch_shapes=buf + [pltpu.VMEM(st_shape, F32)],
        compiler_params=_params(1),
        name="scan_ctx",
    )(u_rows, u_rows, bmat, lam, cmat, h0)


def _scan_lat(u_rows, bmat, lam_pair, cmat, h0_pair):
    n, fwd, bwd, y_shape, buf = _scan_specs(u_rows.shape[0])
    st_shape = (2, 2 * DEC_BATCH, SSM_LANES)
    return pl.pallas_call(
        functools.partial(_scan_pair_kernel, lane_w=512),
        grid=(n,),
        in_specs=[fwd, bwd, _const_spec(bmat.shape), _const_spec(st_shape), _const_spec(cmat.shape),
                  _const_spec(st_shape)],
        out_specs=[fwd, bwd],
        out_shape=[y_shape, y_shape],
        scratch_shapes=buf + [pltpu.VMEM(st_shape, F32)],
        compiler_params=_params(1),
        name="scan_lat",
    )(u_rows, u_rows, bmat, lam_pair, cmat, h0_pair)


def _ssm_discretise(lam_re, lam_im, log_step, b_re, b_im, c_re, c_im):
    step = jnp.exp(log_step.astype(F32))[..., None]
    lr, li = lam_re.astype(F32), lam_im.astype(F32)
    mag = jnp.exp(lr * step)
    bar_r = mag * jnp.cos(li * step)
    bar_i = mag * jnp.sin(li * step)
    den = lr * lr + li * li
    coef_r = (((bar_r - 1) * lr + bar_i * li) / den)[..., None]
    coef_i = ((bar_i * lr - (bar_r - 1) * li) / den)[..., None]
    br, bi = b_re.astype(F32), b_im.astype(F32)
    bbar_r = coef_r * br - coef_i * bi
    bbar_i = coef_r * bi + coef_i * br
    eye = jnp.eye(SSM_GROUPS, dtype=F32)
    blk_b = lambda a: jnp.einsum('dgpc,gh->dgchp', a, eye).reshape(2, SSM_WIDTH, SSM_LANES)
    bmat = jnp.concatenate([blk_b(bbar_r), blk_b(bbar_i)], axis=-1)
    blk_c = lambda a: jnp.einsum('dgcp,gh->dgphc', a, eye).reshape(2, SSM_LANES, SSM_WIDTH)
    cmat = jnp.concatenate([blk_c(c_re.astype(F32)), -blk_c(c_im.astype(F32))], axis=1)
    lam_flat = jnp.stack([bar_r, bar_i], axis=1).reshape(2, 2, 1, SSM_LANES)
    return bmat.astype(BF16), lam_flat, cmat.astype(BF16)


def _merge_kernel(x_ref, oa_ref, yf_ref, yb_ref, u_ref, oc_ref, mod_ref, g1_ref, wg_ref, d_ref, wglu_ref,
                  wa_ref, wb_ref, wc_ref, wo_ref, g2_ref, wgu_ref, wd_ref, fg_ref, o_ref, *slabs, final):
    nb, tt, _ = x_ref.shape
    tc = tt // ROW_CHAINS
    rows = nb * tc
    mod = mod_ref[...]
    shift1 = mod[:, :, 0:D_MODEL]
    scale1 = mod[:, :, D_MODEL:2 * D_MODEL]
    gate1 = mod[:, :, 2 * D_MODEL:3 * D_MODEL]
    shift2 = mod[:, :, 3 * D_MODEL:4 * D_MODEL]
    scale2 = mod[:, :, 4 * D_MODEL:5 * D_MODEL]
    gate2 = mod[:, :, 5 * D_MODEL:6 * D_MODEL]
    def chain(c):
        ts = slice(c * tc, (c + 1) * tc)
        rs = slice(c * rows, (c + 1) * rows)
        flat = lambda ref, lo=0, hi=None: ref[:, ts, lo:hi].reshape(rows, -1)
        y = _gelu_tanh(yf_ref[rs, :] + yb_ref[rs, :] + d_ref[...] * u_ref[rs, :])
        ob = y * jax.nn.sigmoid(_dot(y.astype(BF16), wglu_ref[...]))
        ob = _to_batch_major(ob, slabs[2 * c:2 * c + 2], nb)
        x = x_ref[:, ts, :]
        h1 = (_rms(x, g1_ref[...]) * (1 + scale1) + shift1).reshape(rows, D_MODEL).astype(BF16)
        yield
        gate = lambda i: jax.nn.sigmoid(_dot(h1, wg_ref[0, :, i * D_MODEL:(i + 1) * D_MODEL]))
        merged = (gate(0) * _dot(flat(oa_ref), wa_ref[...])
                  + gate(1) * _dot(ob.astype(BF16), wb_ref[...])
                  + gate(2) * _dot(flat(oc_ref), wc_ref[...]))
        yield
        x1 = x + gate1 * _dot(merged.astype(BF16), wo_ref[...]).reshape(nb, tc, D_MODEL)
        h2 = _rms(x1, g2_ref[...]) * (1 + scale2) + shift2
        yield
        gu = _dot(h2.reshape(rows, D_MODEL).astype(BF16), wgu_ref[...])
        act = _silu(gu[:, :D_FF]) * gu[:, D_FF:]
        yield
        x2 = x1 + gate2 * _dot(act.astype(BF16), wd_ref[...]).reshape(nb, tc, D_MODEL)
        if final:
            x2 = _rms(x2, fg_ref[...])
        o_ref[:, ts, :] = x2

    _round_robin([chain(c) for c in range(ROW_CHAINS)])


def _merge(x, oa, yf, yb, u_rows, oc, mod_rows, g1, w_in, weights, final_g, layer, *, name):
    final = layer == DEPTH - 1
    nb, seq, _ = x.shape
    tt = ROW_TILE // nb
    slab = lambda w: pl.BlockSpec((nb, tt, w), lambda j: (0, j, 0))
    tmaj = pl.BlockSpec((ROW_TILE, SSM_WIDTH), lambda j: (j, 0))
    return pl.pallas_call(
        functools.partial(_merge_kernel, final=final),
        grid=(seq // tt,),
        in_specs=[slab(D_MODEL), slab(WIDTH_A), tmaj, tmaj, tmaj, slab(WIDTH_C), _const_spec(mod_rows.shape),
                  _layer_spec(g1, layer), _layer_cols_spec(w_in, layer, OFF_G, N_BRANCH * D_MODEL)]
                 + [_layer_spec(w, layer) for w in weights]
                 + [_const_spec(final_g.shape)],
        out_specs=slab(D_MODEL),
        out_shape=jax.ShapeDtypeStruct((nb, seq, D_MODEL), F32),
        scratch_shapes=[pltpu.VMEM((ROW_TILE // ROW_CHAINS, LANES), F32)] * (2 * ROW_CHAINS),
        compiler_params=_params(1),
        name=name + ("_final" if final else ""),
    )(x, oa, yf, yb, u_rows, oc, mod_rows, g1, w_in, *weights, final_g)


def _rope_tables():
    t = jnp.arange(DEC_SEQ)
    row = (t // GRID_W).astype(F32)
    col = (t % GRID_W).astype(F32)
    inv = 1.0 / (ROPE_THETA ** (jnp.arange(ROT_FREQS, dtype=F32) / ROT_FREQS))
    ar = row[:, None] * inv[None]
    ac = col[:, None] * inv[None]
    cos = jnp.concatenate([jnp.cos(ar), jnp.cos(ar), jnp.cos(ac), jnp.cos(ac)], axis=-1)
    sin = jnp.concatenate([-jnp.sin(ar), jnp.sin(ar), -jnp.sin(ac), jnp.sin(ac)], axis=-1)
    return jnp.tile(cos, (1, 2)), jnp.tile(sin, (1, 2))


def kernel(x_prompt, x_sample, c, cache_ga_k, cache_ga_v, cache_na_k, cache_na_v, state_ssm, c_ctx, w_mod, b_mod, norm1_g, w_in, qn_g, kn_g, ssm_lam_re, ssm_lam_im, ssm_log_step, ssm_b_re, ssm_b_im, ssm_c_re, ssm_c_im, ssm_d, ssm_w_glu, na_bias, w_br_a, w_br_b, w_br_c, w_out, norm2_g, w_ffn_gu, w_ffn_d, final_g):
    cvec = jnp.concatenate([c_ctx[None, :], c, jnp.zeros((N_MOD_ROWS - 1 - DEC_BATCH, D_MODEL), F32)], axis=0)
    mod = _adaln(cvec, w_mod, b_mod).reshape(DEPTH, N_MOD_ROWS, 1, 6 * D_MODEL)

    seg = jnp.kron(jnp.eye(MXU_DIM // HEAD_DIM, dtype=F32),
                   jnp.full((HEAD_DIM, HEAD_DIM), 1.0 / HEAD_DIM, F32)).astype(BF16)
    rope_tabs = _rope_tables()
    fg = final_g.reshape(1, D_MODEL)
    ck_a = cache_ga_k.reshape(DEC_BATCH, DEPTH, PAST_LEN, KV_WIDTH_A)
    cv_a = cache_ga_v.reshape(DEC_BATCH, DEPTH, PAST_LEN, KV_WIDTH_A)
    ck_c = cache_na_k.reshape(DEC_BATCH, DEPTH, PAST_LEN, WIDTH_C)
    cv_c = cache_na_v.reshape(DEC_BATCH, DEPTH, PAST_LEN, WIDTH_C)
    zero_state = jnp.zeros((2, 2, BATCH, SSM_LANES), F32)

    row = lambda p: p.reshape(DEPTH, 1, p.shape[-1])
    g1 = row(norm1_g)
    w_in_b = w_in.astype(BF16)
    qg = row(jnp.tile(qn_g, (1, N_HEADS_A)))
    kg = row(jnp.tile(kn_g, (1, N_KV_A)))
    merge_w = [row(ssm_d), ssm_w_glu.astype(BF16), w_br_a.astype(BF16), w_br_b.astype(BF16), w_br_c.astype(BF16),
               w_out.astype(BF16), row(norm2_g), w_ffn_gu.astype(BF16), w_ffn_d.astype(BF16)]

    xp, xs = x_prompt, x_sample
    cache = ()
    ssm_st = []
    for l in range(DEPTH):
        bmat, lam, cmat = _ssm_discretise(ssm_lam_re[l], ssm_lam_im[l], ssm_log_step[l], ssm_b_re[l], ssm_b_im[l],
                                          ssm_c_re[l], ssm_c_im[l])
        mod_ctx = mod[l, 0:1]
        mod_lat = mod[l, 1:1 + DEC_BATCH]

        qa, krep, vrep, ka, va, u_rows, qc, kc, vc = _inproj(xp, mod_ctx, g1, w_in_b, seg, qg, kg, l,
                                                             prev_cache=cache)
        cache = (ka, va, kc, vc)
        oa, oc = _attn_ctx(qa, krep, vrep, qc, kc, vc, l)
        yf, yb, hfin = _scan_ctx(u_rows, bmat, lam, cmat, zero_state)
        xp = _merge(xp, oa, yf, yb, u_rows, oc, mod_ctx, g1, w_in_b, merge_w, fg, l, name="merge_ctx")
        ssm_st.append(jnp.transpose(hfin, (2, 0, 1, 3)))

        qa, krep, vrep, u_rows, qc, kc, vc = _inproj(xs, mod_lat, g1, w_in_b, seg, qg, kg, l, rope_tabs=rope_tabs)
        oa = _attn_a_lat(qa, krep, vrep, ck_a, cv_a, l)
        oc = _na_lat(qc, kc, vc, ck_c, cv_c, _na_bias_rows(na_bias[l]), l)
        h0 = jnp.transpose(state_ssm[:, l].reshape(DEC_BATCH, 2, 2, SSM_LANES), (2, 1, 0, 3))
        h0 = h0.reshape(2, 2 * DEC_BATCH, SSM_LANES)
        lam_pair = jnp.broadcast_to(jnp.transpose(lam, (1, 0, 2, 3)), (2, 2, DEC_BATCH, SSM_LANES))
        lam_pair = lam_pair.reshape(2, 2 * DEC_BATCH, SSM_LANES)
        yf, yb = _scan_lat(u_rows, bmat, lam_pair, cmat, h0)
        xs = _merge(xs, oa, yf, yb, u_rows, oc, mod_lat, g1, w_in_b, merge_w, fg, l, name="merge_lat")

    ga_k, ga_v, na_k, na_v = cache
    new_ssm = jnp.stack(ssm_st, axis=1).reshape(BATCH, DEPTH, 2, 2, SSM_GROUPS, SSM_STATE)
    return (xp, xs,
            ga_k.reshape(BATCH, DEPTH, SEQ, N_KV_A, HEAD_DIM), ga_v.reshape(BATCH, DEPTH, SEQ, N_KV_A, HEAD_DIM),
            na_k.reshape(BATCH, DEPTH, SEQ, N_HEADS_C, HEAD_DIM), na_v.reshape(BATCH, DEPTH, SEQ, N_HEADS_C, HEAD_DIM),
            new_ssm)
```

```python
import functools
import math

import jax
import jax.numpy as jnp
from jax import lax
from jax.experimental import pallas as pl
from jax.experimental.pallas import tpu as pltpu

D_MODEL = 1024
BATCH = 16
SEQ = 256
DEPTH = 2
DEC_BATCH = 4
DEC_SEQ = 2048
PAST_LEN = 256
GRID_W = 64
GRID_ROWS = DEC_SEQ // GRID_W
HEAD_DIM = 64
N_HEADS_A = 8
N_KV_A = 2
REP_A = N_HEADS_A // N_KV_A
N_HEADS_C = 4
SSM_WIDTH = 256
SSM_GROUP = 16
SSM_GROUPS = SSM_WIDTH // SSM_GROUP
SSM_STATE = 64
SSM_LANES = SSM_GROUPS * SSM_STATE
NA_WIN_R = 8
NA_WIN_C = 16
D_FF = -(-8 * D_MODEL // (3 * 256)) * 256
ROPE_THETA = 10000.0
ROT_HALF = HEAD_DIM // 2
ROT_FREQS = ROT_HALF // 2
WIDTH_A = N_HEADS_A * HEAD_DIM
KV_WIDTH_A = N_KV_A * HEAD_DIM
WIDTH_C = N_HEADS_C * HEAD_DIM
N_BRANCH = 3
IN_WIDTH = WIDTH_A + 2 * KV_WIDTH_A + SSM_WIDTH + 3 * WIDTH_C + N_BRANCH * D_MODEL
EPS = 1e-6

OFF_QA = 0
OFF_KA = OFF_QA + WIDTH_A
OFF_VA = OFF_KA + KV_WIDTH_A
OFF_U = OFF_VA + KV_WIDTH_A
OFF_QC = OFF_U + SSM_WIDTH
OFF_KC = OFF_QC + WIDTH_C
OFF_VC = OFF_KC + WIDTH_C
OFF_G = OFF_VC + WIDTH_C

N_MOD_ROWS = 8
ROW_TILE = 512
INPROJ_ROWS = 1024
Q_TILE_A = 512
Q_CHAIN_ROWS = 256
NA_Q_ROWS = 2
NA_K_ROWS = 10
NA_R0_MAX = GRID_ROWS - NA_K_ROWS
NA_STEP_BLOCKS = 8
CTX_STEP_SEQS = 4
ROW_CHAINS = 2
SCAN_ROWS = 2048
SCAN_BLOCK = 256
SCAN_RING = 3
LOG2_E = math.log2(math.e)
Q_SCALE = HEAD_DIM ** -0.5 * LOG2_E
LANES = 128
MXU_DIM = 256
NEG = -1e30
VMEM_LIMIT_V7X = 56 * 1024 * 1024

F32 = jnp.float32
BF16 = jnp.bfloat16


def _dot(a, b):
    return jnp.dot(a, b, preferred_element_type=F32)


def _dot_t(a, b):
    return lax.dot_general(a, b, (((1,), (1,)), ((), ())), preferred_element_type=F32)


def _params(n_axes):
    return pltpu.CompilerParams(dimension_semantics=("arbitrary",) * n_axes,
                                vmem_limit_bytes=VMEM_LIMIT_V7X)


def _const_spec(shape):
    nd = len(shape)
    return pl.BlockSpec(shape, lambda *_: (0,) * nd, pipeline_mode=pl.Buffered(1))


def _layer_spec(stacked, layer):
    shape = stacked.shape[1:]
    return pl.BlockSpec((None,) + shape, lambda *_: (layer,) + (0,) * len(shape), pipeline_mode=pl.Buffered(1))


def _layer_cols_spec(stacked, layer, start, width):
    k = stacked.shape[1]
    if start == 0:
        return pl.BlockSpec((None, k, width), lambda *_: (layer, 0, 0), pipeline_mode=pl.Buffered(1))
    return pl.BlockSpec((pl.Element(1), pl.Element(k), pl.Element(width)), lambda *_: (layer, 0, start),
                        pipeline_mode=pl.Buffered(1))


def _rms(x, g):
    return x * lax.rsqrt(jnp.mean(x * x, axis=-1, keepdims=True) + EPS) * g


def _silu(x):
    return x * jax.nn.sigmoid(x)


def _gelu_tanh(x):
    c = math.sqrt(2.0 / math.pi)
    return x * (0.5 * (1.0 + jnp.tanh(c * (x + 0.044715 * (x * x * x)))))


def _seg_rms(x, seg, g):
    w = seg.shape[0]
    x2 = x * x
    hi = x2.astype(BF16)
    lo = (x2 - hi.astype(F32)).astype(BF16)
    ms = jnp.concatenate([_dot(hi[:, c:c + w], seg) + _dot(lo[:, c:c + w], seg)
                          for c in range(0, x.shape[1], w)], axis=1)
    return x * lax.rsqrt(ms + EPS) * g


def _rope(x, cos, sin_signed):
    w = x.shape[-1]
    lane = lax.broadcasted_iota(jnp.int32, x.shape, 1)
    first = (lane & ROT_FREQS) == 0
    partner = jnp.where(first, pltpu.roll(x, w - ROT_FREQS, 1), pltpu.roll(x, ROT_FREQS, 1))
    return x * cos + partner * sin_signed


def _rep_heads(kv):
    lane = lax.broadcasted_iota(jnp.int32, kv.shape, 1)
    swapped = pltpu.roll(kv, HEAD_DIM, 1)
    lo = lane < HEAD_DIM
    h0 = jnp.where(lo, kv, swapped)
    h1 = jnp.where(lo, swapped, kv)
    return jnp.concatenate([h0, h0, h1, h1], axis=1)


def _round_robin(chains, stagger=0):
    done = [False] * len(chains)
    rnd = 0
    while not all(done):
        for i, ch in enumerate(chains):
            if not done[i] and rnd >= i * stagger:
                try:
                    next(ch)
                except StopIteration:
                    done[i] = True
        rnd += 1


def _to_time_major(val, slabs, nb):
    tt = val.shape[0] // nb
    for s, slab in enumerate(slabs):
        for b in range(nb):
            slab[pl.ds(b, tt, stride=nb), :] = val[b * tt:(b + 1) * tt, s * LANES:(s + 1) * LANES]
    return jnp.concatenate([slab[...] for slab in slabs], axis=1)


def _to_batch_major(val, slabs, nb):
    tt = val.shape[0] // nb
    for s, slab in enumerate(slabs):
        slab[...] = val[:, s * LANES:(s + 1) * LANES]
    return jnp.concatenate(
        [jnp.concatenate([slab[pl.ds(b, tt, stride=nb), :] for slab in slabs], axis=1) for b in range(nb)], axis=0)


def _adaln_kernel(c_ref, w_ref, b_ref, o_ref):
    s = _silu(c_ref[...])
    o_ref[...] = _dot(s.astype(BF16), w_ref[...].astype(BF16)) + b_ref[...]


def _adaln(cvec, w_mod, b_mod):
    n_col = 6 * D_MODEL
    tn = n_col // 4
    return pl.pallas_call(
        _adaln_kernel,
        grid=(DEPTH, n_col // tn),
        in_specs=[pl.BlockSpec((N_MOD_ROWS, D_MODEL), lambda l, n: (0, 0)),
                  pl.BlockSpec((None, D_MODEL, tn), lambda l, n: (l, 0, n)),
                  pl.BlockSpec((None, 1, tn), lambda l, n: (l, 0, n))],
        out_specs=pl.BlockSpec((None, N_MOD_ROWS, tn), lambda l, n: (l, 0, n)),
        out_shape=jax.ShapeDtypeStruct((DEPTH, N_MOD_ROWS, n_col), F32),
        compiler_params=_params(2),
        name="adaln",
    )(cvec, w_mod, b_mod.reshape(DEPTH, 1, n_col))


def _inproj_kernel(*refs, latent, n_alias):
    if latent:
        (x_ref, xn_ref, mod_ref, g1_ref, w_ref, seg_ref, qg_ref, kg_ref, cos_ref, sin_ref,
         qa_ref, krep_ref, vrep_ref, u_ref, qc_ref, kc_ref, vc_ref, hb_ref, slab0, slab1) = refs
    else:
        x_ref, xn_ref, mod_ref, g1_ref, w_ref, seg_ref, qg_ref, kg_ref = refs[:8]
        (qa_ref, krep_ref, vrep_ref, ka_ref, va_ref, u_ref, qc_ref, kc_ref, vc_ref,
         hb_ref, slab0, slab1) = refs[8 + n_alias:]
    nb, tt, _ = x_ref.shape
    rows = nb * tt
    mod = mod_ref[...]
    shift = mod[:, :, 0:D_MODEL]
    scale = mod[:, :, D_MODEL:2 * D_MODEL]
    seg = seg_ref[...]
    slot = pl.program_id(0) % 2

    def normed(ref):
        h = _rms(ref[...], g1_ref[...]) * (1 + scale) + shift
        return h.reshape(rows, D_MODEL).astype(BF16)

    def put(ref, val):
        val = val.reshape(nb, tt, ref.shape[-1]).astype(ref.dtype)
        if len(ref.shape) == 3:
            ref[...] = val
        else:
            ref[:, 0] = val
            ref[:, 1:] = jnp.zeros((nb, DEPTH - 1, tt, ref.shape[-1]), ref.dtype)

    @pl.when(pl.program_id(0) == 0)
    def _():
        hb_ref[0] = normed(x_ref)

    def project():
        hb = hb_ref[slot]
        u_ref[...] = _to_time_major(_dot(hb, w_ref[:, OFF_U:OFF_QC]), (slab0, slab1), nb)
        put(qc_ref, _dot(hb, w_ref[:, OFF_QC:OFF_KC]) * Q_SCALE)
        put(kc_ref, _dot(hb, w_ref[:, OFF_KC:OFF_VC]))
        put(vc_ref, _dot(hb, w_ref[:, OFF_VC:OFF_G]))
        yield
        qa = _seg_rms(_dot(hb, w_ref[:, OFF_QA:OFF_KA]), seg, qg_ref[...])
        ka = _seg_rms(_dot(hb, w_ref[:, OFF_KA:OFF_VA]), seg[0:KV_WIDTH_A, 0:KV_WIDTH_A], kg_ref[...])
        va = _dot(hb, w_ref[:, OFF_VA:OFF_U])
        if latent:
            cos = jnp.concatenate([cos_ref[...]] * nb, axis=0)
            sin = jnp.concatenate([sin_ref[...]] * nb, axis=0)
            qa = _rope(qa, jnp.concatenate([cos] * REP_A, axis=1), jnp.concatenate([sin] * REP_A, axis=1))
            ka = _rope(ka, cos, sin)
        else:
            put(ka_ref, ka)
            put(va_ref, va)
        put(qa_ref, qa * Q_SCALE)
        put(krep_ref, _rep_heads(ka))
        put(vrep_ref, _rep_heads(va))

    def prepare_next():
        yield
        hb_ref[1 - slot] = normed(xn_ref)

    _round_robin([project(), prepare_next()])


def _inproj(x, mod_rows, g1, w_in, seg, qg, kg, layer, *, rope_tabs=None, prev_cache=()):
    latent = rope_tabs is not None
    nb, seq, _ = x.shape
    tt = INPROJ_ROWS // nb
    slab = lambda w: pl.BlockSpec((nb, tt, w), lambda j: (0, j, 0))
    act = lambda w, dt: jax.ShapeDtypeStruct((nb, seq, w), dt)
    if layer == 0:
        cache_slab = lambda w: pl.BlockSpec((nb, DEPTH, tt, w), lambda j: (0, 0, j, 0))
    else:
        cache_slab = lambda w: pl.BlockSpec((nb, None, tt, w), lambda j: (0, layer, j, 0))
    cache = lambda w: jax.ShapeDtypeStruct((nb, DEPTH, seq, w), F32)
    n_steps = seq // tt
    next_slab = pl.BlockSpec((nb, tt, D_MODEL), lambda j: (0, jnp.minimum(j + 1, n_steps - 1), 0))
    first_slab = pl.BlockSpec((nb, tt, D_MODEL), lambda j: (0, 0, 0))
    in_specs = [first_slab, next_slab, _const_spec(mod_rows.shape), _layer_spec(g1, layer),
                _layer_cols_spec(w_in, layer, 0, OFF_G), _const_spec(seg.shape), _layer_spec(qg, layer),
                _layer_spec(kg, layer)]
    args = [x, x, mod_rows, g1, w_in, seg, qg, kg]
    out_shape = [act(WIDTH_A, BF16), act(REP_A * KV_WIDTH_A, BF16), act(REP_A * KV_WIDTH_A, BF16)]
    out_specs = [slab(WIDTH_A), slab(REP_A * KV_WIDTH_A), slab(REP_A * KV_WIDTH_A)]
    aliases = {}
    if latent:
        in_specs += [pl.BlockSpec((tt, 2 * HEAD_DIM), lambda j: (j, 0))] * 2
        args += list(rope_tabs)
        kv_c = [act(WIDTH_C, BF16)] * 2
        kv_c_specs = [slab(WIDTH_C)] * 2
    else:
        aliases = {len(args) + i: o for i, o in enumerate((3, 4, 7, 8)[:len(prev_cache)])}
        in_specs += [pl.BlockSpec(memory_space=pl.ANY)] * len(prev_cache)
        args += list(prev_cache)
        out_shape += [cache(KV_WIDTH_A)] * 2
        out_specs += [cache_slab(KV_WIDTH_A)] * 2
        kv_c = [cache(WIDTH_C)] * 2
        kv_c_specs = [cache_slab(WIDTH_C)] * 2
    out_shape += [jax.ShapeDtypeStruct((seq * nb, SSM_WIDTH), F32),
                  act(WIDTH_C, BF16)] + kv_c
    out_specs += [pl.BlockSpec((INPROJ_ROWS, SSM_WIDTH), lambda j: (j, 0)),
                  slab(WIDTH_C)] + kv_c_specs
    return pl.pallas_call(
        functools.partial(_inproj_kernel, latent=latent, n_alias=len(prev_cache)),
        grid=(seq // tt,),
        in_specs=in_specs,
        out_specs=out_specs,
        out_shape=out_shape,
        scratch_shapes=[pltpu.VMEM((2, INPROJ_ROWS, D_MODEL), BF16)] + [pltpu.VMEM((INPROJ_ROWS, LANES), F32)] * 2,
        input_output_aliases=aliases,
        compiler_params=_params(1),
        name="inproj_lat" if latent else "inproj_ctx",
    )(*args)


def _head_mask(shape, head):
    lane = lax.broadcasted_iota(jnp.int32, shape, 1)
    return (lane // HEAD_DIM) == head


def _head_chain(q, heads, keys, vals, acc, key, bias=None):
    m_rows = q.shape[0]
    masks = [_head_mask(q.shape, h) for h in heads]
    qs = jnp.concatenate([jnp.where(hm, q, jnp.zeros_like(q)) for hm in masks], axis=0)
    s = _dot_t(qs, keys())
    if bias is not None:
        s = s + bias()
    yield
    p = jnp.exp2(s - jnp.max(s, axis=-1, keepdims=True))
    l = jnp.sum(p, axis=-1, keepdims=True)
    p = p.astype(BF16)
    yield
    o = _dot(p, vals()) / l
    for i, hm in enumerate(masks):
        acc[key] = jnp.where(hm, o[i * m_rows:(i + 1) * m_rows], acc[key])


def _attn_a_kernel(q_ref, kn_ref, vn_ref, ck_ref, cv_ref, o_ref, k_ref, v_ref):
    @pl.when(pl.program_id(1) == 0)
    def _():
        k_ref[0:PAST_LEN, :] = _rep_heads(ck_ref[...]).astype(BF16)
        v_ref[0:PAST_LEN, :] = _rep_heads(cv_ref[...]).astype(BF16)
        k_ref[PAST_LEN:, :] = kn_ref[...]
        v_ref[PAST_LEN:, :] = vn_ref[...]

    gw = REP_A * HEAD_DIM
    n_sub = q_ref.shape[0] // Q_CHAIN_ROWS
    acc = {(t, g): jnp.zeros((Q_CHAIN_ROWS, gw), F32) for t in range(n_sub) for g in range(N_KV_A)}
    chains = []
    for t in range(n_sub):
        rows = slice(t * Q_CHAIN_ROWS, (t + 1) * Q_CHAIN_ROWS)
        for g in range(N_KV_A):
            sl = slice(g * gw, (g + 1) * gw)
            for j in range(REP_A):
                chains.append(_head_chain(q_ref[rows, sl], [j], lambda sl=sl: k_ref[:, sl],
                                          lambda sl=sl: v_ref[:, sl], acc, (t, g)))
    _round_robin(chains, stagger=1)
    for (t, g), val in acc.items():
        o_ref[t * Q_CHAIN_ROWS:(t + 1) * Q_CHAIN_ROWS, g * gw:(g + 1) * gw] = val.astype(BF16)


def _attn_a_lat(qa, krep, vrep, cache_k, cache_v, layer):
    seq_blk = pl.BlockSpec((None, DEC_SEQ, WIDTH_A), lambda b, t: (b, 0, 0))
    cache_blk = pl.BlockSpec((None, None, PAST_LEN, KV_WIDTH_A), lambda b, t: (b, layer, 0, 0))
    q_blk = pl.BlockSpec((None, Q_TILE_A, WIDTH_A), lambda b, t: (b, t, 0))
    return pl.pallas_call(
        _attn_a_kernel,
        grid=(DEC_BATCH, DEC_SEQ // Q_TILE_A),
        in_specs=[q_blk, seq_blk, seq_blk, cache_blk, cache_blk],
        out_specs=q_blk,
        out_shape=jax.ShapeDtypeStruct((DEC_BATCH, DEC_SEQ, WIDTH_A), BF16),
        scratch_shapes=[pltpu.VMEM((PAST_LEN + DEC_SEQ, WIDTH_A), BF16)] * 2,
        compiler_params=_params(2),
        name="attn_a_lat",
    )(qa, krep, vrep, cache_k, cache_v)


def _attn_ctx_kernel(qa_ref, ka_ref, va_ref, qc_ref, kc_ref, vc_ref, oa_ref, oc_ref):
    gw = REP_A * HEAD_DIM
    acc = {}
    chains = []
    for b in range(CTX_STEP_SEQS):
        for g in range(N_KV_A):
            sl = slice(g * gw, (g + 1) * gw)
            acc["a", b, g] = jnp.zeros((SEQ, gw), F32)
            for j in range(REP_A):
                chains.append(_head_chain(qa_ref[b, :, sl], [j], lambda b=b, sl=sl: ka_ref[b, :, sl],
                                          lambda b=b, sl=sl: va_ref[b, :, sl], acc, ("a", b, g)))
        acc["c", b] = jnp.zeros((SEQ, WIDTH_C), F32)
        for h in range(0, N_HEADS_C, 2):
            chains.append(_head_chain(qc_ref[b], [h, h + 1], lambda b=b: kc_ref[b].astype(BF16),
                                      lambda b=b: vc_ref[b].astype(BF16), acc, ("c", b)))
    _round_robin(chains, stagger=1)
    for b in range(CTX_STEP_SEQS):
        for g in range(N_KV_A):
            oa_ref[b, :, g * gw:(g + 1) * gw] = acc["a", b, g].astype(BF16)
        oc_ref[b] = acc["c", b].astype(BF16)


def _attn_ctx(qa, krep, vrep, qc, kc, vc, layer):
    blk = lambda w: pl.BlockSpec((CTX_STEP_SEQS, SEQ, w), lambda i: (i, 0, 0))
    kv_blk = pl.BlockSpec((CTX_STEP_SEQS, None, SEQ, WIDTH_C), lambda i: (i, layer, 0, 0))
    return pl.pallas_call(
        _attn_ctx_kernel,
        grid=(BATCH // CTX_STEP_SEQS,),
        in_specs=[blk(WIDTH_A), blk(WIDTH_A), blk(WIDTH_A), blk(WIDTH_C), kv_blk, kv_blk],
        out_specs=[blk(WIDTH_A), blk(WIDTH_C)],
        out_shape=[jax.ShapeDtypeStruct((BATCH, SEQ, WIDTH_A), BF16),
                   jax.ShapeDtypeStruct((BATCH, SEQ, WIDTH_C), BF16)],
        compiler_params=_params(1),
        name="attn_ctx",
    )(qa, krep, vrep, qc, kc, vc)


def _na_kernel(q_ref, k_ref, v_ref, ck_ref, cv_ref, brow_ref, o_ref, b2_ref):
    nq = NA_Q_ROWS * GRID_W
    nk = NA_K_ROWS * GRID_W
    n_pair = 2 * NA_WIN_R

    @pl.when((pl.program_id(0) == 0) & (pl.program_id(1) == 0))
    def _():
        shp = (GRID_W, 2 * GRID_W)
        c = lax.broadcasted_iota(jnp.int32, shp, 0)
        kc = lax.broadcasted_iota(jnp.int32, shp, 1) & (GRID_W - 1)
        c0 = jnp.clip(c - NA_WIN_C // 2, 0, GRID_W - NA_WIN_C)
        col_bias = jnp.where(kc < c0, NEG, jnp.where(kc >= c0 + NA_WIN_C, NEG, 0.0))
        for h in range(N_HEADS_C):
            for e in range(n_pair):
                row = jnp.broadcast_to(brow_ref[h, e:e + 1, :], shp)
                toep = pltpu.roll(row, 2 * GRID_W - (NA_WIN_C - 1), 1, stride=1, stride_axis=0)
                b2_ref[h, e] = jnp.where(col_bias < 0.0, NEG, toep * LOG2_E)

    no_bias = jnp.zeros((nq, PAST_LEN), F32)
    ck = ck_ref[...].astype(BF16)
    cv = cv_ref[...].astype(BF16)
    acc = [jnp.zeros((nq, WIDTH_C), F32) for _ in range(NA_STEP_BLOCKS)]

    def chain(blk, heads):
        qrow0 = NA_Q_ROWS * (NA_STEP_BLOCKS * pl.program_id(1) + blk)
        row0 = jnp.clip(qrow0 - NA_WIN_R // 2, 0, NA_R0_MAX)
        start = pl.multiple_of(row0 * GRID_W, GRID_W)
        qrow = qrow0 + lax.broadcasted_iota(jnp.int32, (nq, nk), 0) // GRID_W
        krow = row0 + lax.broadcasted_iota(jnp.int32, (nq, nk), 1) // GRID_W
        win0 = jnp.clip(qrow - NA_WIN_R // 2, 0, GRID_ROWS - NA_WIN_R)
        row_bias = jnp.where(krow < win0, NEG, jnp.where(krow >= win0 + NA_WIN_R, NEG, 0.0))

        def head_bias(h):
            rows = []
            for i in range(NA_Q_ROWS):
                tiles = []
                for m in range(NA_K_ROWS // 2):
                    e = jnp.clip(row0 + 2 * m - (qrow0 + i) + NA_WIN_R, 0, n_pair - 1)
                    tiles.append(b2_ref[h, e])
                rows.append(jnp.concatenate(tiles, axis=1))
            return jnp.concatenate([jnp.concatenate(rows, axis=0) + row_bias, no_bias], axis=1)

        return _head_chain(q_ref[blk * nq:(blk + 1) * nq, :], heads,
                           lambda: jnp.concatenate([k_ref[pl.ds(start, nk), :], ck], axis=0),
                           lambda: jnp.concatenate([v_ref[pl.ds(start, nk), :], cv], axis=0), acc, blk,
                           bias=lambda: jnp.concatenate([head_bias(h) for h in heads], axis=0))

    pairs = [[h, h + 1] for h in range(0, N_HEADS_C, 2)]
    _round_robin([chain(blk, heads) for blk in range(NA_STEP_BLOCKS) for heads in pairs], stagger=1)
    for blk in range(NA_STEP_BLOCKS):
        o_ref[blk * nq:(blk + 1) * nq, :] = acc[blk].astype(BF16)


def _na_bias_rows(tbl):
    pad = jnp.pad(tbl, ((0, 0), (1, 1), (0, GRID_W - tbl.shape[-1])))
    return jnp.concatenate([pad[:, :-1], pad[:, 1:]], axis=-1)


def _na_lat(qc, kc, vc, cache_k, cache_v, brow, layer):
    step_rows = NA_STEP_BLOCKS * NA_Q_ROWS * GRID_W
    seq_blk = pl.BlockSpec((None, DEC_SEQ, WIDTH_C), lambda b, j: (b, 0, 0))
    cache_blk = pl.BlockSpec((None, None, PAST_LEN, WIDTH_C), lambda b, j: (b, layer, 0, 0))
    q_blk = pl.BlockSpec((None, step_rows, WIDTH_C), lambda b, j: (b, j, 0))
    return pl.pallas_call(
        _na_kernel,
        grid=(DEC_BATCH, DEC_SEQ // step_rows),
        in_specs=[q_blk, seq_blk, seq_blk, cache_blk, cache_blk, _const_spec(brow.shape)],
        out_specs=q_blk,
        out_shape=jax.ShapeDtypeStruct((DEC_BATCH, DEC_SEQ, WIDTH_C), BF16),
        scratch_shapes=[pltpu.VMEM((N_HEADS_C, 2 * NA_WIN_R, GRID_W, 2 * GRID_W), F32)],
        compiler_params=_params(2),
        name="na_lat",
    )(qc, kc, vc, cache_k, cache_v, brow)


def _scan_kernel(uf_ref, ub_ref, bmat_ref, lam_ref, cmat_ref, h0_ref, yf_ref, yb_ref, hfin_ref,
                 *scratch, nb, lane_w):
    n_blk = SCAN_ROWS // SCAN_BLOCK
    hf_blk, hb_blk, st_ref = scratch[:SCAN_RING], scratch[SCAN_RING:2 * SCAN_RING], scratch[2 * SCAN_RING]
    steps = SCAN_BLOCK // nb
    n_lb = SSM_LANES // lane_w

    @pl.when(pl.program_id(0) == 0)
    def _():
        st_ref[...] = h0_ref[...]

    lanes = [(slice(lb * lane_w, (lb + 1) * lane_w), slice(SSM_LANES + lb * lane_w, SSM_LANES + (lb + 1) * lane_w))
             for lb in range(n_lb)]
    state = [[st_ref[d, c, :, re] for d in range(2) for c in range(2)] for re, _ in lanes]

    def block(i):
        f0 = i * SCAN_BLOCK
        b0 = (n_blk - 1 - i) * SCAN_BLOCK
        hf_ref, hb_ref = hf_blk[i % SCAN_RING], hb_blk[i % SCAN_RING]
        uf = uf_ref[f0:f0 + SCAN_BLOCK, :].astype(BF16)
        ub = ub_ref[b0:b0 + SCAN_BLOCK, :].astype(BF16)
        for cols in lanes:
            for sl in cols:
                hf_ref[:, sl] = _dot(uf, bmat_ref[0, :, sl])
                hb_ref[:, sl] = _dot(ub, bmat_ref[1, :, sl])
            yield
        for lb, (re, im) in enumerate(lanes):
            if lb:
                yield
            lam = [[jnp.broadcast_to(lam_ref[d, c, :, re], (nb, lane_w)) for c in range(2)] for d in range(2)]
            fr, fi, br, bi = state[lb]
            for k in range(steps):
                rf = slice(k * nb, (k + 1) * nb)
                rb = slice((steps - 1 - k) * nb, (steps - k) * nb)
                fr, fi = (lam[0][0] * fr - lam[0][1] * fi + hf_ref[rf, re],
                          lam[0][0] * fi + lam[0][1] * fr + hf_ref[rf, im])
                br, bi = (lam[1][0] * br - lam[1][1] * bi + hb_ref[rb, re],
                          lam[1][0] * bi + lam[1][1] * br + hb_ref[rb, im])
                hf_ref[rf, re] = fr
                hf_ref[rf, im] = fi
                hb_ref[rb, re] = br
                hb_ref[rb, im] = bi
            state[lb] = [fr, fi, br, bi]
        yf = yb = 0.0
        for cols in lanes:
            yield
            for sl in cols:
                yf = yf + _dot(hf_ref[:, sl].astype(BF16), cmat_ref[0, sl, :])
                yb = yb + _dot(hb_ref[:, sl].astype(BF16), cmat_ref[1, sl, :])
        yf_ref[f0:f0 + SCAN_BLOCK, :] = yf
        yb_ref[b0:b0 + SCAN_BLOCK, :] = yb

    _round_robin([block(i) for i in range(n_blk)], stagger=n_lb)
    for (re, _), vals in zip(lanes, state):
        for d in range(2):
            for c in range(2):
                st_ref[d, c, :, re] = vals[2 * d + c]
    hfin_ref[...] = st_ref[...]


def _scan_pair_kernel(uf_ref, ub_ref, bmat_ref, lam_ref, cmat_ref, h0_ref, yf_ref, yb_ref, *scratch, lane_w):
    half = DEC_BATCH
    tile = 2 * half
    n_blk = SCAN_ROWS // SCAN_BLOCK
    hf_blk, hb_blk, st_ref = scratch[:SCAN_RING], scratch[SCAN_RING:2 * SCAN_RING], scratch[2 * SCAN_RING]
    tiles = SCAN_BLOCK // tile
    n_lb = SSM_LANES // lane_w

    @pl.when(pl.program_id(0) == 0)
    def _():
        st_ref[...] = h0_ref[...]

    top = lax.broadcasted_iota(jnp.int32, (tile, lane_w), 0) < half
    swap = lambda a: pltpu.roll(a, half, 0)
    lanes = [(slice(lb * lane_w, (lb + 1) * lane_w), slice(SSM_LANES + lb * lane_w, SSM_LANES + (lb + 1) * lane_w))
             for lb in range(n_lb)]
    state = [[st_ref[0, :, re], st_ref[1, :, re]] for re, _ in lanes]

    def block(i):
        f0 = i * SCAN_BLOCK
        b0 = (n_blk - 1 - i) * SCAN_BLOCK
        hf_ref, hb_ref = hf_blk[i % SCAN_RING], hb_blk[i % SCAN_RING]
        uf = uf_ref[f0:f0 + SCAN_BLOCK, :].astype(BF16)
        ub = ub_ref[b0:b0 + SCAN_BLOCK, :].astype(BF16)
        for cols in lanes:
            for sl in cols:
                hf_ref[:, sl] = _dot(uf, bmat_ref[0, :, sl])
                hb_ref[:, sl] = _dot(ub, bmat_ref[1, :, sl])
            yield
        for lb, (re, im) in enumerate(lanes):
            if lb:
                yield
            la_r = lam_ref[0, :, re]
            la_i = lam_ref[1, :, re]
            lb_r = swap(la_r)
            lb_i = swap(la_i)
            sr, si = state[lb]
            for m in range(tiles):
                rf = slice(m * tile, (m + 1) * tile)
                rb = slice((tiles - 1 - m) * tile, (tiles - m) * tile)
                fr, fi = hf_ref[rf, re], hf_ref[rf, im]
                br, bi = hb_ref[rb, re], hb_ref[rb, im]
                vr = la_r * sr - la_i * si + jnp.where(top, fr, br)
                vi = la_r * si + la_i * sr + jnp.where(top, fi, bi)
                tr = swap(vr)
                ti = swap(vi)
                wr = lb_r * tr - lb_i * ti + jnp.where(top, br, fr)
                wi = lb_r * ti + lb_i * tr + jnp.where(top, bi, fi)
                hf_ref[rf, re] = jnp.where(top, vr, wr)
                hf_ref[rf, im] = jnp.where(top, vi, wi)
                hb_ref[rb, re] = jnp.where(top, wr, vr)
                hb_ref[rb, im] = jnp.where(top, wi, vi)
                sr, si = swap(wr), swap(wi)
            state[lb] = [sr, si]
        yf = yb = 0.0
        for cols in lanes:
            yield
            for sl in cols:
                yf = yf + _dot(hf_ref[:, sl].astype(BF16), cmat_ref[0, sl, :])
                yb = yb + _dot(hb_ref[:, sl].astype(BF16), cmat_ref[1, sl, :])
        yf_ref[f0:f0 + SCAN_BLOCK, :] = yf
        yb_ref[b0:b0 + SCAN_BLOCK, :] = yb

    _round_robin([block(i) for i in range(n_blk)], stagger=n_lb)
    for (re, _), (sr, si) in zip(lanes, state):
        st_ref[0, :, re] = sr
        st_ref[1, :, re] = si


def _scan_specs(n_rows):
    n = n_rows // SCAN_ROWS
    fwd = pl.BlockSpec((SCAN_ROWS, SSM_WIDTH), lambda j: (j, 0))
    bwd = pl.BlockSpec((SCAN_ROWS, SSM_WIDTH), lambda j: (n - 1 - j, 0))
    y_shape = jax.ShapeDtypeStruct((n_rows, SSM_WIDTH), F32)
    buf = [pltpu.VMEM((SCAN_BLOCK, 2 * SSM_LANES), F32)] * (2 * SCAN_RING)
    return n, fwd, bwd, y_shape, buf


def _scan_ctx(u_rows, bmat, lam, cmat, h0):
    n, fwd, bwd, y_shape, buf = _scan_specs(u_rows.shape[0])
    st_shape = (2, 2, BATCH, SSM_LANES)
    return pl.pallas_call(
        functools.partial(_scan_kernel, nb=BATCH, lane_w=256),
        grid=(n,),
        in_specs=[fwd, bwd, _const_spec(bmat.shape), _const_spec(lam.shape), _const_spec(cmat.shape),
                  _const_spec(st_shape)],
        out_specs=[fwd, bwd, pl.BlockSpec(st_shape, lambda j: (0, 0, 0, 0))],
        out_shape=[y_shape, y_shape, jax.ShapeDtypeStruct(st_shape, F32)],
        scratch_shapes=buf + [pltpu.VMEM(st_shape, F32)],
        compiler_params=_params(1),
        name="scan_ctx",
    )(u_rows, u_rows, bmat, lam, cmat, h0)


def _scan_lat(u_rows, bmat, lam_pair, cmat, h0_pair):
    n, fwd, bwd, y_shape, buf = _scan_specs(u_rows.shape[0])
    st_shape = (2, 2 * DEC_BATCH, SSM_LANES)
    return pl.pallas_call(
        functools.partial(_scan_pair_kernel, lane_w=512),
        grid=(n,),
        in_specs=[fwd, bwd, _const_spec(bmat.shape), _const_spec(st_shape), _const_spec(cmat.shape),
                  _const_spec(st_shape)],
        out_specs=[fwd, bwd],
        out_shape=[y_shape, y_shape],
        scratch_shapes=buf + [pltpu.VMEM(st_shape, F32)],
        compiler_params=_params(1),
        name="scan_lat",
    )(u_rows, u_rows, bmat, lam_pair, cmat, h0_pair)


def _ssm_discretise(lam_re, lam_im, log_step, b_re, b_im, c_re, c_im):
    step = jnp.exp(log_step.astype(F32))[..., None]
    lr, li = lam_re.astype(F32), lam_im.astype(F32)
    mag = jnp.exp(lr * step)
    bar_r = mag * jnp.cos(li * step)
    bar_i = mag * jnp.sin(li * step)
    den = lr * lr + li * li
    coef_r = (((bar_r - 1) * lr + bar_i * li) / den)[..., None]
    coef_i = ((bar_i * lr - (bar_r - 1) * li) / den)[..., None]
    br, bi = b_re.astype(F32), b_im.astype(F32)
    bbar_r = coef_r * br - coef_i * bi
    bbar_i = coef_r * bi + coef_i * br
    eye = jnp.eye(SSM_GROUPS, dtype=F32)
    blk_b = lambda a: jnp.einsum('dgpc,gh->dgchp', a, eye).reshape(2, SSM_WIDTH, SSM_LANES)
    bmat = jnp.concatenate([blk_b(bbar_r), blk_b(bbar_i)], axis=-1)
    blk_c = lambda a: jnp.einsum('dgcp,gh->dgphc', a, eye).reshape(2, SSM_LANES, SSM_WIDTH)
    cmat = jnp.concatenate([blk_c(c_re.astype(F32)), -blk_c(c_im.astype(F32))], axis=1)
    lam_flat = jnp.stack([bar_r, bar_i], axis=1).reshape(2, 2, 1, SSM_LANES)
    return bmat.astype(BF16), lam_flat, cmat.astype(BF16)


def _merge_kernel(x_ref, oa_ref, yf_ref, yb_ref, u_ref, oc_ref, mod_ref, g1_ref, wg_ref, d_ref, wglu_ref,
                  wa_ref, wb_ref, wc_ref, wo_ref, g2_ref, wgu_ref, wd_ref, fg_ref, o_ref, *slabs, final):
    nb, tt, _ = x_ref.shape
    tc = tt // ROW_CHAINS
    rows = nb * tc
    mod = mod_ref[...]
    shift1 = mod[:, :, 0:D_MODEL]
    scale1 = mod[:, :, D_MODEL:2 * D_MODEL]
    gate1 = mod[:, :, 2 * D_MODEL:3 * D_MODEL]
    shift2 = mod[:, :, 3 * D_MODEL:4 * D_MODEL]
    scale2 = mod[:, :, 4 * D_MODEL:5 * D_MODEL]
    gate2 = mod[:, :, 5 * D_MODEL:6 * D_MODEL]
    def chain(c):
        ts = slice(c * tc, (c + 1) * tc)
        rs = slice(c * rows, (c + 1) * rows)
        flat = lambda ref, lo=0, hi=None: ref[:, ts, lo:hi].reshape(rows, -1)
        y = _gelu_tanh(yf_ref[rs, :] + yb_ref[rs, :] + d_ref[...] * u_ref[rs, :])
        ob = y * jax.nn.sigmoid(_dot(y.astype(BF16), wglu_ref[...]))
        ob = _to_batch_major(ob, slabs[2 * c:2 * c + 2], nb)
        x = x_ref[:, ts, :]
        h1 = (_rms(x, g1_ref[...]) * (1 + scale1) + shift1).reshape(rows, D_MODEL).astype(BF16)
        yield
        gate = lambda i: jax.nn.sigmoid(_dot(h1, wg_ref[0, :, i * D_MODEL:(i + 1) * D_MODEL]))
        merged = (gate(0) * _dot(flat(oa_ref), wa_ref[...])
                  + gate(1) * _dot(ob.astype(BF16), wb_ref[...])
                  + gate(2) * _dot(flat(oc_ref), wc_ref[...]))
        yield
        x1 = x + gate1 * _dot(merged.astype(BF16), wo_ref[...]).reshape(nb, tc, D_MODEL)
        h2 = _rms(x1, g2_ref[...]) * (1 + scale2) + shift2
        yield
        gu = _dot(h2.reshape(rows, D_MODEL).astype(BF16), wgu_ref[...])
        act = _silu(gu[:, :D_FF]) * gu[:, D_FF:]
        yield
        x2 = x1 + gate2 * _dot(act.astype(BF16), wd_ref[...]).reshape(nb, tc, D_MODEL)
        if final:
            x2 = _rms(x2, fg_ref[...])
        o_ref[:, ts, :] = x2

    _round_robin([chain(c) for c in range(ROW_CHAINS)])


def _merge(x, oa, yf, yb, u_rows, oc, mod_rows, g1, w_in, weights, final_g, layer, *, name):
    final = layer == DEPTH - 1
    nb, seq, _ = x.shape
    tt = ROW_TILE // nb
    slab = lambda w: pl.BlockSpec((nb, tt, w), lambda j: (0, j, 0))
    tmaj = pl.BlockSpec((ROW_TILE, SSM_WIDTH), lambda j: (j, 0))
    return pl.pallas_call(
        functools.partial(_merge_kernel, final=final),
        grid=(seq // tt,),
        in_specs=[slab(D_MODEL), slab(WIDTH_A), tmaj, tmaj, tmaj, slab(WIDTH_C), _const_spec(mod_rows.shape),
                  _layer_spec(g1, layer), _layer_cols_spec(w_in, layer, OFF_G, N_BRANCH * D_MODEL)]
                 + [_layer_spec(w, layer) for w in weights]
                 + [_const_spec(final_g.shape)],
        out_specs=slab(D_MODEL),
        out_shape=jax.ShapeDtypeStruct((nb, seq, D_MODEL), F32),
        scratch_shapes=[pltpu.VMEM((ROW_TILE // ROW_CHAINS, LANES), F32)] * (2 * ROW_CHAINS),
        compiler_params=_params(1),
        name=name + ("_final" if final else ""),
    )(x, oa, yf, yb, u_rows, oc, mod_rows, g1, w_in, *weights, final_g)


def _rope_tables():
    t = jnp.arange(DEC_SEQ)
    row = (t // GRID_W).astype(F32)
    col = (t % GRID_W).astype(F32)
    inv = 1.0 / (ROPE_THETA ** (jnp.arange(ROT_FREQS, dtype=F32) / ROT_FREQS))
    ar = row[:, None] * inv[None]
    ac = col[:, None] * inv[None]
    cos = jnp.concatenate([jnp.cos(ar), jnp.cos(ar), jnp.cos(ac), jnp.cos(ac)], axis=-1)
    sin = jnp.concatenate([-jnp.sin(ar), jnp.sin(ar), -jnp.sin(ac), jnp.sin(ac)], axis=-1)
    return jnp.tile(cos, (1, 2)), jnp.tile(sin, (1, 2))


def kernel(x_prompt, x_sample, c, cache_ga_k, cache_ga_v, cache_na_k, cache_na_v, state_ssm, c_ctx, w_mod, b_mod, norm1_g, w_in, qn_g, kn_g, ssm_lam_re, ssm_lam_im, ssm_log_step, ssm_b_re, ssm_b_im, ssm_c_re, ssm_c_im, ssm_d, ssm_w_glu, na_bias, w_br_a, w_br_b, w_br_c, w_out, norm2_g, w_ffn_gu, w_ffn_d, final_g):
    cvec = jnp.concatenate([c_ctx[None, :], c, jnp.zeros((N_MOD_ROWS - 1 - DEC_BATCH, D_MODEL), F32)], axis=0)
    mod = _adaln(cvec, w_mod, b_mod).reshape(DEPTH, N_MOD_ROWS, 1, 6 * D_MODEL)

    seg = jnp.kron(jnp.eye(MXU_DIM // HEAD_DIM, dtype=F32),
                   jnp.full((HEAD_DIM, HEAD_DIM), 1.0 / HEAD_DIM, F32)).astype(BF16)
    rope_tabs = _rope_tables()
    fg = final_g.reshape(1, D_MODEL)
    ck_a = cache_ga_k.reshape(DEC_BATCH, DEPTH, PAST_LEN, KV_WIDTH_A)
    cv_a = cache_ga_v.reshape(DEC_BATCH, DEPTH, PAST_LEN, KV_WIDTH_A)
    ck_c = cache_na_k.reshape(DEC_BATCH, DEPTH, PAST_LEN, WIDTH_C)
    cv_c = cache_na_v.reshape(DEC_BATCH, DEPTH, PAST_LEN, WIDTH_C)
    zero_state = jnp.zeros((2, 2, BATCH, SSM_LANES), F32)

    row = lambda p: p.reshape(DEPTH, 1, p.shape[-1])
    g1 = row(norm1_g)
    w_in_b = w_in.astype(BF16)
    qg = row(jnp.tile(qn_g, (1, N_HEADS_A)))
    kg = row(jnp.tile(kn_g, (1, N_KV_A)))
    merge_w = [row(ssm_d), ssm_w_glu.astype(BF16), w_br_a.astype(BF16), w_br_b.astype(BF16), w_br_c.astype(BF16),
               w_out.astype(BF16), row(norm2_g), w_ffn_gu.astype(BF16), w_ffn_d.astype(BF16)]

    xp, xs = x_prompt, x_sample
    cache = ()
    ssm_st = []
    for l in range(DEPTH):
        bmat, lam, cmat = _ssm_discretise(ssm_lam_re[l], ssm_lam_im[l], ssm_log_step[l], ssm_b_re[l], ssm_b_im[l],
                                          ssm_c_re[l], ssm_c_im[l])
        mod_ctx = mod[l, 0:1]
        mod_lat = mod[l, 1:1 + DEC_BATCH]

        qa, krep, vrep, ka, va, u_rows, qc, kc, vc = _inproj(xp, mod_ctx, g1, w_in_b, seg, qg, kg, l,
                                                             prev_cache=cache)
        cache = (ka, va, kc, vc)
        oa, oc = _attn_ctx(qa, krep, vrep, qc, kc, vc, l)
        yf, yb, hfin = _scan_ctx(u_rows, bmat, lam, cmat, zero_state)
        xp = _merge(xp, oa, yf, yb, u_rows, oc, mod_ctx, g1, w_in_b, merge_w, fg, l, name="merge_ctx")
        ssm_st.append(jnp.transpose(hfin, (2, 0, 1, 3)))

        qa, krep, vrep, u_rows, qc, kc, vc = _inproj(xs, mod_lat, g1, w_in_b, seg, qg, kg, l, rope_tabs=rope_tabs)
        oa = _attn_a_lat(qa, krep, vrep, ck_a, cv_a, l)
        oc = _na_lat(qc, kc, vc, ck_c, cv_c, _na_bias_rows(na_bias[l]), l)
        h0 = jnp.transpose(state_ssm[:, l].reshape(DEC_BATCH, 2, 2, SSM_LANES), (2, 1, 0, 3))
        h0 = h0.reshape(2, 2 * DEC_BATCH, SSM_LANES)
        lam_pair = jnp.broadcast_to(jnp.transpose(lam, (1, 0, 2, 3)), (2, 2, DEC_BATCH, SSM_LANES))
        lam_pair = lam_pair.reshape(2, 2 * DEC_BATCH, SSM_LANES)
        yf, yb = _scan_lat(u_rows, bmat, lam_pair, cmat, h0)
        xs = _merge(xs, oa, yf, yb, u_rows, oc, mod_lat, g1, w_in_b, merge_w, fg, l, name="merge_lat")

    ga_k, ga_v, na_k, na_v = cache
    new_ssm = jnp.stack(ssm_st, axis=1).reshape(BATCH, DEPTH, 2, 2, SSM_GROUPS, SSM_STATE)
    return (xp, xs,
            ga_k.reshape(BATCH, DEPTH, SEQ, N_KV_A, HEAD_DIM), ga_v.reshape(BATCH, DEPTH, SEQ, N_KV_A, HEAD_DIM),
            na_k.reshape(BATCH, DEPTH, SEQ, N_HEADS_C, HEAD_DIM), na_v.reshape(BATCH, DEPTH, SEQ, N_HEADS_C, HEAD_DIM),
            new_ssm)
```
